```python
import math
import jax, jax.numpy as jnp
from jax import lax
import numpy as np

D_MODEL = 1024
BATCH = 8
SEQ = 16384
DEPTH = 2

N_A_LAYERS = DEPTH // 2
N_B_LAYERS = DEPTH - N_A_LAYERS

A_HEADS = 4
A_QK_DIM = D_MODEL // 2 // A_HEADS
A_V_DIM = D_MODEL // A_HEADS
A_CHUNK = 64
GATE_SOFTCAP = 15.0
A_IN_DIM = 2 * A_HEADS * A_QK_DIM + 2 * A_HEADS * A_V_DIM + 2 * A_HEADS

B_GROUPS = ((128, 1), (512, 4), (2048, 16))
N_GROUPS = len(B_GROUPS)
B_HEAD_DIM = 64
B_HEADS = D_MODEL // B_HEAD_DIM
B_BLOCK = 128
REL_BUCKETS = 32
REL_MAX_DIST = 2048

D_FF = 2816
CONV_WIDTH = 3
EPS = 1e-6

kernel_name = "yoco_mlstm_dilated_attn_convffn"


def rmsnorm(x, g):
    xf = x.astype(jnp.float32)
    y = xf * lax.rsqrt(jnp.mean(xf * xf, axis=-1, keepdims=True) + EPS)
    return (y * g.astype(jnp.float32)).astype(x.dtype)


def softcap(z):
    return GATE_SOFTCAP * jnp.tanh(z / GATE_SOFTCAP)


def mlstm_chunkwise(q, k, v, li, lf):
    Bsz, S, H, dk = q.shape
    dv = v.shape[-1]
    L = A_CHUNK
    NC = S // L

    def to_chunks(t):
        return t.reshape(Bsz, NC, L, H, t.shape[-1]).transpose(1, 0, 3, 2, 4)

    def gate_chunks(t):
        return t.reshape(Bsz, NC, L, H).transpose(1, 0, 3, 2)

    causal = jnp.tril(jnp.ones((L, L), dtype=bool))

    def step(carry, xs):
        C, n, m = carry
        qc, kc, vc, lic, lfc = xs
        b = jnp.cumsum(lfc, axis=-1)
        logD = b[..., :, None] - b[..., None, :] + lic[..., None, :]
        logD = jnp.where(causal, logD, -jnp.inf)
        m_inter = b + m[..., None]
        m_t = jnp.maximum(m_inter, jnp.max(logD, axis=-1))
        Sm = jnp.einsum('bhtd,bhsd->bhts', qc, kc) * jnp.exp(logD - m_t[..., None])
        w_inter = jnp.exp(m_inter - m_t)
        num = (jnp.einsum('bhts,bhsv->bhtv', Sm, vc)
               + w_inter[..., None] * jnp.einsum('bhtd,bhdv->bhtv', qc, C))
        den = jnp.sum(Sm, axis=-1) + w_inter * jnp.einsum('bhtd,bhd->bht', qc, n)
        h = num / jnp.maximum(jnp.abs(den), jnp.exp(-m_t))[..., None]
        bL = b[..., -1]
        g = bL[..., None] - b + lic
        m_new = jnp.maximum(bL + m, jnp.max(g, axis=-1))
        wk = jnp.exp(g - m_new[..., None])
        decay = jnp.exp(bL + m - m_new)
        C_new = decay[..., None, None] * C + jnp.einsum('bhs,bhsd,bhsv->bhdv', wk, kc, vc)
        n_new = decay[..., None] * n + jnp.einsum('bhs,bhsd->bhd', wk, kc)
        return (C_new, n_new, m_new), h

    init = (jnp.zeros((Bsz, H, dk, dv), jnp.float32),
            jnp.zeros((Bsz, H, dk), jnp.float32),
            jnp.zeros((Bsz, H), jnp.float32))
    _, hs = lax.scan(step, init, (to_chunks(q), to_chunks(k), to_chunks(v),
                                  gate_chunks(li), gate_chunks(lf)))
    return hs.transpose(1, 0, 3, 2, 4).reshape(Bsz, S, H, dv)


def mlstm_layer(x, norm_g, w_in, b_if, hnorm_g, w_out):
    Bsz, S, _ = x.shape
    nq = A_HEADS * A_QK_DIM
    nv = A_HEADS * A_V_DIM
    z = rmsnorm(x, norm_g) @ w_in
    q, k, v, o, gi, gf = jnp.split(
        z, [nq, 2 * nq, 2 * nq + nv, 2 * nq + 2 * nv, 2 * nq + 2 * nv + A_HEADS], axis=-1)
    q = q.reshape(Bsz, S, A_HEADS, A_QK_DIM).astype(jnp.float32) * (A_QK_DIM ** -0.5)
    k = k.reshape(Bsz, S, A_HEADS, A_QK_DIM).astype(jnp.float32)
    v = v.reshape(Bsz, S, A_HEADS, A_V_DIM).astype(jnp.float32)
    bf = b_if.astype(jnp.float32)
    li = softcap(gi.astype(jnp.float32) + bf[:A_HEADS])
    lf = jax.nn.log_sigmoid(softcap(gf.astype(jnp.float32) + bf[A_HEADS:]))
    h = mlstm_chunkwise(q, k, v, li, lf)
    h = h * lax.rsqrt(jnp.mean(h * h, axis=-1, keepdims=True) + EPS) * hnorm_g.astype(jnp.float32)
    h = h.astype(x.dtype) * jax.nn.sigmoid(o.reshape(Bsz, S, A_HEADS, A_V_DIM))
    return h.reshape(Bsz, S, nv) @ w_out


def conv_ffn(x, norm_g, w_up, conv_w, conv_b, w_down):
    u = rmsnorm(x, norm_g) @ w_up
    up = jnp.pad(u, ((0, 0), (CONV_WIDTH - 1, 0), (0, 0)))
    u = up[:, :-2] * conv_w[0] + up[:, 1:-1] * conv_w[1] + up[:, 2:] * conv_w[2] + conv_b
    gate, val = jnp.split(u, 2, axis=-1)
    return (jax.nn.silu(gate) * val) @ w_down


def t5_bucket(dist):
    max_exact = REL_BUCKETS // 2
    d = np.maximum(dist, 0)
    log_ratio = np.log(np.maximum(d, 1) / max_exact) / math.log(REL_MAX_DIST / max_exact)
    large = np.minimum(max_exact + (log_ratio * (REL_BUCKETS - max_exact)).astype(np.int64),
                       REL_BUCKETS - 1)
    return np.where(d < max_exact, d, large).astype(np.int32)


def dilated_group(q, k, v, bias, win, dil):
    Bsz, S_pad, H, dh = q.shape
    nb = S_pad // (dil * B_BLOCK)

    def blocks(t):
        return t.reshape(Bsz, nb, B_BLOCK, dil, H, dh)

    def with_prev(t):
        prev = jnp.pad(t[:, :-1], ((0, 0), (1, 0), (0, 0), (0, 0), (0, 0), (0, 0)))
        return jnp.concatenate([prev, t], axis=2)

    qb = blocks(q)
    kc, vc = with_prev(blocks(k)), with_prev(blocks(v))
    delta = B_BLOCK + np.arange(B_BLOCK)[:, None] - np.arange(2 * B_BLOCK)[None, :]
    band = (delta >= 0) & (delta <= win // dil)
    first = (np.arange(nb)[:, None, None] == 0) & (np.arange(2 * B_BLOCK)[None, None, :] < B_BLOCK)
    valid = jnp.asarray(band[None] & ~first)[None, :, None, None]
    s = jnp.einsum('bnqrhe,bnkrhe->bnrhqk', qb, kc).astype(jnp.float32) * (dh ** -0.5) + bias
    s = jnp.where(valid, s, -jnp.inf)
    m = jnp.max(s, axis=-1, keepdims=True)
    p = jnp.exp(s - m)
    l = jnp.sum(p, axis=-1)
    o = jnp.einsum('bnrhqk,bnkrhe->bnrhqe', p.astype(vc.dtype), vc).astype(jnp.float32) / l[..., None]
    lse = m[..., 0] + jnp.log(l)
    o = o.transpose(0, 1, 4, 2, 3, 5).reshape(Bsz, S_pad, H, dh)
    lse = lse.transpose(0, 1, 4, 2, 3).reshape(Bsz, S_pad, H)
    return o, lse


def dilated_attention_layer(x, k_sh, v_sh, norm_g, w_q, w_out, rel_bias):
    Bsz, S, _ = x.shape
    q = (rmsnorm(x, norm_g) @ w_q).reshape(Bsz, S, N_GROUPS, B_HEADS, B_HEAD_DIM)
    span = max(d for _, d in B_GROUPS) * B_BLOCK
    S_pad = -(-S // span) * span
    padw = ((0, 0), (0, S_pad - S), (0, 0), (0, 0), (0, 0))
    q, k, v = jnp.pad(q, padw), jnp.pad(k_sh, padw), jnp.pad(v_sh, padw)
    outs, lses = [], []
    for g, (win, dil) in enumerate(B_GROUPS):
        delta = B_BLOCK + np.arange(B_BLOCK)[:, None] - np.arange(2 * B_BLOCK)[None, :]
        bucket = t5_bucket(delta * dil)
        bias = rel_bias[bucket][..., g * B_HEADS:(g + 1) * B_HEADS]
        bias = bias.transpose(2, 0, 1).astype(jnp.float32)
        o, lse = dilated_group(q[:, :, g], k[:, :, g], v[:, :, g], bias, win, dil)
        outs.append(o)
        lses.append(lse)
    w = jax.nn.softmax(jnp.stack(lses), axis=0)
    out = jnp.sum(w[..., None] * jnp.stack(outs), axis=0)[:, :S]
    return out.astype(x.dtype).reshape(Bsz, S, B_HEADS * B_HEAD_DIM) @ w_out


def shared_kv(x, kv_norm_g, w_kv):
    Bsz, S, _ = x.shape
    kv = (rmsnorm(x, kv_norm_g) @ w_kv).reshape(Bsz, S, 2, N_GROUPS, B_HEADS, B_HEAD_DIM)
    return kv[:, :, 0], kv[:, :, 1]


def _fwd_setup_inputs(seed: int = 0) -> dict:
    key = jax.random.key(seed)
    ks = jax.random.split(key, 20)
    f32 = jnp.float32
    nrm = lambda k, s, sc: jax.random.normal(k, s, f32) * sc
    att_w = N_GROUPS * B_HEADS * B_HEAD_DIM
    return {
        "x": nrm(ks[0], (BATCH, SEQ, D_MODEL), 1.0),
        "a_norm_g": 1.0 + nrm(ks[1], (N_A_LAYERS, D_MODEL), 0.02),
        "a_w_in": nrm(ks[2], (N_A_LAYERS, D_MODEL, A_IN_DIM), D_MODEL ** -0.5),
        "a_b_if": jnp.concatenate([nrm(ks[3], (N_A_LAYERS, A_HEADS), 0.1),
                                   3.0 + 3.0 * jax.random.uniform(ks[4], (N_A_LAYERS, A_HEADS), f32)], axis=-1),
        "a_hnorm_g": 1.0 + nrm(ks[5], (N_A_LAYERS, A_HEADS, A_V_DIM), 0.02),
        "a_w_out": nrm(ks[6], (N_A_LAYERS, A_HEADS * A_V_DIM, D_MODEL), (A_HEADS * A_V_DIM) ** -0.5),
        "kv_norm_g": 1.0 + nrm(ks[7], (D_MODEL,), 0.02),
        "w_kv": nrm(ks[8], (D_MODEL, 2 * att_w), D_MODEL ** -0.5),
        "b_norm_g": 1.0 + nrm(ks[9], (N_B_LAYERS, D_MODEL), 0.02),
        "b_w_q": nrm(ks[10], (N_B_LAYERS, D_MODEL, att_w), D_MODEL ** -0.5),
        "b_w_out": nrm(ks[11], (N_B_LAYERS, B_HEADS * B_HEAD_DIM, D_MODEL), (B_HEADS * B_HEAD_DIM) ** -0.5),
        "rel_bias": nrm(ks[12], (REL_BUCKETS, N_GROUPS * B_HEADS), 0.5),
        "f_norm_g": 1.0 + nrm(ks[13], (DEPTH, D_MODEL), 0.02),
        "f_w_up": nrm(ks[14], (DEPTH, D_MODEL, 2 * D_FF), D_MODEL ** -0.5),
        "f_conv_w": nrm(ks[15], (DEPTH, CONV_WIDTH, 2 * D_FF), CONV_WIDTH ** -0.5),
        "f_conv_b": nrm(ks[16], (DEPTH, 2 * D_FF), 0.01),
        "f_w_down": nrm(ks[17], (DEPTH, D_FF, D_MODEL), D_FF ** -0.5),
        "final_norm_g": 1.0 + nrm(ks[18], (D_MODEL,), 0.02),
    }


def _fwd_reference(x, a_norm_g, a_w_in, a_b_if, a_hnorm_g, a_w_out, kv_norm_g, w_kv,
              b_norm_g, b_w_q, b_w_out, rel_bias, f_norm_g, f_w_up, f_conv_w,
              f_conv_b, f_w_down, final_norm_g):
    k_sh = v_sh = None
    for layer in range(DEPTH):
        if layer < N_A_LAYERS:
            x = x + mlstm_layer(x, a_norm_g[layer], a_w_in[layer], a_b_if[layer],
                                a_hnorm_g[layer], a_w_out[layer])
        else:
            j = layer - N_A_LAYERS
            if j == 0:
                k_sh, v_sh = shared_kv(x, kv_norm_g, w_kv)
            x = x + dilated_attention_layer(x, k_sh, v_sh, b_norm_g[j], b_w_q[j],
                                            b_w_out[j], rel_bias)
        x = x + conv_ffn(x, f_norm_g[layer], f_w_up[layer], f_conv_w[layer],
                         f_conv_b[layer], f_w_down[layer])
    return rmsnorm(x, final_norm_g)


import jax as _jax
import jax.numpy as _jnp

TWIN_FORMAT = 'train_step'
FWD_PARAMS = ['x', 'a_norm_g', 'a_w_in', 'a_b_if', 'a_hnorm_g', 'a_w_out', 'kv_norm_g', 'w_kv', 'b_norm_g', 'b_w_q', 'b_w_out', 'rel_bias', 'f_norm_g', 'f_w_up', 'f_conv_w', 'f_conv_b', 'f_w_down', 'final_norm_g']
TWIN_WEIGHTS = ['a_norm_g', 'a_w_in', 'a_b_if', 'a_hnorm_g', 'a_w_out', 'kv_norm_g', 'w_kv', 'b_norm_g', 'b_w_q', 'b_w_out', 'rel_bias', 'f_norm_g', 'f_w_up', 'f_conv_w', 'f_conv_b', 'f_w_down', 'final_norm_g']
TWIN_DIFF_INPUT = 'x'
TWIN_INPUTS = ['x', 'a_norm_g', 'a_w_in', 'a_b_if', 'a_hnorm_g', 'a_w_out', 'kv_norm_g', 'w_kv', 'b_norm_g', 'b_w_q', 'b_w_out', 'rel_bias', 'f_norm_g', 'f_w_up', 'f_conv_w', 'f_conv_b', 'f_w_down', 'final_norm_g', 'loss_target', 'm_a_norm_g', 'm_a_w_in', 'm_a_b_if', 'm_a_hnorm_g', 'm_a_w_out', 'm_kv_norm_g', 'm_w_kv', 'm_b_norm_g', 'm_b_w_q', 'm_b_w_out', 'm_rel_bias', 'm_f_norm_g', 'm_f_w_up', 'm_f_conv_w', 'm_f_conv_b', 'm_f_w_down', 'm_final_norm_g', 'v_a_norm_g', 'v_a_w_in', 'v_a_b_if', 'v_a_hnorm_g', 'v_a_w_out', 'v_kv_norm_g', 'v_w_kv', 'v_b_norm_g', 'v_b_w_q', 'v_b_w_out', 'v_rel_bias', 'v_f_norm_g', 'v_f_w_up', 'v_f_conv_w', 'v_f_conv_b', 'v_f_w_down', 'v_final_norm_g']
TWIN_OUTPUTS = ['loss', 'grad_x', 'grad_a_norm_g', 'grad_a_w_in', 'grad_a_b_if', 'grad_a_hnorm_g', 'grad_a_w_out', 'grad_kv_norm_g', 'grad_w_kv', 'grad_b_norm_g', 'grad_b_w_q', 'grad_b_w_out', 'grad_rel_bias', 'grad_f_norm_g', 'grad_f_w_up', 'grad_f_conv_w', 'grad_f_conv_b', 'grad_f_w_down', 'grad_final_norm_g', 'delta_a_norm_g', 'delta_a_w_in', 'delta_a_b_if', 'delta_a_hnorm_g', 'delta_a_w_out', 'delta_kv_norm_g', 'delta_w_kv', 'delta_b_norm_g', 'delta_b_w_q', 'delta_b_w_out', 'delta_rel_bias', 'delta_f_norm_g', 'delta_f_w_up', 'delta_f_conv_w', 'delta_f_conv_b', 'delta_f_w_down', 'delta_final_norm_g', 'new_m_a_norm_g', 'new_m_a_w_in', 'new_m_a_b_if', 'new_m_a_hnorm_g', 'new_m_a_w_out', 'new_m_kv_norm_g', 'new_m_w_kv', 'new_m_b_norm_g', 'new_m_b_w_q', 'new_m_b_w_out', 'new_m_rel_bias', 'new_m_f_norm_g', 'new_m_f_w_up', 'new_m_f_conv_w', 'new_m_f_conv_b', 'new_m_f_w_down', 'new_m_final_norm_g', 'new_v_a_norm_g', 'new_v_a_w_in', 'new_v_a_b_if', 'new_v_a_hnorm_g', 'new_v_a_w_out', 'new_v_kv_norm_g', 'new_v_w_kv', 'new_v_b_norm_g', 'new_v_b_w_q', 'new_v_b_w_out', 'new_v_rel_bias', 'new_v_f_norm_g', 'new_v_f_w_up', 'new_v_f_conv_w', 'new_v_f_conv_b', 'new_v_f_w_down', 'new_v_final_norm_g']
TWIN_LEAF_KINDS = {'loss': 'loss', 'grad_x': 'grad_x', 'grad_a_norm_g': 'grad_w', 'grad_a_w_in': 'grad_w', 'grad_a_b_if': 'grad_w', 'grad_a_hnorm_g': 'grad_w', 'grad_a_w_out': 'grad_w', 'grad_kv_norm_g': 'grad_w', 'grad_w_kv': 'grad_w', 'grad_b_norm_g': 'grad_w', 'grad_b_w_q': 'grad_w', 'grad_b_w_out': 'grad_w', 'grad_rel_bias': 'grad_w', 'grad_f_norm_g': 'grad_w', 'grad_f_w_up': 'grad_w', 'grad_f_conv_w': 'grad_w', 'grad_f_conv_b': 'grad_w', 'grad_f_w_down': 'grad_w', 'grad_final_norm_g': 'grad_w', 'delta_a_norm_g': 'delta_w', 'delta_a_w_in': 'delta_w', 'delta_a_b_if': 'delta_w', 'delta_a_hnorm_g': 'delta_w', 'delta_a_w_out': 'delta_w', 'delta_kv_norm_g': 'delta_w', 'delta_w_kv': 'delta_w', 'delta_b_norm_g': 'delta_w', 'delta_b_w_q': 'delta_w', 'delta_b_w_out': 'delta_w', 'delta_rel_bias': 'delta_w', 'delta_f_norm_g': 'delta_w', 'delta_f_w_up': 'delta_w', 'delta_f_conv_w': 'delta_w', 'delta_f_conv_b': 'delta_w', 'delta_f_w_down': 'delta_w', 'delta_final_norm_g': 'delta_w', 'new_m_a_norm_g': 'new_m', 'new_m_a_w_in': 'new_m', 'new_m_a_b_if': 'new_m', 'new_m_a_hnorm_g': 'new_m', 'new_m_a_w_out': 'new_m', 'new_m_kv_norm_g': 'new_m', 'new_m_w_kv': 'new_m', 'new_m_b_norm_g': 'new_m', 'new_m_b_w_q': 'new_m', 'new_m_b_w_out': 'new_m', 'new_m_rel_bias': 'new_m', 'new_m_f_norm_g': 'new_m', 'new_m_f_w_up': 'new_m', 'new_m_f_conv_w': 'new_m', 'new_m_f_conv_b': 'new_m', 'new_m_f_w_down': 'new_m', 'new_m_final_norm_g': 'new_m', 'new_v_a_norm_g': 'new_v', 'new_v_a_w_in': 'new_v', 'new_v_a_b_if': 'new_v', 'new_v_a_hnorm_g': 'new_v', 'new_v_a_w_out': 'new_v', 'new_v_kv_norm_g': 'new_v', 'new_v_w_kv': 'new_v', 'new_v_b_norm_g': 'new_v', 'new_v_b_w_q': 'new_v', 'new_v_b_w_out': 'new_v', 'new_v_rel_bias': 'new_v', 'new_v_f_norm_g': 'new_v', 'new_v_f_w_up': 'new_v', 'new_v_f_conv_w': 'new_v', 'new_v_f_conv_b': 'new_v', 'new_v_f_w_down': 'new_v', 'new_v_final_norm_g': 'new_v'}


def _forward(args):
    return _fwd_reference(*[args[k] for k in FWD_PARAMS])


def _output_shape():
    def fwd():
        inp = _fwd_setup_inputs(0)
        return _fwd_reference(*[inp[k] for k in FWD_PARAMS])
    out = _jax.eval_shape(fwd)
    return out.shape, out.dtype

N_MICROBATCH = 1
ADAM_LR = 0.001
ADAM_B1 = 0.9
ADAM_B2 = 0.999
ADAM_EPS = 1e-08
ADAM_WD = 0.01
ADAM_STEP = 10
PER_EXAMPLE_BATCH_AXIS = {'x': 0, 'loss_target': 0}
SHARED_INPUTS = []
_WEIGHT_DTYPES = {'a_norm_g': _jnp.float32, 'a_w_in': _jnp.float32, 'a_b_if': _jnp.float32, 'a_hnorm_g': _jnp.float32, 'a_w_out': _jnp.float32, 'kv_norm_g': _jnp.float32, 'w_kv': _jnp.float32, 'b_norm_g': _jnp.float32, 'b_w_q': _jnp.float32, 'b_w_out': _jnp.float32, 'rel_bias': _jnp.float32, 'f_norm_g': _jnp.float32, 'f_w_up': _jnp.float32, 'f_conv_w': _jnp.float32, 'f_conv_b': _jnp.float32, 'f_w_down': _jnp.float32, 'final_norm_g': _jnp.float32}
MOMENT_SCALE = {'a_norm_g': 7.417042e-01, 'a_w_in': 4.618855e-01, 'a_b_if': 1.018674e+00, 'a_hnorm_g': 2.143827e-01, 'a_w_out': 2.092084e-01, 'kv_norm_g': 8.136175e-02, 'w_kv': 3.184188e-02, 'b_norm_g': 5.256229e-02, 'b_w_q': 2.952804e-02, 'b_w_out': 5.895685e-02, 'rel_bias': 3.660756e-02, 'f_norm_g': 2.491378e-01, 'f_w_up': 1.039084e-01, 'f_conv_w': 1.050255e-01, 'f_conv_b': 1.014254e-01, 'f_w_down': 1.706964e-01, 'final_norm_g': 1.280896e+02}


def _to_microbatches(a, axis):
    t = _jnp.moveaxis(a, axis, 0)
    t = t.reshape((N_MICROBATCH, t.shape[0] // N_MICROBATCH) + t.shape[1:])
    return _jnp.moveaxis(t, 1, axis + 1)


def setup_inputs(seed: int = 0) -> dict:
    inp = _fwd_setup_inputs(seed)
    key = _jax.random.fold_in(_jax.random.key(seed), 7919)
    shape, _ = _output_shape()
    out = dict(inp)
    out["loss_target"] = _jax.random.normal(_jax.random.fold_in(key, 0), shape, _jnp.float32)
    for i, name in enumerate(TWIN_WEIGHTS):
        w = inp[name].astype(_jnp.float32)
        if MOMENT_SCALE is None:
            s = _jnp.sqrt(_jnp.mean(_jnp.square(w)) + 1e-30)
        else:
            s = MOMENT_SCALE[name]
        km, kv = _jax.random.split(_jax.random.fold_in(key, i + 1))
        out[name] = w
        out["m_" + name] = s * _jax.random.normal(km, w.shape, _jnp.float32)
        out["v_" + name] = (s * s) * _jax.random.uniform(kv, w.shape, _jnp.float32, 0.5, 1.5)
    if N_MICROBATCH > 1:
        for name, axis in PER_EXAMPLE_BATCH_AXIS.items():
            out[name] = _to_microbatches(out[name], axis)
    return {'x': out['x'], 'a_norm_g': out['a_norm_g'], 'a_w_in': out['a_w_in'], 'a_b_if': out['a_b_if'], 'a_hnorm_g': out['a_hnorm_g'], 'a_w_out': out['a_w_out'], 'kv_norm_g': out['kv_norm_g'], 'w_kv': out['w_kv'], 'b_norm_g': out['b_norm_g'], 'b_w_q': out['b_w_q'], 'b_w_out': out['b_w_out'], 'rel_bias': out['rel_bias'], 'f_norm_g': out['f_norm_g'], 'f_w_up': out['f_w_up'], 'f_conv_w': out['f_conv_w'], 'f_conv_b': out['f_conv_b'], 'f_w_down': out['f_w_down'], 'final_norm_g': out['final_norm_g'], 'loss_target': out['loss_target'], 'm_a_norm_g': out['m_a_norm_g'], 'm_a_w_in': out['m_a_w_in'], 'm_a_b_if': out['m_a_b_if'], 'm_a_hnorm_g': out['m_a_hnorm_g'], 'm_a_w_out': out['m_a_w_out'], 'm_kv_norm_g': out['m_kv_norm_g'], 'm_w_kv': out['m_w_kv'], 'm_b_norm_g': out['m_b_norm_g'], 'm_b_w_q': out['m_b_w_q'], 'm_b_w_out': out['m_b_w_out'], 'm_rel_bias': out['m_rel_bias'], 'm_f_norm_g': out['m_f_norm_g'], 'm_f_w_up': out['m_f_w_up'], 'm_f_conv_w': out['m_f_conv_w'], 'm_f_conv_b': out['m_f_conv_b'], 'm_f_w_down': out['m_f_w_down'], 'm_final_norm_g': out['m_final_norm_g'], 'v_a_norm_g': out['v_a_norm_g'], 'v_a_w_in': out['v_a_w_in'], 'v_a_b_if': out['v_a_b_if'], 'v_a_hnorm_g': out['v_a_hnorm_g'], 'v_a_w_out': out['v_a_w_out'], 'v_kv_norm_g': out['v_kv_norm_g'], 'v_w_kv': out['v_w_kv'], 'v_b_norm_g': out['v_b_norm_g'], 'v_b_w_q': out['v_b_w_q'], 'v_b_w_out': out['v_b_w_out'], 'v_rel_bias': out['v_rel_bias'], 'v_f_norm_g': out['v_f_norm_g'], 'v_f_w_up': out['v_f_w_up'], 'v_f_conv_w': out['v_f_conv_w'], 'v_f_conv_b': out['v_f_conv_b'], 'v_f_w_down': out['v_f_w_down'], 'v_final_norm_g': out['v_final_norm_g']}


def _loss(weights, diff, rest, loss_target):
    with _jax.named_scope("forward"):
        args = {**rest, TWIN_DIFF_INPUT: diff, **{k: w.astype(_WEIGHT_DTYPES[k]) for k, w in weights.items()}}
        y = _forward(args)
    with _jax.named_scope("loss_head"):
        err = _jnp.square(y.astype(_jnp.float32) - loss_target)
        return 0.5 * _jnp.sum(_jnp.mean(err, axis=-1)) if err.ndim else 0.5 * err


def _adamw(w, g, m, v):
    m = ADAM_B1 * m + (1.0 - ADAM_B1) * g
    v = ADAM_B2 * v + (1.0 - ADAM_B2) * _jnp.square(g)
    m_hat = m / (1.0 - ADAM_B1 ** ADAM_STEP)
    v_hat = v / (1.0 - ADAM_B2 ** ADAM_STEP)
    delta = -ADAM_LR * (m_hat / (_jnp.sqrt(v_hat) + ADAM_EPS) + ADAM_WD * w)
    return delta, m, v


def reference(x, a_norm_g, a_w_in, a_b_if, a_hnorm_g, a_w_out, kv_norm_g, w_kv, b_norm_g, b_w_q, b_w_out, rel_bias, f_norm_g, f_w_up, f_conv_w, f_conv_b, f_w_down, final_norm_g, loss_target, m_a_norm_g, m_a_w_in, m_a_b_if, m_a_hnorm_g, m_a_w_out, m_kv_norm_g, m_w_kv, m_b_norm_g, m_b_w_q, m_b_w_out, m_rel_bias, m_f_norm_g, m_f_w_up, m_f_conv_w, m_f_conv_b, m_f_w_down, m_final_norm_g, v_a_norm_g, v_a_w_in, v_a_b_if, v_a_hnorm_g, v_a_w_out, v_kv_norm_g, v_w_kv, v_b_norm_g, v_b_w_q, v_b_w_out, v_rel_bias, v_f_norm_g, v_f_w_up, v_f_conv_w, v_f_conv_b, v_f_w_down, v_final_norm_g):
    given = dict(x=x, a_norm_g=a_norm_g, a_w_in=a_w_in, a_b_if=a_b_if, a_hnorm_g=a_hnorm_g, a_w_out=a_w_out, kv_norm_g=kv_norm_g, w_kv=w_kv, b_norm_g=b_norm_g, b_w_q=b_w_q, b_w_out=b_w_out, rel_bias=rel_bias, f_norm_g=f_norm_g, f_w_up=f_w_up, f_conv_w=f_conv_w, f_conv_b=f_conv_b, f_w_down=f_w_down, final_norm_g=final_norm_g, loss_target=loss_target, m_a_norm_g=m_a_norm_g, m_a_w_in=m_a_w_in, m_a_b_if=m_a_b_if, m_a_hnorm_g=m_a_hnorm_g, m_a_w_out=m_a_w_out, m_kv_norm_g=m_kv_norm_g, m_w_kv=m_w_kv, m_b_norm_g=m_b_norm_g, m_b_w_q=m_b_w_q, m_b_w_out=m_b_w_out, m_rel_bias=m_rel_bias, m_f_norm_g=m_f_norm_g, m_f_w_up=m_f_w_up, m_f_conv_w=m_f_conv_w, m_f_conv_b=m_f_conv_b, m_f_w_down=m_f_w_down, m_final_norm_g=m_final_norm_g, v_a_norm_g=v_a_norm_g, v_a_w_in=v_a_w_in, v_a_b_if=v_a_b_if, v_a_hnorm_g=v_a_hnorm_g, v_a_w_out=v_a_w_out, v_kv_norm_g=v_kv_norm_g, v_w_kv=v_w_kv, v_b_norm_g=v_b_norm_g, v_b_w_q=v_b_w_q, v_b_w_out=v_b_w_out, v_rel_bias=v_rel_bias, v_f_norm_g=v_f_norm_g, v_f_w_up=v_f_w_up, v_f_conv_w=v_f_conv_w, v_f_conv_b=v_f_conv_b, v_f_w_down=v_f_w_down, v_final_norm_g=v_final_norm_g)
    weights = {n: given[n] for n in TWIN_WEIGHTS}
    shared = {n: given[n] for n in SHARED_INPUTS}
    per_example = {n: given[n] for n in ['x']}
    grad_fn = _jax.value_and_grad(_loss, argnums=(0, 1))

    def one_microbatch(ex, loss_target):
        ex = dict(ex)
        diff = ex.pop(TWIN_DIFF_INPUT)
        return grad_fn(weights, diff, {**shared, **ex}, loss_target)

    if N_MICROBATCH == 1:
        loss, (grad_w, grad_x) = one_microbatch(per_example, given["loss_target"])
    else:
        def body(carry, xs):
            loss_sum, grad_sum = carry
            l_k, (gw_k, gx_k) = one_microbatch(xs[0], xs[1])
            with _jax.named_scope("update"):
                return (loss_sum + l_k, _jax.tree.map(_jnp.add, grad_sum, gw_k)), gx_k

        init = (_jnp.zeros((), _jnp.float32), _jax.tree.map(_jnp.zeros_like, weights))
        (loss, grad_w), grad_x = _jax.lax.scan(body, init, (per_example, given["loss_target"]))
    with _jax.named_scope("update"):
        delta_w, new_m, new_v = {}, {}, {}
        for n in TWIN_WEIGHTS:
            delta_w[n], new_m[n], new_v[n] = _adamw(weights[n], grad_w[n], given["m_" + n], given["v_" + n])
    return (loss, grad_x, *[grad_w[n] for n in TWIN_WEIGHTS], *[delta_w[n] for n in TWIN_WEIGHTS],
            *[new_m[n] for n in TWIN_WEIGHTS], *[new_v[n] for n in TWIN_WEIGHTS])
```

```python
import functools
import math

import numpy as np
import jax
import jax.numpy as jnp
from jax import lax
from jax.experimental import pallas as pl
from jax.experimental.pallas import tpu as pltpu

F32 = jnp.float32
BF16 = jnp.bfloat16
HIGHEST = lax.Precision.HIGHEST
MESH = pl.DeviceIdType.MESH

D = 1024
A_HEADS = 4
A_QK = 128
A_V = 256
SOFTCAP = 15.0
N_GROUPS = 3
B_GROUPS = ((128, 1), (512, 4), (2048, 16))
B_HEADS = 16
B_DH = 64
BLK = 128
REL_BUCKETS = 32
REL_MAX_DIST = 2048
D_FF = 2816
EPS = 1e-6
ADAM_LR = 0.001
ADAM_B1 = 0.9
ADAM_B2 = 0.999
ADAM_EPS = 1e-08
ADAM_WD = 0.01
ADAM_STEP = 10

N_DEV = 8
V7X_VMEM_BYTES = 64 * 1024 * 1024
VMEM_LIMIT = V7X_VMEM_BYTES - 8 * 1024 * 1024
MLSTM_CHUNK = 256
Z_W = 3328
Z_GI = 3072
Z_GF = 3200


def _cp(sem):
    return pltpu.CompilerParams(dimension_semantics=sem, vmem_limit_bytes=VMEM_LIMIT)


def _dot(a, b, **kw):
    return jnp.dot(a, b, preferred_element_type=F32, **kw)


def _dot_nt(a, b):
    return lax.dot_general(a, b, (((1,), (1,)), ((), ())), preferred_element_type=F32)


def _dot_tn(a, b):
    return lax.dot_general(a, b, (((0,), (0,)), ((), ())), preferred_element_type=F32)


def mm(a, b, *, name, out_dtype, tn, tm=512, res=None):
    M, K = a.shape
    N = b.shape[1]
    assert M % tm == 0 and N % tn == 0 and b.shape[0] == K

    def body(a_ref, b_ref, *rest):
        o_ref = rest[-1]
        acc = _dot(a_ref[...].astype(BF16), b_ref[...])
        if res is not None:
            acc = acc + rest[0][...]
        o_ref[...] = acc.astype(out_dtype)

    in_specs = [pl.BlockSpec((tm, K), lambda j, i: (i, 0)), pl.BlockSpec((K, tn), lambda j, i: (0, j))]
    args = [a, b]
    if res is not None:
        in_specs.append(pl.BlockSpec((tm, tn), lambda j, i: (i, j)))
        args.append(res)
    return pl.pallas_call(
        body, name=name, grid=(N // tn, M // tm), in_specs=in_specs,
        out_specs=pl.BlockSpec((tm, tn), lambda j, i: (i, j)),
        out_shape=jax.ShapeDtypeStruct((M, N), out_dtype),
        compiler_params=_cp(("parallel", "parallel")),
    )(*args)


def mm_tn(a, b, *, name, tn, tk=512):
    S, Kd = a.shape
    N = b.shape[1]
    assert S % tk == 0 and N % tn == 0 and b.shape[0] == S

    def body(a_ref, b_ref, o_ref):
        @pl.when(pl.program_id(1) == 0)
        def _():
            o_ref[...] = jnp.zeros_like(o_ref)

        o_ref[...] += _dot_tn(a_ref[...].astype(BF16), b_ref[...].astype(BF16))

    return pl.pallas_call(
        body, name=name, grid=(N // tn, S // tk),
        in_specs=[pl.BlockSpec((tk, Kd), lambda j, k: (k, 0)), pl.BlockSpec((tk, tn), lambda j, k: (k, j))],
        out_specs=pl.BlockSpec((Kd, tn), lambda j, k: (0, j)),
        out_shape=jax.ShapeDtypeStruct((Kd, N), F32),
        compiler_params=_cp(("parallel", "arbitrary")),
    )(a, b)


def rms_fwd(x, gains, *, name, tm=1024):
    S = x.shape[0]
    n = len(gains)

    def body(x_ref, *rest):
        xf = x_ref[...]
        y = xf * lax.rsqrt(jnp.mean(xf * xf, axis=-1, keepdims=True) + EPS)
        for i in range(n):
            rest[n + i][...] = (y * rest[i][...]).astype(BF16)

    return pl.pallas_call(
        body, name=name, grid=(S // tm,),
        in_specs=[pl.BlockSpec((tm, D), lambda i: (i, 0))] + [pl.BlockSpec((1, D), lambda i: (0, 0))] * n,
        out_specs=[pl.BlockSpec((tm, D), lambda i: (i, 0))] * n,
        out_shape=[jax.ShapeDtypeStruct((S, D), BF16)] * n,
        compiler_params=_cp(("parallel",)),
    )(x, *gains)


def rms_bwd(x, dres, branches, *, name, tm=512):
    S = x.shape[0]
    n = len(branches)

    def body(x_ref, dres_ref, *rest):
        dxn_refs = rest[:n]
        g_refs = rest[n:2 * n]
        dx_ref = rest[2 * n]
        dg_refs = rest[2 * n + 1:]

        @pl.when(pl.program_id(0) == 0)
        def _():
            for r in dg_refs:
                r[...] = jnp.zeros_like(r)

        xf = x_ref[...]
        r = lax.rsqrt(jnp.mean(xf * xf, axis=-1, keepdims=True) + EPS)
        xhat = xf * r
        total = dres_ref[...]
        for i in range(n):
            dy = dxn_refs[i][...]
            dg_refs[i][...] += jnp.sum(dy * xhat, axis=0, keepdims=True)
            dyg = dy * g_refs[i][...]
            total = total + r * (dyg - xhat * jnp.mean(dyg * xhat, axis=-1, keepdims=True))
        dx_ref[...] = total

    row = pl.BlockSpec((tm, D), lambda i: (i, 0))
    vec = pl.BlockSpec((1, D), lambda i: (0, 0))
    outs = pl.pallas_call(
        body, name=name, grid=(S // tm,),
        in_specs=[row, row] + [row] * n + [vec] * n,
        out_specs=[row] + [vec] * n,
        out_shape=[jax.ShapeDtypeStruct((S, D), F32)] + [jax.ShapeDtypeStruct((1, D), F32)] * n,
        compiler_params=_cp(("arbitrary",)),
    )(x, dres, *[b[0] for b in branches], *[b[1] for b in branches])
    return outs[0], outs[1:]


def loss_head(x, target, g, *, name, tm=512):
    S = x.shape[0]

    def body(x_ref, t_ref, g_ref, dx_ref, dg_ref, loss_ref):
        @pl.when(pl.program_id(0) == 0)
        def _():
            dg_ref[...] = jnp.zeros_like(dg_ref)
            loss_ref[...] = jnp.zeros_like(loss_ref)

        xf = x_ref[...]
        gg = g_ref[...]
        r = lax.rsqrt(jnp.mean(xf * xf, axis=-1, keepdims=True) + EPS)
        xhat = xf * r
        e = xhat * gg - t_ref[...]
        per_tok = jnp.mean(e * e, axis=-1, keepdims=True)
        loss_ref[...] += 0.5 * jnp.sum(per_tok, axis=0, keepdims=True)
        dy = e * (1.0 / D)
        dg_ref[...] += jnp.sum(dy * xhat, axis=0, keepdims=True)
        dyg = dy * gg
        dx_ref[...] = r * (dyg - xhat * jnp.mean(dyg * xhat, axis=-1, keepdims=True))

    row = pl.BlockSpec((tm, D), lambda i: (i, 0))
    vec = pl.BlockSpec((1, D), lambda i: (0, 0))
    return pl.pallas_call(
        body, name=name, grid=(S // tm,),
        in_specs=[row, row, vec],
        out_specs=[row, vec, pl.BlockSpec((1, 128), lambda i: (0, 0))],
        out_shape=[jax.ShapeDtypeStruct((S, D), F32), jax.ShapeDtypeStruct((1, D), F32),
                   jax.ShapeDtypeStruct((1, 128), F32)],
        compiler_params=_cp(("arbitrary",)),
    )(x, target, g)


def _sigmoid(x):
    return 1.0 / (1.0 + jnp.exp(-x))


def _gates(z_ref, bi_ref, bf_ref):
    li = SOFTCAP * jnp.tanh((z_ref[:, Z_GI:Z_GI + 128] + bi_ref[...]) * (1.0 / SOFTCAP))
    scf = SOFTCAP * jnp.tanh((z_ref[:, Z_GF:Z_GF + 128] + bf_ref[...]) * (1.0 / SOFTCAP))
    lf = jnp.minimum(scf, 0.0) - jnp.log(1.0 + jnp.exp(-jnp.abs(scf)))
    return li, scf, lf


def _tri(L, lower):
    r = lax.broadcasted_iota(jnp.int32, (L, L), 0)
    c = lax.broadcasted_iota(jnp.int32, (L, L), 1)
    return (r >= c) if lower else (r <= c)


def mlstm_fwd(z, bi, bf, *, name):
    S = z.shape[0]
    L = MLSTM_CHUNK
    NC = S // L
    scale = A_QK ** -0.5

    def body(z_ref, bi_ref, bf_ref, h_ref, cst_ref, nst_ref, C_s, n_s):
        @pl.when(pl.program_id(0) == 0)
        def _():
            C_s[...] = jnp.zeros_like(C_s)
            n_s[...] = jnp.zeros_like(n_s)

        li, _, lf = _gates(z_ref, bi_ref, bf_ref)
        causal = _tri(L, True)
        b = _dot(causal.astype(F32), lf, precision=HIGHEST)
        liT = li.T
        bT = b.T
        cst_ref[0] = C_s[...].astype(BF16)
        nst_ref[0] = n_s[...]
        for h in range(A_HEADS):
            q = z_ref[:, h * A_QK:(h + 1) * A_QK] * scale
            k = z_ref[:, 512 + h * A_QK:512 + (h + 1) * A_QK]
            qb = q.astype(BF16)
            kb = k.astype(BF16)
            vb = z_ref[:, 1024 + h * A_V:1024 + (h + 1) * A_V].astype(BF16)
            a_col, b_col = li[:, h:h + 1], b[:, h:h + 1]
            a_row, b_row = liT[h:h + 1, :], bT[h:h + 1, :]
            Dm = jnp.exp(jnp.where(causal, b_col - b_row + a_row, -jnp.inf))
            A = _dot_nt(qb, kb) * Dm
            eb = jnp.exp(b_col)
            Ch = C_s[h]
            nh = n_s[h:h + 1, :]
            num = _dot(A.astype(BF16), vb) + eb * _dot(qb, Ch.astype(BF16))
            den = jnp.sum(A, axis=-1, keepdims=True) + eb * jnp.sum(q * nh, axis=-1, keepdims=True)
            h_ref[:, h * A_V:(h + 1) * A_V] = num / jnp.maximum(jnp.abs(den), 1.0)
            bL = b_col[L - 1:L, :]
            kw = jnp.exp(bL - b_col + a_col) * k
            decay = jnp.exp(bL)
            C_s[h] = decay * Ch + _dot_tn(kw.astype(BF16), vb)
            n_s[h:h + 1, :] = decay * nh + jnp.sum(kw, axis=0, keepdims=True)

    vec = pl.BlockSpec((1, 128), lambda c: (0, 0))
    return pl.pallas_call(
        body, name=name, grid=(NC,),
        in_specs=[pl.BlockSpec((L, Z_W), lambda c: (c, 0)), vec, vec],
        out_specs=[pl.BlockSpec((L, 1024), lambda c: (c, 0)),
                   pl.BlockSpec((1, A_HEADS, A_QK, A_V), lambda c: (c, 0, 0, 0)),
                   pl.BlockSpec((1, 8, 128), lambda c: (c, 0, 0))],
        out_shape=[jax.ShapeDtypeStruct((S, 1024), F32),
                   jax.ShapeDtypeStruct((NC, A_HEADS, A_QK, A_V), BF16),
                   jax.ShapeDtypeStruct((NC, 8, 128), F32)],
        scratch_shapes=[pltpu.VMEM((A_HEADS, A_QK, A_V), F32), pltpu.VMEM((8, 128), F32)],
        compiler_params=_cp(("arbitrary",)),
    )(z, bi, bf)


def mlstm_bwd(z, bi, bf, cst, nst, dh, dzo, *, name):
    S = z.shape[0]
    L = MLSTM_CHUNK
    NC = S // L
    scale = A_QK ** -0.5

    def body(z_ref, bi_ref, bf_ref, cst_ref, nst_ref, dh_ref, dzo_ref, dz_ref, db_ref, dC_s, dn_s):
        @pl.when(pl.program_id(0) == 0)
        def _():
            dC_s[...] = jnp.zeros_like(dC_s)
            dn_s[...] = jnp.zeros_like(dn_s)
            db_ref[...] = jnp.zeros_like(db_ref)

        li, scf, lf = _gates(z_ref, bi_ref, bf_ref)
        causal = _tri(L, True)
        b = _dot(causal.astype(F32), lf, precision=HIGHEST)
        liT = li.T
        bT = b.T
        lane = lax.broadcasted_iota(jnp.int32, (L, 128), 1)
        sub = lax.broadcasted_iota(jnp.int32, (128, L), 0)
        lane1 = lax.broadcasted_iota(jnp.int32, (1, 128), 1)
        Rm = jnp.zeros((L, 128), F32)
        KIm = jnp.zeros((L, 128), F32)
        csm = jnp.zeros((128, L), F32)
        Xm = jnp.zeros((1, 128), F32)
        for h in range(A_HEADS):
            q = z_ref[:, h * A_QK:(h + 1) * A_QK] * scale
            k = z_ref[:, 512 + h * A_QK:512 + (h + 1) * A_QK]
            qb = q.astype(BF16)
            kb = k.astype(BF16)
            vb = z_ref[:, 1024 + h * A_V:1024 + (h + 1) * A_V].astype(BF16)
            a_col, b_col = li[:, h:h + 1], b[:, h:h + 1]
            a_row, b_row = liT[h:h + 1, :], bT[h:h + 1, :]
            Dm = jnp.exp(jnp.where(causal, b_col - b_row + a_row, -jnp.inf))
            Sqk = _dot_nt(qb, kb)
            A = Sqk * Dm
            Ab = A.astype(BF16)
            eb = jnp.exp(b_col)
            Cb = cst_ref[0, h]
            nh = nst_ref[0, h:h + 1, :]
            num = _dot(Ab, vb) + eb * _dot(qb, Cb)
            den = jnp.sum(A, axis=-1, keepdims=True) + eb * jnp.sum(q * nh, axis=-1, keepdims=True)
            aden = jnp.abs(den)
            u = 1.0 / jnp.maximum(aden, 1.0)
            dhh = dh_ref[:, h * A_V:(h + 1) * A_V]
            dnum = dhh * u
            dden = jnp.where(aden > 1.0, -jnp.sum(dhh * num, axis=-1, keepdims=True) * u * u * jnp.sign(den), 0.0)
            dnb = dnum.astype(BF16)
            G = Dm * (_dot_nt(dnb, vb) + dden)
            Gb = G.astype(BF16)
            E = G * Sqk
            rs = jnp.sum(E, axis=-1, keepdims=True)
            cs = jnp.sum(E, axis=0, keepdims=True)
            dCh = dC_s[h]
            dnh = dn_s[h:h + 1, :]
            dCb = dCh.astype(BF16)
            bL = b_col[L - 1:L, :]
            wk = jnp.exp(bL - b_col + a_col)
            decay = jnp.exp(bL)
            dq_inter = eb * (_dot_nt(dnb, Cb) + dden * nh)
            dk_inter = wk * (_dot_nt(vb, dCb) + dnh)
            dq = _dot(Gb, kb) + dq_inter
            dk = _dot_tn(Gb, qb) + dk_inter
            dv = _dot_tn(Ab, dnb) + wk * _dot(kb, dCb)
            dz_ref[:, h * A_QK:(h + 1) * A_QK] = (dq * scale).astype(BF16)
            dz_ref[:, 512 + h * A_QK:512 + (h + 1) * A_QK] = dk.astype(BF16)
            dz_ref[:, 1024 + h * A_V:1024 + (h + 1) * A_V] = dv.astype(BF16)
            KI = jnp.sum(k * dk_inter, axis=-1, keepdims=True)
            R = rs + jnp.sum(q * dq_inter, axis=-1, keepdims=True)
            cross = (jnp.sum(jnp.sum(dCh * Cb.astype(F32), axis=0, keepdims=True), axis=1, keepdims=True)
                     + jnp.sum(dnh * nh, axis=1, keepdims=True))
            Xm = jnp.where(lane1 == h, decay * cross, Xm)
            Rm = jnp.where(lane == h, R, Rm)
            KIm = jnp.where(lane == h, KI, KIm)
            csm = jnp.where(sub == h, cs, csm)
            ebq = eb * q
            dC_s[h] = decay * dCh + _dot_tn(ebq.astype(BF16), dnb)
            dn_s[h:h + 1, :] = decay * dnh + jnp.sum(ebq * dden, axis=0, keepdims=True)
        dz_ref[:, 2048:3072] = dzo_ref[...]
        cs_col = csm.T
        da = cs_col + KIm
        rr = lax.broadcasted_iota(jnp.int32, (L, L), 0)
        cc = lax.broadcasted_iota(jnp.int32, (L, L), 1)
        dlf = (_dot((rr <= cc).astype(F32), Rm - cs_col, precision=HIGHEST)
               + _dot((rr > cc).astype(F32), KIm, precision=HIGHEST) + Xm)
        dpre_i = da * (1.0 - (li * (1.0 / SOFTCAP)) ** 2)
        dpre_f = dlf * (1.0 - _sigmoid(scf)) * (1.0 - (scf * (1.0 / SOFTCAP)) ** 2)
        dz_ref[:, Z_GI:Z_GI + 128] = dpre_i.astype(BF16)
        dz_ref[:, Z_GF:Z_GF + 128] = dpre_f.astype(BF16)
        db_ref[0:1, :] += jnp.sum(dpre_i, axis=0, keepdims=True)
        db_ref[1:2, :] += jnp.sum(dpre_f, axis=0, keepdims=True)

    vec = pl.BlockSpec((1, 128), lambda c: (0, 0))
    rev = lambda c: (NC - 1 - c, 0)
    return pl.pallas_call(
        body, name=name, grid=(NC,),
        in_specs=[pl.BlockSpec((L, Z_W), rev), vec, vec,
                  pl.BlockSpec((1, A_HEADS, A_QK, A_V), lambda c: (NC - 1 - c, 0, 0, 0)),
                  pl.BlockSpec((1, 8, 128), lambda c: (NC - 1 - c, 0, 0)),
                  pl.BlockSpec((L, 1024), rev), pl.BlockSpec((L, 1024), rev)],
        out_specs=[pl.BlockSpec((L, Z_W), rev), pl.BlockSpec((8, 128), lambda c: (0, 0))],
        out_shape=[jax.ShapeDtypeStruct((S, Z_W), BF16), jax.ShapeDtypeStruct((8, 128), F32)],
        scratch_shapes=[pltpu.VMEM((A_HEADS, A_QK, A_V), F32), pltpu.VMEM((8, 128), F32)],
        compiler_params=_cp(("arbitrary",)),
    )(z, bi, bf, cst, nst, dh, dzo)


def ao_fwd(h, z, gh, w_out, x, *, name, tm=512):
    S = h.shape[0]

    def body(h_ref, o_ref, gh_ref, w_ref, x_ref, x1_ref, hg_ref):
        for hd in range(A_HEADS):
            sl = slice(hd * A_V, (hd + 1) * A_V)
            hs = h_ref[:, sl]
            hn = hs * lax.rsqrt(jnp.mean(hs * hs, axis=-1, keepdims=True) + EPS) * gh_ref[:, sl]
            hg_ref[:, sl] = (hn * _sigmoid(o_ref[:, sl])).astype(BF16)
        x1_ref[...] = x_ref[...] + _dot(hg_ref[...], w_ref[...])

    row = pl.BlockSpec((tm, 1024), lambda i: (i, 0))
    return pl.pallas_call(
        body, name=name, grid=(S // tm,),
        in_specs=[row, pl.BlockSpec((tm, 1024), lambda i: (i, 2)), pl.BlockSpec((1, 1024), lambda i: (0, 0)),
                  pl.BlockSpec((1024, 1024), lambda i: (0, 0)), row],
        out_specs=[row, row],
        out_shape=[jax.ShapeDtypeStruct((S, 1024), F32), jax.ShapeDtypeStruct((S, 1024), BF16)],
        compiler_params=_cp(("parallel",)),
    )(h, z, gh, w_out, x)


def ao_bwd(g1, w_out_t, h, z, gh, *, name, tm=512):
    S = h.shape[0]

    def body(g_ref, w_ref, h_ref, o_ref, gh_ref, dh_ref, dzo_ref, dgh_ref):
        @pl.when(pl.program_id(0) == 0)
        def _():
            dgh_ref[...] = jnp.zeros_like(dgh_ref)

        dhg = _dot(g_ref[...].astype(BF16), w_ref[...])
        for hd in range(A_HEADS):
            sl = slice(hd * A_V, (hd + 1) * A_V)
            hs = h_ref[:, sl]
            r = lax.rsqrt(jnp.mean(hs * hs, axis=-1, keepdims=True) + EPS)
            hhat = hs * r
            ghs = gh_ref[:, sl]
            sig = _sigmoid(o_ref[:, sl])
            d = dhg[:, sl]
            dhn = d * sig
            dzo_ref[:, sl] = (d * hhat * ghs * sig * (1.0 - sig)).astype(BF16)
            dgh_ref[:, sl] += jnp.sum(dhn * hhat, axis=0, keepdims=True)
            dhhat = dhn * ghs
            dh_ref[:, sl] = r * (dhhat - hhat * jnp.mean(dhhat * hhat, axis=-1, keepdims=True))

    row = pl.BlockSpec((tm, 1024), lambda i: (i, 0))
    vec = pl.BlockSpec((1, 1024), lambda i: (0, 0))
    return pl.pallas_call(
        body, name=name, grid=(S // tm,),
        in_specs=[row, pl.BlockSpec((1024, 1024), lambda i: (0, 0)), row,
                  pl.BlockSpec((tm, 1024), lambda i: (i, 2)), vec],
        out_specs=[row, row, vec],
        out_shape=[jax.ShapeDtypeStruct((S, 1024), F32), jax.ShapeDtypeStruct((S, 1024), BF16),
                   jax.ShapeDtypeStruct((1, 1024), F32)],
        compiler_params=_cp(("arbitrary",)),
    )(g1, w_out_t, h, z, gh)


CONV_TC = 1408
CONV_HALO = 16


def _causal_conv(u_ref, halo_ref, w_ref, b_ref, first):
    u = u_ref[...].astype(F32)
    T = u.shape[0]
    hl = jnp.where(first, 0.0, halo_ref[...].astype(F32))
    row = lax.broadcasted_iota(jnp.int32, u.shape, 0)
    u1 = jnp.where(row == 0, hl[CONV_HALO - 1:CONV_HALO], pltpu.roll(u, 1, axis=0))
    u2 = jnp.where(row == 0, hl[CONV_HALO - 2:CONV_HALO - 1],
                   jnp.where(row == 1, hl[CONV_HALO - 1:CONV_HALO], pltpu.roll(u, 2, axis=0)))
    w = w_ref[...]
    conv = u * w[2:3] + u1 * w[1:2] + u2 * w[0:1] + b_ref[...]
    return u, u1, u2, conv


def conv_act(u, cw, cb, *, name, tm=512):
    S = u.shape[0]
    hb = tm // CONV_HALO

    def body(ug_ref, uv_ref, hg_ref, hv_ref, wg_ref, wv_ref, bg_ref, bv_ref, a_ref):
        first = pl.program_id(1) == 0
        g = _causal_conv(ug_ref, hg_ref, wg_ref, bg_ref, first)[3]
        v = _causal_conv(uv_ref, hv_ref, wv_ref, bv_ref, first)[3]
        a_ref[...] = (g * _sigmoid(g) * v).astype(BF16)

    def blk(off):
        return pl.BlockSpec((tm, CONV_TC), lambda j, i: (i, j + off))

    def halo(off):
        return pl.BlockSpec((CONV_HALO, CONV_TC), lambda j, i: (jnp.maximum(i * hb - 1, 0), j + off))

    def wsp(rows, off):
        return pl.BlockSpec((rows, CONV_TC), lambda j, i: (0, j + off))

    return pl.pallas_call(
        body, name=name, grid=(2, S // tm),
        in_specs=[blk(0), blk(2), halo(0), halo(2), wsp(3, 0), wsp(3, 2), wsp(1, 0), wsp(1, 2)],
        out_specs=pl.BlockSpec((tm, CONV_TC), lambda j, i: (i, j)),
        out_shape=jax.ShapeDtypeStruct((S, D_FF), BF16),
        compiler_params=_cp(("parallel", "parallel")),
    )(u, u, u, u, cw, cw, cb, cb)


def conv_bwd1(da, u, cw, cb, *, name, tm=512):
    S = u.shape[0]
    hb = tm // CONV_HALO

    def body(da_ref, ug_ref, uv_ref, hg_ref, hv_ref, wg_ref, wv_ref, bg_ref, bv_ref, duc_ref, dwb_ref):
        j = pl.program_id(0)
        first = pl.program_id(1) == 0

        @pl.when(first)
        def _():
            dwb_ref[...] = jnp.zeros_like(dwb_ref)

        gu = _causal_conv(ug_ref, hg_ref, wg_ref, bg_ref, first)
        vu = _causal_conv(uv_ref, hv_ref, wv_ref, bv_ref, first)
        g, v = gu[3], vu[3]
        sg = _sigmoid(g)
        dav = da_ref[...]

        def emit(d, taps):
            db16 = d.astype(BF16)
            duc_ref[...] = db16
            dwb_ref[0:1, :] += jnp.sum(taps[2] * d, axis=0, keepdims=True)
            dwb_ref[1:2, :] += jnp.sum(taps[1] * d, axis=0, keepdims=True)
            dwb_ref[2:3, :] += jnp.sum(taps[0] * d, axis=0, keepdims=True)
            dwb_ref[3:4, :] += jnp.sum(d, axis=0, keepdims=True)

        @pl.when(j < 2)
        def _():
            emit(dav * v * (sg * (1.0 + g * (1.0 - sg))), gu)

        @pl.when(j >= 2)
        def _():
            emit(dav * (g * sg), vu)

    def blk(off):
        return pl.BlockSpec((tm, CONV_TC), lambda j, i: (i, j % 2 + off))

    def halo(off):
        return pl.BlockSpec((CONV_HALO, CONV_TC), lambda j, i: (jnp.maximum(i * hb - 1, 0), j % 2 + off))

    def wsp(rows, off):
        return pl.BlockSpec((rows, CONV_TC), lambda j, i: (0, j % 2 + off))

    return pl.pallas_call(
        body, name=name, grid=(4, S // tm),
        in_specs=[blk(0), blk(0), blk(2), halo(0), halo(2), wsp(3, 0), wsp(3, 2), wsp(1, 0), wsp(1, 2)],
        out_specs=[pl.BlockSpec((tm, CONV_TC), lambda j, i: (i, j)), pl.BlockSpec((8, CONV_TC), lambda j, i: (0, j))],
        out_shape=[jax.ShapeDtypeStruct((S, 2 * D_FF), BF16), jax.ShapeDtypeStruct((8, 2 * D_FF), F32)],
        compiler_params=_cp(("parallel", "arbitrary")),
    )(da, u, u, u, u, cw, cw, cb, cb)


def conv_bwd2(duc, cw, *, name, tm=512):
    S = duc.shape[0]
    hb = tm // CONV_HALO
    nblk = S // tm

    def body(d_ref, halo_ref, w_ref, du_ref):
        last = pl.program_id(1) == nblk - 1
        d = d_ref[...].astype(F32)
        hl = jnp.where(last, 0.0, halo_ref[...].astype(F32))
        row = lax.broadcasted_iota(jnp.int32, d.shape, 0)
        d1 = jnp.where(row == tm - 1, hl[0:1], pltpu.roll(d, tm - 1, axis=0))
        d2 = jnp.where(row == tm - 1, hl[1:2], jnp.where(row == tm - 2, hl[0:1], pltpu.roll(d, tm - 2, axis=0)))
        w = w_ref[...]
        du_ref[...] = (d * w[2:3] + d1 * w[1:2] + d2 * w[0:1]).astype(BF16)

    return pl.pallas_call(
        body, name=name, grid=(4, nblk),
        in_specs=[pl.BlockSpec((tm, CONV_TC), lambda j, i: (i, j)),
                  pl.BlockSpec((CONV_HALO, CONV_TC), lambda j, i: (jnp.minimum((i + 1) * hb, nblk * hb - 1), j)),
                  pl.BlockSpec((3, CONV_TC), lambda j, i: (0, j))],
        out_specs=pl.BlockSpec((tm, CONV_TC), lambda j, i: (i, j)),
        out_shape=jax.ShapeDtypeStruct((S, 2 * D_FF), BF16),
        compiler_params=_cp(("parallel", "parallel")),
    )(duc, duc, cw)


def _t5_bucket(dist):
    max_exact = REL_BUCKETS // 2
    d = np.maximum(dist, 0)
    log_ratio = np.log(np.maximum(d, 1) / max_exact) / math.log(REL_MAX_DIST / max_exact)
    large = np.minimum(max_exact + (log_ratio * (REL_BUCKETS - max_exact)).astype(np.int64), REL_BUCKETS - 1)
    return np.where(d < max_exact, d, large).astype(np.int32)


def _bucket_tables():
    delta = BLK + np.arange(BLK)[:, None] - np.arange(2 * BLK)[None, :]
    return np.stack([_t5_bucket(delta * dil) for _, dil in B_GROUPS]).astype(np.int32)


def bias_build(rel_bias, buckets, *, name):
    def body(rel_ref, bk_ref, o_ref):
        g = pl.program_id(0)
        bk = bk_ref[0]
        for h in range(B_HEADS):
            acc = jnp.zeros((BLK, 2 * BLK), F32)
            for bb in range(REL_BUCKETS):
                acc = jnp.where(bk == bb, rel_ref[bb, g * B_HEADS + h], acc)
            o_ref[0, h] = acc

    return pl.pallas_call(
        body, name=name, grid=(N_GROUPS,),
        in_specs=[pl.BlockSpec(memory_space=pltpu.SMEM), pl.BlockSpec((1, BLK, 2 * BLK), lambda g: (g, 0, 0))],
        out_specs=pl.BlockSpec((1, B_HEADS, BLK, 2 * BLK), lambda g: (g, 0, 0, 0)),
        out_shape=jax.ShapeDtypeStruct((N_GROUPS, B_HEADS, BLK, 2 * BLK), F32),
        compiler_params=_cp(("arbitrary",)),
    )(rel_bias, buckets)


def bias_grad(dbias, buckets, *, name):
    def body(db_ref, bk_ref, o_ref):
        g = pl.program_id(0)

        @pl.when(g == 0)
        def _():
            o_ref[...] = jnp.zeros_like(o_ref)

        bk = bk_ref[0]
        rr = lax.broadcasted_iota(jnp.int32, (REL_BUCKETS, 128), 0)
        cc = lax.broadcasted_iota(jnp.int32, (REL_BUCKETS, 128), 1)
        acc = jnp.zeros((REL_BUCKETS, 128), F32)
        for h in range(B_HEADS):
            dbh = db_ref[0, h]
            for bb in range(REL_BUCKETS):
                part = jnp.sum(jnp.where(bk == bb, dbh, 0.0), axis=0, keepdims=True)
                s = jnp.sum(part, axis=1, keepdims=True)
                acc = acc + jnp.where((rr == bb) & (cc == g * B_HEADS + h), s, 0.0)
        o_ref[...] += acc

    return pl.pallas_call(
        body, name=name, grid=(N_GROUPS,),
        in_specs=[pl.BlockSpec((1, B_HEADS, BLK, 2 * BLK), lambda g: (g, 0, 0, 0)),
                  pl.BlockSpec((1, BLK, 2 * BLK), lambda g: (g, 0, 0))],
        out_specs=pl.BlockSpec((REL_BUCKETS, 128), lambda g: (0, 0)),
        out_shape=jax.ShapeDtypeStruct((REL_BUCKETS, 128), F32),
        compiler_params=_cp(("arbitrary",)),
    )(dbias, buckets)


def _pair_masks(dtype):
    lane = lax.broadcasted_iota(jnp.int32, (BLK, 128), 1)
    return [(lane < B_DH).astype(dtype), (lane >= B_DH).astype(dtype)]


def _band_masks():
    iq = lax.broadcasted_iota(jnp.int32, (BLK, BLK), 0)
    ik = lax.broadcasted_iota(jnp.int32, (BLK, BLK), 1)
    return iq <= ik, iq >= ik


def attn_fwd(q, kv, bias, g, dil, *, name):
    S = q.shape[0]
    S2 = S // dil
    nb = S2 // BLK
    q2 = q.reshape(S2, dil * 3072)
    kv2 = kv.reshape(S2, dil * 6144)

    def body(q_ref, kc_ref, kp_ref, vc_ref, vp_ref, b_ref, o_ref, lse_ref):
        has_prev = pl.program_id(1) > 0
        vp_m, vc_m = _band_masks()
        vp_m = vp_m & has_prev
        mb = _pair_masks(BF16)
        mf = _pair_masks(F32)
        lane = lax.broadcasted_iota(jnp.int32, (BLK, 128), 1)
        lse_acc = jnp.zeros((BLK, 128), F32)
        for hp in range(B_HEADS // 2):
            sl = slice(hp * 128, (hp + 1) * 128)
            qp, kc, kp, vc, vp = q_ref[:, sl], kc_ref[:, sl], kp_ref[:, sl], vc_ref[:, sl], vp_ref[:, sl]
            o_pair = jnp.zeros((BLK, 128), F32)
            for e in range(2):
                h = 2 * hp + e
                qm = qp * mb[e]
                s_p = jnp.where(vp_m, _dot_nt(qm, kp) * (B_DH ** -0.5) + b_ref[0, h, :, 0:BLK], -jnp.inf)
                s_c = jnp.where(vc_m, _dot_nt(qm, kc) * (B_DH ** -0.5) + b_ref[0, h, :, BLK:2 * BLK], -jnp.inf)
                m = jnp.maximum(jnp.max(s_p, axis=-1, keepdims=True), jnp.max(s_c, axis=-1, keepdims=True))
                p_p = jnp.exp(s_p - m)
                p_c = jnp.exp(s_c - m)
                l = jnp.sum(p_p, axis=-1, keepdims=True) + jnp.sum(p_c, axis=-1, keepdims=True)
                o = (_dot(p_p.astype(BF16), vp) + _dot(p_c.astype(BF16), vc)) / l
                o_pair = o_pair + o * mf[e]
                lse_acc = jnp.where(lane == h, m + jnp.log(l), lse_acc)
            o_ref[:, sl] = o_pair
        lse_ref[...] = lse_acc

    def spec(width_blocks, off, prev):
        if prev:
            return pl.BlockSpec((BLK, 1024), lambda r, n: (jnp.maximum(n - 1, 0), r * width_blocks + off))
        return pl.BlockSpec((BLK, 1024), lambda r, n: (n, r * width_blocks + off))

    o, lse = pl.pallas_call(
        body, name=name, grid=(dil, nb),
        in_specs=[spec(3, g, False), spec(6, g, False), spec(6, g, True), spec(6, 3 + g, False), spec(6, 3 + g, True),
                  pl.BlockSpec((1, B_HEADS, BLK, 2 * BLK), lambda r, n: (g, 0, 0, 0))],
        out_specs=[pl.BlockSpec((BLK, 1024), lambda r, n: (n, r)), pl.BlockSpec((BLK, 128), lambda r, n: (n, r))],
        out_shape=[jax.ShapeDtypeStruct((S2, dil * 1024), F32), jax.ShapeDtypeStruct((S2, dil * 128), F32)],
        compiler_params=_cp(("parallel", "arbitrary")),
    )(q2, kv2, kv2, kv2, kv2, bias)
    return o.reshape(S, 1024), lse.reshape(S, 128)


def attn_merge(os_, lses, *, name, tm=512):
    S = os_[0].shape[0]
    expand = np.zeros((128, 1024), np.float32)
    for h in range(B_HEADS):
        expand[h, h * B_DH:(h + 1) * B_DH] = 1.0
    expand = jnp.asarray(expand)

    def body(o0, o1, o2, l0, l1, l2, e_ref, out_ref, lse_ref):
        ls = [l0[...], l1[...], l2[...]]
        m = jnp.maximum(jnp.maximum(ls[0], ls[1]), ls[2])
        ws = [jnp.exp(l - m) for l in ls]
        tot = ws[0] + ws[1] + ws[2]
        lse_ref[...] = m + jnp.log(tot)
        acc = jnp.zeros((tm, 1024), F32)
        for w, o in zip(ws, (o0, o1, o2)):
            acc = acc + _dot(w / tot, e_ref[...], precision=HIGHEST) * o[...]
        out_ref[...] = acc.astype(BF16)

    row = pl.BlockSpec((tm, 1024), lambda i: (i, 0))
    lrow = pl.BlockSpec((tm, 128), lambda i: (i, 0))
    return pl.pallas_call(
        body, name=name, grid=(S // tm,),
        in_specs=[row, row, row, lrow, lrow, lrow, pl.BlockSpec((128, 1024), lambda i: (0, 0))],
        out_specs=[row, lrow],
        out_shape=[jax.ShapeDtypeStruct((S, 1024), BF16), jax.ShapeDtypeStruct((S, 128), F32)],
        compiler_params=_cp(("parallel",)),
    )(*os_, *lses, expand)


def attn_dq(q, kv, bias, dout, out, lse, dq_buf, g, dil, *, name):
    S = q.shape[0]
    S2 = S // dil
    nb = S2 // BLK
    scale = B_DH ** -0.5

    def body(q_ref, kc_ref, kp_ref, vc_ref, vp_ref, b_ref, do_ref, out_ref, lse_ref, buf_ref, dq_ref, db_ref):
        del buf_ref

        @pl.when((pl.program_id(0) == 0) & (pl.program_id(1) == 0))
        def _():
            db_ref[...] = jnp.zeros_like(db_ref)

        has_prev = pl.program_id(1) > 0
        vp_m, vc_m = _band_masks()
        vp_m = vp_m & has_prev
        mb = _pair_masks(BF16)
        mf = _pair_masks(F32)
        lse_blk = lse_ref[...]
        for hp in range(B_HEADS // 2):
            sl = slice(hp * 128, (hp + 1) * 128)
            qp, kc, kp, vc, vp = q_ref[:, sl], kc_ref[:, sl], kp_ref[:, sl], vc_ref[:, sl], vp_ref[:, sl]
            dof = do_ref[:, sl]
            dob = dof.astype(BF16)
            dd = dof * out_ref[:, sl].astype(F32)
            dq_pair = jnp.zeros((BLK, 128), F32)
            for e in range(2):
                h = 2 * hp + e
                qm = qp * mb[e]
                dom = dob * mb[e]
                lse_h = lse_blk[:, h:h + 1]
                Dh = jnp.sum(dd * mf[e], axis=-1, keepdims=True)
                s_p = jnp.where(vp_m, _dot_nt(qm, kp) * scale + b_ref[0, h, :, 0:BLK] - lse_h, -jnp.inf)
                s_c = jnp.where(vc_m, _dot_nt(qm, kc) * scale + b_ref[0, h, :, BLK:2 * BLK] - lse_h, -jnp.inf)
                ds_p = jnp.exp(s_p) * (_dot_nt(dom, vp) - Dh)
                ds_c = jnp.exp(s_c) * (_dot_nt(dom, vc) - Dh)
                db_ref[h, :, 0:BLK] += ds_p
                db_ref[h, :, BLK:2 * BLK] += ds_c
                dq_pair = dq_pair + (_dot(ds_p.astype(BF16), kp) + _dot(ds_c.astype(BF16), kc)) * mf[e]
            dq_ref[:, sl] = (dq_pair * scale).astype(BF16)

    def spec(width_blocks, off, prev):
        if prev:
            return pl.BlockSpec((BLK, 1024), lambda r, n: (jnp.maximum(n - 1, 0), r * width_blocks + off))
        return pl.BlockSpec((BLK, 1024), lambda r, n: (n, r * width_blocks + off))

    dq2, dbias = pl.pallas_call(
        body, name=name, grid=(dil, nb),
        in_specs=[spec(3, g, False), spec(6, g, False), spec(6, g, True), spec(6, 3 + g, False), spec(6, 3 + g, True),
                  pl.BlockSpec((1, B_HEADS, BLK, 2 * BLK), lambda r, n: (g, 0, 0, 0)),
                  spec(1, 0, False), spec(1, 0, False), pl.BlockSpec((BLK, 128), lambda r, n: (n, r)),
                  pl.BlockSpec(memory_space=pl.ANY)],
        out_specs=[spec(3, g, False), pl.BlockSpec((B_HEADS, BLK, 2 * BLK), lambda r, n: (0, 0, 0))],
        out_shape=[jax.ShapeDtypeStruct((S2, dil * 3072), BF16), jax.ShapeDtypeStruct((B_HEADS, BLK, 2 * BLK), F32)],
        input_output_aliases={9: 0},
        compiler_params=_cp(("arbitrary", "arbitrary")),
    )(q.reshape(S2, dil * 3072), *([kv.reshape(S2, dil * 6144)] * 4), bias, dout.reshape(S2, dil * 1024),
      out.reshape(S2, dil * 1024), lse.reshape(S2, dil * 128), dq_buf.reshape(S2, dil * 3072))
    return dq2.reshape(S, 3072), dbias


def attn_dkv(q, kv, bias, dout, out, lse, dk_buf, dv_buf, g, dil, *, name):
    S = q.shape[0]
    S2 = S // dil
    nb = S2 // BLK
    scale = B_DH ** -0.5

    def body(k_ref, v_ref, b_ref, qa_ref, qb_ref, doa_ref, dob_ref, oa_ref, ob_ref, la_ref, lb_ref,
             kbuf_ref, vbuf_ref, dk_ref, dv_ref):
        del kbuf_ref, vbuf_ref
        has_next = pl.program_id(1) < nb - 1
        vb_m, va_m = _band_masks()
        vb_m = vb_m & has_next
        mb = _pair_masks(BF16)
        mf = _pair_masks(F32)
        la, lb = la_ref[...], lb_ref[...]
        for hp in range(B_HEADS // 2):
            sl = slice(hp * 128, (hp + 1) * 128)
            kk, vv = k_ref[:, sl], v_ref[:, sl]
            tiles = []
            for q_ref, do_ref, o_ref, lse_blk, valid, lo in ((qa_ref, doa_ref, oa_ref, la, va_m, BLK),
                                                             (qb_ref, dob_ref, ob_ref, lb, vb_m, 0)):
                dof = do_ref[:, sl]
                tiles.append((q_ref[:, sl], dof.astype(BF16), dof * o_ref[:, sl].astype(F32), lse_blk, valid, lo))
            dk_pair = jnp.zeros((BLK, 128), F32)
            dv_pair = jnp.zeros((BLK, 128), F32)
            for e in range(2):
                h = 2 * hp + e
                for qp, dob, dd, lse_blk, valid, lo in tiles:
                    qm = qp * mb[e]
                    dom = dob * mb[e]
                    Dh = jnp.sum(dd * mf[e], axis=-1, keepdims=True)
                    s = jnp.where(valid, _dot_nt(qm, kk) * scale + b_ref[0, h, :, lo:lo + BLK] - lse_blk[:, h:h + 1],
                                  -jnp.inf)
                    p = jnp.exp(s)
                    ds = p * (_dot_nt(dom, vv) - Dh)
                    dv_pair = dv_pair + _dot_tn(p.astype(BF16), dom)
                    dk_pair = dk_pair + _dot_tn(ds.astype(BF16), qm)
            dk_ref[:, sl] = (dk_pair * scale).astype(BF16)
            dv_ref[:, sl] = dv_pair.astype(BF16)

    def cur(width_blocks, off):
        return pl.BlockSpec((BLK, 1024), lambda r, n: (n, r * width_blocks + off))

    def nxt(width_blocks, off):
        return pl.BlockSpec((BLK, 1024), lambda r, n: (jnp.minimum(n + 1, nb - 1), r * width_blocks + off))

    q2 = q.reshape(S2, dil * 3072)
    kv2 = kv.reshape(S2, dil * 6144)
    do2 = dout.reshape(S2, dil * 1024)
    o2 = out.reshape(S2, dil * 1024)
    l2 = lse.reshape(S2, dil * 128)
    any_spec = pl.BlockSpec(memory_space=pl.ANY)
    dk2, dv2 = pl.pallas_call(
        body, name=name, grid=(dil, nb),
        in_specs=[cur(6, g), cur(6, 3 + g), pl.BlockSpec((1, B_HEADS, BLK, 2 * BLK), lambda r, n: (g, 0, 0, 0)),
                  cur(3, g), nxt(3, g), cur(1, 0), nxt(1, 0), cur(1, 0), nxt(1, 0),
                  pl.BlockSpec((BLK, 128), lambda r, n: (n, r)),
                  pl.BlockSpec((BLK, 128), lambda r, n: (jnp.minimum(n + 1, nb - 1), r)),
                  any_spec, any_spec],
        out_specs=[cur(3, g), cur(3, g)],
        out_shape=[jax.ShapeDtypeStruct((S2, dil * 3072), BF16)] * 2,
        input_output_aliases={11: 0, 12: 1},
        compiler_params=_cp(("parallel", "parallel")),
    )(kv2, kv2, bias, q2, q2, do2, do2, o2, o2, l2, l2, dk_buf.reshape(S2, dil * 3072), dv_buf.reshape(S2, dil * 3072))
    return dk2.reshape(S, 3072), dv2.reshape(S, 3072)


def _slot(px, py, pc):
    return 4 * px + 2 * py + pc


def ag_weights(wb, ws):
    def body(wb_ref, ws_ref, ob_ref, os_ref, send_sems, recv_sems, local_sems):
        x, y, c = lax.axis_index("x"), lax.axis_index("y"), lax.axis_index("c")
        me, sibling = (x, y, c), (x, y, 1 - c)
        chips = [(1 - x, y), (x, 1 - y), (1 - x, 1 - y)]
        arrays = [(wb_ref, ob_ref), (ws_ref, os_ref)]

        def copy(a, k, block, to, from_input=False):
            src_in, out = arrays[a]
            dst = out.at[_slot(*block)]
            return pltpu.make_async_remote_copy(
                src_ref=src_in if from_input else dst, dst_ref=dst,
                send_sem=send_sems.at[7 * a + k], recv_sem=recv_sems.at[7 * a + k],
                device_id=to, device_id_type=MESH)

        mine = [pltpu.make_async_copy(arrays[a][0], arrays[a][1].at[_slot(*me)], local_sems.at[a]) for a in range(2)]
        for cp in mine:
            cp.start()
        first = []
        for a in range(2):
            first.append(copy(a, 0, me, sibling, True))
            first += [copy(a, 1 + j, me, (*chip, c), True) for j, chip in enumerate(chips)]
        for cp in first:
            cp.start()
        passed = []
        for a in range(2):
            for j, chip in enumerate(chips):
                copy(a, 1 + j, (*chip, c), me).wait_recv()
                fw = copy(a, 4 + j, (*chip, c), sibling)
                fw.start()
                passed.append(fw)
        for a in range(2):
            copy(a, 0, sibling, me).wait_recv()
            for j, chip in enumerate(chips):
                copy(a, 4 + j, (*chip, 1 - c), me).wait_recv()
        for cp in first + passed:
            cp.wait_send()
        for cp in mine:
            cp.wait()

    any_spec = pl.BlockSpec(memory_space=pl.ANY)
    return pl.pallas_call(
        body, name="ag_weights",
        in_specs=[any_spec, any_spec], out_specs=[any_spec, any_spec],
        out_shape=[jax.ShapeDtypeStruct((N_DEV,) + wb.shape, wb.dtype), jax.ShapeDtypeStruct((N_DEV,) + ws.shape, ws.dtype)],
        scratch_shapes=[pltpu.SemaphoreType.DMA((14,)), pltpu.SemaphoreType.DMA((14,)), pltpu.SemaphoreType.DMA((2,))],
    )(wb, ws)


def rs_exchange(gpack, spack):
    def body(g_ref, s_ref, rb_ref, sa_ref, send_sems, recv_sems, local_sems):
        x, y, c = lax.axis_index("x"), lax.axis_index("y"), lax.axis_index("c")
        my = _slot(x, y, c)
        mine = [pltpu.make_async_copy(g_ref.at[my], rb_ref.at[my], local_sems.at[0]),
                pltpu.make_async_copy(s_ref, sa_ref.at[my], local_sems.at[1])]
        for cp in mine:
            cp.start()
        sends, recvs = [], []
        for k in range(1, N_DEV):
            peer = (1 - x if k & 4 else x, 1 - y if k & 2 else y, 1 - c if k & 1 else c)
            pid = _slot(*peer)
            sends.append(pltpu.make_async_remote_copy(
                src_ref=g_ref.at[pid], dst_ref=rb_ref.at[my], send_sem=send_sems.at[k - 1], recv_sem=recv_sems.at[k - 1],
                device_id=peer, device_id_type=MESH))
            sends.append(pltpu.make_async_remote_copy(
                src_ref=s_ref, dst_ref=sa_ref.at[my], send_sem=send_sems.at[6 + k], recv_sem=recv_sems.at[6 + k],
                device_id=peer, device_id_type=MESH))
            recvs.append(pltpu.make_async_remote_copy(
                src_ref=g_ref.at[pid], dst_ref=rb_ref.at[pid], send_sem=send_sems.at[k - 1], recv_sem=recv_sems.at[k - 1],
                device_id=peer, device_id_type=MESH))
            recvs.append(pltpu.make_async_remote_copy(
                src_ref=s_ref, dst_ref=sa_ref.at[pid], send_sem=send_sems.at[6 + k], recv_sem=recv_sems.at[6 + k],
                device_id=peer, device_id_type=MESH))
        for cp in sends:
            cp.start()
        for cp in recvs:
            cp.wait_recv()
        for cp in sends:
            cp.wait_send()
        for cp in mine:
            cp.wait()

    any_spec = pl.BlockSpec(memory_space=pl.ANY)
    return pl.pallas_call(
        body, name="rs_exchange",
        in_specs=[any_spec, any_spec], out_specs=[any_spec, any_spec],
        out_shape=[jax.ShapeDtypeStruct(gpack.shape, gpack.dtype), jax.ShapeDtypeStruct((N_DEV,) + spack.shape, spack.dtype)],
        scratch_shapes=[pltpu.SemaphoreType.DMA((14,)), pltpu.SemaphoreType.DMA((14,)), pltpu.SemaphoreType.DMA((2,))],
    )(gpack, spack)


def reduce_adam(parts, w, m, v, *, name, tr):
    R = w.shape[0]
    assert R % tr == 0
    c1 = 1.0 - ADAM_B1 ** ADAM_STEP
    c2 = 1.0 - ADAM_B2 ** ADAM_STEP

    def body(p_ref, w_ref, m_ref, v_ref, g_ref, d_ref, mo_ref, vo_ref):
        g = p_ref[0]
        for i in range(1, N_DEV):
            g = g + p_ref[i]
        mn = ADAM_B1 * m_ref[...] + (1.0 - ADAM_B1) * g
        vn = ADAM_B2 * v_ref[...] + (1.0 - ADAM_B2) * (g * g)
        g_ref[...] = g
        mo_ref[...] = mn
        vo_ref[...] = vn
        d_ref[...] = -ADAM_LR * ((mn / c1) / (jnp.sqrt(vn / c2) + ADAM_EPS) + ADAM_WD * w_ref[...])

    row = pl.BlockSpec((tr, 1024), lambda i: (i, 0))
    return pl.pallas_call(
        body, name=name, grid=(R // tr,),
        in_specs=[pl.BlockSpec((N_DEV, tr, 1024), lambda i: (0, i, 0)), row, row, row],
        out_specs=[row] * 4,
        out_shape=[jax.ShapeDtypeStruct((R, 1024), F32)] * 4,
        compiler_params=_cp(("parallel",)),
    )(parts, w, m, v)


BIG = (("a_w_in", 385, 400), ("a_w_out", 128, 128), ("w_kv", 768, 768), ("b_w_q", 384, 384),
       ("b_w_out", 128, 128), ("f_w_up", 1408, 1408), ("f_w_down", 704, 704))
SMALL_SHARDED = (("a_norm_g", 128), ("a_hnorm_g", 128), ("f_conv_w", 4224))
SMALL_ROWS = 48
PACK_ROWS = sum(b[2] for b in BIG) + SMALL_ROWS
REPL = (("kv_norm_g", 1024, 1), ("b_norm_g", 1024, 1), ("f_norm_g", 2048, 2), ("f_conv_b", 11264, 11),
        ("final_norm_g", 1024, 1), ("rel_bias", 1536, 2), ("a_b_if", 8, 1))
REPL_ROWS = 24
LOSS_ROW = 19


def _rows(a, rows, padded):
    a = a.reshape(rows, 1024)
    return a if padded == rows else jnp.pad(a, ((0, padded - rows), (0, 0)))


def pack_shards(t, dtype, with_small):
    parts = [_rows(t[n].astype(dtype), r, p) for n, r, p in BIG]
    if with_small:
        flat = jnp.concatenate([t[n].astype(dtype).reshape(-1) for n, _ in SMALL_SHARDED])
        parts.append(jnp.pad(flat, (0, SMALL_ROWS * 1024 - flat.shape[0])).reshape(SMALL_ROWS, 1024))
    return jnp.concatenate(parts, axis=0)


def unpack_shards(pack, shapes):
    out = {}
    r0 = 0
    for n, r, p in BIG:
        out[n] = pack[r0:r0 + r].reshape(shapes[n])
        r0 += p
    flat = pack[r0:r0 + SMALL_ROWS].reshape(-1)
    e0 = 0
    for n, e in SMALL_SHARDED:
        out[n] = flat[e0:e0 + e].reshape(shapes[n])
        e0 += e
    return out


def pack_repl(t):
    parts = []
    for n, e, r in REPL:
        parts.append(jnp.pad(t[n].astype(F32).reshape(-1), (0, r * 1024 - e)))
    rows = sum(r for _, _, r in REPL)
    parts.append(jnp.zeros(((REPL_ROWS - rows) * 1024,), F32))
    return jnp.concatenate(parts).reshape(REPL_ROWS, 1024)


def unpack_repl(pack, shapes):
    out = {}
    r0 = 0
    for n, e, r in REPL:
        out[n] = pack[r0:r0 + r].reshape(-1)[:e].reshape(shapes[n])
        r0 += r
    return out


def split_cols(full, n):
    lead = full.shape[:-1]
    return jnp.moveaxis(full.reshape(lead + (N_DEV, n)), -2, 0)


def join_cols(parts):
    t = jnp.moveaxis(parts, 0, -2)
    return t.reshape(t.shape[:-2] + (t.shape[-2] * t.shape[-1],))


def kernel(x, a_norm_g, a_w_in, a_b_if, a_hnorm_g, a_w_out, kv_norm_g, w_kv, b_norm_g, b_w_q, b_w_out, rel_bias, f_norm_g, f_w_up, f_conv_w, f_conv_b, f_w_down, final_norm_g, loss_target, m_a_norm_g, m_a_w_in, m_a_b_if, m_a_hnorm_g, m_a_w_out, m_kv_norm_g, m_w_kv, m_b_norm_g, m_b_w_q, m_b_w_out, m_rel_bias, m_f_norm_g, m_f_w_up, m_f_conv_w, m_f_conv_b, m_f_w_down, m_final_norm_g, v_a_norm_g, v_a_w_in, v_a_b_if, v_a_hnorm_g, v_a_w_out, v_kv_norm_g, v_w_kv, v_b_norm_g, v_b_w_q, v_b_w_out, v_rel_bias, v_f_norm_g, v_f_w_up, v_f_conv_w, v_f_conv_b, v_f_w_down, v_final_norm_g):
    names = ["a_norm_g", "a_w_in", "a_b_if", "a_hnorm_g", "a_w_out", "kv_norm_g", "w_kv", "b_norm_g", "b_w_q", "b_w_out",
             "rel_bias", "f_norm_g", "f_w_up", "f_conv_w", "f_conv_b", "f_w_down", "final_norm_g"]
    w = dict(zip(names, (a_norm_g, a_w_in, a_b_if, a_hnorm_g, a_w_out, kv_norm_g, w_kv, b_norm_g, b_w_q, b_w_out,
                         rel_bias, f_norm_g, f_w_up, f_conv_w, f_conv_b, f_w_down, final_norm_g)))
    mom = dict(zip(names, (m_a_norm_g, m_a_w_in, m_a_b_if, m_a_hnorm_g, m_a_w_out, m_kv_norm_g, m_w_kv, m_b_norm_g, m_b_w_q,
                           m_b_w_out, m_rel_bias, m_f_norm_g, m_f_w_up, m_f_conv_w, m_f_conv_b, m_f_w_down, m_final_norm_g)))
    vel = dict(zip(names, (v_a_norm_g, v_a_w_in, v_a_b_if, v_a_hnorm_g, v_a_w_out, v_kv_norm_g, v_w_kv, v_b_norm_g, v_b_w_q,
                           v_b_w_out, v_rel_bias, v_f_norm_g, v_f_w_up, v_f_conv_w, v_f_conv_b, v_f_w_down, v_final_norm_g)))
    shapes = {n: w[n].shape for n in names}
    S = x.shape[1]
    assert x.shape[0] == 1 and S % (16 * BLK) == 0 and S % 1024 == 0
    X0 = x.reshape(S, D)
    target = loss_target.reshape(S, D)

    wb_all, ws_all = ag_weights(pack_shards(w, BF16, False),
                                pack_shards(w, F32, True)[PACK_ROWS - SMALL_ROWS:])
    seg = {}
    r0 = 0
    for n, r, p in BIG:
        seg[n] = wb_all[:, r0:r0 + r]
        r0 += p
    W_in = join_cols(seg["a_w_in"].reshape(N_DEV, D, 385))
    W_in = jnp.concatenate([jnp.pad(W_in[:, :3076], ((0, 0), (0, 124))),
                            jnp.pad(W_in[:, 3076:3080], ((0, 0), (0, 124)))], axis=1)
    W_out = seg["a_w_out"].reshape(1024, D)
    W_kv = join_cols(seg["w_kv"].reshape(N_DEV, D, 768))
    W_q = join_cols(seg["b_w_q"].reshape(N_DEV, D, 384))
    W_bout = seg["b_w_out"].reshape(1024, D)
    W_up = join_cols(seg["f_w_up"].reshape(N_DEV, 2, D, 704))
    W_down = jnp.moveaxis(seg["f_w_down"].reshape(N_DEV, 2, 352, D), 0, 1).reshape(2, D_FF, D)
    sflat = ws_all.reshape(N_DEV, SMALL_ROWS * 1024)
    g_a = sflat[:, 0:128].reshape(1, D)
    g_h = jnp.moveaxis(sflat[:, 128:256].reshape(N_DEV, A_HEADS, 32), 0, 1).reshape(1, A_HEADS * A_V)
    conv_w = join_cols(sflat[:, 256:256 + 4224].reshape(N_DEV, 2, 3, 704))
    bi = jnp.pad(a_b_if[:, :A_HEADS], ((0, 0), (0, 128 - A_HEADS)))
    bfg = jnp.pad(a_b_if[:, A_HEADS:], ((0, 0), (0, 128 - A_HEADS)))
    buckets = jnp.asarray(_bucket_tables())

    (xn_a,) = rms_fwd(X0, [g_a], name="rms_a")
    z = mm(xn_a, W_in, name="mm_a_in", out_dtype=F32, tn=1664)
    h, cst, nst = mlstm_fwd(z, bi, bfg, name="mlstm_fwd")
    X1, hg = ao_fwd(h, z, g_h, W_out, X0, name="ao_fwd")

    def ffn_fwd(X, l, tag):
        (xn,) = rms_fwd(X, [f_norm_g[l:l + 1]], name="rms_f" + tag)
        u = mm(xn, W_up[l], name="mm_up" + tag, out_dtype=BF16, tn=1408)
        a = conv_act(u, conv_w[l], f_conv_b[l:l + 1], name="conv_act" + tag)
        Xn = mm(a, W_down[l], name="mm_down" + tag, out_dtype=F32, tn=1024, res=X)
        return Xn, (xn, u, a)

    X2, sav0 = ffn_fwd(X1, 0, "0")

    xkn, xbn = rms_fwd(X2, [kv_norm_g.reshape(1, D), b_norm_g], name="rms_kv_b")
    kv = mm(xkn, W_kv, name="mm_kv", out_dtype=BF16, tn=1536)
    q = mm(xbn, W_q, name="mm_q", out_dtype=BF16, tn=1536)
    bias = bias_build(rel_bias, buckets, name="bias_build")
    og, lg = [], []
    for g, (_, dil) in enumerate(B_GROUPS):
        o_, l_ = attn_fwd(q, kv, bias, g, dil, name="attn_fwd%d" % g)
        og.append(o_)
        lg.append(l_)
    att, lse = attn_merge(og, lg, name="attn_merge")
    X3 = mm(att, W_bout, name="mm_b_out", out_dtype=F32, tn=1024, res=X2)
    X4, sav1 = ffn_fwd(X3, 1, "1")

    dX4, d_final_g, loss_part = loss_head(X4, target, final_norm_g.reshape(1, D), name="loss_head")

    def ffn_bwd(X, dXn, l, sav, tag):
        xn, u, a = sav
        dW_down = mm_tn(a, dXn, name="tn_down" + tag, tn=1024)
        da = mm(dXn, W_down[l].T, name="mm_da" + tag, out_dtype=F32, tn=1408)
        duc, dwb = conv_bwd1(da, u, conv_w[l], f_conv_b[l:l + 1], name="conv_bwd1" + tag)
        du = conv_bwd2(duc, conv_w[l], name="conv_bwd2" + tag)
        dW_up = mm_tn(xn, du, name="tn_up" + tag, tn=1408)
        dxn = mm(du, W_up[l].T, name="mm_dxn_f" + tag, out_dtype=F32, tn=1024)
        dX, (dg,) = rms_bwd(X, dXn, [(dxn, f_norm_g[l:l + 1])], name="rms_bwd_f" + tag)
        return dX, dW_down, dW_up, dwb, dg

    dX3, dWd1, dWu1, dwb1, dgf1 = ffn_bwd(X3, dX4, 1, sav1, "1")

    dW_bout = mm_tn(att, dX3, name="tn_b_out", tn=1024)
    dout = mm(dX3, W_bout.T, name="mm_dout", out_dtype=F32, tn=1024)
    dq = jnp.zeros((S, 3072), BF16)
    dk = jnp.zeros((S, 3072), BF16)
    dv = jnp.zeros((S, 3072), BF16)
    dbias = []
    for g, (_, dil) in enumerate(B_GROUPS):
        dq, db_ = attn_dq(q, kv, bias, dout, att, lse, dq, g, dil, name="attn_dq%d" % g)
        dk, dv = attn_dkv(q, kv, bias, dout, att, lse, dk, dv, g, dil, name="attn_dkv%d" % g)
        dbias.append(db_)
    d_rel = bias_grad(jnp.stack(dbias), buckets, name="bias_grad")[:, :N_GROUPS * B_HEADS]
    dW_q = mm_tn(xbn, dq, name="tn_q", tn=1536)
    dW_kv = jnp.concatenate([mm_tn(xkn, dk, name="tn_k", tn=1536), mm_tn(xkn, dv, name="tn_v", tn=1536)], axis=1)
    dxn_b = mm(dq, W_q.T, name="mm_dxn_b", out_dtype=F32, tn=1024)
    W_kvT = W_kv.T
    dxn_kv = mm(dk, W_kvT[:3072], name="mm_dxn_k", out_dtype=F32, tn=1024)
    dxn_kv = mm(dv, W_kvT[3072:], name="mm_dxn_v", out_dtype=F32, tn=1024, res=dxn_kv)
    dX2, (dg_kv, dg_b) = rms_bwd(X2, dX3, [(dxn_kv, kv_norm_g.reshape(1, D)), (dxn_b, b_norm_g)], name="rms_bwd_kv_b")

    dX1, dWd0, dWu0, dwb0, dgf0 = ffn_bwd(X1, dX2, 0, sav0, "0")

    dW_out = mm_tn(hg, dX1, name="tn_a_out", tn=1024)
    dh, dzo, dgh = ao_bwd(dX1, W_out.T, h, z, g_h, name="ao_bwd")
    dz, db_if = mlstm_bwd(z, bi, bfg, cst, nst, dh, dzo, name="mlstm_bwd")
    dW_in = mm_tn(xn_a, dz, name="tn_a_in", tn=1664)
    dW_in = jnp.concatenate([dW_in[:, :3076], dW_in[:, Z_GF:Z_GF + 4]], axis=1)
    dxn_a = mm(dz, W_in.T, name="mm_dxn_a", out_dtype=F32, tn=1024)
    dX0, (dg_a,) = rms_bwd(X0, dX1, [(dxn_a, g_a)], name="rms_bwd_a")

    dWu = jnp.stack([dWu0, dWu1])
    dWd = jnp.stack([dWd0, dWd1])
    dwb = jnp.stack([dwb0, dwb1])
    slots = [
        jnp.pad(split_cols(dW_in, 385).reshape(N_DEV, 385, 1024), ((0, 0), (0, 15), (0, 0))),
        dW_out.reshape(N_DEV, 128, 1024),
        split_cols(dW_kv, 768).reshape(N_DEV, 768, 1024),
        split_cols(dW_q, 384).reshape(N_DEV, 384, 1024),
        dW_bout.reshape(N_DEV, 128, 1024),
        split_cols(dWu, 704).reshape(N_DEV, 1408, 1024),
        jnp.moveaxis(dWd.reshape(2, N_DEV, 352, D), 1, 0).reshape(N_DEV, 704, 1024),
    ]
    small = jnp.concatenate([
        dg_a.reshape(N_DEV, 128),
        split_cols(dgh.reshape(A_HEADS, A_V), 32).reshape(N_DEV, 128),
        split_cols(dwb[:, 0:3], 704).reshape(N_DEV, 4224)], axis=1)
    slots.append(jnp.pad(small, ((0, 0), (0, SMALL_ROWS * 1024 - small.shape[1]))).reshape(N_DEV, SMALL_ROWS, 1024))
    gpack = jnp.concatenate(slots, axis=1)
    repl_g = {"kv_norm_g": dg_kv, "b_norm_g": dg_b, "f_norm_g": jnp.concatenate([dgf0, dgf1]),
              "f_conv_b": dwb[:, 3], "final_norm_g": d_final_g, "rel_bias": d_rel,
              "a_b_if": jnp.concatenate([db_if[0, :A_HEADS], db_if[1, :A_HEADS]])}
    spack = pack_repl(repl_g)
    spack = spack.at[LOSS_ROW, 0].set(loss_part[0, 0])

    parts, sparts = rs_exchange(gpack, spack)
    gb, db, mb, vb = reduce_adam(parts, pack_shards(w, F32, True), pack_shards(mom, F32, True),
                                 pack_shards(vel, F32, True), name="reduce_adam_big", tr=PACK_ROWS // 16)
    gs, ds, ms, vs = reduce_adam(sparts, pack_repl(w), pack_repl(mom), pack_repl(vel), name="reduce_adam_small", tr=REPL_ROWS)
    loss = gs[LOSS_ROW, 0]

    def collect(big, sm):
        t = unpack_shards(big, shapes)
        t.update(unpack_repl(sm, shapes))
        return [t[n] for n in names]

    return (loss, dX0.reshape(1, S, D), *collect(gb, gs), *collect(db, ds), *collect(mb, ms), *collect(vb, vs))
```

```python
import functools
import math

import numpy as np
import jax
import jax.numpy as jnp
from jax import lax
from jax.experimental import pallas as pl
from jax.experimental.pallas import tpu as pltpu

F32 = jnp.float32
BF16 = jnp.bfloat16
HIGHEST = lax.Precision.HIGHEST
MESH = pl.DeviceIdType.MESH

D = 1024
A_HEADS = 4
A_QK = 128
A_V = 256
SOFTCAP = 15.0
N_GROUPS = 3
B_GROUPS = ((128, 1), (512, 4), (2048, 16))
B_HEADS = 16
B_DH = 64
BLK = 128
REL_BUCKETS = 32
REL_MAX_DIST = 2048
D_FF = 2816
EPS = 1e-6
ADAM_LR = 0.001
ADAM_B1 = 0.9
ADAM_B2 = 0.999
ADAM_EPS = 1e-08
ADAM_WD = 0.01
ADAM_STEP = 10

N_DEV = 8
V7X_VMEM_BYTES = 64 * 1024 * 1024
VMEM_LIMIT = V7X_VMEM_BYTES - 8 * 1024 * 1024
MLSTM_CHUNK = 256
Z_W = 3328
Z_GI = 3072
Z_GF = 3200


def _cp(sem):
    return pltpu.CompilerParams(dimension_semantics=sem, vmem_limit_bytes=VMEM_LIMIT)


def _dot(a, b, **kw):
    return jnp.dot(a, b, preferred_element_type=F32, **kw)


def _dot_nt(a, b):
    return lax.dot_general(a, b, (((1,), (1,)), ((), ())), preferred_element_type=F32)


def _dot_tn(a, b):
    return lax.dot_general(a, b, (((0,), (0,)), ((), ())), preferred_element_type=F32)


def mm(a, b, *, name, out_dtype, tn, tm=512, res=None):
    M, K = a.shape
    N = b.shape[1]
    assert M % tm == 0 and N % tn == 0 and b.shape[0] == K

    def body(a_ref, b_ref, *rest):
        o_ref = rest[-1]
        acc = _dot(a_ref[...].astype(BF16), b_ref[...])
        if res is not None:
            acc = acc + rest[0][...]
        o_ref[...] = acc.astype(out_dtype)

    in_specs = [pl.BlockSpec((tm, K), lambda j, i: (i, 0)), pl.BlockSpec((K, tn), lambda j, i: (0, j))]
    args = [a, b]
    if res is not None:
        in_specs.append(pl.BlockSpec((tm, tn), lambda j, i: (i, j)))
        args.append(res)
    return pl.pallas_call(
        body, name=name, grid=(N // tn, M // tm), in_specs=in_specs,
        out_specs=pl.BlockSpec((tm, tn), lambda j, i: (i, j)),
        out_shape=jax.ShapeDtypeStruct((M, N), out_dtype),
        compiler_params=_cp(("parallel", "parallel")),
    )(*args)


def mm_sum(a_list, b_list, *, name, tm=512):
    M, K = a_list[0].shape
    N = b_list[0].shape[1]
    n = len(a_list)
    assert M % tm == 0

    def body(*refs):
        o_ref = refs[-1]
        acc = _dot(refs[0][...], refs[n][...])
        for i in range(1, n):
            acc = acc + _dot(refs[i][...], refs[n + i][...])
        o_ref[...] = acc

    return pl.pallas_call(
        body, name=name, grid=(M // tm,),
        in_specs=[pl.BlockSpec((tm, K), lambda i: (i, 0))] * n + [pl.BlockSpec((K, N), lambda i: (0, 0))] * n,
        out_specs=pl.BlockSpec((tm, N), lambda i: (i, 0)),
        out_shape=jax.ShapeDtypeStruct((M, N), F32),
        compiler_params=_cp(("parallel",)),
    )(*a_list, *b_list)


def mm_tn(a, b, *, name, tn, tk=512):
    S, Kd = a.shape
    N = b.shape[1]
    assert S % tk == 0 and N % tn == 0 and b.shape[0] == S

    def body(a_ref, b_ref, o_ref):
        @pl.when(pl.program_id(1) == 0)
        def _():
            o_ref[...] = jnp.zeros_like(o_ref)

        o_ref[...] += _dot_tn(a_ref[...].astype(BF16), b_ref[...].astype(BF16))

    return pl.pallas_call(
        body, name=name, grid=(N // tn, S // tk),
        in_specs=[pl.BlockSpec((tk, Kd), lambda j, k: (k, 0)), pl.BlockSpec((tk, tn), lambda j, k: (k, j))],
        out_specs=pl.BlockSpec((Kd, tn), lambda j, k: (0, j)),
        out_shape=jax.ShapeDtypeStruct((Kd, N), F32),
        compiler_params=_cp(("parallel", "arbitrary")),
    )(a, b)


def rms_fwd(x, gains, *, name, tm=1024):
    S = x.shape[0]
    n = len(gains)

    def body(x_ref, *rest):
        xf = x_ref[...]
        y = xf * lax.rsqrt(jnp.mean(xf * xf, axis=-1, keepdims=True) + EPS)
        for i in range(n):
            rest[n + i][...] = (y * rest[i][...]).astype(BF16)

    return pl.pallas_call(
        body, name=name, grid=(S // tm,),
        in_specs=[pl.BlockSpec((tm, D), lambda i: (i, 0))] + [pl.BlockSpec((1, D), lambda i: (0, 0))] * n,
        out_specs=[pl.BlockSpec((tm, D), lambda i: (i, 0))] * n,
        out_shape=[jax.ShapeDtypeStruct((S, D), BF16)] * n,
        compiler_params=_cp(("parallel",)),
    )(x, *gains)


def rms_bwd(x, dres, branches, *, name, tm=512):
    S = x.shape[0]
    n = len(branches)

    def body(x_ref, dres_ref, *rest):
        dxn_refs = rest[:n]
        g_refs = rest[n:2 * n]
        dx_ref = rest[2 * n]
        dg_refs = rest[2 * n + 1:]

        @pl.when(pl.program_id(0) == 0)
        def _():
            for r in dg_refs:
                r[...] = jnp.zeros_like(r)

        xf = x_ref[...]
        r = lax.rsqrt(jnp.mean(xf * xf, axis=-1, keepdims=True) + EPS)
        xhat = xf * r
        total = dres_ref[...]
        for i in range(n):
            dy = dxn_refs[i][...]
            dg_refs[i][...] += jnp.sum(dy * xhat, axis=0, keepdims=True)
            dyg = dy * g_refs[i][...]
            total = total + r * (dyg - xhat * jnp.mean(dyg * xhat, axis=-1, keepdims=True))
        dx_ref[...] = total

    row = pl.BlockSpec((tm, D), lambda i: (i, 0))
    vec = pl.BlockSpec((1, D), lambda i: (0, 0))
    outs = pl.pallas_call(
        body, name=name, grid=(S // tm,),
        in_specs=[row, row] + [row] * n + [vec] * n,
        out_specs=[row] + [vec] * n,
        out_shape=[jax.ShapeDtypeStruct((S, D), F32)] + [jax.ShapeDtypeStruct((1, D), F32)] * n,
        compiler_params=_cp(("arbitrary",)),
    )(x, dres, *[b[0] for b in branches], *[b[1] for b in branches])
    return outs[0], outs[1:]


def loss_head(x, target, g, *, name, tm=512):
    S = x.shape[0]

    def body(x_ref, t_ref, g_ref, dx_ref, dg_ref, loss_ref):
        @pl.when(pl.program_id(0) == 0)
        def _():
            dg_ref[...] = jnp.zeros_like(dg_ref)
            loss_ref[...] = jnp.zeros_like(loss_ref)

        xf = x_ref[...]
        gg = g_ref[...]
        r = lax.rsqrt(jnp.mean(xf * xf, axis=-1, keepdims=True) + EPS)
        xhat = xf * r
        e = xhat * gg - t_ref[...]
        per_tok = jnp.mean(e * e, axis=-1, keepdims=True)
        loss_ref[...] += 0.5 * jnp.sum(per_tok, axis=0, keepdims=True)
        dy = e * (1.0 / D)
        dg_ref[...] += jnp.sum(dy * xhat, axis=0, keepdims=True)
        dyg = dy * gg
        dx_ref[...] = r * (dyg - xhat * jnp.mean(dyg * xhat, axis=-1, keepdims=True))

    row = pl.BlockSpec((tm, D), lambda i: (i, 0))
    vec = pl.BlockSpec((1, D), lambda i: (0, 0))
    return pl.pallas_call(
        body, name=name, grid=(S // tm,),
        in_specs=[row, row, vec],
        out_specs=[row, vec, pl.BlockSpec((1, 128), lambda i: (0, 0))],
        out_shape=[jax.ShapeDtypeStruct((S, D), F32), jax.ShapeDtypeStruct((1, D), F32),
                   jax.ShapeDtypeStruct((1, 128), F32)],
        compiler_params=_cp(("arbitrary",)),
    )(x, target, g)


def _sigmoid(x):
    return 1.0 / (1.0 + jnp.exp(-x))


def _gates(z_ref, bi_ref, bf_ref):
    li = SOFTCAP * jnp.tanh((z_ref[:, Z_GI:Z_GI + 128] + bi_ref[...]) * (1.0 / SOFTCAP))
    scf = SOFTCAP * jnp.tanh((z_ref[:, Z_GF:Z_GF + 128] + bf_ref[...]) * (1.0 / SOFTCAP))
    lf = jnp.minimum(scf, 0.0) - jnp.log(1.0 + jnp.exp(-jnp.abs(scf)))
    return li, scf, lf


def _tri(L, lower):
    r = lax.broadcasted_iota(jnp.int32, (L, L), 0)
    c = lax.broadcasted_iota(jnp.int32, (L, L), 1)
    return (r >= c) if lower else (r <= c)


def mlstm_fwd(z, bi, bf, *, name):
    S = z.shape[0]
    L = MLSTM_CHUNK
    NC = S // L
    scale = A_QK ** -0.5

    def body(z_ref, bi_ref, bf_ref, h_ref, cst_ref, nst_ref, C_s, n_s):
        @pl.when(pl.program_id(0) == 0)
        def _():
            C_s[...] = jnp.zeros_like(C_s)
            n_s[...] = jnp.zeros_like(n_s)

        li, _, lf = _gates(z_ref, bi_ref, bf_ref)
        causal = _tri(L, True)
        b = _dot(causal.astype(F32), lf, precision=HIGHEST)
        liT = li.T
        bT = b.T
        cst_ref[0] = C_s[...].astype(BF16)
        nst_ref[0] = n_s[...]
        for h in range(A_HEADS):
            q = z_ref[:, h * A_QK:(h + 1) * A_QK] * scale
            k = z_ref[:, 512 + h * A_QK:512 + (h + 1) * A_QK]
            qb = q.astype(BF16)
            kb = k.astype(BF16)
            vb = z_ref[:, 1024 + h * A_V:1024 + (h + 1) * A_V].astype(BF16)
            a_col, b_col = li[:, h:h + 1], b[:, h:h + 1]
            a_row, b_row = liT[h:h + 1, :], bT[h:h + 1, :]
            Dm = jnp.exp(jnp.where(causal, b_col - b_row + a_row, -jnp.inf))
            A = _dot_nt(qb, kb) * Dm
            eb = jnp.exp(b_col)
            Ch = C_s[h]
            nh = n_s[h:h + 1, :]
            num = _dot(A.astype(BF16), vb) + eb * _dot(qb, Ch.astype(BF16))
            den = jnp.sum(A, axis=-1, keepdims=True) + eb * jnp.sum(q * nh, axis=-1, keepdims=True)
            h_ref[:, h * A_V:(h + 1) * A_V] = num / jnp.maximum(jnp.abs(den), 1.0)
            bL = b_col[L - 1:L, :]
            kw = jnp.exp(bL - b_col + a_col) * k
            decay = jnp.exp(bL)
            C_s[h] = decay * Ch + _dot_tn(kw.astype(BF16), vb)
            n_s[h:h + 1, :] = decay * nh + jnp.sum(kw, axis=0, keepdims=True)

    vec = pl.BlockSpec((1, 128), lambda c: (0, 0))
    return pl.pallas_call(
        body, name=name, grid=(NC,),
        in_specs=[pl.BlockSpec((L, Z_W), lambda c: (c, 0)), vec, vec],
        out_specs=[pl.BlockSpec((L, 1024), lambda c: (c, 0)),
                   pl.BlockSpec((1, A_HEADS, A_QK, A_V), lambda c: (c, 0, 0, 0)),
                   pl.BlockSpec((1, 8, 128), lambda c: (c, 0, 0))],
        out_shape=[jax.ShapeDtypeStruct((S, 1024), F32),
                   jax.ShapeDtypeStruct((NC, A_HEADS, A_QK, A_V), BF16),
                   jax.ShapeDtypeStruct((NC, 8, 128), F32)],
        scratch_shapes=[pltpu.VMEM((A_HEADS, A_QK, A_V), F32), pltpu.VMEM((8, 128), F32)],
        compiler_params=_cp(("arbitrary",)),
    )(z, bi, bf)


def mlstm_bwd(z, bi, bf, cst, nst, dh, dzo, *, name):
    S = z.shape[0]
    L = MLSTM_CHUNK
    NC = S // L
    scale = A_QK ** -0.5

    def body(z_ref, bi_ref, bf_ref, cst_ref, nst_ref, dh_ref, dzo_ref, dz_ref, db_ref, dC_s, dn_s):
        @pl.when(pl.program_id(0) == 0)
        def _():
            dC_s[...] = jnp.zeros_like(dC_s)
            dn_s[...] = jnp.zeros_like(dn_s)
            db_ref[...] = jnp.zeros_like(db_ref)

        li, scf, lf = _gates(z_ref, bi_ref, bf_ref)
        causal = _tri(L, True)
        b = _dot(causal.astype(F32), lf, precision=HIGHEST)
        liT = li.T
        bT = b.T
        lane = lax.broadcasted_iota(jnp.int32, (L, 128), 1)
        sub = lax.broadcasted_iota(jnp.int32, (128, L), 0)
        lane1 = lax.broadcasted_iota(jnp.int32, (1, 128), 1)
        Rm = jnp.zeros((L, 128), F32)
        KIm = jnp.zeros((L, 128), F32)
        csm = jnp.zeros((128, L), F32)
        Xm = jnp.zeros((1, 128), F32)
        for h in range(A_HEADS):
            q = z_ref[:, h * A_QK:(h + 1) * A_QK] * scale
            k = z_ref[:, 512 + h * A_QK:512 + (h + 1) * A_QK]
            qb = q.astype(BF16)
            kb = k.astype(BF16)
            vb = z_ref[:, 1024 + h * A_V:1024 + (h + 1) * A_V].astype(BF16)
            a_col, b_col = li[:, h:h + 1], b[:, h:h + 1]
            a_row, b_row = liT[h:h + 1, :], bT[h:h + 1, :]
            Dm = jnp.exp(jnp.where(causal, b_col - b_row + a_row, -jnp.inf))
            Sqk = _dot_nt(qb, kb)
            A = Sqk * Dm
            Ab = A.astype(BF16)
            eb = jnp.exp(b_col)
            Cb = cst_ref[0, h]
            nh = nst_ref[0, h:h + 1, :]
            num = _dot(Ab, vb) + eb * _dot(qb, Cb)
            den = jnp.sum(A, axis=-1, keepdims=True) + eb * jnp.sum(q * nh, axis=-1, keepdims=True)
            aden = jnp.abs(den)
            u = 1.0 / jnp.maximum(aden, 1.0)
            dhh = dh_ref[:, h * A_V:(h + 1) * A_V]
            dnum = dhh * u
            dden = jnp.where(aden > 1.0, -jnp.sum(dhh * num, axis=-1, keepdims=True) * u * u * jnp.sign(den), 0.0)
            dnb = dnum.astype(BF16)
            G = Dm * (_dot_nt(dnb, vb) + dden)
            Gb = G.astype(BF16)
            E = G * Sqk
            rs = jnp.sum(E, axis=-1, keepdims=True)
            cs = jnp.sum(E, axis=0, keepdims=True)
            dCh = dC_s[h]
            dnh = dn_s[h:h + 1, :]
            dCb = dCh.astype(BF16)
            bL = b_col[L - 1:L, :]
            wk = jnp.exp(bL - b_col + a_col)
            decay = jnp.exp(bL)
            dq_inter = eb * (_dot_nt(dnb, Cb) + dden * nh)
            dk_inter = wk * (_dot_nt(vb, dCb) + dnh)
            dq = _dot(Gb, kb) + dq_inter
            dk = _dot_tn(Gb, qb) + dk_inter
            dv = _dot_tn(Ab, dnb) + wk * _dot(kb, dCb)
            dz_ref[:, h * A_QK:(h + 1) * A_QK] = (dq * scale).astype(BF16)
            dz_ref[:, 512 + h * A_QK:512 + (h + 1) * A_QK] = dk.astype(BF16)
            dz_ref[:, 1024 + h * A_V:1024 + (h + 1) * A_V] = dv.astype(BF16)
            KI = jnp.sum(k * dk_inter, axis=-1, keepdims=True)
            R = rs + jnp.sum(q * dq_inter, axis=-1, keepdims=True)
            cross = (jnp.sum(jnp.sum(dCh * Cb.astype(F32), axis=0, keepdims=True), axis=1, keepdims=True)
                     + jnp.sum(dnh * nh, axis=1, keepdims=True))
            Xm = jnp.where(lane1 == h, decay * cross, Xm)
            Rm = jnp.where(lane == h, R, Rm)
            KIm = jnp.where(lane == h, KI, KIm)
            csm = jnp.where(sub == h, cs, csm)
            ebq = eb * q
            dC_s[h] = decay * dCh + _dot_tn(ebq.astype(BF16), dnb)
            dn_s[h:h + 1, :] = decay * dnh + jnp.sum(ebq * dden, axis=0, keepdims=True)
        dz_ref[:, 2048:3072] = dzo_ref[...]
        cs_col = csm.T
        da = cs_col + KIm
        rr = lax.broadcasted_iota(jnp.int32, (L, L), 0)
        cc = lax.broadcasted_iota(jnp.int32, (L, L), 1)
        dlf = (_dot((rr <= cc).astype(F32), Rm - cs_col, precision=HIGHEST)
               + _dot((rr > cc).astype(F32), KIm, precision=HIGHEST) + Xm)
        dpre_i = da * (1.0 - (li * (1.0 / SOFTCAP)) ** 2)
        dpre_f = dlf * (1.0 - _sigmoid(scf)) * (1.0 - (scf * (1.0 / SOFTCAP)) ** 2)
        dz_ref[:, Z_GI:Z_GI + 128] = dpre_i.astype(BF16)
        dz_ref[:, Z_GF:Z_GF + 128] = dpre_f.astype(BF16)
        db_ref[0:1, :] += jnp.sum(dpre_i, axis=0, keepdims=True)
        db_ref[1:2, :] += jnp.sum(dpre_f, axis=0, keepdims=True)

    vec = pl.BlockSpec((1, 128), lambda c: (0, 0))
    rev = lambda c: (NC - 1 - c, 0)
    return pl.pallas_call(
        body, name=name, grid=(NC,),
        in_specs=[pl.BlockSpec((L, Z_W), rev), vec, vec,
                  pl.BlockSpec((1, A_HEADS, A_QK, A_V), lambda c: (NC - 1 - c, 0, 0, 0)),
                  pl.BlockSpec((1, 8, 128), lambda c: (NC - 1 - c, 0, 0)),
                  pl.BlockSpec((L, 1024), rev), pl.BlockSpec((L, 1024), rev)],
        out_specs=[pl.BlockSpec((L, Z_W), rev), pl.BlockSpec((8, 128), lambda c: (0, 0))],
        out_shape=[jax.ShapeDtypeStruct((S, Z_W), BF16), jax.ShapeDtypeStruct((8, 128), F32)],
        scratch_shapes=[pltpu.VMEM((A_HEADS, A_QK, A_V), F32), pltpu.VMEM((8, 128), F32)],
        compiler_params=_cp(("arbitrary",)),
    )(z, bi, bf, cst, nst, dh, dzo)


def ao_fwd(h, z, gh, w_out, x, *, name, tm=512):
    S = h.shape[0]

    def body(h_ref, o_ref, gh_ref, w_ref, x_ref, x1_ref, hg_ref):
        for hd in range(A_HEADS):
            sl = slice(hd * A_V, (hd + 1) * A_V)
            hs = h_ref[:, sl]
            hn = hs * lax.rsqrt(jnp.mean(hs * hs, axis=-1, keepdims=True) + EPS) * gh_ref[:, sl]
            hg_ref[:, sl] = (hn * _sigmoid(o_ref[:, sl])).astype(BF16)
        x1_ref[...] = x_ref[...] + _dot(hg_ref[...], w_ref[...])

    row = pl.BlockSpec((tm, 1024), lambda i: (i, 0))
    return pl.pallas_call(
        body, name=name, grid=(S // tm,),
        in_specs=[row, pl.BlockSpec((tm, 1024), lambda i: (i, 2)), pl.BlockSpec((1, 1024), lambda i: (0, 0)),
                  pl.BlockSpec((1024, 1024), lambda i: (0, 0)), row],
        out_specs=[row, row],
        out_shape=[jax.ShapeDtypeStruct((S, 1024), F32), jax.ShapeDtypeStruct((S, 1024), BF16)],
        compiler_params=_cp(("parallel",)),
    )(h, z, gh, w_out, x)


def ao_bwd(g1, w_out_t, h, z, gh, *, name, tm=512):
    S = h.shape[0]

    def body(g_ref, w_ref, h_ref, o_ref, gh_ref, dh_ref, dzo_ref, dgh_ref):
        @pl.when(pl.program_id(0) == 0)
        def _():
            dgh_ref[...] = jnp.zeros_like(dgh_ref)

        dhg = _dot(g_ref[...].astype(BF16), w_ref[...])
        for hd in range(A_HEADS):
            sl = slice(hd * A_V, (hd + 1) * A_V)
            hs = h_ref[:, sl]
            r = lax.rsqrt(jnp.mean(hs * hs, axis=-1, keepdims=True) + EPS)
            hhat = hs * r
            ghs = gh_ref[:, sl]
            sig = _sigmoid(o_ref[:, sl])
            d = dhg[:, sl]
            dhn = d * sig
            dzo_ref[:, sl] = (d * hhat * ghs * sig * (1.0 - sig)).astype(BF16)
            dgh_ref[:, sl] += jnp.sum(dhn * hhat, axis=0, keepdims=True)
            dhhat = dhn * ghs
            dh_ref[:, sl] = r * (dhhat - hhat * jnp.mean(dhhat * hhat, axis=-1, keepdims=True))

    row = pl.BlockSpec((tm, 1024), lambda i: (i, 0))
    vec = pl.BlockSpec((1, 1024), lambda i: (0, 0))
    return pl.pallas_call(
        body, name=name, grid=(S // tm,),
        in_specs=[row, pl.BlockSpec((1024, 1024), lambda i: (0, 0)), row,
                  pl.BlockSpec((tm, 1024), lambda i: (i, 2)), vec],
        out_specs=[row, row, vec],
        out_shape=[jax.ShapeDtypeStruct((S, 1024), F32), jax.ShapeDtypeStruct((S, 1024), BF16),
                   jax.ShapeDtypeStruct((1, 1024), F32)],
        compiler_params=_cp(("arbitrary",)),
    )(g1, w_out_t, h, z, gh)


CONV_TC = 1408
CONV_HALO = 16


def _causal_conv(u_ref, halo_ref, w_ref, b_ref, first):
    u = u_ref[...].astype(F32)
    T = u.shape[0]
    hl = jnp.where(first, 0.0, halo_ref[...].astype(F32))
    row = lax.broadcasted_iota(jnp.int32, u.shape, 0)
    u1 = jnp.where(row == 0, hl[CONV_HALO - 1:CONV_HALO], pltpu.roll(u, 1, axis=0))
    u2 = jnp.where(row == 0, hl[CONV_HALO - 2:CONV_HALO - 1],
                   jnp.where(row == 1, hl[CONV_HALO - 1:CONV_HALO], pltpu.roll(u, 2, axis=0)))
    w = w_ref[...]
    conv = u * w[2:3] + u1 * w[1:2] + u2 * w[0:1] + b_ref[...]
    return u, u1, u2, conv


def conv_act(u, cw, cb, *, name, tm=512):
    S = u.shape[0]
    hb = tm // CONV_HALO

    def body(ug_ref, uv_ref, hg_ref, hv_ref, wg_ref, wv_ref, bg_ref, bv_ref, a_ref):
        first = pl.program_id(1) == 0
        g = _causal_conv(ug_ref, hg_ref, wg_ref, bg_ref, first)[3]
        v = _causal_conv(uv_ref, hv_ref, wv_ref, bv_ref, first)[3]
        a_ref[...] = (g * _sigmoid(g) * v).astype(BF16)

    def blk(off):
        return pl.BlockSpec((tm, CONV_TC), lambda j, i: (i, 2 * j + off))

    def halo(off):
        return pl.BlockSpec((CONV_HALO, CONV_TC), lambda j, i: (jnp.maximum(i * hb - 1, 0), 2 * j + off))

    def wsp(rows, off):
        return pl.BlockSpec((rows, CONV_TC), lambda j, i: (0, 2 * j + off))

    return pl.pallas_call(
        body, name=name, grid=(2, S // tm),
        in_specs=[blk(0), blk(1), halo(0), halo(1), wsp(3, 0), wsp(3, 1), wsp(1, 0), wsp(1, 1)],
        out_specs=pl.BlockSpec((tm, CONV_TC), lambda j, i: (i, j)),
        out_shape=jax.ShapeDtypeStruct((S, D_FF), BF16),
        compiler_params=_cp(("parallel", "parallel")),
    )(u, u, u, u, cw, cw, cb, cb)


def conv_bwd1(da, u, cw, cb, *, name, tm=512):
    S = u.shape[0]
    hb = tm // CONV_HALO

    def body(da_ref, ug_ref, uv_ref, hg_ref, hv_ref, wg_ref, wv_ref, bg_ref, bv_ref, duc_ref, dwb_ref):
        first = pl.program_id(1) == 0

        @pl.when(first)
        def _():
            dwb_ref[...] = jnp.zeros_like(dwb_ref)

        gu = _causal_conv(ug_ref, hg_ref, wg_ref, bg_ref, first)
        vu = _causal_conv(uv_ref, hv_ref, wv_ref, bv_ref, first)
        g, v = gu[3], vu[3]
        sg = _sigmoid(g)
        dav = da_ref[...]

        def emit(d, taps, cols):
            duc_ref[:, cols] = d.astype(BF16)
            dwb_ref[0:1, cols] += jnp.sum(taps[2] * d, axis=0, keepdims=True)
            dwb_ref[1:2, cols] += jnp.sum(taps[1] * d, axis=0, keepdims=True)
            dwb_ref[2:3, cols] += jnp.sum(taps[0] * d, axis=0, keepdims=True)
            dwb_ref[3:4, cols] += jnp.sum(d, axis=0, keepdims=True)

        emit(dav * v * (sg * (1.0 + g * (1.0 - sg))), gu, slice(0, CONV_TC))
        emit(dav * (g * sg), vu, slice(CONV_TC, 2 * CONV_TC))

    def blk(off):
        return pl.BlockSpec((tm, CONV_TC), lambda j, i: (i, 2 * j + off))

    def halo(off):
        return pl.BlockSpec((CONV_HALO, CONV_TC), lambda j, i: (jnp.maximum(i * hb - 1, 0), 2 * j + off))

    def wsp(rows, off):
        return pl.BlockSpec((rows, CONV_TC), lambda j, i: (0, 2 * j + off))

    return pl.pallas_call(
        body, name=name, grid=(2, S // tm),
        in_specs=[pl.BlockSpec((tm, CONV_TC), lambda j, i: (i, j)), blk(0), blk(1), halo(0), halo(1),
                  wsp(3, 0), wsp(3, 1), wsp(1, 0), wsp(1, 1)],
        out_specs=[pl.BlockSpec((tm, 2 * CONV_TC), lambda j, i: (i, j)),
                   pl.BlockSpec((8, 2 * CONV_TC), lambda j, i: (0, j))],
        out_shape=[jax.ShapeDtypeStruct((S, 2 * D_FF), BF16), jax.ShapeDtypeStruct((8, 2 * D_FF), F32)],
        compiler_params=_cp(("parallel", "arbitrary")),
    )(da, u, u, u, u, cw, cw, cb, cb)


def conv_bwd2(duc, cw, *, name, tm=512):
    S = duc.shape[0]
    hb = tm // CONV_HALO
    nblk = S // tm

    def body(d_ref, halo_ref, w_ref, du_ref):
        last = pl.program_id(1) == nblk - 1
        d = d_ref[...].astype(F32)
        hl = jnp.where(last, 0.0, halo_ref[...].astype(F32))
        row = lax.broadcasted_iota(jnp.int32, d.shape, 0)
        d1 = jnp.where(row == tm - 1, hl[0:1], pltpu.roll(d, tm - 1, axis=0))
        d2 = jnp.where(row == tm - 1, hl[1:2], jnp.where(row == tm - 2, hl[0:1], pltpu.roll(d, tm - 2, axis=0)))
        w = w_ref[...]
        du_ref[...] = (d * w[2:3] + d1 * w[1:2] + d2 * w[0:1]).astype(BF16)

    return pl.pallas_call(
        body, name=name, grid=(4, nblk),
        in_specs=[pl.BlockSpec((tm, CONV_TC), lambda j, i: (i, j)),
                  pl.BlockSpec((CONV_HALO, CONV_TC), lambda j, i: (jnp.minimum((i + 1) * hb, nblk * hb - 1), j)),
                  pl.BlockSpec((3, CONV_TC), lambda j, i: (0, j))],
        out_specs=pl.BlockSpec((tm, CONV_TC), lambda j, i: (i, j)),
        out_shape=jax.ShapeDtypeStruct((S, 2 * D_FF), BF16),
        compiler_params=_cp(("parallel", "parallel")),
    )(duc, duc, cw)


def _t5_bucket(dist):
    max_exact = REL_BUCKETS // 2
    d = np.maximum(dist, 0)
    log_ratio = np.log(np.maximum(d, 1) / max_exact) / math.log(REL_MAX_DIST / max_exact)
    large = np.minimum(max_exact + (log_ratio * (REL_BUCKETS - max_exact)).astype(np.int64), REL_BUCKETS - 1)
    return np.where(d < max_exact, d, large).astype(np.int32)


def _bucket_tables():
    delta = BLK + np.arange(BLK)[:, None] - np.arange(2 * BLK)[None, :]
    return np.stack([_t5_bucket(delta * dil) for _, dil in B_GROUPS]).astype(np.int32)


def bias_build(rel_bias, buckets, *, name):
    def body(rel_ref, bk_ref, o_ref):
        g = pl.program_id(0)
        bk = bk_ref[0]
        for h in range(B_HEADS):
            acc = jnp.zeros((BLK, 2 * BLK), F32)
            for bb in range(REL_BUCKETS):
                acc = jnp.where(bk == bb, rel_ref[bb, g * B_HEADS + h], acc)
            o_ref[0, h] = acc

    return pl.pallas_call(
        body, name=name, grid=(N_GROUPS,),
        in_specs=[pl.BlockSpec(memory_space=pltpu.SMEM), pl.BlockSpec((1, BLK, 2 * BLK), lambda g: (g, 0, 0))],
        out_specs=pl.BlockSpec((1, B_HEADS, BLK, 2 * BLK), lambda g: (g, 0, 0, 0)),
        out_shape=jax.ShapeDtypeStruct((N_GROUPS, B_HEADS, BLK, 2 * BLK), F32),
        compiler_params=_cp(("arbitrary",)),
    )(rel_bias, buckets)


def bias_grad(dbias, buckets, *, name):
    def body(db_ref, bk_ref, o_ref):
        g = pl.program_id(0)

        @pl.when(g == 0)
        def _():
            o_ref[...] = jnp.zeros_like(o_ref)

        bk = bk_ref[0]
        rr = lax.broadcasted_iota(jnp.int32, (REL_BUCKETS, 128), 0)
        cc = lax.broadcasted_iota(jnp.int32, (REL_BUCKETS, 128), 1)
        acc = jnp.zeros((REL_BUCKETS, 128), F32)
        for h in range(B_HEADS):
            dbh = db_ref[0, h]
            for bb in range(REL_BUCKETS):
                part = jnp.sum(jnp.where(bk == bb, dbh, 0.0), axis=0, keepdims=True)
                s = jnp.sum(part, axis=1, keepdims=True)
                acc = acc + jnp.where((rr == bb) & (cc == g * B_HEADS + h), s, 0.0)
        o_ref[...] += acc

    return pl.pallas_call(
        body, name=name, grid=(N_GROUPS,),
        in_specs=[pl.BlockSpec((1, B_HEADS, BLK, 2 * BLK), lambda g: (g, 0, 0, 0)),
                  pl.BlockSpec((1, BLK, 2 * BLK), lambda g: (g, 0, 0))],
        out_specs=pl.BlockSpec((REL_BUCKETS, 128), lambda g: (0, 0)),
        out_shape=jax.ShapeDtypeStruct((REL_BUCKETS, 128), F32),
        compiler_params=_cp(("arbitrary",)),
    )(dbias, buckets)


HG = 4
GW = HG * B_DH


def _head_masks(dtype):
    lane = lax.broadcasted_iota(jnp.int32, (BLK, GW), 1)
    return [((lane >= h * B_DH) & (lane < (h + 1) * B_DH)).astype(dtype) for h in range(HG)]


def _band_masks():
    iq = lax.broadcasted_iota(jnp.int32, (BLK, BLK), 0)
    ik = lax.broadcasted_iota(jnp.int32, (BLK, BLK), 1)
    return iq <= ik, iq >= ik


def attn_fwd(qg, kg, vg, bias, g, dil, *, name):
    S = qg.shape[0]
    S2 = S // dil
    nb = S2 // BLK
    W = dil * 1024
    scale = B_DH ** -0.5

    def body(q_ref, kc_ref, kp_ref, vc_ref, vp_ref, b_ref, o_ref, lse_ref):
        has_prev = pl.program_id(1) > 0
        vp_m, vc_m = _band_masks()
        valid = jnp.concatenate([vp_m & has_prev, vc_m], axis=1)
        mb = _head_masks(BF16)
        mf = _head_masks(F32)
        lane = lax.broadcasted_iota(jnp.int32, (BLK, 128), 1)
        lse_acc = jnp.zeros((BLK, 128), F32)
        for hg in range(B_HEADS // HG):
            sl = slice(hg * GW, (hg + 1) * GW)
            q4 = q_ref[:, sl]
            kcat = jnp.concatenate([kp_ref[:, sl], kc_ref[:, sl]], axis=0)
            vcat = jnp.concatenate([vp_ref[:, sl], vc_ref[:, sl]], axis=0)
            s4 = _dot_nt(jnp.concatenate([q4 * mb[h] for h in range(HG)], axis=0), kcat)
            ps, rl = [], []
            for h in range(HG):
                hh = hg * HG + h
                s = jnp.where(valid, s4[h * BLK:(h + 1) * BLK] * scale + b_ref[0, hh], -jnp.inf)
                m = jnp.max(s, axis=-1, keepdims=True)
                p = jnp.exp(s - m)
                l = jnp.sum(p, axis=-1, keepdims=True)
                ps.append(p.astype(BF16))
                rl.append(1.0 / l)
                lse_acc = jnp.where(lane == hh, m + jnp.log(l), lse_acc)
            o4 = _dot(jnp.concatenate(ps, axis=0), vcat)
            acc = jnp.zeros((BLK, GW), F32)
            for h in range(HG):
                acc = acc + o4[h * BLK:(h + 1) * BLK] * (mf[h] * rl[h])
            o_ref[:, sl] = acc
        lse_ref[...] = lse_acc

    cur = pl.BlockSpec((BLK, 1024), lambda r, n: (n, r))
    prev = pl.BlockSpec((BLK, 1024), lambda r, n: (jnp.maximum(n - 1, 0), r))
    q2, k2, v2 = qg.reshape(S2, W), kg.reshape(S2, W), vg.reshape(S2, W)
    o, lse = pl.pallas_call(
        body, name=name, grid=(dil, nb),
        in_specs=[cur, cur, prev, cur, prev, pl.BlockSpec((1, B_HEADS, BLK, 2 * BLK), lambda r, n: (g, 0, 0, 0))],
        out_specs=[cur, pl.BlockSpec((BLK, 128), lambda r, n: (n, r))],
        out_shape=[jax.ShapeDtypeStruct((S2, W), F32), jax.ShapeDtypeStruct((S2, dil * 128), F32)],
        compiler_params=_cp(("parallel", "arbitrary")),
    )(q2, k2, k2, v2, v2, bias)
    return o.reshape(S, 1024), lse.reshape(S, 128)


def attn_merge(os_, lses, *, name, tm=512):
    S = os_[0].shape[0]
    expand = np.zeros((128, 1024), np.float32)
    for h in range(B_HEADS):
        expand[h, h * B_DH:(h + 1) * B_DH] = 1.0
    expand = jnp.asarray(expand)

    def body(o0, o1, o2, l0, l1, l2, e_ref, out_ref, lse_ref):
        ls = [l0[...], l1[...], l2[...]]
        m = jnp.maximum(jnp.maximum(ls[0], ls[1]), ls[2])
        ws = [jnp.exp(l - m) for l in ls]
        tot = ws[0] + ws[1] + ws[2]
        lse_ref[...] = m + jnp.log(tot)
        acc = jnp.zeros((tm, 1024), F32)
        for w, o in zip(ws, (o0, o1, o2)):
            acc = acc + _dot(w / tot, e_ref[...], precision=HIGHEST) * o[...]
        out_ref[...] = acc.astype(BF16)

    row = pl.BlockSpec((tm, 1024), lambda i: (i, 0))
    lrow = pl.BlockSpec((tm, 128), lambda i: (i, 0))
    return pl.pallas_call(
        body, name=name, grid=(S // tm,),
        in_specs=[row, row, row, lrow, lrow, lrow, pl.BlockSpec((128, 1024), lambda i: (0, 0))],
        out_specs=[row, lrow],
        out_shape=[jax.ShapeDtypeStruct((S, 1024), BF16), jax.ShapeDtypeStruct((S, 128), F32)],
        compiler_params=_cp(("parallel",)),
    )(*os_, *lses, expand)


def attn_bwd(qg, kg, vg, bias, dout, out, lse, g, dil, *, name):
    S = qg.shape[0]
    S2 = S // dil
    nb = S2 // BLK
    W = dil * 1024
    scale = B_DH ** -0.5

    def body(q_ref, kc_ref, kp_ref, vc_ref, vp_ref, b_ref, do_ref, out_ref, lse_ref,
             dq_ref, dk_ref, dv_ref, db_ref, ck_s, cv_s):
        n = pl.program_id(1)

        @pl.when((pl.program_id(0) == 0) & (n == 0))
        def _():
            db_ref[...] = jnp.zeros_like(db_ref)

        @pl.when(n == 0)
        def _():
            ck_s[...] = jnp.zeros_like(ck_s)
            cv_s[...] = jnp.zeros_like(cv_s)

        @pl.when(n == nb)
        def _():
            dk_ref[...] = ck_s[...].astype(BF16)
            dv_ref[...] = cv_s[...].astype(BF16)

        @pl.when(n < nb)
        def _():
            vp_m, vc_m = _band_masks()
            valid = jnp.concatenate([vp_m & (n > 0), vc_m], axis=1)
            mb = _head_masks(BF16)
            mf = _head_masks(F32)
            lse_blk = lse_ref[...]
            for hg in range(B_HEADS // HG):
                sl = slice(hg * GW, (hg + 1) * GW)
                kcat = jnp.concatenate([kp_ref[:, sl], kc_ref[:, sl]], axis=0)
                vcat = jnp.concatenate([vp_ref[:, sl], vc_ref[:, sl]], axis=0)
                dof = do_ref[:, sl]
                dob = dof.astype(BF16)
                dd = dof * out_ref[:, sl].astype(F32)
                q4 = q_ref[:, sl]
                q4m = jnp.concatenate([q4 * mb[h] for h in range(HG)], axis=0)
                do4m = jnp.concatenate([dob * mb[h] for h in range(HG)], axis=0)
                s4 = _dot_nt(q4m, kcat)
                dp4 = _dot_nt(do4m, vcat)
                ps, dss = [], []
                for h in range(HG):
                    hh = hg * HG + h
                    rows = slice(h * BLK, (h + 1) * BLK)
                    Dh = jnp.sum(dd * mf[h], axis=-1, keepdims=True)
                    s = jnp.where(valid, s4[rows] * scale + b_ref[0, hh] - lse_blk[:, hh:hh + 1], -jnp.inf)
                    p = jnp.exp(s)
                    ds = p * (dp4[rows] - Dh)
                    db_ref[hh] += ds
                    ps.append(p.astype(BF16))
                    dss.append(ds.astype(BF16))
                p4 = jnp.concatenate(ps, axis=0)
                ds4 = jnp.concatenate(dss, axis=0)
                dq4 = _dot(ds4, kcat)
                acc = jnp.zeros((BLK, GW), F32)
                for h in range(HG):
                    acc = acc + dq4[h * BLK:(h + 1) * BLK] * mf[h]
                dq_ref[:, sl] = (acc * scale).astype(BF16)
                dkc = _dot_tn(ds4, q4m) * scale
                dvc = _dot_tn(p4, do4m)
                dk_ref[:, sl] = (ck_s[:, sl] + dkc[0:BLK]).astype(BF16)
                dv_ref[:, sl] = (cv_s[:, sl] + dvc[0:BLK]).astype(BF16)
                ck_s[:, sl] = dkc[BLK:2 * BLK]
                cv_s[:, sl] = dvc[BLK:2 * BLK]

    last = nb - 1
    cur = lambda r, n: (jnp.minimum(n, last), r)
    prev = lambda r, n: (jnp.clip(n - 1, 0, last), r)
    row = lambda im: pl.BlockSpec((BLK, 1024), im)
    k2, v2 = kg.reshape(S2, W), vg.reshape(S2, W)
    dq, dk, dv, dbias = pl.pallas_call(
        body, name=name, grid=(dil, nb + 1),
        in_specs=[row(cur), row(cur), row(prev), row(cur), row(prev),
                  pl.BlockSpec((1, B_HEADS, BLK, 2 * BLK), lambda r, n: (g, 0, 0, 0)),
                  row(cur), row(cur), pl.BlockSpec((BLK, 128), cur)],
        out_specs=[row(cur), row(prev), row(prev), pl.BlockSpec((B_HEADS, BLK, 2 * BLK), lambda r, n: (0, 0, 0))],
        out_shape=[jax.ShapeDtypeStruct((S2, W), BF16)] * 3 + [jax.ShapeDtypeStruct((B_HEADS, BLK, 2 * BLK), F32)],
        scratch_shapes=[pltpu.VMEM((BLK, 1024), F32), pltpu.VMEM((BLK, 1024), F32)],
        compiler_params=_cp(("arbitrary", "arbitrary")),
    )(qg.reshape(S2, W), k2, k2, v2, v2, bias, dout.reshape(S2, W), out.reshape(S2, W), lse.reshape(S2, dil * 128))
    return dq.reshape(S, 1024), dk.reshape(S, 1024), dv.reshape(S, 1024), dbias


def _slot(px, py, pc):
    return 4 * px + 2 * py + pc


def ag_weights(wb, ws):
    def body(wb_ref, ws_ref, ob_ref, os_ref, send_sems, recv_sems, local_sems):
        x, y, c = lax.axis_index("x"), lax.axis_index("y"), lax.axis_index("c")
        me, sibling = (x, y, c), (x, y, 1 - c)
        chips = [(1 - x, y), (x, 1 - y), (1 - x, 1 - y)]
        arrays = [(wb_ref, ob_ref), (ws_ref, os_ref)]

        def copy(a, k, block, to, from_input=False):
            src_in, out = arrays[a]
            dst = out.at[_slot(*block)]
            return pltpu.make_async_remote_copy(
                src_ref=src_in if from_input else dst, dst_ref=dst,
                send_sem=send_sems.at[7 * a + k], recv_sem=recv_sems.at[7 * a + k],
                device_id=to, device_id_type=MESH)

        mine = [pltpu.make_async_copy(arrays[a][0], arrays[a][1].at[_slot(*me)], local_sems.at[a]) for a in range(2)]
        for cp in mine:
            cp.start()
        first = []
        for a in range(2):
            first.append(copy(a, 0, me, sibling, True))
            first += [copy(a, 1 + j, me, (*chip, c), True) for j, chip in enumerate(chips)]
        for cp in first:
            cp.start()
        passed = []
        for a in range(2):
            for j, chip in enumerate(chips):
                copy(a, 1 + j, (*chip, c), me).wait_recv()
                fw = copy(a, 4 + j, (*chip, c), sibling)
                fw.start()
                passed.append(fw)
        for a in range(2):
            copy(a, 0, sibling, me).wait_recv()
            for j, chip in enumerate(chips):
                copy(a, 4 + j, (*chip, 1 - c), me).wait_recv()
        for cp in first + passed:
            cp.wait_send()
        for cp in mine:
            cp.wait()

    any_spec = pl.BlockSpec(memory_space=pl.ANY)
    return pl.pallas_call(
        body, name="ag_weights",
        in_specs=[any_spec, any_spec], out_specs=[any_spec, any_spec],
        out_shape=[jax.ShapeDtypeStruct((N_DEV,) + wb.shape, wb.dtype), jax.ShapeDtypeStruct((N_DEV,) + ws.shape, ws.dtype)],
        scratch_shapes=[pltpu.SemaphoreType.DMA((14,)), pltpu.SemaphoreType.DMA((14,)), pltpu.SemaphoreType.DMA((2,))],
    )(wb, ws)


def rs_exchange(gpack, spack):
    def body(g_ref, s_ref, rb_ref, sa_ref, send_sems, recv_sems, local_sems):
        x, y, c = lax.axis_index("x"), lax.axis_index("y"), lax.axis_index("c")
        my = _slot(x, y, c)
        mine = [pltpu.make_async_copy(g_ref.at[my], rb_ref.at[my], local_sems.at[0]),
                pltpu.make_async_copy(s_ref, sa_ref.at[my], local_sems.at[1])]
        for cp in mine:
            cp.start()
        sends, recvs = [], []
        for k in range(1, N_DEV):
            peer = (1 - x if k & 4 else x, 1 - y if k & 2 else y, 1 - c if k & 1 else c)
            pid = _slot(*peer)
            sends.append(pltpu.make_async_remote_copy(
                src_ref=g_ref.at[pid], dst_ref=rb_ref.at[my], send_sem=send_sems.at[k - 1], recv_sem=recv_sems.at[k - 1],
                device_id=peer, device_id_type=MESH))
            sends.append(pltpu.make_async_remote_copy(
                src_ref=s_ref, dst_ref=sa_ref.at[my], send_sem=send_sems.at[6 + k], recv_sem=recv_sems.at[6 + k],
                device_id=peer, device_id_type=MESH))
            recvs.append(pltpu.make_async_remote_copy(
                src_ref=g_ref.at[pid], dst_ref=rb_ref.at[pid], send_sem=send_sems.at[k - 1], recv_sem=recv_sems.at[k - 1],
                device_id=peer, device_id_type=MESH))
            recvs.append(pltpu.make_async_remote_copy(
                src_ref=s_ref, dst_ref=sa_ref.at[pid], send_sem=send_sems.at[6 + k], recv_sem=recv_sems.at[6 + k],
                device_id=peer, device_id_type=MESH))
        for cp in sends:
            cp.start()
        for cp in recvs:
            cp.wait_recv()
        for cp in sends:
            cp.wait_send()
        for cp in mine:
            cp.wait()

    any_spec = pl.BlockSpec(memory_space=pl.ANY)
    return pl.pallas_call(
        body, name="rs_exchange",
        in_specs=[any_spec, any_spec], out_specs=[any_spec, any_spec],
        out_shape=[jax.ShapeDtypeStruct(gpack.shape, gpack.dtype), jax.ShapeDtypeStruct((N_DEV,) + spack.shape, spack.dtype)],
        scratch_shapes=[pltpu.SemaphoreType.DMA((14,)), pltpu.SemaphoreType.DMA((14,)), pltpu.SemaphoreType.DMA((2,))],
    )(gpack, spack)


def reduce_adam(parts, w, m, v, *, name, tr):
    R = w.shape[0]
    assert R % tr == 0
    c1 = 1.0 - ADAM_B1 ** ADAM_STEP
    c2 = 1.0 - ADAM_B2 ** ADAM_STEP

    def body(p_ref, w_ref, m_ref, v_ref, g_ref, d_ref, mo_ref, vo_ref):
        g = p_ref[0].astype(F32)
        for i in range(1, N_DEV):
            g = g + p_ref[i].astype(F32)
        mn = ADAM_B1 * m_ref[...] + (1.0 - ADAM_B1) * g
        vn = ADAM_B2 * v_ref[...] + (1.0 - ADAM_B2) * (g * g)
        g_ref[...] = g
        mo_ref[...] = mn
        vo_ref[...] = vn
        d_ref[...] = -ADAM_LR * ((mn / c1) / (jnp.sqrt(vn / c2) + ADAM_EPS) + ADAM_WD * w_ref[...])

    row = pl.BlockSpec((tr, 1024), lambda i: (i, 0))
    return pl.pallas_call(
        body, name=name, grid=(R // tr,),
        in_specs=[pl.BlockSpec((N_DEV, tr, 1024), lambda i: (0, i, 0)), row, row, row],
        out_specs=[row] * 4,
        out_shape=[jax.ShapeDtypeStruct((R, 1024), F32)] * 4,
        compiler_params=_cp(("parallel",)),
    )(parts, w, m, v)


BIG = (("a_w_in", 385, 400), ("a_w_out", 128, 128), ("w_kv", 768, 768), ("b_w_q", 384, 384),
       ("b_w_out", 128, 128), ("f_w_up", 1408, 1408), ("f_w_down", 704, 704))
SMALL_SHARDED = (("a_norm_g", 128), ("a_hnorm_g", 128), ("f_conv_w", 4224))
SMALL_ROWS = 48
PACK_ROWS = sum(b[2] for b in BIG) + SMALL_ROWS
REPL = (("kv_norm_g", 1024, 1), ("b_norm_g", 1024, 1), ("f_norm_g", 2048, 2), ("f_conv_b", 11264, 11),
        ("final_norm_g", 1024, 1), ("rel_bias", 1536, 2), ("a_b_if", 8, 1))
REPL_ROWS = 24
LOSS_ROW = 19


def _rows(a, rows, padded):
    a = a.reshape(rows, 1024)
    return a if padded == rows else jnp.pad(a, ((0, padded - rows), (0, 0)))


def pack_shards(t, dtype, with_small):
    parts = [_rows(t[n].astype(dtype), r, p) for n, r, p in BIG]
    if with_small:
        flat = jnp.concatenate([t[n].astype(dtype).reshape(-1) for n, _ in SMALL_SHARDED])
        parts.append(jnp.pad(flat, (0, SMALL_ROWS * 1024 - flat.shape[0])).reshape(SMALL_ROWS, 1024))
    return jnp.concatenate(parts, axis=0)


def unpack_shards(pack, shapes):
    out = {}
    r0 = 0
    for n, r, p in BIG:
        out[n] = pack[r0:r0 + r].reshape(shapes[n])
        r0 += p
    flat = pack[r0:r0 + SMALL_ROWS].reshape(-1)
    e0 = 0
    for n, e in SMALL_SHARDED:
        out[n] = flat[e0:e0 + e].reshape(shapes[n])
        e0 += e
    return out


def pack_repl(t):
    parts = []
    for n, e, r in REPL:
        parts.append(jnp.pad(t[n].astype(F32).reshape(-1), (0, r * 1024 - e)))
    rows = sum(r for _, _, r in REPL)
    parts.append(jnp.zeros(((REPL_ROWS - rows) * 1024,), F32))
    return jnp.concatenate(parts).reshape(REPL_ROWS, 1024)


def unpack_repl(pack, shapes):
    out = {}
    r0 = 0
    for n, e, r in REPL:
        out[n] = pack[r0:r0 + r].reshape(-1)[:e].reshape(shapes[n])
        r0 += r
    return out


def ff_blocks(a):
    b = [a[..., i * CONV_TC:(i + 1) * CONV_TC] for i in range(4)]
    return jnp.concatenate([b[0], b[2], b[1], b[3]], axis=-1)


def split_cols(full, n):
    lead = full.shape[:-1]
    return jnp.moveaxis(full.reshape(lead + (N_DEV, n)), -2, 0)


def join_cols(parts):
    t = jnp.moveaxis(parts, 0, -2)
    return t.reshape(t.shape[:-2] + (t.shape[-2] * t.shape[-1],))


def kernel(x, a_norm_g, a_w_in, a_b_if, a_hnorm_g, a_w_out, kv_norm_g, w_kv, b_norm_g, b_w_q, b_w_out, rel_bias, f_norm_g, f_w_up, f_conv_w, f_conv_b, f_w_down, final_norm_g, loss_target, m_a_norm_g, m_a_w_in, m_a_b_if, m_a_hnorm_g, m_a_w_out, m_kv_norm_g, m_w_kv, m_b_norm_g, m_b_w_q, m_b_w_out, m_rel_bias, m_f_norm_g, m_f_w_up, m_f_conv_w, m_f_conv_b, m_f_w_down, m_final_norm_g, v_a_norm_g, v_a_w_in, v_a_b_if, v_a_hnorm_g, v_a_w_out, v_kv_norm_g, v_w_kv, v_b_norm_g, v_b_w_q, v_b_w_out, v_rel_bias, v_f_norm_g, v_f_w_up, v_f_conv_w, v_f_conv_b, v_f_w_down, v_final_norm_g):
    names = ["a_norm_g", "a_w_in", "a_b_if", "a_hnorm_g", "a_w_out", "kv_norm_g", "w_kv", "b_norm_g", "b_w_q", "b_w_out",
             "rel_bias", "f_norm_g", "f_w_up", "f_conv_w", "f_conv_b", "f_w_down", "final_norm_g"]
    w = dict(zip(names, (a_norm_g, a_w_in, a_b_if, a_hnorm_g, a_w_out, kv_norm_g, w_kv, b_norm_g, b_w_q, b_w_out,
                         rel_bias, f_norm_g, f_w_up, f_conv_w, f_conv_b, f_w_down, final_norm_g)))
    mom = dict(zip(names, (m_a_norm_g, m_a_w_in, m_a_b_if, m_a_hnorm_g, m_a_w_out, m_kv_norm_g, m_w_kv, m_b_norm_g, m_b_w_q,
                           m_b_w_out, m_rel_bias, m_f_norm_g, m_f_w_up, m_f_conv_w, m_f_conv_b, m_f_w_down, m_final_norm_g)))
    vel = dict(zip(names, (v_a_norm_g, v_a_w_in, v_a_b_if, v_a_hnorm_g, v_a_w_out, v_kv_norm_g, v_w_kv, v_b_norm_g, v_b_w_q,
                           v_b_w_out, v_rel_bias, v_f_norm_g, v_f_w_up, v_f_conv_w, v_f_conv_b, v_f_w_down, v_final_norm_g)))
    shapes = {n: w[n].shape for n in names}
    S = x.shape[1]
    assert x.shape[0] == 1 and S % (16 * BLK) == 0 and S % 1024 == 0
    X0 = x.reshape(S, D)
    target = loss_target.reshape(S, D)

    wb_all, ws_all = ag_weights(pack_shards(w, BF16, False),
                                pack_shards(w, F32, True)[PACK_ROWS - SMALL_ROWS:])
    seg = {}
    r0 = 0
    for n, r, p in BIG:
        seg[n] = wb_all[:, r0:r0 + r]
        r0 += p
    W_in = join_cols(seg["a_w_in"].reshape(N_DEV, D, 385))
    W_in = jnp.concatenate([jnp.pad(W_in[:, :3076], ((0, 0), (0, 124))),
                            jnp.pad(W_in[:, 3076:3080], ((0, 0), (0, 124)))], axis=1)
    W_out = seg["a_w_out"].reshape(1024, D)
    W_kv = join_cols(seg["w_kv"].reshape(N_DEV, D, 768))
    W_q = join_cols(seg["b_w_q"].reshape(N_DEV, D, 384))
    W_bout = seg["b_w_out"].reshape(1024, D)
    W_up = join_cols(seg["f_w_up"].reshape(N_DEV, 2, D, 704))
    W_down = jnp.moveaxis(seg["f_w_down"].reshape(N_DEV, 2, 352, D), 0, 1).reshape(2, D_FF, D)
    sflat = ws_all.reshape(N_DEV, SMALL_ROWS * 1024)
    g_a = sflat[:, 0:128].reshape(1, D)
    g_h = jnp.moveaxis(sflat[:, 128:256].reshape(N_DEV, A_HEADS, 32), 0, 1).reshape(1, A_HEADS * A_V)
    conv_w = ff_blocks(join_cols(sflat[:, 256:256 + 4224].reshape(N_DEV, 2, 3, 704)))
    conv_b = ff_blocks(f_conv_b)
    W_up = ff_blocks(W_up)
    bi = jnp.pad(a_b_if[:, :A_HEADS], ((0, 0), (0, 128 - A_HEADS)))
    bfg = jnp.pad(a_b_if[:, A_HEADS:], ((0, 0), (0, 128 - A_HEADS)))
    buckets = jnp.asarray(_bucket_tables())

    (xn_a,) = rms_fwd(X0, [g_a], name="rms_a")
    z = mm(xn_a, W_in, name="mm_a_in", out_dtype=F32, tn=1664)
    h, cst, nst = mlstm_fwd(z, bi, bfg, name="mlstm_fwd")
    X1, hg = ao_fwd(h, z, g_h, W_out, X0, name="ao_fwd")

    def ffn_fwd(X, l, tag):
        (xn,) = rms_fwd(X, [f_norm_g[l:l + 1]], name="rms_f" + tag)
        u = mm(xn, W_up[l], name="mm_up" + tag, out_dtype=BF16, tn=1408)
        a = conv_act(u, conv_w[l], conv_b[l:l + 1], name="conv_act" + tag)
        Xn = mm(a, W_down[l], name="mm_down" + tag, out_dtype=F32, tn=1024, res=X)
        return Xn, (xn, u, a)

    X2, sav0 = ffn_fwd(X1, 0, "0")

    xkn, xbn = rms_fwd(X2, [kv_norm_g.reshape(1, D), b_norm_g], name="rms_kv_b")
    bias = bias_build(rel_bias, buckets, name="bias_build")
    col = lambda wmat, i: wmat[:, i * 1024:(i + 1) * 1024]
    qs, kk, vv, og, lg = [], [], [], [], []
    for g, (_, dil) in enumerate(B_GROUPS):
        qs.append(mm(xbn, col(W_q, g), name="mm_q%d" % g, out_dtype=BF16, tn=1024))
        kk.append(mm(xkn, col(W_kv, g), name="mm_k%d" % g, out_dtype=BF16, tn=1024))
        vv.append(mm(xkn, col(W_kv, N_GROUPS + g), name="mm_v%d" % g, out_dtype=BF16, tn=1024))
        o_, l_ = attn_fwd(qs[g], kk[g], vv[g], bias, g, dil, name="attn_fwd%d" % g)
        og.append(o_)
        lg.append(l_)
    att, lse = attn_merge(og, lg, name="attn_merge")
    X3 = mm(att, W_bout, name="mm_b_out", out_dtype=F32, tn=1024, res=X2)
    X4, sav1 = ffn_fwd(X3, 1, "1")

    dX4, d_final_g, loss_part = loss_head(X4, target, final_norm_g.reshape(1, D), name="loss_head")

    def ffn_bwd(X, dXn, l, sav, tag):
        xn, u, a = sav
        dW_down = mm_tn(a, dXn, name="tn_down" + tag, tn=1024)
        da = mm(dXn, W_down[l].T, name="mm_da" + tag, out_dtype=F32, tn=1408)
        duc, dwb = conv_bwd1(da, u, conv_w[l], conv_b[l:l + 1], name="conv_bwd1" + tag)
        du = conv_bwd2(duc, conv_w[l], name="conv_bwd2" + tag)
        dW_up = mm_tn(xn, du, name="tn_up" + tag, tn=1408)
        dxn = mm(du, W_up[l].T, name="mm_dxn_f" + tag, out_dtype=F32, tn=1024)
        dX, (dg,) = rms_bwd(X, dXn, [(dxn, f_norm_g[l:l + 1])], name="rms_bwd_f" + tag)
        return dX, dW_down, dW_up, dwb, dg

    dX3, dWd1, dWu1, dwb1, dgf1 = ffn_bwd(X3, dX4, 1, sav1, "1")

    dW_bout = mm_tn(att, dX3, name="tn_b_out", tn=1024)
    dout = mm(dX3, W_bout.T, name="mm_dout", out_dtype=F32, tn=1024)
    dqs, dks, dvs, dbias = [], [], [], []
    for g, (_, dil) in enumerate(B_GROUPS):
        dq_, dk_, dv_, db_ = attn_bwd(qs[g], kk[g], vv[g], bias, dout, att, lse, g, dil, name="attn_bwd%d" % g)
        dqs.append(dq_)
        dks.append(dk_)
        dvs.append(dv_)
        dbias.append(db_)
    d_rel = bias_grad(jnp.stack(dbias), buckets, name="bias_grad")[:, :N_GROUPS * B_HEADS]
    dW_q = jnp.concatenate([mm_tn(xbn, d_, name="tn_q%d" % g, tn=1024) for g, d_ in enumerate(dqs)], axis=1)
    dW_kv = jnp.concatenate([mm_tn(xkn, d_, name="tn_kv%d" % i, tn=1024) for i, d_ in enumerate(dks + dvs)], axis=1)
    W_qT, W_kvT = W_q.T, W_kv.T
    rows = lambda wmat, i: wmat[i * 1024:(i + 1) * 1024]
    dxn_b = mm_sum(dqs, [rows(W_qT, g) for g in range(N_GROUPS)], name="mm_dxn_b")
    dxn_kv = mm_sum(dks + dvs, [rows(W_kvT, i) for i in range(2 * N_GROUPS)], name="mm_dxn_kv")
    dX2, (dg_kv, dg_b) = rms_bwd(X2, dX3, [(dxn_kv, kv_norm_g.reshape(1, D)), (dxn_b, b_norm_g)], name="rms_bwd_kv_b")

    dX1, dWd0, dWu0, dwb0, dgf0 = ffn_bwd(X1, dX2, 0, sav0, "0")

    dW_out = mm_tn(hg, dX1, name="tn_a_out", tn=1024)
    dh, dzo, dgh = ao_bwd(dX1, W_out.T, h, z, g_h, name="ao_bwd")
    dz, db_if = mlstm_bwd(z, bi, bfg, cst, nst, dh, dzo, name="mlstm_bwd")
    dW_in = mm_tn(xn_a, dz, name="tn_a_in", tn=1664)
    dW_in = jnp.concatenate([dW_in[:, :3076], dW_in[:, Z_GF:Z_GF + 4]], axis=1)
    dxn_a = mm(dz, W_in.T, name="mm_dxn_a", out_dtype=F32, tn=1024)
    dX0, (dg_a,) = rms_bwd(X0, dX1, [(dxn_a, g_a)], name="rms_bwd_a")

    dWu = ff_blocks(jnp.stack([dWu0, dWu1]))
    dWd = jnp.stack([dWd0, dWd1])
    dwb = ff_blocks(jnp.stack([dwb0, dwb1]))
    slots = [
        jnp.pad(split_cols(dW_in, 385).reshape(N_DEV, 385, 1024), ((0, 0), (0, 15), (0, 0))),
        dW_out.reshape(N_DEV, 128, 1024),
        split_cols(dW_kv, 768).reshape(N_DEV, 768, 1024),
        split_cols(dW_q, 384).reshape(N_DEV, 384, 1024),
        dW_bout.reshape(N_DEV, 128, 1024),
        split_cols(dWu, 704).reshape(N_DEV, 1408, 1024),
        jnp.moveaxis(dWd.reshape(2, N_DEV, 352, D), 1, 0).reshape(N_DEV, 704, 1024),
    ]
    small = jnp.concatenate([
        dg_a.reshape(N_DEV, 128),
        split_cols(dgh.reshape(A_HEADS, A_V), 32).reshape(N_DEV, 128),
        split_cols(dwb[:, 0:3], 704).reshape(N_DEV, 4224)], axis=1)
    slots.append(jnp.pad(small, ((0, 0), (0, SMALL_ROWS * 1024 - small.shape[1]))).reshape(N_DEV, SMALL_ROWS, 1024))
    gpack = jnp.concatenate([t.astype(BF16) for t in slots], axis=1)
    repl_g = {"kv_norm_g": dg_kv, "b_norm_g": dg_b, "f_norm_g": jnp.concatenate([dgf0, dgf1]),
              "f_conv_b": dwb[:, 3], "final_norm_g": d_final_g, "rel_bias": d_rel,
              "a_b_if": jnp.concatenate([db_if[0, :A_HEADS], db_if[1, :A_HEADS]])}
    spack = pack_repl(repl_g)
    spack = spack.at[LOSS_ROW, 0].set(loss_part[0, 0])

    parts, sparts = rs_exchange(gpack, spack)
    gb, db, mb, vb = reduce_adam(parts, pack_shards(w, F32, True), pack_shards(mom, F32, True),
                                 pack_shards(vel, F32, True), name="reduce_adam_big", tr=PACK_ROWS // 8)
    gs, ds, ms, vs = reduce_adam(sparts, pack_repl(w), pack_repl(mom), pack_repl(vel), name="reduce_adam_small", tr=REPL_ROWS)
    loss = gs[LOSS_ROW, 0]

    def collect(big, sm):
        t = unpack_shards(big, shapes)
        t.update(unpack_repl(sm, shapes))
        return [t[n] for n in names]

    return (loss, dX0.reshape(1, S, D), *collect(gb, gs), *collect(db, ds), *collect(mb, ms), *collect(vb, vs))
```

```python
import functools
import math

import numpy as np
import jax
import jax.numpy as jnp
from jax import lax
from jax.experimental import pallas as pl
from jax.experimental.pallas import tpu as pltpu

F32 = jnp.float32
BF16 = jnp.bfloat16
HIGHEST = lax.Precision.HIGHEST
MESH = pl.DeviceIdType.MESH

D = 1024
A_HEADS = 4
A_QK = 128
A_V = 256
SOFTCAP = 15.0
N_GROUPS = 3
B_GROUPS = ((128, 1), (512, 4), (2048, 16))
B_HEADS = 16
B_DH = 64
BLK = 128
REL_BUCKETS = 32
REL_MAX_DIST = 2048
D_FF = 2816
EPS = 1e-6
ADAM_LR = 0.001
ADAM_B1 = 0.9
ADAM_B2 = 0.999
ADAM_EPS = 1e-08
ADAM_WD = 0.01
ADAM_STEP = 10

N_DEV = 8
V7X_VMEM_BYTES = 64 * 1024 * 1024
VMEM_LIMIT = V7X_VMEM_BYTES - 8 * 1024 * 1024
MLSTM_CHUNK = 256
Z_W = 3328
Z_GI = 3072
Z_GF = 3200


def _cp(sem):
    return pltpu.CompilerParams(dimension_semantics=sem, vmem_limit_bytes=VMEM_LIMIT)


def _dot(a, b, **kw):
    return jnp.dot(a, b, preferred_element_type=F32, **kw)


def _dot_nt(a, b):
    return lax.dot_general(a, b, (((1,), (1,)), ((), ())), preferred_element_type=F32)


def _dot_tn(a, b):
    return lax.dot_general(a, b, (((0,), (0,)), ((), ())), preferred_element_type=F32)


def mm(a, b, *, name, out_dtype, tn, tm=1024, res=None):
    M, K = a.shape
    N = b.shape[1]
    assert M % tm == 0 and N % tn == 0 and b.shape[0] == K

    def body(a_ref, b_ref, *rest):
        o_ref = rest[-1]
        acc = _dot(a_ref[...].astype(BF16), b_ref[...])
        if res is not None:
            acc = acc + rest[0][...]
        o_ref[...] = acc.astype(out_dtype)

    in_specs = [pl.BlockSpec((tm, K), lambda j, i: (i, 0)), pl.BlockSpec((K, tn), lambda j, i: (0, j))]
    args = [a, b]
    if res is not None:
        in_specs.append(pl.BlockSpec((tm, tn), lambda j, i: (i, j)))
        args.append(res)
    return pl.pallas_call(
        body, name=name, grid=(N // tn, M // tm), in_specs=in_specs,
        out_specs=pl.BlockSpec((tm, tn), lambda j, i: (i, j)),
        out_shape=jax.ShapeDtypeStruct((M, N), out_dtype),
        compiler_params=_cp(("parallel", "parallel")),
    )(*args)


def mm_sum(a_list, b_list, *, name, tm=512):
    M, K = a_list[0].shape
    N = b_list[0].shape[1]
    n = len(a_list)
    assert M % tm == 0

    def body(*refs):
        o_ref = refs[-1]
        acc = _dot(refs[0][...], refs[n][...])
        for i in range(1, n):
            acc = acc + _dot(refs[i][...], refs[n + i][...])
        o_ref[...] = acc

    return pl.pallas_call(
        body, name=name, grid=(M // tm,),
        in_specs=[pl.BlockSpec((tm, K), lambda i: (i, 0))] * n + [pl.BlockSpec((K, N), lambda i: (0, 0))] * n,
        out_specs=pl.BlockSpec((tm, N), lambda i: (i, 0)),
        out_shape=jax.ShapeDtypeStruct((M, N), F32),
        compiler_params=_cp(("parallel",)),
    )(*a_list, *b_list)


def mm_tn(a, b, *, name, tn, tk=1024):
    S, Kd = a.shape
    N = b.shape[1]
    assert S % tk == 0 and N % tn == 0 and b.shape[0] == S

    def body(a_ref, b_ref, o_ref):
        @pl.when(pl.program_id(1) == 0)
        def _():
            o_ref[...] = jnp.zeros_like(o_ref)

        o_ref[...] += _dot_tn(a_ref[...].astype(BF16), b_ref[...].astype(BF16))

    return pl.pallas_call(
        body, name=name, grid=(N // tn, S // tk),
        in_specs=[pl.BlockSpec((tk, Kd), lambda j, k: (k, 0)), pl.BlockSpec((tk, tn), lambda j, k: (k, j))],
        out_specs=pl.BlockSpec((Kd, tn), lambda j, k: (0, j)),
        out_shape=jax.ShapeDtypeStruct((Kd, N), F32),
        compiler_params=_cp(("parallel", "arbitrary")),
    )(a, b)


def rms_fwd(x, gains, *, name, tm=1024):
    S = x.shape[0]
    n = len(gains)

    def body(x_ref, *rest):
        xf = x_ref[...]
        y = xf * lax.rsqrt(jnp.mean(xf * xf, axis=-1, keepdims=True) + EPS)
        for i in range(n):
            rest[n + i][...] = (y * rest[i][...]).astype(BF16)

    return pl.pallas_call(
        body, name=name, grid=(S // tm,),
        in_specs=[pl.BlockSpec((tm, D), lambda i: (i, 0))] + [pl.BlockSpec((1, D), lambda i: (0, 0))] * n,
        out_specs=[pl.BlockSpec((tm, D), lambda i: (i, 0))] * n,
        out_shape=[jax.ShapeDtypeStruct((S, D), BF16)] * n,
        compiler_params=_cp(("parallel",)),
    )(x, *gains)


def rms_bwd(x, dres, branches, *, name, tm=512):
    S = x.shape[0]
    n = len(branches)

    def body(x_ref, dres_ref, *rest):
        dxn_refs = rest[:n]
        g_refs = rest[n:2 * n]
        dx_ref = rest[2 * n]
        dg_refs = rest[2 * n + 1:]

        @pl.when(pl.program_id(0) == 0)
        def _():
            for r in dg_refs:
                r[...] = jnp.zeros_like(r)

        xf = x_ref[...]
        r = lax.rsqrt(jnp.mean(xf * xf, axis=-1, keepdims=True) + EPS)
        xhat = xf * r
        total = dres_ref[...]
        for i in range(n):
            dy = dxn_refs[i][...]
            dg_refs[i][...] += jnp.sum(dy * xhat, axis=0, keepdims=True)
            dyg = dy * g_refs[i][...]
            total = total + r * (dyg - xhat * jnp.mean(dyg * xhat, axis=-1, keepdims=True))
        dx_ref[...] = total

    row = pl.BlockSpec((tm, D), lambda i: (i, 0))
    vec = pl.BlockSpec((1, D), lambda i: (0, 0))
    outs = pl.pallas_call(
        body, name=name, grid=(S // tm,),
        in_specs=[row, row] + [row] * n + [vec] * n,
        out_specs=[row] + [vec] * n,
        out_shape=[jax.ShapeDtypeStruct((S, D), F32)] + [jax.ShapeDtypeStruct((1, D), F32)] * n,
        compiler_params=_cp(("arbitrary",)),
    )(x, dres, *[b[0] for b in branches], *[b[1] for b in branches])
    return outs[0], outs[1:]


def loss_head(x, target, g, *, name, tm=512):
    S = x.shape[0]

    def body(x_ref, t_ref, g_ref, dx_ref, dg_ref, loss_ref):
        @pl.when(pl.program_id(0) == 0)
        def _():
            dg_ref[...] = jnp.zeros_like(dg_ref)
            loss_ref[...] = jnp.zeros_like(loss_ref)

        xf = x_ref[...]
        gg = g_ref[...]
        r = lax.rsqrt(jnp.mean(xf * xf, axis=-1, keepdims=True) + EPS)
        xhat = xf * r
        e = xhat * gg - t_ref[...]
        per_tok = jnp.mean(e * e, axis=-1, keepdims=True)
        loss_ref[...] += 0.5 * jnp.sum(per_tok, axis=0, keepdims=True)
        dy = e * (1.0 / D)
        dg_ref[...] += jnp.sum(dy * xhat, axis=0, keepdims=True)
        dyg = dy * gg
        dx_ref[...] = r * (dyg - xhat * jnp.mean(dyg * xhat, axis=-1, keepdims=True))

    row = pl.BlockSpec((tm, D), lambda i: (i, 0))
    vec = pl.BlockSpec((1, D), lambda i: (0, 0))
    return pl.pallas_call(
        body, name=name, grid=(S // tm,),
        in_specs=[row, row, vec],
        out_specs=[row, vec, pl.BlockSpec((1, 128), lambda i: (0, 0))],
        out_shape=[jax.ShapeDtypeStruct((S, D), F32), jax.ShapeDtypeStruct((1, D), F32),
                   jax.ShapeDtypeStruct((1, 128), F32)],
        compiler_params=_cp(("arbitrary",)),
    )(x, target, g)


def _sigmoid(x):
    return 1.0 / (1.0 + jnp.exp(-x))


def _gates(z_ref, bi_ref, bf_ref):
    li = SOFTCAP * jnp.tanh((z_ref[:, Z_GI:Z_GI + 128] + bi_ref[...]) * (1.0 / SOFTCAP))
    scf = SOFTCAP * jnp.tanh((z_ref[:, Z_GF:Z_GF + 128] + bf_ref[...]) * (1.0 / SOFTCAP))
    lf = jnp.minimum(scf, 0.0) - jnp.log(1.0 + jnp.exp(-jnp.abs(scf)))
    return li, scf, lf


def _tri(L, lower):
    r = lax.broadcasted_iota(jnp.int32, (L, L), 0)
    c = lax.broadcasted_iota(jnp.int32, (L, L), 1)
    return (r >= c) if lower else (r <= c)


def mlstm_fwd(z, bi, bf, *, name):
    S = z.shape[0]
    L = MLSTM_CHUNK
    NC = S // L
    scale = A_QK ** -0.5

    def body(z_ref, bi_ref, bf_ref, h_ref, cst_ref, nst_ref, C_s, n_s):
        @pl.when(pl.program_id(0) == 0)
        def _():
            C_s[...] = jnp.zeros_like(C_s)
            n_s[...] = jnp.zeros_like(n_s)

        li, _, lf = _gates(z_ref, bi_ref, bf_ref)
        causal = _tri(L, True)
        b = _dot(causal.astype(F32), lf, precision=HIGHEST)
        liT = li.T
        bT = b.T
        cst_ref[0] = C_s[...].astype(BF16)
        nst_ref[0] = n_s[...]
        for h in range(A_HEADS):
            q = z_ref[:, h * A_QK:(h + 1) * A_QK] * scale
            k = z_ref[:, 512 + h * A_QK:512 + (h + 1) * A_QK]
            qb = q.astype(BF16)
            kb = k.astype(BF16)
            vb = z_ref[:, 1024 + h * A_V:1024 + (h + 1) * A_V].astype(BF16)
            a_col, b_col = li[:, h:h + 1], b[:, h:h + 1]
            a_row, b_row = liT[h:h + 1, :], bT[h:h + 1, :]
            Dm = jnp.exp(jnp.where(causal, b_col - b_row + a_row, -jnp.inf))
            A = _dot_nt(qb, kb) * Dm
            eb = jnp.exp(b_col)
            Ch = C_s[h]
            nh = n_s[h:h + 1, :]
            num = _dot(A.astype(BF16), vb) + eb * _dot(qb, Ch.astype(BF16))
            den = jnp.sum(A, axis=-1, keepdims=True) + eb * jnp.sum(q * nh, axis=-1, keepdims=True)
            h_ref[:, h * A_V:(h + 1) * A_V] = num / jnp.maximum(jnp.abs(den), 1.0)
            bL = b_col[L - 1:L, :]
            kw = jnp.exp(bL - b_col + a_col) * k
            decay = jnp.exp(bL)
            C_s[h] = decay * Ch + _dot_tn(kw.astype(BF16), vb)
            n_s[h:h + 1, :] = decay * nh + jnp.sum(kw, axis=0, keepdims=True)

    vec = pl.BlockSpec((1, 128), lambda c: (0, 0))
    return pl.pallas_call(
        body, name=name, grid=(NC,),
        in_specs=[pl.BlockSpec((L, Z_W), lambda c: (c, 0)), vec, vec],
        out_specs=[pl.BlockSpec((L, 1024), lambda c: (c, 0)),
                   pl.BlockSpec((1, A_HEADS, A_QK, A_V), lambda c: (c, 0, 0, 0)),
                   pl.BlockSpec((1, 8, 128), lambda c: (c, 0, 0))],
        out_shape=[jax.ShapeDtypeStruct((S, 1024), F32),
                   jax.ShapeDtypeStruct((NC, A_HEADS, A_QK, A_V), BF16),
                   jax.ShapeDtypeStruct((NC, 8, 128), F32)],
        scratch_shapes=[pltpu.VMEM((A_HEADS, A_QK, A_V), F32), pltpu.VMEM((8, 128), F32)],
        compiler_params=_cp(("arbitrary",)),
    )(z, bi, bf)


def mlstm_bwd(z, bi, bf, cst, nst, dh, dzo, *, name):
    S = z.shape[0]
    L = MLSTM_CHUNK
    NC = S // L
    scale = A_QK ** -0.5

    def body(z_ref, bi_ref, bf_ref, cst_ref, nst_ref, dh_ref, dzo_ref, dz_ref, db_ref, dC_s, dn_s):
        @pl.when(pl.program_id(0) == 0)
        def _():
            dC_s[...] = jnp.zeros_like(dC_s)
            dn_s[...] = jnp.zeros_like(dn_s)
            db_ref[...] = jnp.zeros_like(db_ref)

        li, scf, lf = _gates(z_ref, bi_ref, bf_ref)
        causal = _tri(L, True)
        b = _dot(causal.astype(F32), lf, precision=HIGHEST)
        liT = li.T
        bT = b.T
        lane = lax.broadcasted_iota(jnp.int32, (L, 128), 1)
        sub = lax.broadcasted_iota(jnp.int32, (128, L), 0)
        lane1 = lax.broadcasted_iota(jnp.int32, (1, 128), 1)
        Rm = jnp.zeros((L, 128), F32)
        KIm = jnp.zeros((L, 128), F32)
        csm = jnp.zeros((128, L), F32)
        Xm = jnp.zeros((1, 128), F32)
        for h in range(A_HEADS):
            q = z_ref[:, h * A_QK:(h + 1) * A_QK] * scale
            k = z_ref[:, 512 + h * A_QK:512 + (h + 1) * A_QK]
            qb = q.astype(BF16)
            kb = k.astype(BF16)
            vb = z_ref[:, 1024 + h * A_V:1024 + (h + 1) * A_V].astype(BF16)
            a_col, b_col = li[:, h:h + 1], b[:, h:h + 1]
            a_row, b_row = liT[h:h + 1, :], bT[h:h + 1, :]
            Dm = jnp.exp(jnp.where(causal, b_col - b_row + a_row, -jnp.inf))
            Sqk = _dot_nt(qb, kb)
            A = Sqk * Dm
            Ab = A.astype(BF16)
            eb = jnp.exp(b_col)
            Cb = cst_ref[0, h]
            nh = nst_ref[0, h:h + 1, :]
            num = _dot(Ab, vb) + eb * _dot(qb, Cb)
            den = jnp.sum(A, axis=-1, keepdims=True) + eb * jnp.sum(q * nh, axis=-1, keepdims=True)
            aden = jnp.abs(den)
            u = 1.0 / jnp.maximum(aden, 1.0)
            dhh = dh_ref[:, h * A_V:(h + 1) * A_V]
            dnum = dhh * u
            dden = jnp.where(aden > 1.0, -jnp.sum(dhh * num, axis=-1, keepdims=True) * u * u * jnp.sign(den), 0.0)
            dnb = dnum.astype(BF16)
            G = Dm * (_dot_nt(dnb, vb) + dden)
            Gb = G.astype(BF16)
            E = G * Sqk
            rs = jnp.sum(E, axis=-1, keepdims=True)
            cs = jnp.sum(E, axis=0, keepdims=True)
            dCh = dC_s[h]
            dnh = dn_s[h:h + 1, :]
            dCb = dCh.astype(BF16)
            bL = b_col[L - 1:L, :]
            wk = jnp.exp(bL - b_col + a_col)
            decay = jnp.exp(bL)
            dq_inter = eb * (_dot_nt(dnb, Cb) + dden * nh)
            dk_inter = wk * (_dot_nt(vb, dCb) + dnh)
            dq = _dot(Gb, kb) + dq_inter
            dk = _dot_tn(Gb, qb) + dk_inter
            dv = _dot_tn(Ab, dnb) + wk * _dot(kb, dCb)
            dz_ref[:, h * A_QK:(h + 1) * A_QK] = (dq * scale).astype(BF16)
            dz_ref[:, 512 + h * A_QK:512 + (h + 1) * A_QK] = dk.astype(BF16)
            dz_ref[:, 1024 + h * A_V:1024 + (h + 1) * A_V] = dv.astype(BF16)
            KI = jnp.sum(k * dk_inter, axis=-1, keepdims=True)
            R = rs + jnp.sum(q * dq_inter, axis=-1, keepdims=True)
            cross = (jnp.sum(jnp.sum(dCh * Cb.astype(F32), axis=0, keepdims=True), axis=1, keepdims=True)
                     + jnp.sum(dnh * nh, axis=1, keepdims=True))
            Xm = jnp.where(lane1 == h, decay * cross, Xm)
            Rm = jnp.where(lane == h, R, Rm)
            KIm = jnp.where(lane == h, KI, KIm)
            csm = jnp.where(sub == h, cs, csm)
            ebq = eb * q
            dC_s[h] = decay * dCh + _dot_tn(ebq.astype(BF16), dnb)
            dn_s[h:h + 1, :] = decay * dnh + jnp.sum(ebq * dden, axis=0, keepdims=True)
        dz_ref[:, 2048:3072] = dzo_ref[...]
        cs_col = csm.T
        da = cs_col + KIm
        rr = lax.broadcasted_iota(jnp.int32, (L, L), 0)
        cc = lax.broadcasted_iota(jnp.int32, (L, L), 1)
        dlf = (_dot((rr <= cc).astype(F32), Rm - cs_col, precision=HIGHEST)
               + _dot((rr > cc).astype(F32), KIm, precision=HIGHEST) + Xm)
        dpre_i = da * (1.0 - (li * (1.0 / SOFTCAP)) ** 2)
        dpre_f = dlf * (1.0 - _sigmoid(scf)) * (1.0 - (scf * (1.0 / SOFTCAP)) ** 2)
        dz_ref[:, Z_GI:Z_GI + 128] = dpre_i.astype(BF16)
        dz_ref[:, Z_GF:Z_GF + 128] = dpre_f.astype(BF16)
        db_ref[0:1, :] += jnp.sum(dpre_i, axis=0, keepdims=True)
        db_ref[1:2, :] += jnp.sum(dpre_f, axis=0, keepdims=True)

    vec = pl.BlockSpec((1, 128), lambda c: (0, 0))
    rev = lambda c: (NC - 1 - c, 0)
    return pl.pallas_call(
        body, name=name, grid=(NC,),
        in_specs=[pl.BlockSpec((L, Z_W), rev), vec, vec,
                  pl.BlockSpec((1, A_HEADS, A_QK, A_V), lambda c: (NC - 1 - c, 0, 0, 0)),
                  pl.BlockSpec((1, 8, 128), lambda c: (NC - 1 - c, 0, 0)),
                  pl.BlockSpec((L, 1024), rev), pl.BlockSpec((L, 1024), rev)],
        out_specs=[pl.BlockSpec((L, Z_W), rev), pl.BlockSpec((8, 128), lambda c: (0, 0))],
        out_shape=[jax.ShapeDtypeStruct((S, Z_W), BF16), jax.ShapeDtypeStruct((8, 128), F32)],
        scratch_shapes=[pltpu.VMEM((A_HEADS, A_QK, A_V), F32), pltpu.VMEM((8, 128), F32)],
        compiler_params=_cp(("arbitrary",)),
    )(z, bi, bf, cst, nst, dh, dzo)


def ao_fwd(h, z, gh, w_out, x, *, name, tm=512):
    S = h.shape[0]

    def body(h_ref, o_ref, gh_ref, w_ref, x_ref, x1_ref, hg_ref):
        for hd in range(A_HEADS):
            sl = slice(hd * A_V, (hd + 1) * A_V)
            hs = h_ref[:, sl]
            hn = hs * lax.rsqrt(jnp.mean(hs * hs, axis=-1, keepdims=True) + EPS) * gh_ref[:, sl]
            hg_ref[:, sl] = (hn * _sigmoid(o_ref[:, sl])).astype(BF16)
        x1_ref[...] = x_ref[...] + _dot(hg_ref[...], w_ref[...])

    row = pl.BlockSpec((tm, 1024), lambda i: (i, 0))
    return pl.pallas_call(
        body, name=name, grid=(S // tm,),
        in_specs=[row, pl.BlockSpec((tm, 1024), lambda i: (i, 2)), pl.BlockSpec((1, 1024), lambda i: (0, 0)),
                  pl.BlockSpec((1024, 1024), lambda i: (0, 0)), row],
        out_specs=[row, row],
        out_shape=[jax.ShapeDtypeStruct((S, 1024), F32), jax.ShapeDtypeStruct((S, 1024), BF16)],
        compiler_params=_cp(("parallel",)),
    )(h, z, gh, w_out, x)


def ao_bwd(g1, w_out_t, h, z, gh, *, name, tm=512):
    S = h.shape[0]

    def body(g_ref, w_ref, h_ref, o_ref, gh_ref, dh_ref, dzo_ref, dgh_ref):
        @pl.when(pl.program_id(0) == 0)
        def _():
            dgh_ref[...] = jnp.zeros_like(dgh_ref)

        dhg = _dot(g_ref[...].astype(BF16), w_ref[...])
        for hd in range(A_HEADS):
            sl = slice(hd * A_V, (hd + 1) * A_V)
            hs = h_ref[:, sl]
            r = lax.rsqrt(jnp.mean(hs * hs, axis=-1, keepdims=True) + EPS)
            hhat = hs * r
            ghs = gh_ref[:, sl]
            sig = _sigmoid(o_ref[:, sl])
            d = dhg[:, sl]
            dhn = d * sig
            dzo_ref[:, sl] = (d * hhat * ghs * sig * (1.0 - sig)).astype(BF16)
            dgh_ref[:, sl] += jnp.sum(dhn * hhat, axis=0, keepdims=True)
            dhhat = dhn * ghs
            dh_ref[:, sl] = r * (dhhat - hhat * jnp.mean(dhhat * hhat, axis=-1, keepdims=True))

    row = pl.BlockSpec((tm, 1024), lambda i: (i, 0))
    vec = pl.BlockSpec((1, 1024), lambda i: (0, 0))
    return pl.pallas_call(
        body, name=name, grid=(S // tm,),
        in_specs=[row, pl.BlockSpec((1024, 1024), lambda i: (0, 0)), row,
                  pl.BlockSpec((tm, 1024), lambda i: (i, 2)), vec],
        out_specs=[row, row, vec],
        out_shape=[jax.ShapeDtypeStruct((S, 1024), F32), jax.ShapeDtypeStruct((S, 1024), BF16),
                   jax.ShapeDtypeStruct((1, 1024), F32)],
        compiler_params=_cp(("arbitrary",)),
    )(g1, w_out_t, h, z, gh)


CONV_TC = 1408
CONV_HALO = 16


def _causal_conv(u_ref, halo_ref, w_ref, b_ref, first):
    u = u_ref[...].astype(F32)
    T = u.shape[0]
    hl = jnp.where(first, 0.0, halo_ref[...].astype(F32))
    row = lax.broadcasted_iota(jnp.int32, u.shape, 0)
    u1 = jnp.where(row == 0, hl[CONV_HALO - 1:CONV_HALO], pltpu.roll(u, 1, axis=0))
    u2 = jnp.where(row == 0, hl[CONV_HALO - 2:CONV_HALO - 1],
                   jnp.where(row == 1, hl[CONV_HALO - 1:CONV_HALO], pltpu.roll(u, 2, axis=0)))
    w = w_ref[...]
    conv = u * w[2:3] + u1 * w[1:2] + u2 * w[0:1] + b_ref[...]
    return u, u1, u2, conv


def mm_up_conv(xn, w_up, cw, cb, *, name, tm=512):
    S = xn.shape[0]
    TN = 2 * CONV_TC

    def body(x_ref, w_ref, cw_ref, cb_ref, u_ref, a_ref, carry):
        @pl.when(pl.program_id(1) == 0)
        def _():
            carry[...] = jnp.zeros_like(carry)

        ub = _dot(x_ref[...], w_ref[...]).astype(BF16)
        u_ref[...] = ub
        u = ub.astype(F32)
        prev = carry[...]
        row = lax.broadcasted_iota(jnp.int32, u.shape, 0)
        u1 = jnp.where(row == 0, prev[7:8], pltpu.roll(u, 1, axis=0))
        u2 = jnp.where(row == 0, prev[6:7], jnp.where(row == 1, prev[7:8], pltpu.roll(u, 2, axis=0)))
        carry[...] = u[tm - 8:tm]
        w = cw_ref[...]
        c = u * w[2:3] + u1 * w[1:2] + u2 * w[0:1] + cb_ref[...]
        g, v = c[:, :CONV_TC], c[:, CONV_TC:]
        a_ref[...] = (g * _sigmoid(g) * v).astype(BF16)

    return pl.pallas_call(
        body, name=name, grid=(2, S // tm),
        in_specs=[pl.BlockSpec((tm, D), lambda j, i: (i, 0)), pl.BlockSpec((D, TN), lambda j, i: (0, j)),
                  pl.BlockSpec((3, TN), lambda j, i: (0, j)), pl.BlockSpec((1, TN), lambda j, i: (0, j))],
        out_specs=[pl.BlockSpec((tm, TN), lambda j, i: (i, j)), pl.BlockSpec((tm, CONV_TC), lambda j, i: (i, j))],
        out_shape=[jax.ShapeDtypeStruct((S, 2 * D_FF), BF16), jax.ShapeDtypeStruct((S, D_FF), BF16)],
        scratch_shapes=[pltpu.VMEM((8, TN), F32)],
        compiler_params=_cp(("parallel", "arbitrary")),
    )(xn, w_up, cw, cb)


def mm_da_conv(dxn, w_down_t, u, cw, cb, *, name, tm=512):
    S = u.shape[0]
    hb = tm // CONV_HALO

    def body(x_ref, w_ref, ug_ref, uv_ref, hg_ref, hv_ref, wg_ref, wv_ref, bg_ref, bv_ref, duc_ref, dwb_ref):
        first = pl.program_id(1) == 0

        @pl.when(first)
        def _():
            dwb_ref[...] = jnp.zeros_like(dwb_ref)

        dav = _dot(x_ref[...].astype(BF16), w_ref[...])
        gu = _causal_conv(ug_ref, hg_ref, wg_ref, bg_ref, first)
        vu = _causal_conv(uv_ref, hv_ref, wv_ref, bv_ref, first)
        g, v = gu[3], vu[3]
        sg = _sigmoid(g)

        def emit(d, taps, cols):
            duc_ref[:, cols] = d.astype(BF16)
            dwb_ref[0:1, cols] += jnp.sum(taps[2] * d, axis=0, keepdims=True)
            dwb_ref[1:2, cols] += jnp.sum(taps[1] * d, axis=0, keepdims=True)
            dwb_ref[2:3, cols] += jnp.sum(taps[0] * d, axis=0, keepdims=True)
            dwb_ref[3:4, cols] += jnp.sum(d, axis=0, keepdims=True)

        emit(dav * v * (sg * (1.0 + g * (1.0 - sg))), gu, slice(0, CONV_TC))
        emit(dav * (g * sg), vu, slice(CONV_TC, 2 * CONV_TC))

    def blk(off):
        return pl.BlockSpec((tm, CONV_TC), lambda j, i: (i, 2 * j + off))

    def halo(off):
        return pl.BlockSpec((CONV_HALO, CONV_TC), lambda j, i: (jnp.maximum(i * hb - 1, 0), 2 * j + off))

    def wsp(rows, off):
        return pl.BlockSpec((rows, CONV_TC), lambda j, i: (0, 2 * j + off))

    return pl.pallas_call(
        body, name=name, grid=(2, S // tm),
        in_specs=[pl.BlockSpec((tm, D), lambda j, i: (i, 0)), pl.BlockSpec((D, CONV_TC), lambda j, i: (0, j)),
                  blk(0), blk(1), halo(0), halo(1), wsp(3, 0), wsp(3, 1), wsp(1, 0), wsp(1, 1)],
        out_specs=[pl.BlockSpec((tm, 2 * CONV_TC), lambda j, i: (i, j)),
                   pl.BlockSpec((8, 2 * CONV_TC), lambda j, i: (0, j))],
        out_shape=[jax.ShapeDtypeStruct((S, 2 * D_FF), BF16), jax.ShapeDtypeStruct((8, 2 * D_FF), F32)],
        compiler_params=_cp(("parallel", "arbitrary")),
    )(dxn, w_down_t, u, u, u, u, cw, cw, cb, cb)


def tn_up_conv(xn, duc, cw, *, name, tk=512):
    S, Kd = xn.shape
    N = duc.shape[1]
    hb = tk // CONV_HALO
    nblk = S // tk

    def body(x_ref, d_ref, halo_ref, w_ref, o_ref, du_ref):
        k = pl.program_id(1)

        @pl.when(k == 0)
        def _():
            o_ref[...] = jnp.zeros_like(o_ref)

        d = d_ref[...].astype(F32)
        hl = jnp.where(k == nblk - 1, 0.0, halo_ref[...].astype(F32))
        row = lax.broadcasted_iota(jnp.int32, d.shape, 0)
        d1 = jnp.where(row == tk - 1, hl[0:1], pltpu.roll(d, tk - 1, axis=0))
        d2 = jnp.where(row == tk - 1, hl[1:2], jnp.where(row == tk - 2, hl[0:1], pltpu.roll(d, tk - 2, axis=0)))
        w = w_ref[...]
        du = (d * w[2:3] + d1 * w[1:2] + d2 * w[0:1]).astype(BF16)
        du_ref[...] = du
        o_ref[...] += _dot_tn(x_ref[...], du)

    return pl.pallas_call(
        body, name=name, grid=(N // CONV_TC, nblk),
        in_specs=[pl.BlockSpec((tk, Kd), lambda j, k: (k, 0)), pl.BlockSpec((tk, CONV_TC), lambda j, k: (k, j)),
                  pl.BlockSpec((CONV_HALO, CONV_TC), lambda j, k: (jnp.minimum((k + 1) * hb, nblk * hb - 1), j)),
                  pl.BlockSpec((3, CONV_TC), lambda j, k: (0, j))],
        out_specs=[pl.BlockSpec((Kd, CONV_TC), lambda j, k: (0, j)), pl.BlockSpec((tk, CONV_TC), lambda j, k: (k, j))],
        out_shape=[jax.ShapeDtypeStruct((Kd, N), F32), jax.ShapeDtypeStruct((S, N), BF16)],
        compiler_params=_cp(("parallel", "arbitrary")),
    )(xn, duc, duc, cw)


def _t5_bucket(dist):
    max_exact = REL_BUCKETS // 2
    d = np.maximum(dist, 0)
    log_ratio = np.log(np.maximum(d, 1) / max_exact) / math.log(REL_MAX_DIST / max_exact)
    large = np.minimum(max_exact + (log_ratio * (REL_BUCKETS - max_exact)).astype(np.int64), REL_BUCKETS - 1)
    return np.where(d < max_exact, d, large).astype(np.int32)


def _bucket_tables():
    delta = BLK + np.arange(BLK)[:, None] - np.arange(2 * BLK)[None, :]
    return np.stack([_t5_bucket(delta * dil) for _, dil in B_GROUPS]).astype(np.int32)


def bias_build(rel_bias, buckets, *, name):
    def body(rel_ref, bk_ref, o_ref):
        g = pl.program_id(0)
        bk = bk_ref[0]
        for h in range(B_HEADS):
            acc = jnp.zeros((BLK, 2 * BLK), F32)
            for bb in range(REL_BUCKETS):
                acc = jnp.where(bk == bb, rel_ref[bb, g * B_HEADS + h], acc)
            o_ref[0, h] = acc

    return pl.pallas_call(
        body, name=name, grid=(N_GROUPS,),
        in_specs=[pl.BlockSpec(memory_space=pltpu.SMEM), pl.BlockSpec((1, BLK, 2 * BLK), lambda g: (g, 0, 0))],
        out_specs=pl.BlockSpec((1, B_HEADS, BLK, 2 * BLK), lambda g: (g, 0, 0, 0)),
        out_shape=jax.ShapeDtypeStruct((N_GROUPS, B_HEADS, BLK, 2 * BLK), F32),
        compiler_params=_cp(("arbitrary",)),
    )(rel_bias, buckets)


def bias_grad(dbias, buckets, *, name):
    def body(db_ref, bk_ref, o_ref):
        g = pl.program_id(0)

        @pl.when(g == 0)
        def _():
            o_ref[...] = jnp.zeros_like(o_ref)

        bk = bk_ref[0]
        rr = lax.broadcasted_iota(jnp.int32, (REL_BUCKETS, 128), 0)
        cc = lax.broadcasted_iota(jnp.int32, (REL_BUCKETS, 128), 1)
        acc = jnp.zeros((REL_BUCKETS, 128), F32)
        for h in range(B_HEADS):
            dbh = db_ref[0, h]
            for bb in range(REL_BUCKETS):
                part = jnp.sum(jnp.where(bk == bb, dbh, 0.0), axis=0, keepdims=True)
                s = jnp.sum(part, axis=1, keepdims=True)
                acc = acc + jnp.where((rr == bb) & (cc == g * B_HEADS + h), s, 0.0)
        o_ref[...] += acc

    return pl.pallas_call(
        body, name=name, grid=(N_GROUPS,),
        in_specs=[pl.BlockSpec((1, B_HEADS, BLK, 2 * BLK), lambda g: (g, 0, 0, 0)),
                  pl.BlockSpec((1, BLK, 2 * BLK), lambda g: (g, 0, 0))],
        out_specs=pl.BlockSpec((REL_BUCKETS, 128), lambda g: (0, 0)),
        out_shape=jax.ShapeDtypeStruct((REL_BUCKETS, 128), F32),
        compiler_params=_cp(("arbitrary",)),
    )(dbias, buckets)


HG = 4
GW = HG * B_DH


def _head_masks(dtype):
    lane = lax.broadcasted_iota(jnp.int32, (BLK, GW), 1)
    return [((lane >= h * B_DH) & (lane < (h + 1) * B_DH)).astype(dtype) for h in range(HG)]


def _band_masks():
    iq = lax.broadcasted_iota(jnp.int32, (BLK, BLK), 0)
    ik = lax.broadcasted_iota(jnp.int32, (BLK, BLK), 1)
    return iq <= ik, iq >= ik


def attn_fwd(qg, kg, vg, bias, g, dil, *, name):
    S = qg.shape[0]
    S2 = S // dil
    nb = S2 // BLK
    W = dil * 1024
    scale = B_DH ** -0.5

    def body(q_ref, kc_ref, kp_ref, vc_ref, vp_ref, b_ref, o_ref, lse_ref):
        has_prev = pl.program_id(1) > 0
        vp_m, vc_m = _band_masks()
        valid = jnp.concatenate([vp_m & has_prev, vc_m], axis=1)
        mb = _head_masks(BF16)
        mf = _head_masks(F32)
        lane = lax.broadcasted_iota(jnp.int32, (BLK, 128), 1)
        lse_acc = jnp.zeros((BLK, 128), F32)
        for hg in range(B_HEADS // HG):
            sl = slice(hg * GW, (hg + 1) * GW)
            q4 = q_ref[:, sl]
            kcat = jnp.concatenate([kp_ref[:, sl], kc_ref[:, sl]], axis=0)
            vcat = jnp.concatenate([vp_ref[:, sl], vc_ref[:, sl]], axis=0)
            s4 = _dot_nt(jnp.concatenate([q4 * mb[h] for h in range(HG)], axis=0), kcat)
            ps, rl = [], []
            for h in range(HG):
                hh = hg * HG + h
                s = jnp.where(valid, s4[h * BLK:(h + 1) * BLK] * scale + b_ref[0, hh], -jnp.inf)
                m = jnp.max(s, axis=-1, keepdims=True)
                p = jnp.exp(s - m)
                l = jnp.sum(p, axis=-1, keepdims=True)
                ps.append(p.astype(BF16))
                rl.append(1.0 / l)
                lse_acc = jnp.where(lane == hh, m + jnp.log(l), lse_acc)
            o4 = _dot(jnp.concatenate(ps, axis=0), vcat)
            acc = jnp.zeros((BLK, GW), F32)
            for h in range(HG):
                acc = acc + o4[h * BLK:(h + 1) * BLK] * (mf[h] * rl[h])
            o_ref[:, sl] = acc
        lse_ref[...] = lse_acc

    cur = pl.BlockSpec((BLK, 1024), lambda r, n: (n, r))
    prev = pl.BlockSpec((BLK, 1024), lambda r, n: (jnp.maximum(n - 1, 0), r))
    q2, k2, v2 = qg.reshape(S2, W), kg.reshape(S2, W), vg.reshape(S2, W)
    o, lse = pl.pallas_call(
        body, name=name, grid=(dil, nb),
        in_specs=[cur, cur, prev, cur, prev, pl.BlockSpec((1, B_HEADS, BLK, 2 * BLK), lambda r, n: (g, 0, 0, 0))],
        out_specs=[cur, pl.BlockSpec((BLK, 128), lambda r, n: (n, r))],
        out_shape=[jax.ShapeDtypeStruct((S2, W), F32), jax.ShapeDtypeStruct((S2, dil * 128), F32)],
        compiler_params=_cp(("parallel", "arbitrary")),
    )(q2, k2, k2, v2, v2, bias)
    return o.reshape(S, 1024), lse.reshape(S, 128)


def attn_merge(os_, lses, *, name, tm=512):
    S = os_[0].shape[0]
    expand = np.zeros((128, 1024), np.float32)
    for h in range(B_HEADS):
        expand[h, h * B_DH:(h + 1) * B_DH] = 1.0
    expand = jnp.asarray(expand)

    def body(o0, o1, o2, l0, l1, l2, e_ref, out_ref, lse_ref):
        ls = [l0[...], l1[...], l2[...]]
        m = jnp.maximum(jnp.maximum(ls[0], ls[1]), ls[2])
        ws = [jnp.exp(l - m) for l in ls]
        tot = ws[0] + ws[1] + ws[2]
        lse_ref[...] = m + jnp.log(tot)
        acc = jnp.zeros((tm, 1024), F32)
        for w, o in zip(ws, (o0, o1, o2)):
            acc = acc + _dot(w / tot, e_ref[...], precision=HIGHEST) * o[...]
        out_ref[...] = acc.astype(BF16)

    row = pl.BlockSpec((tm, 1024), lambda i: (i, 0))
    lrow = pl.BlockSpec((tm, 128), lambda i: (i, 0))
    return pl.pallas_call(
        body, name=name, grid=(S // tm,),
        in_specs=[row, row, row, lrow, lrow, lrow, pl.BlockSpec((128, 1024), lambda i: (0, 0))],
        out_specs=[row, lrow],
        out_shape=[jax.ShapeDtypeStruct((S, 1024), BF16), jax.ShapeDtypeStruct((S, 128), F32)],
        compiler_params=_cp(("parallel",)),
    )(*os_, *lses, expand)


def attn_bwd(qg, kg, vg, bias, dout, out, lse, g, dil, *, name):
    S = qg.shape[0]
    S2 = S // dil
    nb = S2 // BLK
    W = dil * 1024
    scale = B_DH ** -0.5

    def body(q_ref, kc_ref, kp_ref, vc_ref, vp_ref, b_ref, do_ref, out_ref, lse_ref,
             dq_ref, dk_ref, dv_ref, db_ref, ck_s, cv_s):
        n = pl.program_id(1)

        @pl.when((pl.program_id(0) == 0) & (n == 0))
        def _():
            db_ref[...] = jnp.zeros_like(db_ref)

        @pl.when(n == 0)
        def _():
            ck_s[...] = jnp.zeros_like(ck_s)
            cv_s[...] = jnp.zeros_like(cv_s)

        @pl.when(n == nb)
        def _():
            dk_ref[...] = ck_s[...].astype(BF16)
            dv_ref[...] = cv_s[...].astype(BF16)

        @pl.when(n < nb)
        def _():
            vp_m, vc_m = _band_masks()
            valid = jnp.concatenate([vp_m & (n > 0), vc_m], axis=1)
            mb = _head_masks(BF16)
            mf = _head_masks(F32)
            lse_blk = lse_ref[...]
            for hg in range(B_HEADS // HG):
                sl = slice(hg * GW, (hg + 1) * GW)
                kcat = jnp.concatenate([kp_ref[:, sl], kc_ref[:, sl]], axis=0)
                vcat = jnp.concatenate([vp_ref[:, sl], vc_ref[:, sl]], axis=0)
                dof = do_ref[:, sl]
                dob = dof.astype(BF16)
                dd = dof * out_ref[:, sl].astype(F32)
                q4 = q_ref[:, sl]
                q4m = jnp.concatenate([q4 * mb[h] for h in range(HG)], axis=0)
                do4m = jnp.concatenate([dob * mb[h] for h in range(HG)], axis=0)
                s4 = _dot_nt(q4m, kcat)
                dp4 = _dot_nt(do4m, vcat)
                ps, dss = [], []
                for h in range(HG):
                    hh = hg * HG + h
                    rows = slice(h * BLK, (h + 1) * BLK)
                    Dh = jnp.sum(dd * mf[h], axis=-1, keepdims=True)
                    s = jnp.where(valid, s4[rows] * scale + b_ref[0, hh] - lse_blk[:, hh:hh + 1], -jnp.inf)
                    p = jnp.exp(s)
                    ds = p * (dp4[rows] - Dh)
                    db_ref[hh] += ds
                    ps.append(p.astype(BF16))
                    dss.append(ds.astype(BF16))
                p4 = jnp.concatenate(ps, axis=0)
                ds4 = jnp.concatenate(dss, axis=0)
                dq4 = _dot(ds4, kcat)
                acc = jnp.zeros((BLK, GW), F32)
                for h in range(HG):
                    acc = acc + dq4[h * BLK:(h + 1) * BLK] * mf[h]
                dq_ref[:, sl] = (acc * scale).astype(BF16)
                dkc = _dot_tn(ds4, q4m) * scale
                dvc = _dot_tn(p4, do4m)
                dk_ref[:, sl] = (ck_s[:, sl] + dkc[0:BLK]).astype(BF16)
                dv_ref[:, sl] = (cv_s[:, sl] + dvc[0:BLK]).astype(BF16)
                ck_s[:, sl] = dkc[BLK:2 * BLK]
                cv_s[:, sl] = dvc[BLK:2 * BLK]

    last = nb - 1
    cur = lambda r, n: (jnp.minimum(n, last), r)
    prev = lambda r, n: (jnp.clip(n - 1, 0, last), r)
    row = lambda im: pl.BlockSpec((BLK, 1024), im)
    k2, v2 = kg.reshape(S2, W), vg.reshape(S2, W)
    dq, dk, dv, dbias = pl.pallas_call(
        body, name=name, grid=(dil, nb + 1),
        in_specs=[row(cur), row(cur), row(prev), row(cur), row(prev),
                  pl.BlockSpec((1, B_HEADS, BLK, 2 * BLK), lambda r, n: (g, 0, 0, 0)),
                  row(cur), row(cur), pl.BlockSpec((BLK, 128), cur)],
        out_specs=[row(cur), row(prev), row(prev), pl.BlockSpec((B_HEADS, BLK, 2 * BLK), lambda r, n: (0, 0, 0))],
        out_shape=[jax.ShapeDtypeStruct((S2, W), BF16)] * 3 + [jax.ShapeDtypeStruct((B_HEADS, BLK, 2 * BLK), F32)],
        scratch_shapes=[pltpu.VMEM((BLK, 1024), F32), pltpu.VMEM((BLK, 1024), F32)],
        compiler_params=_cp(("arbitrary", "arbitrary")),
    )(qg.reshape(S2, W), k2, k2, v2, v2, bias, dout.reshape(S2, W), out.reshape(S2, W), lse.reshape(S2, dil * 128))
    return dq.reshape(S, 1024), dk.reshape(S, 1024), dv.reshape(S, 1024), dbias


def _slot(px, py, pc):
    return 4 * px + 2 * py + pc


def ag_weights(wb, ws):
    def body(wb_ref, ws_ref, ob_ref, os_ref, send_sems, recv_sems, local_sems):
        x, y, c = lax.axis_index("x"), lax.axis_index("y"), lax.axis_index("c")
        me, sibling = (x, y, c), (x, y, 1 - c)
        chips = [(1 - x, y), (x, 1 - y), (1 - x, 1 - y)]
        arrays = [(wb_ref, ob_ref), (ws_ref, os_ref)]

        def copy(a, k, block, to, from_input=False):
            src_in, out = arrays[a]
            dst = out.at[_slot(*block)]
            return pltpu.make_async_remote_copy(
                src_ref=src_in if from_input else dst, dst_ref=dst,
                send_sem=send_sems.at[7 * a + k], recv_sem=recv_sems.at[7 * a + k],
                device_id=to, device_id_type=MESH)

        mine = [pltpu.make_async_copy(arrays[a][0], arrays[a][1].at[_slot(*me)], local_sems.at[a]) for a in range(2)]
        for cp in mine:
            cp.start()
        first = []
        for a in range(2):
            first.append(copy(a, 0, me, sibling, True))
            first += [copy(a, 1 + j, me, (*chip, c), True) for j, chip in enumerate(chips)]
        for cp in first:
            cp.start()
        passed = []
        for a in range(2):
            for j, chip in enumerate(chips):
                copy(a, 1 + j, (*chip, c), me).wait_recv()
                fw = copy(a, 4 + j, (*chip, c), sibling)
                fw.start()
                passed.append(fw)
        for a in range(2):
            copy(a, 0, sibling, me).wait_recv()
            for j, chip in enumerate(chips):
                copy(a, 4 + j, (*chip, 1 - c), me).wait_recv()
        for cp in first + passed:
            cp.wait_send()
        for cp in mine:
            cp.wait()

    any_spec = pl.BlockSpec(memory_space=pl.ANY)
    return pl.pallas_call(
        body, name="ag_weights",
        in_specs=[any_spec, any_spec], out_specs=[any_spec, any_spec],
        out_shape=[jax.ShapeDtypeStruct((N_DEV,) + wb.shape, wb.dtype), jax.ShapeDtypeStruct((N_DEV,) + ws.shape, ws.dtype)],
        scratch_shapes=[pltpu.SemaphoreType.DMA((14,)), pltpu.SemaphoreType.DMA((14,)), pltpu.SemaphoreType.DMA((2,))],
    )(wb, ws)


def rs_sibling(gpack, spack):
    def body(g_ref, s_ref, rb_ref, sa_ref, send_sems, recv_sems, local_sem):
        x, y, c = lax.axis_index("x"), lax.axis_index("y"), lax.axis_index("c")
        my = _slot(x, y, c)
        mine = pltpu.make_async_copy(s_ref, sa_ref.at[my], local_sem)
        mine.start()
        sends, recvs = [], []
        for j in range(4):
            both = dict(dst_ref=rb_ref.at[j], send_sem=send_sems.at[j], recv_sem=recv_sems.at[j],
                        device_id=(x, y, 1 - c), device_id_type=MESH)
            sends.append(pltpu.make_async_remote_copy(src_ref=g_ref.at[2 * j + 1 - c], **both))
            recvs.append(sends[-1])
        for k in range(1, N_DEV):
            peer = (1 - x if k & 4 else x, 1 - y if k & 2 else y, 1 - c if k & 1 else c)
            sems = dict(send_sem=send_sems.at[3 + k], recv_sem=recv_sems.at[3 + k], device_id=peer, device_id_type=MESH)
            sends.append(pltpu.make_async_remote_copy(src_ref=s_ref, dst_ref=sa_ref.at[my], **sems))
            recvs.append(pltpu.make_async_remote_copy(src_ref=s_ref, dst_ref=sa_ref.at[_slot(*peer)], **sems))
        for cp in sends:
            cp.start()
        for cp in recvs:
            cp.wait_recv()
        for cp in sends:
            cp.wait_send()
        mine.wait()

    any_spec = pl.BlockSpec(memory_space=pl.ANY)
    return pl.pallas_call(
        body, name="rs_sibling",
        in_specs=[any_spec, any_spec], out_specs=[any_spec, any_spec],
        out_shape=[jax.ShapeDtypeStruct((4,) + gpack.shape[1:], gpack.dtype),
                   jax.ShapeDtypeStruct((N_DEV,) + spack.shape, spack.dtype)],
        scratch_shapes=[pltpu.SemaphoreType.DMA((11,)), pltpu.SemaphoreType.DMA((11,)), pltpu.SemaphoreType.DMA],
    )(gpack, spack)


def pair_add(a, b, *, name, tr):
    R = a.shape[1]

    def body(a_ref, b_ref, o_ref):
        o_ref[...] = (a_ref[...].astype(F32) + b_ref[...].astype(F32)).astype(BF16)

    blk = pl.BlockSpec((4, tr, 1024), lambda i: (0, i, 0))
    return pl.pallas_call(
        body, name=name, grid=(R // tr,), in_specs=[blk, blk], out_specs=blk,
        out_shape=jax.ShapeDtypeStruct(a.shape, BF16), compiler_params=_cp(("parallel",)),
    )(a, b)


def rs_chips(part):
    def body(p_ref, rb_ref, send_sems, recv_sems, local_sem):
        x, y, c = lax.axis_index("x"), lax.axis_index("y"), lax.axis_index("c")
        jm = 2 * x + y
        mine = pltpu.make_async_copy(p_ref.at[jm], rb_ref.at[jm], local_sem)
        mine.start()
        sends, recvs = [], []
        for k in range(1, 4):
            px, py = (1 - x if k & 2 else x), (1 - y if k & 1 else y)
            sems = dict(send_sem=send_sems.at[k - 1], recv_sem=recv_sems.at[k - 1], device_id=(px, py, c), device_id_type=MESH)
            sends.append(pltpu.make_async_remote_copy(src_ref=p_ref.at[2 * px + py], dst_ref=rb_ref.at[jm], **sems))
            recvs.append(pltpu.make_async_remote_copy(src_ref=p_ref.at[jm], dst_ref=rb_ref.at[2 * px + py], **sems))
        for cp in sends:
            cp.start()
        for cp in recvs:
            cp.wait_recv()
        for cp in sends:
            cp.wait_send()
        mine.wait()

    any_spec = pl.BlockSpec(memory_space=pl.ANY)
    return pl.pallas_call(
        body, name="rs_chips", in_specs=[any_spec], out_specs=any_spec,
        out_shape=jax.ShapeDtypeStruct(part.shape, part.dtype),
        scratch_shapes=[pltpu.SemaphoreType.DMA((3,)), pltpu.SemaphoreType.DMA((3,)), pltpu.SemaphoreType.DMA],
    )(part)


def reduce_adam(parts, w, m, v, *, name, tr):
    R = w.shape[0]
    n_parts = parts.shape[0]
    assert R % tr == 0
    c1 = 1.0 - ADAM_B1 ** ADAM_STEP
    c2 = 1.0 - ADAM_B2 ** ADAM_STEP

    def body(p_ref, w_ref, m_ref, v_ref, g_ref, d_ref, mo_ref, vo_ref):
        g = p_ref[0].astype(F32)
        for i in range(1, n_parts):
            g = g + p_ref[i].astype(F32)
        mn = ADAM_B1 * m_ref[...] + (1.0 - ADAM_B1) * g
        vn = ADAM_B2 * v_ref[...] + (1.0 - ADAM_B2) * (g * g)
        g_ref[...] = g
        mo_ref[...] = mn
        vo_ref[...] = vn
        d_ref[...] = -ADAM_LR * ((mn / c1) / (jnp.sqrt(vn / c2) + ADAM_EPS) + ADAM_WD * w_ref[...])

    row = pl.BlockSpec((tr, 1024), lambda i: (i, 0))
    return pl.pallas_call(
        body, name=name, grid=(R // tr,),
        in_specs=[pl.BlockSpec((n_parts, tr, 1024), lambda i: (0, i, 0)), row, row, row],
        out_specs=[row] * 4,
        out_shape=[jax.ShapeDtypeStruct((R, 1024), F32)] * 4,
        compiler_params=_cp(("parallel",)),
    )(parts, w, m, v)


BIG = (("a_w_in", 385, 400), ("a_w_out", 128, 128), ("w_kv", 768, 768), ("b_w_q", 384, 384),
       ("b_w_out", 128, 128), ("f_w_up", 1408, 1408), ("f_w_down", 704, 704))
SMALL_SHARDED = (("a_norm_g", 128), ("a_hnorm_g", 128), ("f_conv_w", 4224))
SMALL_ROWS = 48
PACK_ROWS = sum(b[2] for b in BIG) + SMALL_ROWS
REPL = (("kv_norm_g", 1024, 1), ("b_norm_g", 1024, 1), ("f_norm_g", 2048, 2), ("f_conv_b", 11264, 11),
        ("final_norm_g", 1024, 1), ("rel_bias", 1536, 2), ("a_b_if", 8, 1))
REPL_ROWS = 24
LOSS_ROW = 19


def _rows(a, rows, padded):
    a = a.reshape(rows, 1024)
    return a if padded == rows else jnp.pad(a, ((0, padded - rows), (0, 0)))


def pack_shards(t, dtype, with_small):
    parts = [_rows(t[n].astype(dtype), r, p) for n, r, p in BIG]
    if with_small:
        flat = jnp.concatenate([t[n].astype(dtype).reshape(-1) for n, _ in SMALL_SHARDED])
        parts.append(jnp.pad(flat, (0, SMALL_ROWS * 1024 - flat.shape[0])).reshape(SMALL_ROWS, 1024))
    return jnp.concatenate(parts, axis=0)


def unpack_shards(pack, shapes):
    out = {}
    r0 = 0
    for n, r, p in BIG:
        out[n] = pack[r0:r0 + r].reshape(shapes[n])
        r0 += p
    flat = pack[r0:r0 + SMALL_ROWS].reshape(-1)
    e0 = 0
    for n, e in SMALL_SHARDED:
        out[n] = flat[e0:e0 + e].reshape(shapes[n])
        e0 += e
    return out


def pack_repl(t):
    parts = []
    for n, e, r in REPL:
        parts.append(jnp.pad(t[n].astype(F32).reshape(-1), (0, r * 1024 - e)))
    rows = sum(r for _, _, r in REPL)
    parts.append(jnp.zeros(((REPL_ROWS - rows) * 1024,), F32))
    return jnp.concatenate(parts).reshape(REPL_ROWS, 1024)


def unpack_repl(pack, shapes):
    out = {}
    r0 = 0
    for n, e, r in REPL:
        out[n] = pack[r0:r0 + r].reshape(-1)[:e].reshape(shapes[n])
        r0 += r
    return out


def ff_blocks(a):
    b = [a[..., i * CONV_TC:(i + 1) * CONV_TC] for i in range(4)]
    return jnp.concatenate([b[0], b[2], b[1], b[3]], axis=-1)


def split_cols(full, n):
    lead = full.shape[:-1]
    return jnp.moveaxis(full.reshape(lead + (N_DEV, n)), -2, 0)


def join_cols(parts):
    t = jnp.moveaxis(parts, 0, -2)
    return t.reshape(t.shape[:-2] + (t.shape[-2] * t.shape[-1],))


def kernel(x, a_norm_g, a_w_in, a_b_if, a_hnorm_g, a_w_out, kv_norm_g, w_kv, b_norm_g, b_w_q, b_w_out, rel_bias, f_norm_g, f_w_up, f_conv_w, f_conv_b, f_w_down, final_norm_g, loss_target, m_a_norm_g, m_a_w_in, m_a_b_if, m_a_hnorm_g, m_a_w_out, m_kv_norm_g, m_w_kv, m_b_norm_g, m_b_w_q, m_b_w_out, m_rel_bias, m_f_norm_g, m_f_w_up, m_f_conv_w, m_f_conv_b, m_f_w_down, m_final_norm_g, v_a_norm_g, v_a_w_in, v_a_b_if, v_a_hnorm_g, v_a_w_out, v_kv_norm_g, v_w_kv, v_b_norm_g, v_b_w_q, v_b_w_out, v_rel_bias, v_f_norm_g, v_f_w_up, v_f_conv_w, v_f_conv_b, v_f_w_down, v_final_norm_g):
    names = ["a_norm_g", "a_w_in", "a_b_if", "a_hnorm_g", "a_w_out", "kv_norm_g", "w_kv", "b_norm_g", "b_w_q", "b_w_out",
             "rel_bias", "f_norm_g", "f_w_up", "f_conv_w", "f_conv_b", "f_w_down", "final_norm_g"]
    w = dict(zip(names, (a_norm_g, a_w_in, a_b_if, a_hnorm_g, a_w_out, kv_norm_g, w_kv, b_norm_g, b_w_q, b_w_out,
                         rel_bias, f_norm_g, f_w_up, f_conv_w, f_conv_b, f_w_down, final_norm_g)))
    mom = dict(zip(names, (m_a_norm_g, m_a_w_in, m_a_b_if, m_a_hnorm_g, m_a_w_out, m_kv_norm_g, m_w_kv, m_b_norm_g, m_b_w_q,
                           m_b_w_out, m_rel_bias, m_f_norm_g, m_f_w_up, m_f_conv_w, m_f_conv_b, m_f_w_down, m_final_norm_g)))
    vel = dict(zip(names, (v_a_norm_g, v_a_w_in, v_a_b_if, v_a_hnorm_g, v_a_w_out, v_kv_norm_g, v_w_kv, v_b_norm_g, v_b_w_q,
                           v_b_w_out, v_rel_bias, v_f_norm_g, v_f_w_up, v_f_conv_w, v_f_conv_b, v_f_w_down, v_final_norm_g)))
    shapes = {n: w[n].shape for n in names}
    S = x.shape[1]
    assert x.shape[0] == 1 and S % (16 * BLK) == 0 and S % 1024 == 0
    X0 = x.reshape(S, D)
    target = loss_target.reshape(S, D)

    wb_all, ws_all = ag_weights(pack_shards(w, BF16, False),
                                pack_shards(w, F32, True)[PACK_ROWS - SMALL_ROWS:])
    seg = {}
    r0 = 0
    for n, r, p in BIG:
        seg[n] = wb_all[:, r0:r0 + r]
        r0 += p
    W_in = join_cols(seg["a_w_in"].reshape(N_DEV, D, 385))
    W_in = jnp.concatenate([jnp.pad(W_in[:, :3076], ((0, 0), (0, 124))),
                            jnp.pad(W_in[:, 3076:3080], ((0, 0), (0, 124)))], axis=1)
    W_out = seg["a_w_out"].reshape(1024, D)
    W_kv = join_cols(seg["w_kv"].reshape(N_DEV, D, 768))
    W_q = join_cols(seg["b_w_q"].reshape(N_DEV, D, 384))
    W_bout = seg["b_w_out"].reshape(1024, D)
    W_up = join_cols(seg["f_w_up"].reshape(N_DEV, 2, D, 704))
    W_down = jnp.moveaxis(seg["f_w_down"].reshape(N_DEV, 2, 352, D), 0, 1).reshape(2, D_FF, D)
    sflat = ws_all.reshape(N_DEV, SMALL_ROWS * 1024)
    g_a = sflat[:, 0:128].reshape(1, D)
    g_h = jnp.moveaxis(sflat[:, 128:256].reshape(N_DEV, A_HEADS, 32), 0, 1).reshape(1, A_HEADS * A_V)
    conv_w = ff_blocks(join_cols(sflat[:, 256:256 + 4224].reshape(N_DEV, 2, 3, 704)))
    conv_b = ff_blocks(f_conv_b)
    W_up = ff_blocks(W_up)
    bi = jnp.pad(a_b_if[:, :A_HEADS], ((0, 0), (0, 128 - A_HEADS)))
    bfg = jnp.pad(a_b_if[:, A_HEADS:], ((0, 0), (0, 128 - A_HEADS)))
    buckets = jnp.asarray(_bucket_tables())

    (xn_a,) = rms_fwd(X0, [g_a], name="rms_a")
    z = mm(xn_a, W_in, name="mm_a_in", out_dtype=F32, tn=1664)
    h, cst, nst = mlstm_fwd(z, bi, bfg, name="mlstm_fwd")
    X1, hg = ao_fwd(h, z, g_h, W_out, X0, name="ao_fwd")

    def ffn_fwd(X, l, tag):
        (xn,) = rms_fwd(X, [f_norm_g[l:l + 1]], name="rms_f" + tag)
        u, a = mm_up_conv(xn, W_up[l], conv_w[l], conv_b[l:l + 1], name="mm_up_conv" + tag)
        Xn = mm(a, W_down[l], name="mm_down" + tag, out_dtype=F32, tn=1024, tm=512, res=X)
        return Xn, (xn, u, a)

    X2, sav0 = ffn_fwd(X1, 0, "0")

    xkn, xbn = rms_fwd(X2, [kv_norm_g.reshape(1, D), b_norm_g], name="rms_kv_b")
    bias = bias_build(rel_bias, buckets, name="bias_build")
    col = lambda wmat, i: wmat[:, i * 1024:(i + 1) * 1024]
    qs, kk, vv, og, lg = [], [], [], [], []
    for g, (_, dil) in enumerate(B_GROUPS):
        qs.append(mm(xbn, col(W_q, g), name="mm_q%d" % g, out_dtype=BF16, tn=1024))
        kk.append(mm(xkn, col(W_kv, g), name="mm_k%d" % g, out_dtype=BF16, tn=1024))
        vv.append(mm(xkn, col(W_kv, N_GROUPS + g), name="mm_v%d" % g, out_dtype=BF16, tn=1024))
        o_, l_ = attn_fwd(qs[g], kk[g], vv[g], bias, g, dil, name="attn_fwd%d" % g)
        og.append(o_)
        lg.append(l_)
    att, lse = attn_merge(og, lg, name="attn_merge")
    X3 = mm(att, W_bout, name="mm_b_out", out_dtype=F32, tn=1024, res=X2)
    X4, sav1 = ffn_fwd(X3, 1, "1")

    dX4, d_final_g, loss_part = loss_head(X4, target, final_norm_g.reshape(1, D), name="loss_head")

    def ffn_bwd(X, dXn, l, sav, tag):
        xn, u, a = sav
        dW_down = mm_tn(a, dXn, name="tn_down" + tag, tn=1024, tk=512)
        duc, dwb = mm_da_conv(dXn, W_down[l].T, u, conv_w[l], conv_b[l:l + 1], name="mm_da_conv" + tag)
        dW_up, du = tn_up_conv(xn, duc, conv_w[l], name="tn_up_conv" + tag)
        dxn = mm(du, W_up[l].T, name="mm_dxn_f" + tag, out_dtype=F32, tn=1024, tm=512)
        dX, (dg,) = rms_bwd(X, dXn, [(dxn, f_norm_g[l:l + 1])], name="rms_bwd_f" + tag)
        return dX, dW_down, dW_up, dwb, dg

    dX3, dWd1, dWu1, dwb1, dgf1 = ffn_bwd(X3, dX4, 1, sav1, "1")

    dW_bout = mm_tn(att, dX3, name="tn_b_out", tn=1024)
    dout = mm(dX3, W_bout.T, name="mm_dout", out_dtype=F32, tn=1024)
    dqs, dks, dvs, dbias = [], [], [], []
    for g, (_, dil) in enumerate(B_GROUPS):
        dq_, dk_, dv_, db_ = attn_bwd(qs[g], kk[g], vv[g], bias, dout, att, lse, g, dil, name="attn_bwd%d" % g)
        dqs.append(dq_)
        dks.append(dk_)
        dvs.append(dv_)
        dbias.append(db_)
    d_rel = bias_grad(jnp.stack(dbias), buckets, name="bias_grad")[:, :N_GROUPS * B_HEADS]
    dW_q = jnp.concatenate([mm_tn(xbn, d_, name="tn_q%d" % g, tn=1024) for g, d_ in enumerate(dqs)], axis=1)
    dW_kv = jnp.concatenate([mm_tn(xkn, d_, name="tn_kv%d" % i, tn=1024) for i, d_ in enumerate(dks + dvs)], axis=1)
    W_qT, W_kvT = W_q.T, W_kv.T
    rows = lambda wmat, i: wmat[i * 1024:(i + 1) * 1024]
    dxn_b = mm_sum(dqs, [rows(W_qT, g) for g in range(N_GROUPS)], name="mm_dxn_b")
    dxn_kv = mm_sum(dks + dvs, [rows(W_kvT, i) for i in range(2 * N_GROUPS)], name="mm_dxn_kv")
    dX2, (dg_kv, dg_b) = rms_bwd(X2, dX3, [(dxn_kv, kv_norm_g.reshape(1, D)), (dxn_b, b_norm_g)], name="rms_bwd_kv_b")

    dX1, dWd0, dWu0, dwb0, dgf0 = ffn_bwd(X1, dX2, 0, sav0, "0")

    dW_out = mm_tn(hg, dX1, name="tn_a_out", tn=1024)
    dh, dzo, dgh = ao_bwd(dX1, W_out.T, h, z, g_h, name="ao_bwd")
    dz, db_if = mlstm_bwd(z, bi, bfg, cst, nst, dh, dzo, name="mlstm_bwd")
    dW_in = mm_tn(xn_a, dz, name="tn_a_in", tn=1664)
    dW_in = jnp.concatenate([dW_in[:, :3076], dW_in[:, Z_GF:Z_GF + 4]], axis=1)
    dxn_a = mm(dz, W_in.T, name="mm_dxn_a", out_dtype=F32, tn=1024, tm=512)
    dX0, (dg_a,) = rms_bwd(X0, dX1, [(dxn_a, g_a)], name="rms_bwd_a")

    dWu = ff_blocks(jnp.stack([dWu0, dWu1]))
    dWd = jnp.stack([dWd0, dWd1])
    dwb = ff_blocks(jnp.stack([dwb0, dwb1]))
    slots = [
        jnp.pad(split_cols(dW_in, 385).reshape(N_DEV, 385, 1024), ((0, 0), (0, 15), (0, 0))),
        dW_out.reshape(N_DEV, 128, 1024),
        split_cols(dW_kv, 768).reshape(N_DEV, 768, 1024),
        split_cols(dW_q, 384).reshape(N_DEV, 384, 1024),
        dW_bout.reshape(N_DEV, 128, 1024),
        split_cols(dWu, 704).reshape(N_DEV, 1408, 1024),
        jnp.moveaxis(dWd.reshape(2, N_DEV, 352, D), 1, 0).reshape(N_DEV, 704, 1024),
    ]
    small = jnp.concatenate([
        dg_a.reshape(N_DEV, 128),
        split_cols(dgh.reshape(A_HEADS, A_V), 32).reshape(N_DEV, 128),
        split_cols(dwb[:, 0:3], 704).reshape(N_DEV, 4224)], axis=1)
    slots.append(jnp.pad(small, ((0, 0), (0, SMALL_ROWS * 1024 - small.shape[1]))).reshape(N_DEV, SMALL_ROWS, 1024))
    gpack = jnp.concatenate([t.astype(BF16) for t in slots], axis=1)
    repl_g = {"kv_norm_g": dg_kv, "b_norm_g": dg_b, "f_norm_g": jnp.concatenate([dgf0, dgf1]),
              "f_conv_b": dwb[:, 3], "final_norm_g": d_final_g, "rel_bias": d_rel,
              "a_b_if": jnp.concatenate([db_if[0, :A_HEADS], db_if[1, :A_HEADS]])}
    spack = pack_repl(repl_g)
    spack = spack.at[LOSS_ROW, 0].set(loss_part[0, 0])

    from_sibling, sparts = rs_sibling(gpack, spack)
    own = lax.dynamic_index_in_dim(gpack.reshape(4, 2, PACK_ROWS, 1024), lax.axis_index("c"), axis=1, keepdims=False)
    parts = rs_chips(pair_add(own, from_sibling, name="rs_pair_add", tr=PACK_ROWS // 8))
    gb, db, mb, vb = reduce_adam(parts, pack_shards(w, F32, True), pack_shards(mom, F32, True),
                                 pack_shards(vel, F32, True), name="reduce_adam_big", tr=PACK_ROWS // 8)
    gs, ds, ms, vs = reduce_adam(sparts, pack_repl(w), pack_repl(mom), pack_repl(vel), name="reduce_adam_small", tr=REPL_ROWS)
    loss = gs[LOSS_ROW, 0]

    def collect(big, sm):
        t = unpack_shards(big, shapes)
        t.update(unpack_repl(sm, shapes))
        return [t[n] for n in names]

    return (loss, dX0.reshape(1, S, D), *collect(gb, gs), *collect(db, ds), *collect(mb, ms), *collect(vb, vs))
```

```python
import functools
import math

import numpy as np
import jax
import jax.numpy as jnp
from jax import lax
from jax.experimental import pallas as pl
from jax.experimental.pallas import tpu as pltpu

F32 = jnp.float32
BF16 = jnp.bfloat16
HIGHEST = lax.Precision.HIGHEST
MESH = pl.DeviceIdType.MESH

D = 1024
A_HEADS = 4
A_QK = 128
A_V = 256
SOFTCAP = 15.0
N_GROUPS = 3
B_GROUPS = ((128, 1), (512, 4), (2048, 16))
B_HEADS = 16
B_DH = 64
BLK = 128
REL_BUCKETS = 32
REL_MAX_DIST = 2048
D_FF = 2816
EPS = 1e-6
ADAM_LR = 0.001
ADAM_B1 = 0.9
ADAM_B2 = 0.999
ADAM_EPS = 1e-08
ADAM_WD = 0.01
ADAM_STEP = 10

N_DEV = 8
V7X_VMEM_BYTES = 64 * 1024 * 1024
VMEM_LIMIT = V7X_VMEM_BYTES - 8 * 1024 * 1024
MLSTM_CHUNK = 256
Z_W = 3328
Z_GI = 3072
Z_GF = 3200


def _cp(sem):
    return pltpu.CompilerParams(dimension_semantics=sem, vmem_limit_bytes=VMEM_LIMIT)


def _dot(a, b, **kw):
    return jnp.dot(a, b, preferred_element_type=F32, **kw)


def _dot_nt(a, b):
    return lax.dot_general(a, b, (((1,), (1,)), ((), ())), preferred_element_type=F32)


def _dot_tn(a, b):
    return lax.dot_general(a, b, (((0,), (0,)), ((), ())), preferred_element_type=F32)


def _dot_split(a, b01):
    hi = a.astype(BF16)
    lo = (a - hi.astype(F32)).astype(BF16)
    return _dot(hi, b01) + _dot(lo, b01)


def mm(a, b, *, name, out_dtype, tn, tm=1024, res=None):
    M, K = a.shape
    N = b.shape[1]
    assert M % tm == 0 and N % tn == 0 and b.shape[0] == K

    def body(a_ref, b_ref, *rest):
        o_ref = rest[-1]
        acc = _dot(a_ref[...].astype(BF16), b_ref[...])
        if res is not None:
            acc = acc + rest[0][...]
        o_ref[...] = acc.astype(out_dtype)

    in_specs = [pl.BlockSpec((tm, K), lambda j, i: (i, 0)), pl.BlockSpec((K, tn), lambda j, i: (0, j))]
    args = [a, b]
    if res is not None:
        in_specs.append(pl.BlockSpec((tm, tn), lambda j, i: (i, j)))
        args.append(res)
    return pl.pallas_call(
        body, name=name, grid=(N // tn, M // tm), in_specs=in_specs,
        out_specs=pl.BlockSpec((tm, tn), lambda j, i: (i, j)),
        out_shape=jax.ShapeDtypeStruct((M, N), out_dtype),
        compiler_params=_cp(("parallel", "parallel")),
    )(*args)


def mm_sum(a_list, b_list, *, name, tm=512):
    M, K = a_list[0].shape
    N = b_list[0].shape[1]
    n = len(a_list)
    assert M % tm == 0

    def body(*refs):
        o_ref = refs[-1]
        acc = _dot(refs[0][...], refs[n][...])
        for i in range(1, n):
            acc = acc + _dot(refs[i][...], refs[n + i][...])
        o_ref[...] = acc

    return pl.pallas_call(
        body, name=name, grid=(M // tm,),
        in_specs=[pl.BlockSpec((tm, K), lambda i: (i, 0))] * n + [pl.BlockSpec((K, N), lambda i: (0, 0))] * n,
        out_specs=pl.BlockSpec((tm, N), lambda i: (i, 0)),
        out_shape=jax.ShapeDtypeStruct((M, N), F32),
        compiler_params=_cp(("parallel",)),
    )(*a_list, *b_list)


def mm_tn(a, b, *, name, tn, tk=1024):
    S, Kd = a.shape
    N = b.shape[1]
    assert S % tk == 0 and N % tn == 0 and b.shape[0] == S

    def body(a_ref, b_ref, o_ref):
        @pl.when(pl.program_id(1) == 0)
        def _():
            o_ref[...] = jnp.zeros_like(o_ref)

        o_ref[...] += _dot_tn(a_ref[...].astype(BF16), b_ref[...].astype(BF16))

    return pl.pallas_call(
        body, name=name, grid=(N // tn, S // tk),
        in_specs=[pl.BlockSpec((tk, Kd), lambda j, k: (k, 0)), pl.BlockSpec((tk, tn), lambda j, k: (k, j))],
        out_specs=pl.BlockSpec((Kd, tn), lambda j, k: (0, j)),
        out_shape=jax.ShapeDtypeStruct((Kd, N), F32),
        compiler_params=_cp(("parallel", "arbitrary")),
    )(a, b)


def rms_fwd(x, gains, *, name, tm=1024):
    S = x.shape[0]
    n = len(gains)

    def body(x_ref, *rest):
        xf = x_ref[...]
        y = xf * lax.rsqrt(jnp.mean(xf * xf, axis=-1, keepdims=True) + EPS)
        for i in range(n):
            rest[n + i][...] = (y * rest[i][...]).astype(BF16)

    return pl.pallas_call(
        body, name=name, grid=(S // tm,),
        in_specs=[pl.BlockSpec((tm, D), lambda i: (i, 0))] + [pl.BlockSpec((1, D), lambda i: (0, 0))] * n,
        out_specs=[pl.BlockSpec((tm, D), lambda i: (i, 0))] * n,
        out_shape=[jax.ShapeDtypeStruct((S, D), BF16)] * n,
        compiler_params=_cp(("parallel",)),
    )(x, *gains)


def mm_rms_bwd(a_list, b_list, g, x, dres, extra=(), *, name, tm=512):
    S = x.shape[0]
    n = len(a_list)
    ne = len(extra)

    def body(*refs):
        a_refs, b_refs = refs[:n], refs[n:2 * n]
        g_ref, x_ref, dres_ref = refs[2 * n:2 * n + 3]
        e_refs = refs[2 * n + 3:2 * n + 3 + 2 * ne]
        dx_ref = refs[2 * n + 3 + 2 * ne]
        dg_refs = refs[2 * n + 4 + 2 * ne:]

        @pl.when(pl.program_id(0) == 0)
        def _():
            for r in dg_refs:
                r[...] = jnp.zeros_like(r)

        acc = _dot(a_refs[0][...], b_refs[0][...])
        for i in range(1, n):
            acc = acc + _dot(a_refs[i][...], b_refs[i][...])
        xf = x_ref[...]
        r = lax.rsqrt(jnp.mean(xf * xf, axis=-1, keepdims=True) + EPS)
        xhat = xf * r
        total = dres_ref[...]
        branches = [(acc, g_ref[...])] + [(e_refs[2 * i][...], e_refs[2 * i + 1][...]) for i in range(ne)]
        for i, (dy, gg) in enumerate(branches):
            dg_refs[i][...] += jnp.sum(dy * xhat, axis=0, keepdims=True)
            dyg = dy * gg
            total = total + r * (dyg - xhat * jnp.mean(dyg * xhat, axis=-1, keepdims=True))
        dx_ref[...] = total

    row = pl.BlockSpec((tm, D), lambda i: (i, 0))
    vec = pl.BlockSpec((1, D), lambda i: (0, 0))
    in_specs = ([pl.BlockSpec((tm, a.shape[1]), lambda i: (i, 0)) for a in a_list]
                + [pl.BlockSpec(b.shape, lambda i: (0, 0)) for b in b_list] + [vec, row, row])
    args = list(a_list) + list(b_list) + [g, x, dres]
    for dxn_e, g_e in extra:
        in_specs += [row, vec]
        args += [dxn_e, g_e]
    outs = pl.pallas_call(
        body, name=name, grid=(S // tm,), in_specs=in_specs,
        out_specs=[row] + [vec] * (1 + ne),
        out_shape=[jax.ShapeDtypeStruct((S, D), F32)] + [jax.ShapeDtypeStruct((1, D), F32)] * (1 + ne),
        compiler_params=_cp(("arbitrary",)),
    )(*args)
    return outs[0], outs[1:]


def loss_head(x, target, g, *, name, tm=512):
    S = x.shape[0]

    def body(x_ref, t_ref, g_ref, dx_ref, dg_ref, loss_ref):
        @pl.when(pl.program_id(0) == 0)
        def _():
            dg_ref[...] = jnp.zeros_like(dg_ref)
            loss_ref[...] = jnp.zeros_like(loss_ref)

        xf = x_ref[...]
        gg = g_ref[...]
        r = lax.rsqrt(jnp.mean(xf * xf, axis=-1, keepdims=True) + EPS)
        xhat = xf * r
        e = xhat * gg - t_ref[...]
        per_tok = jnp.mean(e * e, axis=-1, keepdims=True)
        loss_ref[...] += 0.5 * jnp.sum(per_tok, axis=0, keepdims=True)
        dy = e * (1.0 / D)
        dg_ref[...] += jnp.sum(dy * xhat, axis=0, keepdims=True)
        dyg = dy * gg
        dx_ref[...] = r * (dyg - xhat * jnp.mean(dyg * xhat, axis=-1, keepdims=True))

    row = pl.BlockSpec((tm, D), lambda i: (i, 0))
    vec = pl.BlockSpec((1, D), lambda i: (0, 0))
    return pl.pallas_call(
        body, name=name, grid=(S // tm,),
        in_specs=[row, row, vec],
        out_specs=[row, vec, pl.BlockSpec((1, 128), lambda i: (0, 0))],
        out_shape=[jax.ShapeDtypeStruct((S, D), F32), jax.ShapeDtypeStruct((1, D), F32),
                   jax.ShapeDtypeStruct((1, 128), F32)],
        compiler_params=_cp(("arbitrary",)),
    )(x, target, g)


def _sigmoid(x):
    return 1.0 / (1.0 + jnp.exp(-x))


def _gates(z_ref, bi_ref, bf_ref):
    li = SOFTCAP * jnp.tanh((z_ref[:, Z_GI:Z_GI + 128] + bi_ref[...]) * (1.0 / SOFTCAP))
    scf = SOFTCAP * jnp.tanh((z_ref[:, Z_GF:Z_GF + 128] + bf_ref[...]) * (1.0 / SOFTCAP))
    lf = jnp.minimum(scf, 0.0) - jnp.log(1.0 + jnp.exp(-jnp.abs(scf)))
    return li, scf, lf


def _tri(L, lower):
    r = lax.broadcasted_iota(jnp.int32, (L, L), 0)
    c = lax.broadcasted_iota(jnp.int32, (L, L), 1)
    return (r >= c) if lower else (r <= c)


def mlstm_fwd(z, bi, bf, *, name):
    S = z.shape[0]
    L = MLSTM_CHUNK
    NC = S // L
    scale = A_QK ** -0.5

    def body(z_ref, bi_ref, bf_ref, h_ref, cst_ref, nst_ref, C_s, n_s):
        @pl.when(pl.program_id(0) == 0)
        def _():
            C_s[...] = jnp.zeros_like(C_s)
            n_s[...] = jnp.zeros_like(n_s)

        li, _, lf = _gates(z_ref, bi_ref, bf_ref)
        causal = _tri(L, True)
        b = _dot(causal.astype(F32), lf, precision=HIGHEST)
        liT = li.T
        bT = b.T
        cst_ref[0] = C_s[...].astype(BF16)
        nst_ref[0] = n_s[...]
        for h in range(A_HEADS):
            q = z_ref[:, h * A_QK:(h + 1) * A_QK] * scale
            k = z_ref[:, 512 + h * A_QK:512 + (h + 1) * A_QK]
            qb = q.astype(BF16)
            kb = k.astype(BF16)
            vb = z_ref[:, 1024 + h * A_V:1024 + (h + 1) * A_V].astype(BF16)
            a_col, b_col = li[:, h:h + 1], b[:, h:h + 1]
            a_row, b_row = liT[h:h + 1, :], bT[h:h + 1, :]
            Dm = jnp.exp(jnp.where(causal, b_col - b_row + a_row, -jnp.inf))
            A = _dot_nt(qb, kb) * Dm
            eb = jnp.exp(b_col)
            Ch = C_s[h]
            nh = n_s[h:h + 1, :]
            num = _dot(A.astype(BF16), vb) + eb * _dot(qb, Ch.astype(BF16))
            den = jnp.sum(A, axis=-1, keepdims=True) + eb * jnp.sum(q * nh, axis=-1, keepdims=True)
            h_ref[:, h * A_V:(h + 1) * A_V] = num / jnp.maximum(jnp.abs(den), 1.0)
            bL = b_col[L - 1:L, :]
            kw = jnp.exp(bL - b_col + a_col) * k
            decay = jnp.exp(bL)
            C_s[h] = decay * Ch + _dot_tn(kw.astype(BF16), vb)
            n_s[h:h + 1, :] = decay * nh + jnp.sum(kw, axis=0, keepdims=True)

    vec = pl.BlockSpec((1, 128), lambda c: (0, 0))
    return pl.pallas_call(
        body, name=name, grid=(NC,),
        in_specs=[pl.BlockSpec((L, Z_W), lambda c: (c, 0)), vec, vec],
        out_specs=[pl.BlockSpec((L, 1024), lambda c: (c, 0)),
                   pl.BlockSpec((1, A_HEADS, A_QK, A_V), lambda c: (c, 0, 0, 0)),
                   pl.BlockSpec((1, 8, 128), lambda c: (c, 0, 0))],
        out_shape=[jax.ShapeDtypeStruct((S, 1024), F32),
                   jax.ShapeDtypeStruct((NC, A_HEADS, A_QK, A_V), BF16),
                   jax.ShapeDtypeStruct((NC, 8, 128), F32)],
        scratch_shapes=[pltpu.VMEM((A_HEADS, A_QK, A_V), F32), pltpu.VMEM((8, 128), F32)],
        compiler_params=_cp(("arbitrary",)),
    )(z, bi, bf)


def mlstm_bwd(z, bi, bf, cst, nst, dh, dzo, *, name):
    S = z.shape[0]
    L = MLSTM_CHUNK
    NC = S // L
    scale = A_QK ** -0.5

    def body(z_ref, bi_ref, bf_ref, cst_ref, nst_ref, dh_ref, dzo_ref, dz_ref, db_ref, dC_s, dn_s):
        @pl.when(pl.program_id(0) == 0)
        def _():
            dC_s[...] = jnp.zeros_like(dC_s)
            dn_s[...] = jnp.zeros_like(dn_s)
            db_ref[...] = jnp.zeros_like(db_ref)

        li, scf, lf = _gates(z_ref, bi_ref, bf_ref)
        causal = _tri(L, True)
        b = _dot(causal.astype(F32), lf, precision=HIGHEST)
        liT = li.T
        bT = b.T
        lane = lax.broadcasted_iota(jnp.int32, (L, 128), 1)
        sub = lax.broadcasted_iota(jnp.int32, (128, L), 0)
        lane1 = lax.broadcasted_iota(jnp.int32, (1, 128), 1)
        Rm = jnp.zeros((L, 128), F32)
        KIm = jnp.zeros((L, 128), F32)
        csm = jnp.zeros((128, L), F32)
        Xm = jnp.zeros((1, 128), F32)
        for h in range(A_HEADS):
            q = z_ref[:, h * A_QK:(h + 1) * A_QK] * scale
            k = z_ref[:, 512 + h * A_QK:512 + (h + 1) * A_QK]
            qb = q.astype(BF16)
            kb = k.astype(BF16)
            vb = z_ref[:, 1024 + h * A_V:1024 + (h + 1) * A_V].astype(BF16)
            a_col, b_col = li[:, h:h + 1], b[:, h:h + 1]
            a_row, b_row = liT[h:h + 1, :], bT[h:h + 1, :]
            Dm = jnp.exp(jnp.where(causal, b_col - b_row + a_row, -jnp.inf))
            Sqk = _dot_nt(qb, kb)
            A = Sqk * Dm
            Ab = A.astype(BF16)
            eb = jnp.exp(b_col)
            Cb = cst_ref[0, h]
            nh = nst_ref[0, h:h + 1, :]
            num = _dot(Ab, vb) + eb * _dot(qb, Cb)
            den = jnp.sum(A, axis=-1, keepdims=True) + eb * jnp.sum(q * nh, axis=-1, keepdims=True)
            aden = jnp.abs(den)
            u = 1.0 / jnp.maximum(aden, 1.0)
            dhh = dh_ref[:, h * A_V:(h + 1) * A_V]
            dnum = dhh * u
            dden = jnp.where(aden > 1.0, -jnp.sum(dhh * num, axis=-1, keepdims=True) * u * u * jnp.sign(den), 0.0)
            dnb = dnum.astype(BF16)
            G = Dm * (_dot_nt(dnb, vb) + dden)
            Gb = G.astype(BF16)
            E = G * Sqk
            rs = jnp.sum(E, axis=-1, keepdims=True)
            cs = jnp.sum(E, axis=0, keepdims=True)
            dCh = dC_s[h]
            dnh = dn_s[h:h + 1, :]
            dCb = dCh.astype(BF16)
            bL = b_col[L - 1:L, :]
            wk = jnp.exp(bL - b_col + a_col)
            decay = jnp.exp(bL)
            dq_inter = eb * (_dot_nt(dnb, Cb) + dden * nh)
            dk_inter = wk * (_dot_nt(vb, dCb) + dnh)
            dq = _dot(Gb, kb) + dq_inter
            dk = _dot_tn(Gb, qb) + dk_inter
            dv = _dot_tn(Ab, dnb) + wk * _dot(kb, dCb)
            dz_ref[:, h * A_QK:(h + 1) * A_QK] = (dq * scale).astype(BF16)
            dz_ref[:, 512 + h * A_QK:512 + (h + 1) * A_QK] = dk.astype(BF16)
            dz_ref[:, 1024 + h * A_V:1024 + (h + 1) * A_V] = dv.astype(BF16)
            KI = jnp.sum(k * dk_inter, axis=-1, keepdims=True)
            R = rs + jnp.sum(q * dq_inter, axis=-1, keepdims=True)
            cross = (jnp.sum(jnp.sum(dCh * Cb.astype(F32), axis=0, keepdims=True), axis=1, keepdims=True)
                     + jnp.sum(dnh * nh, axis=1, keepdims=True))
            Xm = jnp.where(lane1 == h, decay * cross, Xm)
            Rm = jnp.where(lane == h, R, Rm)
            KIm = jnp.where(lane == h, KI, KIm)
            csm = jnp.where(sub == h, cs, csm)
            ebq = eb * q
            dC_s[h] = decay * dCh + _dot_tn(ebq.astype(BF16), dnb)
            dn_s[h:h + 1, :] = decay * dnh + jnp.sum(ebq * dden, axis=0, keepdims=True)
        dz_ref[:, 2048:3072] = dzo_ref[...]
        cs_col = csm.T
        da = cs_col + KIm
        rr = lax.broadcasted_iota(jnp.int32, (L, L), 0)
        cc = lax.broadcasted_iota(jnp.int32, (L, L), 1)
        dlf = (_dot((rr <= cc).astype(F32), Rm - cs_col, precision=HIGHEST)
               + _dot((rr > cc).astype(F32), KIm, precision=HIGHEST) + Xm)
        dpre_i = da * (1.0 - (li * (1.0 / SOFTCAP)) ** 2)
        dpre_f = dlf * (1.0 - _sigmoid(scf)) * (1.0 - (scf * (1.0 / SOFTCAP)) ** 2)
        dz_ref[:, Z_GI:Z_GI + 128] = dpre_i.astype(BF16)
        dz_ref[:, Z_GF:Z_GF + 128] = dpre_f.astype(BF16)
        db_ref[0:1, :] += jnp.sum(dpre_i, axis=0, keepdims=True)
        db_ref[1:2, :] += jnp.sum(dpre_f, axis=0, keepdims=True)

    vec = pl.BlockSpec((1, 128), lambda c: (0, 0))
    rev = lambda c: (NC - 1 - c, 0)
    return pl.pallas_call(
        body, name=name, grid=(NC,),
        in_specs=[pl.BlockSpec((L, Z_W), rev), vec, vec,
                  pl.BlockSpec((1, A_HEADS, A_QK, A_V), lambda c: (NC - 1 - c, 0, 0, 0)),
                  pl.BlockSpec((1, 8, 128), lambda c: (NC - 1 - c, 0, 0)),
                  pl.BlockSpec((L, 1024), rev), pl.BlockSpec((L, 1024), rev)],
        out_specs=[pl.BlockSpec((L, Z_W), rev), pl.BlockSpec((8, 128), lambda c: (0, 0))],
        out_shape=[jax.ShapeDtypeStruct((S, Z_W), BF16), jax.ShapeDtypeStruct((8, 128), F32)],
        scratch_shapes=[pltpu.VMEM((A_HEADS, A_QK, A_V), F32), pltpu.VMEM((8, 128), F32)],
        compiler_params=_cp(("arbitrary",)),
    )(z, bi, bf, cst, nst, dh, dzo)


def ao_fwd(h, z, gh, w_out, x, *, name, tm=512):
    S = h.shape[0]

    def body(h_ref, o_ref, gh_ref, w_ref, x_ref, x1_ref, hg_ref):
        for hd in range(A_HEADS):
            sl = slice(hd * A_V, (hd + 1) * A_V)
            hs = h_ref[:, sl]
            hn = hs * lax.rsqrt(jnp.mean(hs * hs, axis=-1, keepdims=True) + EPS) * gh_ref[:, sl]
            hg_ref[:, sl] = (hn * _sigmoid(o_ref[:, sl])).astype(BF16)
        x1_ref[...] = x_ref[...] + _dot(hg_ref[...], w_ref[...])

    row = pl.BlockSpec((tm, 1024), lambda i: (i, 0))
    return pl.pallas_call(
        body, name=name, grid=(S // tm,),
        in_specs=[row, pl.BlockSpec((tm, 1024), lambda i: (i, 2)), pl.BlockSpec((1, 1024), lambda i: (0, 0)),
                  pl.BlockSpec((1024, 1024), lambda i: (0, 0)), row],
        out_specs=[row, row],
        out_shape=[jax.ShapeDtypeStruct((S, 1024), F32), jax.ShapeDtypeStruct((S, 1024), BF16)],
        compiler_params=_cp(("parallel",)),
    )(h, z, gh, w_out, x)


def ao_bwd(g1, w_out_t, h, z, gh, *, name, tm=512):
    S = h.shape[0]

    def body(g_ref, w_ref, h_ref, o_ref, gh_ref, dh_ref, dzo_ref, dgh_ref):
        @pl.when(pl.program_id(0) == 0)
        def _():
            dgh_ref[...] = jnp.zeros_like(dgh_ref)

        dhg = _dot(g_ref[...].astype(BF16), w_ref[...])
        for hd in range(A_HEADS):
            sl = slice(hd * A_V, (hd + 1) * A_V)
            hs = h_ref[:, sl]
            r = lax.rsqrt(jnp.mean(hs * hs, axis=-1, keepdims=True) + EPS)
            hhat = hs * r
            ghs = gh_ref[:, sl]
            sig = _sigmoid(o_ref[:, sl])
            d = dhg[:, sl]
            dhn = d * sig
            dzo_ref[:, sl] = (d * hhat * ghs * sig * (1.0 - sig)).astype(BF16)
            dgh_ref[:, sl] += jnp.sum(dhn * hhat, axis=0, keepdims=True)
            dhhat = dhn * ghs
            dh_ref[:, sl] = r * (dhhat - hhat * jnp.mean(dhhat * hhat, axis=-1, keepdims=True))

    row = pl.BlockSpec((tm, 1024), lambda i: (i, 0))
    vec = pl.BlockSpec((1, 1024), lambda i: (0, 0))
    return pl.pallas_call(
        body, name=name, grid=(S // tm,),
        in_specs=[row, pl.BlockSpec((1024, 1024), lambda i: (0, 0)), row,
                  pl.BlockSpec((tm, 1024), lambda i: (i, 2)), vec],
        out_specs=[row, row, vec],
        out_shape=[jax.ShapeDtypeStruct((S, 1024), F32), jax.ShapeDtypeStruct((S, 1024), BF16),
                   jax.ShapeDtypeStruct((1, 1024), F32)],
        compiler_params=_cp(("arbitrary",)),
    )(g1, w_out_t, h, z, gh)


CONV_TC = 1408
CONV_HALO = 16


def _causal_conv(u_ref, halo_ref, w_ref, b_ref, first):
    u = u_ref[...].astype(F32)
    T = u.shape[0]
    hl = jnp.where(first, 0.0, halo_ref[...].astype(F32))
    row = lax.broadcasted_iota(jnp.int32, u.shape, 0)
    u1 = jnp.where(row == 0, hl[CONV_HALO - 1:CONV_HALO], pltpu.roll(u, 1, axis=0))
    u2 = jnp.where(row == 0, hl[CONV_HALO - 2:CONV_HALO - 1],
                   jnp.where(row == 1, hl[CONV_HALO - 1:CONV_HALO], pltpu.roll(u, 2, axis=0)))
    w = w_ref[...]
    conv = u * w[2:3] + u1 * w[1:2] + u2 * w[0:1] + b_ref[...]
    return u, u1, u2, conv


def mm_up_conv(xn, w_up, cw, cb, *, name, tm=512):
    S = xn.shape[0]
    TN = 2 * CONV_TC

    def body(x_ref, w_ref, cw_ref, cb_ref, u_ref, a_ref, carry):
        @pl.when(pl.program_id(1) == 0)
        def _():
            carry[...] = jnp.zeros_like(carry)

        ub = _dot(x_ref[...], w_ref[...]).astype(BF16)
        u_ref[...] = ub
        u = ub.astype(F32)
        prev = carry[...]
        row = lax.broadcasted_iota(jnp.int32, u.shape, 0)
        u1 = jnp.where(row == 0, prev[7:8], pltpu.roll(u, 1, axis=0))
        u2 = jnp.where(row == 0, prev[6:7], jnp.where(row == 1, prev[7:8], pltpu.roll(u, 2, axis=0)))
        carry[...] = u[tm - 8:tm]
        w = cw_ref[...]
        c = u * w[2:3] + u1 * w[1:2] + u2 * w[0:1] + cb_ref[...]
        g, v = c[:, :CONV_TC], c[:, CONV_TC:]
        a_ref[...] = (g * _sigmoid(g) * v).astype(BF16)

    return pl.pallas_call(
        body, name=name, grid=(2, S // tm),
        in_specs=[pl.BlockSpec((tm, D), lambda j, i: (i, 0)), pl.BlockSpec((D, TN), lambda j, i: (0, j)),
                  pl.BlockSpec((3, TN), lambda j, i: (0, j)), pl.BlockSpec((1, TN), lambda j, i: (0, j))],
        out_specs=[pl.BlockSpec((tm, TN), lambda j, i: (i, j)), pl.BlockSpec((tm, CONV_TC), lambda j, i: (i, j))],
        out_shape=[jax.ShapeDtypeStruct((S, 2 * D_FF), BF16), jax.ShapeDtypeStruct((S, D_FF), BF16)],
        scratch_shapes=[pltpu.VMEM((8, TN), F32)],
        compiler_params=_cp(("parallel", "arbitrary")),
    )(xn, w_up, cw, cb)


def mm_da_conv(dxn, w_down_t, u, cw, cb, *, name, tm=512):
    S = u.shape[0]
    hb = tm // CONV_HALO

    def body(x_ref, w_ref, ug_ref, uv_ref, hg_ref, hv_ref, wg_ref, wv_ref, bg_ref, bv_ref, duc_ref, dwb_ref):
        first = pl.program_id(1) == 0

        @pl.when(first)
        def _():
            dwb_ref[...] = jnp.zeros_like(dwb_ref)

        dav = _dot(x_ref[...].astype(BF16), w_ref[...])
        gu = _causal_conv(ug_ref, hg_ref, wg_ref, bg_ref, first)
        vu = _causal_conv(uv_ref, hv_ref, wv_ref, bv_ref, first)
        g, v = gu[3], vu[3]
        sg = _sigmoid(g)

        def emit(d, taps, cols):
            duc_ref[:, cols] = d.astype(BF16)
            dwb_ref[0:1, cols] += jnp.sum(taps[2] * d, axis=0, keepdims=True)
            dwb_ref[1:2, cols] += jnp.sum(taps[1] * d, axis=0, keepdims=True)
            dwb_ref[2:3, cols] += jnp.sum(taps[0] * d, axis=0, keepdims=True)
            dwb_ref[3:4, cols] += jnp.sum(d, axis=0, keepdims=True)

        emit(dav * v * (sg * (1.0 + g * (1.0 - sg))), gu, slice(0, CONV_TC))
        emit(dav * (g * sg), vu, slice(CONV_TC, 2 * CONV_TC))

    def blk(off):
        return pl.BlockSpec((tm, CONV_TC), lambda j, i: (i, 2 * j + off))

    def halo(off):
        return pl.BlockSpec((CONV_HALO, CONV_TC), lambda j, i: (jnp.maximum(i * hb - 1, 0), 2 * j + off))

    def wsp(rows, off):
        return pl.BlockSpec((rows, CONV_TC), lambda j, i: (0, 2 * j + off))

    return pl.pallas_call(
        body, name=name, grid=(2, S // tm),
        in_specs=[pl.BlockSpec((tm, D), lambda j, i: (i, 0)), pl.BlockSpec((D, CONV_TC), lambda j, i: (0, j)),
                  blk(0), blk(1), halo(0), halo(1), wsp(3, 0), wsp(3, 1), wsp(1, 0), wsp(1, 1)],
        out_specs=[pl.BlockSpec((tm, 2 * CONV_TC), lambda j, i: (i, j)),
                   pl.BlockSpec((8, 2 * CONV_TC), lambda j, i: (0, j))],
        out_shape=[jax.ShapeDtypeStruct((S, 2 * D_FF), BF16), jax.ShapeDtypeStruct((8, 2 * D_FF), F32)],
        compiler_params=_cp(("parallel", "arbitrary")),
    )(dxn, w_down_t, u, u, u, u, cw, cw, cb, cb)


def tn_up_conv(xn, duc, cw, *, name, tk=512):
    S, Kd = xn.shape
    N = duc.shape[1]
    hb = tk // CONV_HALO
    nblk = S // tk

    def body(x_ref, d_ref, halo_ref, w_ref, o_ref, du_ref):
        k = pl.program_id(1)

        @pl.when(k == 0)
        def _():
            o_ref[...] = jnp.zeros_like(o_ref)

        d = d_ref[...].astype(F32)
        hl = jnp.where(k == nblk - 1, 0.0, halo_ref[...].astype(F32))
        row = lax.broadcasted_iota(jnp.int32, d.shape, 0)
        d1 = jnp.where(row == tk - 1, hl[0:1], pltpu.roll(d, tk - 1, axis=0))
        d2 = jnp.where(row == tk - 1, hl[1:2], jnp.where(row == tk - 2, hl[0:1], pltpu.roll(d, tk - 2, axis=0)))
        w = w_ref[...]
        du = (d * w[2:3] + d1 * w[1:2] + d2 * w[0:1]).astype(BF16)
        du_ref[...] = du
        o_ref[...] += _dot_tn(x_ref[...], du)

    return pl.pallas_call(
        body, name=name, grid=(N // CONV_TC, nblk),
        in_specs=[pl.BlockSpec((tk, Kd), lambda j, k: (k, 0)), pl.BlockSpec((tk, CONV_TC), lambda j, k: (k, j)),
                  pl.BlockSpec((CONV_HALO, CONV_TC), lambda j, k: (jnp.minimum((k + 1) * hb, nblk * hb - 1), j)),
                  pl.BlockSpec((3, CONV_TC), lambda j, k: (0, j))],
        out_specs=[pl.BlockSpec((Kd, CONV_TC), lambda j, k: (0, j)), pl.BlockSpec((tk, CONV_TC), lambda j, k: (k, j))],
        out_shape=[jax.ShapeDtypeStruct((Kd, N), F32), jax.ShapeDtypeStruct((S, N), BF16)],
        compiler_params=_cp(("parallel", "arbitrary")),
    )(xn, duc, duc, cw)


def _t5_bucket(dist):
    max_exact = REL_BUCKETS // 2
    d = np.maximum(dist, 0)
    log_ratio = np.log(np.maximum(d, 1) / max_exact) / math.log(REL_MAX_DIST / max_exact)
    large = np.minimum(max_exact + (log_ratio * (REL_BUCKETS - max_exact)).astype(np.int64), REL_BUCKETS - 1)
    return np.where(d < max_exact, d, large).astype(np.int32)


def _bucket_tables():
    delta = BLK + np.arange(BLK)[:, None] - np.arange(2 * BLK)[None, :]
    return np.stack([_t5_bucket(delta * dil) for _, dil in B_GROUPS]).astype(np.int32)


def bias_build(rel_bias, buckets, *, name):
    def body(rel_ref, bk_ref, o_ref):
        g = pl.program_id(0)
        bk = bk_ref[0]
        for h in range(B_HEADS):
            acc = jnp.zeros((BLK, 2 * BLK), F32)
            for bb in range(REL_BUCKETS):
                acc = jnp.where(bk == bb, rel_ref[bb, g * B_HEADS + h], acc)
            o_ref[0, h] = acc

    return pl.pallas_call(
        body, name=name, grid=(N_GROUPS,),
        in_specs=[pl.BlockSpec(memory_space=pltpu.SMEM), pl.BlockSpec((1, BLK, 2 * BLK), lambda g: (g, 0, 0))],
        out_specs=pl.BlockSpec((1, B_HEADS, BLK, 2 * BLK), lambda g: (g, 0, 0, 0)),
        out_shape=jax.ShapeDtypeStruct((N_GROUPS, B_HEADS, BLK, 2 * BLK), F32),
        compiler_params=_cp(("arbitrary",)),
    )(rel_bias, buckets)


def bias_grad(dbias, buckets, *, name):
    def body(db_ref, bk_ref, o_ref):
        g = pl.program_id(0)

        @pl.when(g == 0)
        def _():
            o_ref[...] = jnp.zeros_like(o_ref)

        bk = bk_ref[0]
        rr = lax.broadcasted_iota(jnp.int32, (REL_BUCKETS, 128), 0)
        cc = lax.broadcasted_iota(jnp.int32, (REL_BUCKETS, 128), 1)
        acc = jnp.zeros((REL_BUCKETS, 128), F32)
        for h in range(B_HEADS):
            dbh = db_ref[0, h]
            for bb in range(REL_BUCKETS):
                part = jnp.sum(jnp.where(bk == bb, dbh, 0.0), axis=0, keepdims=True)
                s = jnp.sum(part, axis=1, keepdims=True)
                acc = acc + jnp.where((rr == bb) & (cc == g * B_HEADS + h), s, 0.0)
        o_ref[...] += acc

    return pl.pallas_call(
        body, name=name, grid=(N_GROUPS,),
        in_specs=[pl.BlockSpec((1, B_HEADS, BLK, 2 * BLK), lambda g: (g, 0, 0, 0)),
                  pl.BlockSpec((1, BLK, 2 * BLK), lambda g: (g, 0, 0))],
        out_specs=pl.BlockSpec((REL_BUCKETS, 128), lambda g: (0, 0)),
        out_shape=jax.ShapeDtypeStruct((REL_BUCKETS, 128), F32),
        compiler_params=_cp(("arbitrary",)),
    )(dbias, buckets)


HG = 4
GW = HG * B_DH


def _head_masks(dtype):
    lane = lax.broadcasted_iota(jnp.int32, (BLK, GW), 1)
    return [((lane >= h * B_DH) & (lane < (h + 1) * B_DH)).astype(dtype) for h in range(HG)]


def _band_masks():
    iq = lax.broadcasted_iota(jnp.int32, (BLK, BLK), 0)
    ik = lax.broadcasted_iota(jnp.int32, (BLK, BLK), 1)
    return iq <= ik, iq >= ik


def attn_fwd(qg, kg, vg, bias, g, dil, *, name):
    S = qg.shape[0]
    S2 = S // dil
    nb = S2 // BLK
    W = dil * 1024
    scale = B_DH ** -0.5

    def body(q_ref, kc_ref, kp_ref, vc_ref, vp_ref, b_ref, o_ref, lse_ref):
        has_prev = pl.program_id(1) > 0
        vp_m, vc_m = _band_masks()
        valid = jnp.concatenate([vp_m & has_prev, vc_m], axis=1)
        mb = _head_masks(BF16)
        mf = _head_masks(F32)
        lane = lax.broadcasted_iota(jnp.int32, (BLK, 128), 1)
        lse_acc = jnp.zeros((BLK, 128), F32)
        for hg in range(B_HEADS // HG):
            sl = slice(hg * GW, (hg + 1) * GW)
            q4 = q_ref[:, sl]
            kcat = jnp.concatenate([kp_ref[:, sl], kc_ref[:, sl]], axis=0)
            vcat = jnp.concatenate([vp_ref[:, sl], vc_ref[:, sl]], axis=0)
            s4 = _dot_nt(jnp.concatenate([q4 * mb[h] for h in range(HG)], axis=0), kcat)
            ps, rl = [], []
            for h in range(HG):
                hh = hg * HG + h
                s = jnp.where(valid, s4[h * BLK:(h + 1) * BLK] * scale + b_ref[0, hh], -jnp.inf)
                m = jnp.max(s, axis=-1, keepdims=True)
                p = jnp.exp(s - m)
                l = jnp.sum(p, axis=-1, keepdims=True)
                ps.append(p.astype(BF16))
                rl.append(1.0 / l)
                lse_acc = jnp.where(lane == hh, m + jnp.log(l), lse_acc)
            o4 = _dot(jnp.concatenate(ps, axis=0), vcat)
            acc = jnp.zeros((BLK, GW), F32)
            for h in range(HG):
                acc = acc + o4[h * BLK:(h + 1) * BLK] * (mf[h] * rl[h])
            o_ref[:, sl] = acc
        lse_ref[...] = lse_acc

    cur = pl.BlockSpec((BLK, 1024), lambda r, n: (n, r))
    prev = pl.BlockSpec((BLK, 1024), lambda r, n: (jnp.maximum(n - 1, 0), r))
    q2, k2, v2 = qg.reshape(S2, W), kg.reshape(S2, W), vg.reshape(S2, W)
    o, lse = pl.pallas_call(
        body, name=name, grid=(dil, nb),
        in_specs=[cur, cur, prev, cur, prev, pl.BlockSpec((1, B_HEADS, BLK, 2 * BLK), lambda r, n: (g, 0, 0, 0))],
        out_specs=[cur, pl.BlockSpec((BLK, 128), lambda r, n: (n, r))],
        out_shape=[jax.ShapeDtypeStruct((S2, W), F32), jax.ShapeDtypeStruct((S2, dil * 128), F32)],
        compiler_params=_cp(("parallel", "arbitrary")),
    )(q2, k2, k2, v2, v2, bias)
    return o.reshape(S, 1024), lse.reshape(S, 128)


def attn_merge(os_, lses, *, name, tm=512):
    S = os_[0].shape[0]
    expand = np.zeros((128, 1024), np.float32)
    for h in range(B_HEADS):
        expand[h, h * B_DH:(h + 1) * B_DH] = 1.0
    expand = jnp.asarray(expand, BF16)

    def body(o0, o1, o2, l0, l1, l2, e_ref, out_ref, lse_ref):
        ls = [l0[...], l1[...], l2[...]]
        m = jnp.maximum(jnp.maximum(ls[0], ls[1]), ls[2])
        ws = [jnp.exp(l - m) for l in ls]
        tot = ws[0] + ws[1] + ws[2]
        lse_ref[...] = m + jnp.log(tot)
        acc = jnp.zeros((tm, 1024), F32)
        for w, o in zip(ws, (o0, o1, o2)):
            acc = acc + _dot_split(w / tot, e_ref[...]) * o[...]
        out_ref[...] = acc.astype(BF16)

    row = pl.BlockSpec((tm, 1024), lambda i: (i, 0))
    lrow = pl.BlockSpec((tm, 128), lambda i: (i, 0))
    return pl.pallas_call(
        body, name=name, grid=(S // tm,),
        in_specs=[row, row, row, lrow, lrow, lrow, pl.BlockSpec((128, 1024), lambda i: (0, 0))],
        out_specs=[row, lrow],
        out_shape=[jax.ShapeDtypeStruct((S, 1024), BF16), jax.ShapeDtypeStruct((S, 128), F32)],
        compiler_params=_cp(("parallel",)),
    )(*os_, *lses, expand)


def mm_dout(dx, w_t, att, *, name, tm=512):
    S = dx.shape[0]
    heads = np.zeros((1024, 128), np.float32)
    for h in range(B_HEADS):
        heads[h * B_DH:(h + 1) * B_DH, h] = 1.0

    def body(x_ref, w_ref, att_ref, e_ref, do_ref, d_ref):
        acc = _dot(x_ref[...].astype(BF16), w_ref[...])
        do_ref[...] = acc.astype(BF16)
        d_ref[...] = _dot_split(acc * att_ref[...].astype(F32), e_ref[...])

    row = pl.BlockSpec((tm, 1024), lambda i: (i, 0))
    return pl.pallas_call(
        body, name=name, grid=(S // tm,),
        in_specs=[row, pl.BlockSpec((1024, 1024), lambda i: (0, 0)), row, pl.BlockSpec((1024, 128), lambda i: (0, 0))],
        out_specs=[row, pl.BlockSpec((tm, 128), lambda i: (i, 0))],
        out_shape=[jax.ShapeDtypeStruct((S, 1024), BF16), jax.ShapeDtypeStruct((S, 128), F32)],
        compiler_params=_cp(("parallel",)),
    )(dx, w_t, att, jnp.asarray(heads, BF16))


def attn_bwd(qg, kg, vg, bias, dout, dsum, lse, g, dil, *, name):
    S = qg.shape[0]
    S2 = S // dil
    nb = S2 // BLK
    W = dil * 1024
    scale = B_DH ** -0.5

    def body(q_ref, kc_ref, kp_ref, vc_ref, vp_ref, b_ref, do_ref, dsum_ref, lse_ref,
             dq_ref, dk_ref, dv_ref, db_ref, ck_s, cv_s):
        n = pl.program_id(1)

        @pl.when((pl.program_id(0) == 0) & (n == 0))
        def _():
            db_ref[...] = jnp.zeros_like(db_ref)

        @pl.when(n == 0)
        def _():
            ck_s[...] = jnp.zeros_like(ck_s)
            cv_s[...] = jnp.zeros_like(cv_s)

        @pl.when(n == nb)
        def _():
            dk_ref[...] = ck_s[...].astype(BF16)
            dv_ref[...] = cv_s[...].astype(BF16)

        @pl.when(n < nb)
        def _():
            vp_m, vc_m = _band_masks()
            valid = jnp.concatenate([vp_m & (n > 0), vc_m], axis=1)
            mb = _head_masks(BF16)
            mf = _head_masks(F32)
            lse_blk = lse_ref[...]
            dsum_blk = dsum_ref[...]
            for hg in range(B_HEADS // HG):
                sl = slice(hg * GW, (hg + 1) * GW)
                kcat = jnp.concatenate([kp_ref[:, sl], kc_ref[:, sl]], axis=0)
                vcat = jnp.concatenate([vp_ref[:, sl], vc_ref[:, sl]], axis=0)
                dob = do_ref[:, sl]
                q4 = q_ref[:, sl]
                q4m = jnp.concatenate([q4 * mb[h] for h in range(HG)], axis=0)
                do4m = jnp.concatenate([dob * mb[h] for h in range(HG)], axis=0)
                s4 = _dot_nt(q4m, kcat)
                dp4 = _dot_nt(do4m, vcat)
                ps, dss = [], []
                for h in range(HG):
                    hh = hg * HG + h
                    rows = slice(h * BLK, (h + 1) * BLK)
                    s = jnp.where(valid, s4[rows] * scale + b_ref[0, hh] - lse_blk[:, hh:hh + 1], -jnp.inf)
                    p = jnp.exp(s)
                    ds = p * (dp4[rows] - dsum_blk[:, hh:hh + 1])
                    db_ref[hh] += ds
                    ps.append(p.astype(BF16))
                    dss.append(ds.astype(BF16))
                p4 = jnp.concatenate(ps, axis=0)
                ds4 = jnp.concatenate(dss, axis=0)
                dq4 = _dot(ds4, kcat)
                acc = jnp.zeros((BLK, GW), F32)
                for h in range(HG):
                    acc = acc + dq4[h * BLK:(h + 1) * BLK] * mf[h]
                dq_ref[:, sl] = (acc * scale).astype(BF16)
                dkc = _dot_tn(ds4, q4m) * scale
                dvc = _dot_tn(p4, do4m)
                dk_ref[:, sl] = (ck_s[:, sl] + dkc[0:BLK]).astype(BF16)
                dv_ref[:, sl] = (cv_s[:, sl] + dvc[0:BLK]).astype(BF16)
                ck_s[:, sl] = dkc[BLK:2 * BLK]
                cv_s[:, sl] = dvc[BLK:2 * BLK]

    last = nb - 1
    cur = lambda r, n: (jnp.minimum(n, last), r)
    prev = lambda r, n: (jnp.clip(n - 1, 0, last), r)
    row = lambda im: pl.BlockSpec((BLK, 1024), im)
    k2, v2 = kg.reshape(S2, W), vg.reshape(S2, W)
    dq, dk, dv, dbias = pl.pallas_call(
        body, name=name, grid=(dil, nb + 1),
        in_specs=[row(cur), row(cur), row(prev), row(cur), row(prev),
                  pl.BlockSpec((1, B_HEADS, BLK, 2 * BLK), lambda r, n: (g, 0, 0, 0)),
                  row(cur), pl.BlockSpec((BLK, 128), cur), pl.BlockSpec((BLK, 128), cur)],
        out_specs=[row(cur), row(prev), row(prev), pl.BlockSpec((B_HEADS, BLK, 2 * BLK), lambda r, n: (0, 0, 0))],
        out_shape=[jax.ShapeDtypeStruct((S2, W), BF16)] * 3 + [jax.ShapeDtypeStruct((B_HEADS, BLK, 2 * BLK), F32)],
        scratch_shapes=[pltpu.VMEM((BLK, 1024), F32), pltpu.VMEM((BLK, 1024), F32)],
        compiler_params=_cp(("arbitrary", "arbitrary")),
    )(qg.reshape(S2, W), k2, k2, v2, v2, bias, dout.reshape(S2, W), dsum.reshape(S2, dil * 128),
      lse.reshape(S2, dil * 128))
    return dq.reshape(S, 1024), dk.reshape(S, 1024), dv.reshape(S, 1024), dbias


def _slot(px, py, pc):
    return 4 * px + 2 * py + pc


def ag_weights(wb, ws):
    def body(wb_ref, ws_ref, ob_ref, os_ref, send_sems, recv_sems, local_sems):
        x, y, c = lax.axis_index("x"), lax.axis_index("y"), lax.axis_index("c")
        me, sibling = (x, y, c), (x, y, 1 - c)
        chips = [(1 - x, y), (x, 1 - y), (1 - x, 1 - y)]
        arrays = [(wb_ref, ob_ref), (ws_ref, os_ref)]

        def copy(a, k, block, to, from_input=False):
            src_in, out = arrays[a]
            dst = out.at[_slot(*block)]
            return pltpu.make_async_remote_copy(
                src_ref=src_in if from_input else dst, dst_ref=dst,
                send_sem=send_sems.at[7 * a + k], recv_sem=recv_sems.at[7 * a + k],
                device_id=to, device_id_type=MESH)

        mine = [pltpu.make_async_copy(arrays[a][0], arrays[a][1].at[_slot(*me)], local_sems.at[a]) for a in range(2)]
        for cp in mine:
            cp.start()
        first = []
        for a in range(2):
            first.append(copy(a, 0, me, sibling, True))
            first += [copy(a, 1 + j, me, (*chip, c), True) for j, chip in enumerate(chips)]
        for cp in first:
            cp.start()
        passed = []
        for a in range(2):
            for j, chip in enumerate(chips):
                copy(a, 1 + j, (*chip, c), me).wait_recv()
                fw = copy(a, 4 + j, (*chip, c), sibling)
                fw.start()
                passed.append(fw)
        for a in range(2):
            copy(a, 0, sibling, me).wait_recv()
            for j, chip in enumerate(chips):
                copy(a, 4 + j, (*chip, 1 - c), me).wait_recv()
        for cp in first + passed:
            cp.wait_send()
        for cp in mine:
            cp.wait()

    any_spec = pl.BlockSpec(memory_space=pl.ANY)
    return pl.pallas_call(
        body, name="ag_weights",
        in_specs=[any_spec, any_spec], out_specs=[any_spec, any_spec],
        out_shape=[jax.ShapeDtypeStruct((N_DEV,) + wb.shape, wb.dtype), jax.ShapeDtypeStruct((N_DEV,) + ws.shape, ws.dtype)],
        scratch_shapes=[pltpu.SemaphoreType.DMA((14,)), pltpu.SemaphoreType.DMA((14,)), pltpu.SemaphoreType.DMA((2,))],
    )(wb, ws)


def rs_sibling(gpack, spack):
    def body(g_ref, s_ref, rb_ref, sa_ref, send_sems, recv_sems, local_sem):
        x, y, c = lax.axis_index("x"), lax.axis_index("y"), lax.axis_index("c")
        my = _slot(x, y, c)
        mine = pltpu.make_async_copy(s_ref, sa_ref.at[my], local_sem)
        mine.start()
        sends, recvs = [], []
        for j in range(4):
            both = dict(dst_ref=rb_ref.at[j], send_sem=send_sems.at[j], recv_sem=recv_sems.at[j],
                        device_id=(x, y, 1 - c), device_id_type=MESH)
            sends.append(pltpu.make_async_remote_copy(src_ref=g_ref.at[2 * j + 1 - c], **both))
            recvs.append(sends[-1])
        for k in range(1, N_DEV):
            peer = (1 - x if k & 4 else x, 1 - y if k & 2 else y, 1 - c if k & 1 else c)
            sems = dict(send_sem=send_sems.at[3 + k], recv_sem=recv_sems.at[3 + k], device_id=peer, device_id_type=MESH)
            sends.append(pltpu.make_async_remote_copy(src_ref=s_ref, dst_ref=sa_ref.at[my], **sems))
            recvs.append(pltpu.make_async_remote_copy(src_ref=s_ref, dst_ref=sa_ref.at[_slot(*peer)], **sems))
        for cp in sends:
            cp.start()
        for cp in recvs:
            cp.wait_recv()
        for cp in sends:
            cp.wait_send()
        mine.wait()

    any_spec = pl.BlockSpec(memory_space=pl.ANY)
    return pl.pallas_call(
        body, name="rs_sibling",
        in_specs=[any_spec, any_spec], out_specs=[any_spec, any_spec],
        out_shape=[jax.ShapeDtypeStruct((4,) + gpack.shape[1:], gpack.dtype),
                   jax.ShapeDtypeStruct((N_DEV,) + spack.shape, spack.dtype)],
        scratch_shapes=[pltpu.SemaphoreType.DMA((11,)), pltpu.SemaphoreType.DMA((11,)), pltpu.SemaphoreType.DMA],
    )(gpack, spack)


def pair_add(a, b, *, name, tr):
    R = a.shape[1]

    def body(a_ref, b_ref, o_ref):
        o_ref[...] = (a_ref[...].astype(F32) + b_ref[...].astype(F32)).astype(BF16)

    blk = pl.BlockSpec((4, tr, 1024), lambda i: (0, i, 0))
    return pl.pallas_call(
        body, name=name, grid=(R // tr,), in_specs=[blk, blk], out_specs=blk,
        out_shape=jax.ShapeDtypeStruct(a.shape, BF16), compiler_params=_cp(("parallel",)),
    )(a, b)


def rs_chips(part):
    def body(p_ref, rb_ref, send_sems, recv_sems, local_sem):
        x, y, c = lax.axis_index("x"), lax.axis_index("y"), lax.axis_index("c")
        jm = 2 * x + y
        mine = pltpu.make_async_copy(p_ref.at[jm], rb_ref.at[jm], local_sem)
        mine.start()
        sends, recvs = [], []
        for k in range(1, 4):
            px, py = (1 - x if k & 2 else x), (1 - y if k & 1 else y)
            sems = dict(send_sem=send_sems.at[k - 1], recv_sem=recv_sems.at[k - 1], device_id=(px, py, c), device_id_type=MESH)
            sends.append(pltpu.make_async_remote_copy(src_ref=p_ref.at[2 * px + py], dst_ref=rb_ref.at[jm], **sems))
            recvs.append(pltpu.make_async_remote_copy(src_ref=p_ref.at[jm], dst_ref=rb_ref.at[2 * px + py], **sems))
        for cp in sends:
            cp.start()
        for cp in recvs:
            cp.wait_recv()
        for cp in sends:
            cp.wait_send()
        mine.wait()

    any_spec = pl.BlockSpec(memory_space=pl.ANY)
    return pl.pallas_call(
        body, name="rs_chips", in_specs=[any_spec], out_specs=any_spec,
        out_shape=jax.ShapeDtypeStruct(part.shape, part.dtype),
        scratch_shapes=[pltpu.SemaphoreType.DMA((3,)), pltpu.SemaphoreType.DMA((3,)), pltpu.SemaphoreType.DMA],
    )(part)


def reduce_adam(parts, w, m, v, *, name, tr):
    R = w.shape[0]
    n_parts = parts.shape[0]
    assert R % tr == 0
    c1 = 1.0 - ADAM_B1 ** ADAM_STEP
    c2 = 1.0 - ADAM_B2 ** ADAM_STEP

    def body(p_ref, w_ref, m_ref, v_ref, g_ref, d_ref, mo_ref, vo_ref):
        g = p_ref[0].astype(F32)
        for i in range(1, n_parts):
            g = g + p_ref[i].astype(F32)
        mn = ADAM_B1 * m_ref[...] + (1.0 - ADAM_B1) * g
        vn = ADAM_B2 * v_ref[...] + (1.0 - ADAM_B2) * (g * g)
        g_ref[...] = g
        mo_ref[...] = mn
        vo_ref[...] = vn
        d_ref[...] = -ADAM_LR * ((mn / c1) / (jnp.sqrt(vn / c2) + ADAM_EPS) + ADAM_WD * w_ref[...])

    row = pl.BlockSpec((tr, 1024), lambda i: (i, 0))
    return pl.pallas_call(
        body, name=name, grid=(R // tr,),
        in_specs=[pl.BlockSpec((n_parts, tr, 1024), lambda i: (0, i, 0)), row, row, row],
        out_specs=[row] * 4,
        out_shape=[jax.ShapeDtypeStruct((R, 1024), F32)] * 4,
        compiler_params=_cp(("parallel",)),
    )(parts, w, m, v)


BIG = (("a_w_in", 385, 400), ("a_w_out", 128, 128), ("w_kv", 768, 768), ("b_w_q", 384, 384),
       ("b_w_out", 128, 128), ("f_w_up", 1408, 1408), ("f_w_down", 704, 704))
SMALL_SHARDED = (("a_norm_g", 128), ("a_hnorm_g", 128), ("f_conv_w", 4224))
SMALL_ROWS = 48
PACK_ROWS = sum(b[2] for b in BIG) + SMALL_ROWS
REPL = (("kv_norm_g", 1024, 1), ("b_norm_g", 1024, 1), ("f_norm_g", 2048, 2), ("f_conv_b", 11264, 11),
        ("final_norm_g", 1024, 1), ("rel_bias", 1536, 2), ("a_b_if", 8, 1))
REPL_ROWS = 24
LOSS_ROW = 19


def _rows(a, rows, padded):
    a = a.reshape(rows, 1024)
    return a if padded == rows else jnp.pad(a, ((0, padded - rows), (0, 0)))


def pack_shards(t, dtype, with_small):
    parts = [_rows(t[n].astype(dtype), r, p) for n, r, p in BIG]
    if with_small:
        flat = jnp.concatenate([t[n].astype(dtype).reshape(-1) for n, _ in SMALL_SHARDED])
        parts.append(jnp.pad(flat, (0, SMALL_ROWS * 1024 - flat.shape[0])).reshape(SMALL_ROWS, 1024))
    return jnp.concatenate(parts, axis=0)


def unpack_shards(pack, shapes):
    out = {}
    r0 = 0
    for n, r, p in BIG:
        out[n] = pack[r0:r0 + r].reshape(shapes[n])
        r0 += p
    flat = pack[r0:r0 + SMALL_ROWS].reshape(-1)
    e0 = 0
    for n, e in SMALL_SHARDED:
        out[n] = flat[e0:e0 + e].reshape(shapes[n])
        e0 += e
    return out


def pack_repl(t):
    parts = []
    for n, e, r in REPL:
        parts.append(jnp.pad(t[n].astype(F32).reshape(-1), (0, r * 1024 - e)))
    rows = sum(r for _, _, r in REPL)
    parts.append(jnp.zeros(((REPL_ROWS - rows) * 1024,), F32))
    return jnp.concatenate(parts).reshape(REPL_ROWS, 1024)


def unpack_repl(pack, shapes):
    out = {}
    r0 = 0
    for n, e, r in REPL:
        out[n] = pack[r0:r0 + r].reshape(-1)[:e].reshape(shapes[n])
        r0 += r
    return out


def ff_blocks(a):
    b = [a[..., i * CONV_TC:(i + 1) * CONV_TC] for i in range(4)]
    return jnp.concatenate([b[0], b[2], b[1], b[3]], axis=-1)


def split_cols(full, n):
    lead = full.shape[:-1]
    return jnp.moveaxis(full.reshape(lead + (N_DEV, n)), -2, 0)


def join_cols(parts):
    t = jnp.moveaxis(parts, 0, -2)
    return t.reshape(t.shape[:-2] + (t.shape[-2] * t.shape[-1],))


def kernel(x, a_norm_g, a_w_in, a_b_if, a_hnorm_g, a_w_out, kv_norm_g, w_kv, b_norm_g, b_w_q, b_w_out, rel_bias, f_norm_g, f_w_up, f_conv_w, f_conv_b, f_w_down, final_norm_g, loss_target, m_a_norm_g, m_a_w_in, m_a_b_if, m_a_hnorm_g, m_a_w_out, m_kv_norm_g, m_w_kv, m_b_norm_g, m_b_w_q, m_b_w_out, m_rel_bias, m_f_norm_g, m_f_w_up, m_f_conv_w, m_f_conv_b, m_f_w_down, m_final_norm_g, v_a_norm_g, v_a_w_in, v_a_b_if, v_a_hnorm_g, v_a_w_out, v_kv_norm_g, v_w_kv, v_b_norm_g, v_b_w_q, v_b_w_out, v_rel_bias, v_f_norm_g, v_f_w_up, v_f_conv_w, v_f_conv_b, v_f_w_down, v_final_norm_g):
    names = ["a_norm_g", "a_w_in", "a_b_if", "a_hnorm_g", "a_w_out", "kv_norm_g", "w_kv", "b_norm_g", "b_w_q", "b_w_out",
             "rel_bias", "f_norm_g", "f_w_up", "f_conv_w", "f_conv_b", "f_w_down", "final_norm_g"]
    w = dict(zip(names, (a_norm_g, a_w_in, a_b_if, a_hnorm_g, a_w_out, kv_norm_g, w_kv, b_norm_g, b_w_q, b_w_out,
                         rel_bias, f_norm_g, f_w_up, f_conv_w, f_conv_b, f_w_down, final_norm_g)))
    mom = dict(zip(names, (m_a_norm_g, m_a_w_in, m_a_b_if, m_a_hnorm_g, m_a_w_out, m_kv_norm_g, m_w_kv, m_b_norm_g, m_b_w_q,
                           m_b_w_out, m_rel_bias, m_f_norm_g, m_f_w_up, m_f_conv_w, m_f_conv_b, m_f_w_down, m_final_norm_g)))
    vel = dict(zip(names, (v_a_norm_g, v_a_w_in, v_a_b_if, v_a_hnorm_g, v_a_w_out, v_kv_norm_g, v_w_kv, v_b_norm_g, v_b_w_q,
                           v_b_w_out, v_rel_bias, v_f_norm_g, v_f_w_up, v_f_conv_w, v_f_conv_b, v_f_w_down, v_final_norm_g)))
    shapes = {n: w[n].shape for n in names}
    S = x.shape[1]
    assert x.shape[0] == 1 and S % (16 * BLK) == 0 and S % 1024 == 0
    X0 = x.reshape(S, D)
    target = loss_target.reshape(S, D)

    wb_all, ws_all = ag_weights(pack_shards(w, BF16, False),
                                pack_shards(w, F32, True)[PACK_ROWS - SMALL_ROWS:])
    seg = {}
    r0 = 0
    for n, r, p in BIG:
        seg[n] = wb_all[:, r0:r0 + r]
        r0 += p
    W_in = join_cols(seg["a_w_in"].reshape(N_DEV, D, 385))
    W_in = jnp.concatenate([jnp.pad(W_in[:, :3076], ((0, 0), (0, 124))),
                            jnp.pad(W_in[:, 3076:3080], ((0, 0), (0, 124)))], axis=1)
    W_out = seg["a_w_out"].reshape(1024, D)
    W_kv = join_cols(seg["w_kv"].reshape(N_DEV, D, 768))
    W_q = join_cols(seg["b_w_q"].reshape(N_DEV, D, 384))
    W_bout = seg["b_w_out"].reshape(1024, D)
    W_up = join_cols(seg["f_w_up"].reshape(N_DEV, 2, D, 704))
    W_down = jnp.moveaxis(seg["f_w_down"].reshape(N_DEV, 2, 352, D), 0, 1).reshape(2, D_FF, D)
    sflat = ws_all.reshape(N_DEV, SMALL_ROWS * 1024)
    g_a = sflat[:, 0:128].reshape(1, D)
    g_h = jnp.moveaxis(sflat[:, 128:256].reshape(N_DEV, A_HEADS, 32), 0, 1).reshape(1, A_HEADS * A_V)
    conv_w = ff_blocks(join_cols(sflat[:, 256:256 + 4224].reshape(N_DEV, 2, 3, 704)))
    conv_b = ff_blocks(f_conv_b)
    W_up = ff_blocks(W_up)
    bi = jnp.pad(a_b_if[:, :A_HEADS], ((0, 0), (0, 128 - A_HEADS)))
    bfg = jnp.pad(a_b_if[:, A_HEADS:], ((0, 0), (0, 128 - A_HEADS)))
    buckets = jnp.asarray(_bucket_tables())

    (xn_a,) = rms_fwd(X0, [g_a], name="rms_a")
    z = mm(xn_a, W_in, name="mm_a_in", out_dtype=F32, tn=1664)
    h, cst, nst = mlstm_fwd(z, bi, bfg, name="mlstm_fwd")
    X1, hg = ao_fwd(h, z, g_h, W_out, X0, name="ao_fwd")

    def ffn_fwd(X, l, tag):
        (xn,) = rms_fwd(X, [f_norm_g[l:l + 1]], name="rms_f" + tag)
        u, a = mm_up_conv(xn, W_up[l], conv_w[l], conv_b[l:l + 1], name="mm_up_conv" + tag)
        Xn = mm(a, W_down[l], name="mm_down" + tag, out_dtype=F32, tn=1024, tm=512, res=X)
        return Xn, (xn, u, a)

    X2, sav0 = ffn_fwd(X1, 0, "0")

    xkn, xbn = rms_fwd(X2, [kv_norm_g.reshape(1, D), b_norm_g], name="rms_kv_b")
    bias = bias_build(rel_bias, buckets, name="bias_build")
    col = lambda wmat, i: wmat[:, i * 1024:(i + 1) * 1024]
    qs, kk, vv, og, lg = [], [], [], [], []
    for g, (_, dil) in enumerate(B_GROUPS):
        qs.append(mm(xbn, col(W_q, g), name="mm_q%d" % g, out_dtype=BF16, tn=1024))
        kk.append(mm(xkn, col(W_kv, g), name="mm_k%d" % g, out_dtype=BF16, tn=1024))
        vv.append(mm(xkn, col(W_kv, N_GROUPS + g), name="mm_v%d" % g, out_dtype=BF16, tn=1024))
        o_, l_ = attn_fwd(qs[g], kk[g], vv[g], bias, g, dil, name="attn_fwd%d" % g)
        og.append(o_)
        lg.append(l_)
    att, lse = attn_merge(og, lg, name="attn_merge")
    X3 = mm(att, W_bout, name="mm_b_out", out_dtype=F32, tn=1024, res=X2)
    X4, sav1 = ffn_fwd(X3, 1, "1")

    dX4, d_final_g, loss_part = loss_head(X4, target, final_norm_g.reshape(1, D), name="loss_head")

    def ffn_bwd(X, dXn, l, sav, tag):
        xn, u, a = sav
        dW_down = mm_tn(a, dXn, name="tn_down" + tag, tn=1024, tk=512)
        duc, dwb = mm_da_conv(dXn, W_down[l].T, u, conv_w[l], conv_b[l:l + 1], name="mm_da_conv" + tag)
        dW_up, du = tn_up_conv(xn, duc, conv_w[l], name="tn_up_conv" + tag)
        dX, (dg,) = mm_rms_bwd([du], [W_up[l].T], f_norm_g[l:l + 1], X, dXn, name="mm_rms_bwd_f" + tag)
        return dX, dW_down, dW_up, dwb, dg

    dX3, dWd1, dWu1, dwb1, dgf1 = ffn_bwd(X3, dX4, 1, sav1, "1")

    dW_bout = mm_tn(att, dX3, name="tn_b_out", tn=1024)
    dout, dsum = mm_dout(dX3, W_bout.T, att, name="mm_dout")
    dqs, dks, dvs, dbias = [], [], [], []
    for g, (_, dil) in enumerate(B_GROUPS):
        dq_, dk_, dv_, db_ = attn_bwd(qs[g], kk[g], vv[g], bias, dout, dsum, lse, g, dil, name="attn_bwd%d" % g)
        dqs.append(dq_)
        dks.append(dk_)
        dvs.append(dv_)
        dbias.append(db_)
    d_rel = bias_grad(jnp.stack(dbias), buckets, name="bias_grad")[:, :N_GROUPS * B_HEADS]
    dW_q = jnp.concatenate([mm_tn(xbn, d_, name="tn_q%d" % g, tn=1024) for g, d_ in enumerate(dqs)], axis=1)
    dW_kv = jnp.concatenate([mm_tn(xkn, d_, name="tn_kv%d" % i, tn=1024) for i, d_ in enumerate(dks + dvs)], axis=1)
    W_qT, W_kvT = W_q.T, W_kv.T
    rows = lambda wmat, i: wmat[i * 1024:(i + 1) * 1024]
    dxn_kv = mm_sum(dks + dvs, [rows(W_kvT, i) for i in range(2 * N_GROUPS)], name="mm_dxn_kv")
    dX2, (dg_b, dg_kv) = mm_rms_bwd(dqs, [rows(W_qT, g) for g in range(N_GROUPS)], b_norm_g, X2, dX3,
                                    extra=[(dxn_kv, kv_norm_g.reshape(1, D))], name="mm_rms_bwd_b_kv")

    dX1, dWd0, dWu0, dwb0, dgf0 = ffn_bwd(X1, dX2, 0, sav0, "0")

    dW_out = mm_tn(hg, dX1, name="tn_a_out", tn=1024)
    dh, dzo, dgh = ao_bwd(dX1, W_out.T, h, z, g_h, name="ao_bwd")
    dz, db_if = mlstm_bwd(z, bi, bfg, cst, nst, dh, dzo, name="mlstm_bwd")
    dW_in = mm_tn(xn_a, dz, name="tn_a_in", tn=1664)
    dW_in = jnp.concatenate([dW_in[:, :3076], dW_in[:, Z_GF:Z_GF + 4]], axis=1)
    dX0, (dg_a,) = mm_rms_bwd([dz], [W_in.T], g_a, X0, dX1, name="mm_rms_bwd_a")

    dWu = ff_blocks(jnp.stack([dWu0, dWu1]))
    dWd = jnp.stack([dWd0, dWd1])
    dwb = ff_blocks(jnp.stack([dwb0, dwb1]))
    slots = [
        jnp.pad(split_cols(dW_in, 385).reshape(N_DEV, 385, 1024), ((0, 0), (0, 15), (0, 0))),
        dW_out.reshape(N_DEV, 128, 1024),
        split_cols(dW_kv, 768).reshape(N_DEV, 768, 1024),
        split_cols(dW_q, 384).reshape(N_DEV, 384, 1024),
        dW_bout.reshape(N_DEV, 128, 1024),
        split_cols(dWu, 704).reshape(N_DEV, 1408, 1024),
        jnp.moveaxis(dWd.reshape(2, N_DEV, 352, D), 1, 0).reshape(N_DEV, 704, 1024),
    ]
    small = jnp.concatenate([
        dg_a.reshape(N_DEV, 128),
        split_cols(dgh.reshape(A_HEADS, A_V), 32).reshape(N_DEV, 128),
        split_cols(dwb[:, 0:3], 704).reshape(N_DEV, 4224)], axis=1)
    slots.append(jnp.pad(small, ((0, 0), (0, SMALL_ROWS * 1024 - small.shape[1]))).reshape(N_DEV, SMALL_ROWS, 1024))
    gpack = jnp.concatenate([t.astype(BF16) for t in slots], axis=1)
    repl_g = {"kv_norm_g": dg_kv, "b_norm_g": dg_b, "f_norm_g": jnp.concatenate([dgf0, dgf1]),
              "f_conv_b": dwb[:, 3], "final_norm_g": d_final_g, "rel_bias": d_rel,
              "a_b_if": jnp.concatenate([db_if[0, :A_HEADS], db_if[1, :A_HEADS]])}
    spack = pack_repl(repl_g)
    spack = spack.at[LOSS_ROW, 0].set(loss_part[0, 0])

    from_sibling, sparts = rs_sibling(gpack, spack)
    own = lax.dynamic_index_in_dim(gpack.reshape(4, 2, PACK_ROWS, 1024), lax.axis_index("c"), axis=1, keepdims=False)
    parts = rs_chips(pair_add(own, from_sibling, name="rs_pair_add", tr=PACK_ROWS // 8))
    gb, db, mb, vb = reduce_adam(parts, pack_shards(w, F32, True), pack_shards(mom, F32, True),
                                 pack_shards(vel, F32, True), name="reduce_adam_big", tr=PACK_ROWS // 8)
    gs, ds, ms, vs = reduce_adam(sparts, pack_repl(w), pack_repl(mom), pack_repl(vel), name="reduce_adam_small", tr=REPL_ROWS)
    loss = gs[LOSS_ROW, 0]

    def collect(big, sm):
        t = unpack_shards(big, shapes)
        t.update(unpack_repl(sm, shapes))
        return [t[n] for n in names]

    return (loss, dX0.reshape(1, S, D), *collect(gb, gs), *collect(db, ds), *collect(mb, ms), *collect(vb, vs))
```

```python
import functools
import math

import numpy as np
import jax
import jax.numpy as jnp
from jax import lax
from jax.experimental import pallas as pl
from jax.experimental.pallas import tpu as pltpu

F32 = jnp.float32
BF16 = jnp.bfloat16
HIGHEST = lax.Precision.HIGHEST
MESH = pl.DeviceIdType.MESH

D = 1024
A_HEADS = 4
A_QK = 128
A_V = 256
SOFTCAP = 15.0
N_GROUPS = 3
B_GROUPS = ((128, 1), (512, 4), (2048, 16))
B_HEADS = 16
B_DH = 64
BLK = 128
REL_BUCKETS = 32
REL_MAX_DIST = 2048
D_FF = 2816
EPS = 1e-6
ADAM_LR = 0.001
ADAM_B1 = 0.9
ADAM_B2 = 0.999
ADAM_EPS = 1e-08
ADAM_WD = 0.01
ADAM_STEP = 10

N_DEV = 8
V7X_VMEM_BYTES = 64 * 1024 * 1024
VMEM_LIMIT = V7X_VMEM_BYTES - 8 * 1024 * 1024
MLSTM_CHUNK = 256
Z_W = 3328
Z_GI = 3072
Z_GF = 3200


def _cp(sem):
    return pltpu.CompilerParams(dimension_semantics=sem, vmem_limit_bytes=VMEM_LIMIT)


def _dot(a, b, **kw):
    return jnp.dot(a, b, preferred_element_type=F32, **kw)


def _dot_nt(a, b):
    return lax.dot_general(a, b, (((1,), (1,)), ((), ())), preferred_element_type=F32)


def _dot_tn(a, b):
    return lax.dot_general(a, b, (((0,), (0,)), ((), ())), preferred_element_type=F32)


def _dot_split(a, b01):
    hi = a.astype(BF16)
    lo = (a - hi.astype(F32)).astype(BF16)
    return _dot(hi, b01) + _dot(lo, b01)


def mm(a, b, *, name, out_dtype, tn, tm=1024, res=None):
    M, K = a.shape
    N = b.shape[1]
    assert M % tm == 0 and N % tn == 0 and b.shape[0] == K

    def body(a_ref, b_ref, *rest):
        o_ref = rest[-1]
        acc = _dot(a_ref[...].astype(BF16), b_ref[...])
        if res is not None:
            acc = acc + rest[0][...]
        o_ref[...] = acc.astype(out_dtype)

    in_specs = [pl.BlockSpec((tm, K), lambda j, i: (i, 0)), pl.BlockSpec((K, tn), lambda j, i: (0, j))]
    args = [a, b]
    if res is not None:
        in_specs.append(pl.BlockSpec((tm, tn), lambda j, i: (i, j)))
        args.append(res)
    return pl.pallas_call(
        body, name=name, grid=(N // tn, M // tm), in_specs=in_specs,
        out_specs=pl.BlockSpec((tm, tn), lambda j, i: (i, j)),
        out_shape=jax.ShapeDtypeStruct((M, N), out_dtype),
        compiler_params=_cp(("parallel", "parallel")),
    )(*args)


def mm_sum(a_list, b_list, *, name, tm=512):
    M, K = a_list[0].shape
    N = b_list[0].shape[1]
    n = len(a_list)
    assert M % tm == 0

    def body(*refs):
        o_ref = refs[-1]
        acc = _dot(refs[0][...], refs[n][...])
        for i in range(1, n):
            acc = acc + _dot(refs[i][...], refs[n + i][...])
        o_ref[...] = acc

    return pl.pallas_call(
        body, name=name, grid=(M // tm,),
        in_specs=[pl.BlockSpec((tm, K), lambda i: (i, 0))] * n + [pl.BlockSpec((K, N), lambda i: (0, 0))] * n,
        out_specs=pl.BlockSpec((tm, N), lambda i: (i, 0)),
        out_shape=jax.ShapeDtypeStruct((M, N), F32),
        compiler_params=_cp(("parallel",)),
    )(*a_list, *b_list)


def mm_tn(a, b, *, name, tn, tk=1024):
    S, Kd = a.shape
    N = b.shape[1]
    assert S % tk == 0 and N % tn == 0 and b.shape[0] == S

    def body(a_ref, b_ref, o_ref):
        @pl.when(pl.program_id(1) == 0)
        def _():
            o_ref[...] = jnp.zeros_like(o_ref)

        o_ref[...] += _dot_tn(a_ref[...].astype(BF16), b_ref[...].astype(BF16))

    return pl.pallas_call(
        body, name=name, grid=(N // tn, S // tk),
        in_specs=[pl.BlockSpec((tk, Kd), lambda j, k: (k, 0)), pl.BlockSpec((tk, tn), lambda j, k: (k, j))],
        out_specs=pl.BlockSpec((Kd, tn), lambda j, k: (0, j)),
        out_shape=jax.ShapeDtypeStruct((Kd, N), F32),
        compiler_params=_cp(("parallel", "arbitrary")),
    )(a, b)


def rms_fwd(x, gains, *, name, tm=1024):
    S = x.shape[0]
    n = len(gains)

    def body(x_ref, *rest):
        xf = x_ref[...]
        y = xf * lax.rsqrt(jnp.mean(xf * xf, axis=-1, keepdims=True) + EPS)
        for i in range(n):
            rest[n + i][...] = (y * rest[i][...]).astype(BF16)

    return pl.pallas_call(
        body, name=name, grid=(S // tm,),
        in_specs=[pl.BlockSpec((tm, D), lambda i: (i, 0))] + [pl.BlockSpec((1, D), lambda i: (0, 0))] * n,
        out_specs=[pl.BlockSpec((tm, D), lambda i: (i, 0))] * n,
        out_shape=[jax.ShapeDtypeStruct((S, D), BF16)] * n,
        compiler_params=_cp(("parallel",)),
    )(x, *gains)


def mm_rms_bwd(a_list, b_list, g, x, dres, extra=(), *, name, tm=512):
    S = x.shape[0]
    n = len(a_list)
    ne = len(extra)

    def body(*refs):
        a_refs, b_refs = refs[:n], refs[n:2 * n]
        g_ref, x_ref, dres_ref = refs[2 * n:2 * n + 3]
        e_refs = refs[2 * n + 3:2 * n + 3 + 2 * ne]
        dx_ref = refs[2 * n + 3 + 2 * ne]
        dg_refs = refs[2 * n + 4 + 2 * ne:]

        @pl.when(pl.program_id(0) == 0)
        def _():
            for r in dg_refs:
                r[...] = jnp.zeros_like(r)

        acc = _dot(a_refs[0][...], b_refs[0][...])
        for i in range(1, n):
            acc = acc + _dot(a_refs[i][...], b_refs[i][...])
        xf = x_ref[...]
        r = lax.rsqrt(jnp.mean(xf * xf, axis=-1, keepdims=True) + EPS)
        xhat = xf * r
        total = dres_ref[...]
        branches = [(acc, g_ref[...])] + [(e_refs[2 * i][...], e_refs[2 * i + 1][...]) for i in range(ne)]
        for i, (dy, gg) in enumerate(branches):
            dg_refs[i][...] += jnp.sum(dy * xhat, axis=0, keepdims=True)
            dyg = dy * gg
            total = total + r * (dyg - xhat * jnp.mean(dyg * xhat, axis=-1, keepdims=True))
        dx_ref[...] = total

    row = pl.BlockSpec((tm, D), lambda i: (i, 0))
    vec = pl.BlockSpec((1, D), lambda i: (0, 0))
    in_specs = ([pl.BlockSpec((tm, a.shape[1]), lambda i: (i, 0)) for a in a_list]
                + [pl.BlockSpec(b.shape, lambda i: (0, 0)) for b in b_list] + [vec, row, row])
    args = list(a_list) + list(b_list) + [g, x, dres]
    for dxn_e, g_e in extra:
        in_specs += [row, vec]
        args += [dxn_e, g_e]
    outs = pl.pallas_call(
        body, name=name, grid=(S // tm,), in_specs=in_specs,
        out_specs=[row] + [vec] * (1 + ne),
        out_shape=[jax.ShapeDtypeStruct((S, D), F32)] + [jax.ShapeDtypeStruct((1, D), F32)] * (1 + ne),
        compiler_params=_cp(("arbitrary",)),
    )(*args)
    return outs[0], outs[1:]


def loss_head(x, target, g, *, name, tm=512):
    S = x.shape[0]

    def body(x_ref, t_ref, g_ref, dx_ref, dg_ref, loss_ref):
        @pl.when(pl.program_id(0) == 0)
        def _():
            dg_ref[...] = jnp.zeros_like(dg_ref)
            loss_ref[...] = jnp.zeros_like(loss_ref)

        xf = x_ref[...]
        gg = g_ref[...]
        r = lax.rsqrt(jnp.mean(xf * xf, axis=-1, keepdims=True) + EPS)
        xhat = xf * r
        e = xhat * gg - t_ref[...]
        per_tok = jnp.mean(e * e, axis=-1, keepdims=True)
        loss_ref[...] += 0.5 * jnp.sum(per_tok, axis=0, keepdims=True)
        dy = e * (1.0 / D)
        dg_ref[...] += jnp.sum(dy * xhat, axis=0, keepdims=True)
        dyg = dy * gg
        dx_ref[...] = r * (dyg - xhat * jnp.mean(dyg * xhat, axis=-1, keepdims=True))

    row = pl.BlockSpec((tm, D), lambda i: (i, 0))
    vec = pl.BlockSpec((1, D), lambda i: (0, 0))
    return pl.pallas_call(
        body, name=name, grid=(S // tm,),
        in_specs=[row, row, vec],
        out_specs=[row, vec, pl.BlockSpec((1, 128), lambda i: (0, 0))],
        out_shape=[jax.ShapeDtypeStruct((S, D), F32), jax.ShapeDtypeStruct((1, D), F32),
                   jax.ShapeDtypeStruct((1, 128), F32)],
        compiler_params=_cp(("arbitrary",)),
    )(x, target, g)


def _sigmoid(x):
    return 1.0 / (1.0 + jnp.exp(-x))


def _gates(z_ref, bi_ref, bf_ref):
    li = SOFTCAP * jnp.tanh((z_ref[:, Z_GI:Z_GI + 128] + bi_ref[...]) * (1.0 / SOFTCAP))
    scf = SOFTCAP * jnp.tanh((z_ref[:, Z_GF:Z_GF + 128] + bf_ref[...]) * (1.0 / SOFTCAP))
    lf = jnp.minimum(scf, 0.0) - jnp.log(1.0 + jnp.exp(-jnp.abs(scf)))
    return li, scf, lf


def _tri(L, lower):
    r = lax.broadcasted_iota(jnp.int32, (L, L), 0)
    c = lax.broadcasted_iota(jnp.int32, (L, L), 1)
    return (r >= c) if lower else (r <= c)


def mlstm_fwd(z, bi, bf, *, name):
    S = z.shape[0]
    L = MLSTM_CHUNK
    NC = S // L
    scale = A_QK ** -0.5

    def body(z_ref, bi_ref, bf_ref, h_ref, cst_ref, nst_ref, C_s, n_s):
        @pl.when(pl.program_id(0) == 0)
        def _():
            C_s[...] = jnp.zeros_like(C_s)
            n_s[...] = jnp.zeros_like(n_s)

        li, _, lf = _gates(z_ref, bi_ref, bf_ref)
        causal = _tri(L, True)
        b = _dot(causal.astype(F32), lf, precision=HIGHEST)
        liT = li.T
        bT = b.T
        cst_ref[0] = C_s[...].astype(BF16)
        nst_ref[0] = n_s[...]
        for h in range(A_HEADS):
            q = z_ref[:, h * A_QK:(h + 1) * A_QK] * scale
            k = z_ref[:, 512 + h * A_QK:512 + (h + 1) * A_QK]
            qb = q.astype(BF16)
            kb = k.astype(BF16)
            vb = z_ref[:, 1024 + h * A_V:1024 + (h + 1) * A_V].astype(BF16)
            a_col, b_col = li[:, h:h + 1], b[:, h:h + 1]
            a_row, b_row = liT[h:h + 1, :], bT[h:h + 1, :]
            Dm = jnp.exp(jnp.where(causal, b_col - b_row + a_row, -jnp.inf))
            A = _dot_nt(qb, kb) * Dm
            eb = jnp.exp(b_col)
            Ch = C_s[h]
            nh = n_s[h:h + 1, :]
            num = _dot(A.astype(BF16), vb) + eb * _dot(qb, Ch.astype(BF16))
            den = jnp.sum(A, axis=-1, keepdims=True) + eb * jnp.sum(q * nh, axis=-1, keepdims=True)
            h_ref[:, h * A_V:(h + 1) * A_V] = num / jnp.maximum(jnp.abs(den), 1.0)
            bL = b_col[L - 1:L, :]
            kw = jnp.exp(bL - b_col + a_col) * k
            decay = jnp.exp(bL)
            C_s[h] = decay * Ch + _dot_tn(kw.astype(BF16), vb)
            n_s[h:h + 1, :] = decay * nh + jnp.sum(kw, axis=0, keepdims=True)

    vec = pl.BlockSpec((1, 128), lambda c: (0, 0))
    return pl.pallas_call(
        body, name=name, grid=(NC,),
        in_specs=[pl.BlockSpec((L, Z_W), lambda c: (c, 0)), vec, vec],
        out_specs=[pl.BlockSpec((L, 1024), lambda c: (c, 0)),
                   pl.BlockSpec((1, A_HEADS, A_QK, A_V), lambda c: (c, 0, 0, 0)),
                   pl.BlockSpec((1, 8, 128), lambda c: (c, 0, 0))],
        out_shape=[jax.ShapeDtypeStruct((S, 1024), F32),
                   jax.ShapeDtypeStruct((NC, A_HEADS, A_QK, A_V), BF16),
                   jax.ShapeDtypeStruct((NC, 8, 128), F32)],
        scratch_shapes=[pltpu.VMEM((A_HEADS, A_QK, A_V), F32), pltpu.VMEM((8, 128), F32)],
        compiler_params=_cp(("arbitrary",)),
    )(z, bi, bf)


def mlstm_bwd(z, bi, bf, cst, nst, dh, dzo, *, name):
    S = z.shape[0]
    L = MLSTM_CHUNK
    NC = S // L
    scale = A_QK ** -0.5

    def body(z_ref, bi_ref, bf_ref, cst_ref, nst_ref, dh_ref, dzo_ref, dz_ref, db_ref, dC_s, dn_s):
        @pl.when(pl.program_id(0) == 0)
        def _():
            dC_s[...] = jnp.zeros_like(dC_s)
            dn_s[...] = jnp.zeros_like(dn_s)
            db_ref[...] = jnp.zeros_like(db_ref)

        li, scf, lf = _gates(z_ref, bi_ref, bf_ref)
        causal = _tri(L, True)
        b = _dot(causal.astype(F32), lf, precision=HIGHEST)
        liT = li.T
        bT = b.T
        lane = lax.broadcasted_iota(jnp.int32, (L, 128), 1)
        sub = lax.broadcasted_iota(jnp.int32, (128, L), 0)
        lane1 = lax.broadcasted_iota(jnp.int32, (1, 128), 1)
        Rm = jnp.zeros((L, 128), F32)
        KIm = jnp.zeros((L, 128), F32)
        csm = jnp.zeros((128, L), F32)
        Xm = jnp.zeros((1, 128), F32)
        for h in range(A_HEADS):
            q = z_ref[:, h * A_QK:(h + 1) * A_QK] * scale
            k = z_ref[:, 512 + h * A_QK:512 + (h + 1) * A_QK]
            qb = q.astype(BF16)
            kb = k.astype(BF16)
            vb = z_ref[:, 1024 + h * A_V:1024 + (h + 1) * A_V].astype(BF16)
            a_col, b_col = li[:, h:h + 1], b[:, h:h + 1]
            a_row, b_row = liT[h:h + 1, :], bT[h:h + 1, :]
            Dm = jnp.exp(jnp.where(causal, b_col - b_row + a_row, -jnp.inf))
            Sqk = _dot_nt(qb, kb)
            A = Sqk * Dm
            Ab = A.astype(BF16)
            eb = jnp.exp(b_col)
            Cb = cst_ref[0, h]
            nh = nst_ref[0, h:h + 1, :]
            num = _dot(Ab, vb) + eb * _dot(qb, Cb)
            den = jnp.sum(A, axis=-1, keepdims=True) + eb * jnp.sum(q * nh, axis=-1, keepdims=True)
            aden = jnp.abs(den)
            u = 1.0 / jnp.maximum(aden, 1.0)
            dhh = dh_ref[:, h * A_V:(h + 1) * A_V]
            dnum = dhh * u
            dden = jnp.where(aden > 1.0, -jnp.sum(dhh * num, axis=-1, keepdims=True) * u * u * jnp.sign(den), 0.0)
            dnb = dnum.astype(BF16)
            G = Dm * (_dot_nt(dnb, vb) + dden)
            Gb = G.astype(BF16)
            E = G * Sqk
            rs = jnp.sum(E, axis=-1, keepdims=True)
            cs = jnp.sum(E, axis=0, keepdims=True)
            dCh = dC_s[h]
            dnh = dn_s[h:h + 1, :]
            dCb = dCh.astype(BF16)
            bL = b_col[L - 1:L, :]
            wk = jnp.exp(bL - b_col + a_col)
            decay = jnp.exp(bL)
            dq_inter = eb * (_dot_nt(dnb, Cb) + dden * nh)
            dk_inter = wk * (_dot_nt(vb, dCb) + dnh)
            dq = _dot(Gb, kb) + dq_inter
            dk = _dot_tn(Gb, qb) + dk_inter
            dv = _dot_tn(Ab, dnb) + wk * _dot(kb, dCb)
            dz_ref[:, h * A_QK:(h + 1) * A_QK] = (dq * scale).astype(BF16)
            dz_ref[:, 512 + h * A_QK:512 + (h + 1) * A_QK] = dk.astype(BF16)
            dz_ref[:, 1024 + h * A_V:1024 + (h + 1) * A_V] = dv.astype(BF16)
            KI = jnp.sum(k * dk_inter, axis=-1, keepdims=True)
            R = rs + jnp.sum(q * dq_inter, axis=-1, keepdims=True)
            cross = (jnp.sum(jnp.sum(dCh * Cb.astype(F32), axis=0, keepdims=True), axis=1, keepdims=True)
                     + jnp.sum(dnh * nh, axis=1, keepdims=True))
            Xm = jnp.where(lane1 == h, decay * cross, Xm)
            Rm = jnp.where(lane == h, R, Rm)
            KIm = jnp.where(lane == h, KI, KIm)
            csm = jnp.where(sub == h, cs, csm)
            ebq = eb * q
            dC_s[h] = decay * dCh + _dot_tn(ebq.astype(BF16), dnb)
            dn_s[h:h + 1, :] = decay * dnh + jnp.sum(ebq * dden, axis=0, keepdims=True)
        dz_ref[:, 2048:3072] = dzo_ref[...]
        cs_col = csm.T
        da = cs_col + KIm
        rr = lax.broadcasted_iota(jnp.int32, (L, L), 0)
        cc = lax.broadcasted_iota(jnp.int32, (L, L), 1)
        dlf = (_dot((rr <= cc).astype(F32), Rm - cs_col, precision=HIGHEST)
               + _dot((rr > cc).astype(F32), KIm, precision=HIGHEST) + Xm)
        dpre_i = da * (1.0 - (li * (1.0 / SOFTCAP)) ** 2)
        dpre_f = dlf * (1.0 - _sigmoid(scf)) * (1.0 - (scf * (1.0 / SOFTCAP)) ** 2)
        dz_ref[:, Z_GI:Z_GI + 128] = dpre_i.astype(BF16)
        dz_ref[:, Z_GF:Z_GF + 128] = dpre_f.astype(BF16)
        db_ref[0:1, :] += jnp.sum(dpre_i, axis=0, keepdims=True)
        db_ref[1:2, :] += jnp.sum(dpre_f, axis=0, keepdims=True)

    vec = pl.BlockSpec((1, 128), lambda c: (0, 0))
    rev = lambda c: (NC - 1 - c, 0)
    return pl.pallas_call(
        body, name=name, grid=(NC,),
        in_specs=[pl.BlockSpec((L, Z_W), rev), vec, vec,
                  pl.BlockSpec((1, A_HEADS, A_QK, A_V), lambda c: (NC - 1 - c, 0, 0, 0)),
                  pl.BlockSpec((1, 8, 128), lambda c: (NC - 1 - c, 0, 0)),
                  pl.BlockSpec((L, 1024), rev), pl.BlockSpec((L, 1024), rev)],
        out_specs=[pl.BlockSpec((L, Z_W), rev), pl.BlockSpec((8, 128), lambda c: (0, 0))],
        out_shape=[jax.ShapeDtypeStruct((S, Z_W), BF16), jax.ShapeDtypeStruct((8, 128), F32)],
        scratch_shapes=[pltpu.VMEM((A_HEADS, A_QK, A_V), F32), pltpu.VMEM((8, 128), F32)],
        compiler_params=_cp(("arbitrary",)),
    )(z, bi, bf, cst, nst, dh, dzo)


def ao_fwd(h, z, gh, w_out, x, *, name, tm=512):
    S = h.shape[0]

    def body(h_ref, o_ref, gh_ref, w_ref, x_ref, x1_ref, hg_ref):
        for hd in range(A_HEADS):
            sl = slice(hd * A_V, (hd + 1) * A_V)
            hs = h_ref[:, sl]
            hn = hs * lax.rsqrt(jnp.mean(hs * hs, axis=-1, keepdims=True) + EPS) * gh_ref[:, sl]
            hg_ref[:, sl] = (hn * _sigmoid(o_ref[:, sl])).astype(BF16)
        x1_ref[...] = x_ref[...] + _dot(hg_ref[...], w_ref[...])

    row = pl.BlockSpec((tm, 1024), lambda i: (i, 0))
    return pl.pallas_call(
        body, name=name, grid=(S // tm,),
        in_specs=[row, pl.BlockSpec((tm, 1024), lambda i: (i, 2)), pl.BlockSpec((1, 1024), lambda i: (0, 0)),
                  pl.BlockSpec((1024, 1024), lambda i: (0, 0)), row],
        out_specs=[row, row],
        out_shape=[jax.ShapeDtypeStruct((S, 1024), F32), jax.ShapeDtypeStruct((S, 1024), BF16)],
        compiler_params=_cp(("parallel",)),
    )(h, z, gh, w_out, x)


def ao_bwd(g1, w_out_t, h, z, gh, *, name, tm=512):
    S = h.shape[0]

    def body(g_ref, w_ref, h_ref, o_ref, gh_ref, dh_ref, dzo_ref, dgh_ref):
        @pl.when(pl.program_id(0) == 0)
        def _():
            dgh_ref[...] = jnp.zeros_like(dgh_ref)

        dhg = _dot(g_ref[...].astype(BF16), w_ref[...])
        for hd in range(A_HEADS):
            sl = slice(hd * A_V, (hd + 1) * A_V)
            hs = h_ref[:, sl]
            r = lax.rsqrt(jnp.mean(hs * hs, axis=-1, keepdims=True) + EPS)
            hhat = hs * r
            ghs = gh_ref[:, sl]
            sig = _sigmoid(o_ref[:, sl])
            d = dhg[:, sl]
            dhn = d * sig
            dzo_ref[:, sl] = (d * hhat * ghs * sig * (1.0 - sig)).astype(BF16)
            dgh_ref[:, sl] += jnp.sum(dhn * hhat, axis=0, keepdims=True)
            dhhat = dhn * ghs
            dh_ref[:, sl] = r * (dhhat - hhat * jnp.mean(dhhat * hhat, axis=-1, keepdims=True))

    row = pl.BlockSpec((tm, 1024), lambda i: (i, 0))
    vec = pl.BlockSpec((1, 1024), lambda i: (0, 0))
    return pl.pallas_call(
        body, name=name, grid=(S // tm,),
        in_specs=[row, pl.BlockSpec((1024, 1024), lambda i: (0, 0)), row,
                  pl.BlockSpec((tm, 1024), lambda i: (i, 2)), vec],
        out_specs=[row, row, vec],
        out_shape=[jax.ShapeDtypeStruct((S, 1024), F32), jax.ShapeDtypeStruct((S, 1024), BF16),
                   jax.ShapeDtypeStruct((1, 1024), F32)],
        compiler_params=_cp(("arbitrary",)),
    )(g1, w_out_t, h, z, gh)


CONV_TC = 1408
CONV_HALO = 16


def mm_up_conv(xn, w_up, cw, cb, *, name, tm=512):
    S = xn.shape[0]
    TN = 2 * CONV_TC

    def body(x_ref, w_ref, cw_ref, cb_ref, u_ref, c_ref, a_ref, carry):
        @pl.when(pl.program_id(1) == 0)
        def _():
            carry[...] = jnp.zeros_like(carry)

        ub = _dot(x_ref[...], w_ref[...]).astype(BF16)
        u_ref[...] = ub
        u = ub.astype(F32)
        prev = carry[...]
        row = lax.broadcasted_iota(jnp.int32, u.shape, 0)
        u1 = jnp.where(row == 0, prev[7:8], pltpu.roll(u, 1, axis=0))
        u2 = jnp.where(row == 0, prev[6:7], jnp.where(row == 1, prev[7:8], pltpu.roll(u, 2, axis=0)))
        carry[...] = u[tm - 8:tm]
        w = cw_ref[...]
        c = u * w[2:3] + u1 * w[1:2] + u2 * w[0:1] + cb_ref[...]
        c_ref[...] = c.astype(BF16)
        g, v = c[:, :CONV_TC], c[:, CONV_TC:]
        a_ref[...] = (g * _sigmoid(g) * v).astype(BF16)

    big = pl.BlockSpec((tm, TN), lambda j, i: (i, j))
    return pl.pallas_call(
        body, name=name, grid=(2, S // tm),
        in_specs=[pl.BlockSpec((tm, D), lambda j, i: (i, 0)), pl.BlockSpec((D, TN), lambda j, i: (0, j)),
                  pl.BlockSpec((3, TN), lambda j, i: (0, j)), pl.BlockSpec((1, TN), lambda j, i: (0, j))],
        out_specs=[big, big, pl.BlockSpec((tm, CONV_TC), lambda j, i: (i, j))],
        out_shape=[jax.ShapeDtypeStruct((S, 2 * D_FF), BF16), jax.ShapeDtypeStruct((S, 2 * D_FF), BF16),
                   jax.ShapeDtypeStruct((S, D_FF), BF16)],
        scratch_shapes=[pltpu.VMEM((8, TN), F32)],
        compiler_params=_cp(("parallel", "arbitrary")),
    )(xn, w_up, cw, cb)


def mm_da_act(dxn, w_down_t, c, *, name, tm=512):
    S = c.shape[0]
    TN = 2 * CONV_TC

    def body(x_ref, w_ref, c_ref, duc_ref):
        dav = _dot(x_ref[...].astype(BF16), w_ref[...])
        g = c_ref[:, :CONV_TC].astype(F32)
        v = c_ref[:, CONV_TC:].astype(F32)
        sg = _sigmoid(g)
        gs = g * sg
        duc_ref[:, :CONV_TC] = (dav * v * (sg + gs * (1.0 - sg))).astype(BF16)
        duc_ref[:, CONV_TC:] = (dav * gs).astype(BF16)

    return pl.pallas_call(
        body, name=name, grid=(2, S // tm),
        in_specs=[pl.BlockSpec((tm, D), lambda j, i: (i, 0)), pl.BlockSpec((D, CONV_TC), lambda j, i: (0, j)),
                  pl.BlockSpec((tm, TN), lambda j, i: (i, j))],
        out_specs=pl.BlockSpec((tm, TN), lambda j, i: (i, j)),
        out_shape=jax.ShapeDtypeStruct((S, 2 * D_FF), BF16),
        compiler_params=_cp(("parallel", "parallel")),
    )(dxn, w_down_t, c)


def tn_up_conv(xn, duc, u, cw, *, name, tk=512):
    S, Kd = xn.shape
    N = duc.shape[1]
    hb = tk // CONV_HALO
    nblk = S // tk

    def body(x_ref, d_ref, halo_ref, u_ref, w_ref, o_ref, du_ref, dwb_ref):
        k = pl.program_id(1)

        @pl.when(k == 0)
        def _():
            o_ref[...] = jnp.zeros_like(o_ref)
            dwb_ref[...] = jnp.zeros_like(dwb_ref)

        d = d_ref[...].astype(F32)
        hl = jnp.where(k == nblk - 1, 0.0, halo_ref[...].astype(F32))
        row = lax.broadcasted_iota(jnp.int32, d.shape, 0)
        d1 = jnp.where(row == tk - 1, hl[0:1], pltpu.roll(d, tk - 1, axis=0))
        d2 = jnp.where(row == tk - 1, hl[1:2], jnp.where(row == tk - 2, hl[0:1], pltpu.roll(d, tk - 2, axis=0)))
        w = w_ref[...]
        du = (d * w[2:3] + d1 * w[1:2] + d2 * w[0:1]).astype(BF16)
        du_ref[...] = du
        o_ref[...] += _dot_tn(x_ref[...], du)
        u = u_ref[...].astype(F32)
        dwb_ref[0:1, :] += jnp.sum(u * d2, axis=0, keepdims=True)
        dwb_ref[1:2, :] += jnp.sum(u * d1, axis=0, keepdims=True)
        dwb_ref[2:3, :] += jnp.sum(u * d, axis=0, keepdims=True)
        dwb_ref[3:4, :] += jnp.sum(d, axis=0, keepdims=True)

    blk = pl.BlockSpec((tk, CONV_TC), lambda j, k: (k, j))
    return pl.pallas_call(
        body, name=name, grid=(N // CONV_TC, nblk),
        in_specs=[pl.BlockSpec((tk, Kd), lambda j, k: (k, 0)), blk,
                  pl.BlockSpec((CONV_HALO, CONV_TC), lambda j, k: (jnp.minimum((k + 1) * hb, nblk * hb - 1), j)),
                  blk, pl.BlockSpec((3, CONV_TC), lambda j, k: (0, j))],
        out_specs=[pl.BlockSpec((Kd, CONV_TC), lambda j, k: (0, j)), blk, pl.BlockSpec((8, CONV_TC), lambda j, k: (0, j))],
        out_shape=[jax.ShapeDtypeStruct((Kd, N), F32), jax.ShapeDtypeStruct((S, N), BF16),
                   jax.ShapeDtypeStruct((8, N), F32)],
        compiler_params=_cp(("parallel", "arbitrary")),
    )(xn, duc, duc, u, cw)


def _t5_bucket(dist):
    max_exact = REL_BUCKETS // 2
    d = np.maximum(dist, 0)
    log_ratio = np.log(np.maximum(d, 1) / max_exact) / math.log(REL_MAX_DIST / max_exact)
    large = np.minimum(max_exact + (log_ratio * (REL_BUCKETS - max_exact)).astype(np.int64), REL_BUCKETS - 1)
    return np.where(d < max_exact, d, large).astype(np.int32)


def _bucket_tables():
    delta = BLK + np.arange(BLK)[:, None] - np.arange(2 * BLK)[None, :]
    return np.stack([_t5_bucket(delta * dil) for _, dil in B_GROUPS]).astype(np.int32)


def bias_build(rel_bias, buckets, *, name):
    def body(rel_ref, bk_ref, o_ref):
        g = pl.program_id(0)
        bk = bk_ref[0]
        for h in range(B_HEADS):
            acc = jnp.zeros((BLK, 2 * BLK), F32)
            for bb in range(REL_BUCKETS):
                acc = jnp.where(bk == bb, rel_ref[bb, g * B_HEADS + h], acc)
            o_ref[0, h] = acc

    return pl.pallas_call(
        body, name=name, grid=(N_GROUPS,),
        in_specs=[pl.BlockSpec(memory_space=pltpu.SMEM), pl.BlockSpec((1, BLK, 2 * BLK), lambda g: (g, 0, 0))],
        out_specs=pl.BlockSpec((1, B_HEADS, BLK, 2 * BLK), lambda g: (g, 0, 0, 0)),
        out_shape=jax.ShapeDtypeStruct((N_GROUPS, B_HEADS, BLK, 2 * BLK), F32),
        compiler_params=_cp(("arbitrary",)),
    )(rel_bias, buckets)


def bias_grad(dbias, buckets, *, name):
    def body(db_ref, bk_ref, o_ref):
        g = pl.program_id(0)

        @pl.when(g == 0)
        def _():
            o_ref[...] = jnp.zeros_like(o_ref)

        bk = bk_ref[0]
        rr = lax.broadcasted_iota(jnp.int32, (REL_BUCKETS, 128), 0)
        cc = lax.broadcasted_iota(jnp.int32, (REL_BUCKETS, 128), 1)
        acc = jnp.zeros((REL_BUCKETS, 128), F32)
        for h in range(B_HEADS):
            dbh = db_ref[0, h]
            for bb in range(REL_BUCKETS):
                part = jnp.sum(jnp.where(bk == bb, dbh, 0.0), axis=0, keepdims=True)
                s = jnp.sum(part, axis=1, keepdims=True)
                acc = acc + jnp.where((rr == bb) & (cc == g * B_HEADS + h), s, 0.0)
        o_ref[...] += acc

    return pl.pallas_call(
        body, name=name, grid=(N_GROUPS,),
        in_specs=[pl.BlockSpec((1, B_HEADS, BLK, 2 * BLK), lambda g: (g, 0, 0, 0)),
                  pl.BlockSpec((1, BLK, 2 * BLK), lambda g: (g, 0, 0))],
        out_specs=pl.BlockSpec((REL_BUCKETS, 128), lambda g: (0, 0)),
        out_shape=jax.ShapeDtypeStruct((REL_BUCKETS, 128), F32),
        compiler_params=_cp(("arbitrary",)),
    )(dbias, buckets)


HG = 4
GW = HG * B_DH


def _head_masks(dtype):
    lane = lax.broadcasted_iota(jnp.int32, (BLK, GW), 1)
    return [((lane >= h * B_DH) & (lane < (h + 1) * B_DH)).astype(dtype) for h in range(HG)]


def _band_masks():
    iq = lax.broadcasted_iota(jnp.int32, (BLK, BLK), 0)
    ik = lax.broadcasted_iota(jnp.int32, (BLK, BLK), 1)
    return iq <= ik, iq >= ik


def attn_fwd(qg, kg, vg, bias, g, dil, *, name):
    S = qg.shape[0]
    S2 = S // dil
    nb = S2 // BLK
    W = dil * 1024
    scale = B_DH ** -0.5

    def body(q_ref, kc_ref, kp_ref, vc_ref, vp_ref, b_ref, o_ref, lse_ref):
        has_prev = pl.program_id(1) > 0
        vp_m, vc_m = _band_masks()
        valid = jnp.concatenate([vp_m & has_prev, vc_m], axis=1)
        mb = _head_masks(BF16)
        mf = _head_masks(F32)
        lane = lax.broadcasted_iota(jnp.int32, (BLK, 128), 1)
        lse_acc = jnp.zeros((BLK, 128), F32)
        for hg in range(B_HEADS // HG):
            sl = slice(hg * GW, (hg + 1) * GW)
            q4 = q_ref[:, sl]
            kcat = jnp.concatenate([kp_ref[:, sl], kc_ref[:, sl]], axis=0)
            vcat = jnp.concatenate([vp_ref[:, sl], vc_ref[:, sl]], axis=0)
            s4 = _dot_nt(jnp.concatenate([q4 * mb[h] for h in range(HG)], axis=0), kcat)
            ps, rl = [], []
            for h in range(HG):
                hh = hg * HG + h
                s = jnp.where(valid, s4[h * BLK:(h + 1) * BLK] * scale + b_ref[0, hh], -jnp.inf)
                m = jnp.max(s, axis=-1, keepdims=True)
                p = jnp.exp(s - m)
                l = jnp.sum(p, axis=-1, keepdims=True)
                ps.append(p.astype(BF16))
                rl.append(1.0 / l)
                lse_acc = jnp.where(lane == hh, m + jnp.log(l), lse_acc)
            o4 = _dot(jnp.concatenate(ps, axis=0), vcat)
            acc = jnp.zeros((BLK, GW), F32)
            for h in range(HG):
                acc = acc + o4[h * BLK:(h + 1) * BLK] * (mf[h] * rl[h])
            o_ref[:, sl] = acc
        lse_ref[...] = lse_acc

    cur = pl.BlockSpec((BLK, 1024), lambda r, n: (n, r))
    prev = pl.BlockSpec((BLK, 1024), lambda r, n: (jnp.maximum(n - 1, 0), r))
    q2, k2, v2 = qg.reshape(S2, W), kg.reshape(S2, W), vg.reshape(S2, W)
    o, lse = pl.pallas_call(
        body, name=name, grid=(dil, nb),
        in_specs=[cur, cur, prev, cur, prev, pl.BlockSpec((1, B_HEADS, BLK, 2 * BLK), lambda r, n: (g, 0, 0, 0))],
        out_specs=[cur, pl.BlockSpec((BLK, 128), lambda r, n: (n, r))],
        out_shape=[jax.ShapeDtypeStruct((S2, W), F32), jax.ShapeDtypeStruct((S2, dil * 128), F32)],
        compiler_params=_cp(("parallel", "arbitrary")),
    )(q2, k2, k2, v2, v2, bias)
    return o.reshape(S, 1024), lse.reshape(S, 128)


def attn_merge(os_, lses, *, name, tm=512):
    S = os_[0].shape[0]
    expand = np.zeros((128, 1024), np.float32)
    for h in range(B_HEADS):
        expand[h, h * B_DH:(h + 1) * B_DH] = 1.0
    expand = jnp.asarray(expand, BF16)

    def body(o0, o1, o2, l0, l1, l2, e_ref, out_ref, lse_ref):
        ls = [l0[...], l1[...], l2[...]]
        m = jnp.maximum(jnp.maximum(ls[0], ls[1]), ls[2])
        ws = [jnp.exp(l - m) for l in ls]
        tot = ws[0] + ws[1] + ws[2]
        lse_ref[...] = m + jnp.log(tot)
        acc = jnp.zeros((tm, 1024), F32)
        for w, o in zip(ws, (o0, o1, o2)):
            acc = acc + _dot_split(w / tot, e_ref[...]) * o[...]
        out_ref[...] = acc.astype(BF16)

    row = pl.BlockSpec((tm, 1024), lambda i: (i, 0))
    lrow = pl.BlockSpec((tm, 128), lambda i: (i, 0))
    return pl.pallas_call(
        body, name=name, grid=(S // tm,),
        in_specs=[row, row, row, lrow, lrow, lrow, pl.BlockSpec((128, 1024), lambda i: (0, 0))],
        out_specs=[row, lrow],
        out_shape=[jax.ShapeDtypeStruct((S, 1024), BF16), jax.ShapeDtypeStruct((S, 128), F32)],
        compiler_params=_cp(("parallel",)),
    )(*os_, *lses, expand)


def mm_dout(dx, w_t, att, *, name, tm=512):
    S = dx.shape[0]
    heads = np.zeros((1024, 128), np.float32)
    for h in range(B_HEADS):
        heads[h * B_DH:(h + 1) * B_DH, h] = 1.0

    def body(x_ref, w_ref, att_ref, e_ref, do_ref, d_ref):
        acc = _dot(x_ref[...].astype(BF16), w_ref[...])
        do_ref[...] = acc.astype(BF16)
        d_ref[...] = _dot_split(acc * att_ref[...].astype(F32), e_ref[...])

    row = pl.BlockSpec((tm, 1024), lambda i: (i, 0))
    return pl.pallas_call(
        body, name=name, grid=(S // tm,),
        in_specs=[row, pl.BlockSpec((1024, 1024), lambda i: (0, 0)), row, pl.BlockSpec((1024, 128), lambda i: (0, 0))],
        out_specs=[row, pl.BlockSpec((tm, 128), lambda i: (i, 0))],
        out_shape=[jax.ShapeDtypeStruct((S, 1024), BF16), jax.ShapeDtypeStruct((S, 128), F32)],
        compiler_params=_cp(("parallel",)),
    )(dx, w_t, att, jnp.asarray(heads, BF16))


def attn_bwd(qg, kg, vg, bias, dout, dsum, lse, g, dil, *, name):
    S = qg.shape[0]
    S2 = S // dil
    nb = S2 // BLK
    W = dil * 1024
    scale = B_DH ** -0.5

    def body(q_ref, kc_ref, kp_ref, vc_ref, vp_ref, b_ref, do_ref, dsum_ref, lse_ref,
             dq_ref, dk_ref, dv_ref, db_ref, ck_s, cv_s):
        n = pl.program_id(1)

        @pl.when((pl.program_id(0) == 0) & (n == 0))
        def _():
            db_ref[...] = jnp.zeros_like(db_ref)

        @pl.when(n == 0)
        def _():
            ck_s[...] = jnp.zeros_like(ck_s)
            cv_s[...] = jnp.zeros_like(cv_s)

        @pl.when(n == nb)
        def _():
            dk_ref[...] = ck_s[...].astype(BF16)
            dv_ref[...] = cv_s[...].astype(BF16)

        @pl.when(n < nb)
        def _():
            vp_m, vc_m = _band_masks()
            valid = jnp.concatenate([vp_m & (n > 0), vc_m], axis=1)
            mb = _head_masks(BF16)
            mf = _head_masks(F32)
            lse_blk = lse_ref[...]
            dsum_blk = dsum_ref[...]
            for hg in range(B_HEADS // HG):
                sl = slice(hg * GW, (hg + 1) * GW)
                kcat = jnp.concatenate([kp_ref[:, sl], kc_ref[:, sl]], axis=0)
                vcat = jnp.concatenate([vp_ref[:, sl], vc_ref[:, sl]], axis=0)
                dob = do_ref[:, sl]
                q4 = q_ref[:, sl]
                q4m = jnp.concatenate([q4 * mb[h] for h in range(HG)], axis=0)
                do4m = jnp.concatenate([dob * mb[h] for h in range(HG)], axis=0)
                s4 = _dot_nt(q4m, kcat)
                dp4 = _dot_nt(do4m, vcat)
                ps, dss = [], []
                for h in range(HG):
                    hh = hg * HG + h
                    rows = slice(h * BLK, (h + 1) * BLK)
                    s = jnp.where(valid, s4[rows] * scale + b_ref[0, hh] - lse_blk[:, hh:hh + 1], -jnp.inf)
                    p = jnp.exp(s)
                    ds = p * (dp4[rows] - dsum_blk[:, hh:hh + 1])
                    db_ref[hh] += ds
                    ps.append(p.astype(BF16))
                    dss.append(ds.astype(BF16))
                p4 = jnp.concatenate(ps, axis=0)
                ds4 = jnp.concatenate(dss, axis=0)
                dq4 = _dot(ds4, kcat)
                acc = jnp.zeros((BLK, GW), F32)
                for h in range(HG):
                    acc = acc + dq4[h * BLK:(h + 1) * BLK] * mf[h]
                dq_ref[:, sl] = (acc * scale).astype(BF16)
                dkc = _dot_tn(ds4, q4m) * scale
                dvc = _dot_tn(p4, do4m)
                dk_ref[:, sl] = (ck_s[:, sl] + dkc[0:BLK]).astype(BF16)
                dv_ref[:, sl] = (cv_s[:, sl] + dvc[0:BLK]).astype(BF16)
                ck_s[:, sl] = dkc[BLK:2 * BLK]
                cv_s[:, sl] = dvc[BLK:2 * BLK]

    last = nb - 1
    cur = lambda r, n: (jnp.minimum(n, last), r)
    prev = lambda r, n: (jnp.clip(n - 1, 0, last), r)
    row = lambda im: pl.BlockSpec((BLK, 1024), im)
    k2, v2 = kg.reshape(S2, W), vg.reshape(S2, W)
    dq, dk, dv, dbias = pl.pallas_call(
        body, name=name, grid=(dil, nb + 1),
        in_specs=[row(cur), row(cur), row(prev), row(cur), row(prev),
                  pl.BlockSpec((1, B_HEADS, BLK, 2 * BLK), lambda r, n: (g, 0, 0, 0)),
                  row(cur), pl.BlockSpec((BLK, 128), cur), pl.BlockSpec((BLK, 128), cur)],
        out_specs=[row(cur), row(prev), row(prev), pl.BlockSpec((B_HEADS, BLK, 2 * BLK), lambda r, n: (0, 0, 0))],
        out_shape=[jax.ShapeDtypeStruct((S2, W), BF16)] * 3 + [jax.ShapeDtypeStruct((B_HEADS, BLK, 2 * BLK), F32)],
        scratch_shapes=[pltpu.VMEM((BLK, 1024), F32), pltpu.VMEM((BLK, 1024), F32)],
        compiler_params=_cp(("arbitrary", "arbitrary")),
    )(qg.reshape(S2, W), k2, k2, v2, v2, bias, dout.reshape(S2, W), dsum.reshape(S2, dil * 128),
      lse.reshape(S2, dil * 128))
    return dq.reshape(S, 1024), dk.reshape(S, 1024), dv.reshape(S, 1024), dbias


def _slot(px, py, pc):
    return 4 * px + 2 * py + pc


def ag_weights(wb, ws):
    def body(wb_ref, ws_ref, ob_ref, os_ref, send_sems, recv_sems, local_sems):
        x, y, c = lax.axis_index("x"), lax.axis_index("y"), lax.axis_index("c")
        me, sibling = (x, y, c), (x, y, 1 - c)
        chips = [(1 - x, y), (x, 1 - y), (1 - x, 1 - y)]
        arrays = [(wb_ref, ob_ref), (ws_ref, os_ref)]

        def copy(a, k, block, to, from_input=False):
            src_in, out = arrays[a]
            dst = out.at[_slot(*block)]
            return pltpu.make_async_remote_copy(
                src_ref=src_in if from_input else dst, dst_ref=dst,
                send_sem=send_sems.at[7 * a + k], recv_sem=recv_sems.at[7 * a + k],
                device_id=to, device_id_type=MESH)

        mine = [pltpu.make_async_copy(arrays[a][0], arrays[a][1].at[_slot(*me)], local_sems.at[a]) for a in range(2)]
        for cp in mine:
            cp.start()
        first = []
        for a in range(2):
            first.append(copy(a, 0, me, sibling, True))
            first += [copy(a, 1 + j, me, (*chip, c), True) for j, chip in enumerate(chips)]
        for cp in first:
            cp.start()
        passed = []
        for a in range(2):
            for j, chip in enumerate(chips):
                copy(a, 1 + j, (*chip, c), me).wait_recv()
                fw = copy(a, 4 + j, (*chip, c), sibling)
                fw.start()
                passed.append(fw)
        for a in range(2):
            copy(a, 0, sibling, me).wait_recv()
            for j, chip in enumerate(chips):
                copy(a, 4 + j, (*chip, 1 - c), me).wait_recv()
        for cp in first + passed:
            cp.wait_send()
        for cp in mine:
            cp.wait()

    any_spec = pl.BlockSpec(memory_space=pl.ANY)
    return pl.pallas_call(
        body, name="ag_weights",
        in_specs=[any_spec, any_spec], out_specs=[any_spec, any_spec],
        out_shape=[jax.ShapeDtypeStruct((N_DEV,) + wb.shape, wb.dtype), jax.ShapeDtypeStruct((N_DEV,) + ws.shape, ws.dtype)],
        scratch_shapes=[pltpu.SemaphoreType.DMA((14,)), pltpu.SemaphoreType.DMA((14,)), pltpu.SemaphoreType.DMA((2,))],
    )(wb, ws)


def rs_sibling(gpack, spack):
    def body(g_ref, s_ref, rb_ref, sa_ref, send_sems, recv_sems, local_sem):
        x, y, c = lax.axis_index("x"), lax.axis_index("y"), lax.axis_index("c")
        my = _slot(x, y, c)
        mine = pltpu.make_async_copy(s_ref, sa_ref.at[my], local_sem)
        mine.start()
        sends, recvs = [], []
        for j in range(4):
            both = dict(dst_ref=rb_ref.at[j], send_sem=send_sems.at[j], recv_sem=recv_sems.at[j],
                        device_id=(x, y, 1 - c), device_id_type=MESH)
            sends.append(pltpu.make_async_remote_copy(src_ref=g_ref.at[2 * j + 1 - c], **both))
            recvs.append(sends[-1])
        for k in range(1, N_DEV):
            peer = (1 - x if k & 4 else x, 1 - y if k & 2 else y, 1 - c if k & 1 else c)
            sems = dict(send_sem=send_sems.at[3 + k], recv_sem=recv_sems.at[3 + k], device_id=peer, device_id_type=MESH)
            sends.append(pltpu.make_async_remote_copy(src_ref=s_ref, dst_ref=sa_ref.at[my], **sems))
            recvs.append(pltpu.make_async_remote_copy(src_ref=s_ref, dst_ref=sa_ref.at[_slot(*peer)], **sems))
        for cp in sends:
            cp.start()
        for cp in recvs:
            cp.wait_recv()
        for cp in sends:
            cp.wait_send()
        mine.wait()

    any_spec = pl.BlockSpec(memory_space=pl.ANY)
    return pl.pallas_call(
        body, name="rs_sibling",
        in_specs=[any_spec, any_spec], out_specs=[any_spec, any_spec],
        out_shape=[jax.ShapeDtypeStruct((4,) + gpack.shape[1:], gpack.dtype),
                   jax.ShapeDtypeStruct((N_DEV,) + spack.shape, spack.dtype)],
        scratch_shapes=[pltpu.SemaphoreType.DMA((11,)), pltpu.SemaphoreType.DMA((11,)), pltpu.SemaphoreType.DMA],
    )(gpack, spack)


def pair_add(a, b, *, name, tr):
    R = a.shape[1]

    def body(a_ref, b_ref, o_ref):
        o_ref[...] = (a_ref[...].astype(F32) + b_ref[...].astype(F32)).astype(BF16)

    blk = pl.BlockSpec((4, tr, 1024), lambda i: (0, i, 0))
    return pl.pallas_call(
        body, name=name, grid=(R // tr,), in_specs=[blk, blk], out_specs=blk,
        out_shape=jax.ShapeDtypeStruct(a.shape, BF16), compiler_params=_cp(("parallel",)),
    )(a, b)


def rs_chips(part):
    def body(p_ref, rb_ref, send_sems, recv_sems, local_sem):
        x, y, c = lax.axis_index("x"), lax.axis_index("y"), lax.axis_index("c")
        jm = 2 * x + y
        mine = pltpu.make_async_copy(p_ref.at[jm], rb_ref.at[jm], local_sem)
        mine.start()
        sends, recvs = [], []
        for k in range(1, 4):
            px, py = (1 - x if k & 2 else x), (1 - y if k & 1 else y)
            sems = dict(send_sem=send_sems.at[k - 1], recv_sem=recv_sems.at[k - 1], device_id=(px, py, c), device_id_type=MESH)
            sends.append(pltpu.make_async_remote_copy(src_ref=p_ref.at[2 * px + py], dst_ref=rb_ref.at[jm], **sems))
            recvs.append(pltpu.make_async_remote_copy(src_ref=p_ref.at[jm], dst_ref=rb_ref.at[2 * px + py], **sems))
        for cp in sends:
            cp.start()
        for cp in recvs:
            cp.wait_recv()
        for cp in sends:
            cp.wait_send()
        mine.wait()

    any_spec = pl.BlockSpec(memory_space=pl.ANY)
    return pl.pallas_call(
        body, name="rs_chips", in_specs=[any_spec], out_specs=any_spec,
        out_shape=jax.ShapeDtypeStruct(part.shape, part.dtype),
        scratch_shapes=[pltpu.SemaphoreType.DMA((3,)), pltpu.SemaphoreType.DMA((3,)), pltpu.SemaphoreType.DMA],
    )(part)


def reduce_adam(parts, w, m, v, *, name, tr):
    R = w.shape[0]
    n_parts = parts.shape[0]
    assert R % tr == 0
    c1 = 1.0 - ADAM_B1 ** ADAM_STEP
    c2 = 1.0 - ADAM_B2 ** ADAM_STEP

    def body(p_ref, w_ref, m_ref, v_ref, g_ref, d_ref, mo_ref, vo_ref):
        g = p_ref[0].astype(F32)
        for i in range(1, n_parts):
            g = g + p_ref[i].astype(F32)
        mn = ADAM_B1 * m_ref[...] + (1.0 - ADAM_B1) * g
        vn = ADAM_B2 * v_ref[...] + (1.0 - ADAM_B2) * (g * g)
        g_ref[...] = g
        mo_ref[...] = mn
        vo_ref[...] = vn
        d_ref[...] = -ADAM_LR * ((mn / c1) / (jnp.sqrt(vn / c2) + ADAM_EPS) + ADAM_WD * w_ref[...])

    row = pl.BlockSpec((tr, 1024), lambda i: (i, 0))
    return pl.pallas_call(
        body, name=name, grid=(R // tr,),
        in_specs=[pl.BlockSpec((n_parts, tr, 1024), lambda i: (0, i, 0)), row, row, row],
        out_specs=[row] * 4,
        out_shape=[jax.ShapeDtypeStruct((R, 1024), F32)] * 4,
        compiler_params=_cp(("parallel",)),
    )(parts, w, m, v)


BIG = (("a_w_in", 385, 400), ("a_w_out", 128, 128), ("w_kv", 768, 768), ("b_w_q", 384, 384),
       ("b_w_out", 128, 128), ("f_w_up", 1408, 1408), ("f_w_down", 704, 704))
SMALL_SHARDED = (("a_norm_g", 128), ("a_hnorm_g", 128), ("f_conv_w", 4224))
SMALL_ROWS = 48
PACK_ROWS = sum(b[2] for b in BIG) + SMALL_ROWS
REPL = (("kv_norm_g", 1024, 1), ("b_norm_g", 1024, 1), ("f_norm_g", 2048, 2), ("f_conv_b", 11264, 11),
        ("final_norm_g", 1024, 1), ("rel_bias", 1536, 2), ("a_b_if", 8, 1))
REPL_ROWS = 24
LOSS_ROW = 19


def _rows(a, rows, padded):
    a = a.reshape(rows, 1024)
    return a if padded == rows else jnp.pad(a, ((0, padded - rows), (0, 0)))


def pack_shards(t, dtype, with_small):
    parts = [_rows(t[n].astype(dtype), r, p) for n, r, p in BIG]
    if with_small:
        flat = jnp.concatenate([t[n].astype(dtype).reshape(-1) for n, _ in SMALL_SHARDED])
        parts.append(jnp.pad(flat, (0, SMALL_ROWS * 1024 - flat.shape[0])).reshape(SMALL_ROWS, 1024))
    return jnp.concatenate(parts, axis=0)


def unpack_shards(pack, shapes):
    out = {}
    r0 = 0
    for n, r, p in BIG:
        out[n] = pack[r0:r0 + r].reshape(shapes[n])
        r0 += p
    flat = pack[r0:r0 + SMALL_ROWS].reshape(-1)
    e0 = 0
    for n, e in SMALL_SHARDED:
        out[n] = flat[e0:e0 + e].reshape(shapes[n])
        e0 += e
    return out


def pack_repl(t):
    parts = []
    for n, e, r in REPL:
        parts.append(jnp.pad(t[n].astype(F32).reshape(-1), (0, r * 1024 - e)))
    rows = sum(r for _, _, r in REPL)
    parts.append(jnp.zeros(((REPL_ROWS - rows) * 1024,), F32))
    return jnp.concatenate(parts).reshape(REPL_ROWS, 1024)


def unpack_repl(pack, shapes):
    out = {}
    r0 = 0
    for n, e, r in REPL:
        out[n] = pack[r0:r0 + r].reshape(-1)[:e].reshape(shapes[n])
        r0 += r
    return out


def ff_blocks(a):
    b = [a[..., i * CONV_TC:(i + 1) * CONV_TC] for i in range(4)]
    return jnp.concatenate([b[0], b[2], b[1], b[3]], axis=-1)


def split_cols(full, n):
    lead = full.shape[:-1]
    return jnp.moveaxis(full.reshape(lead + (N_DEV, n)), -2, 0)


def join_cols(parts):
    t = jnp.moveaxis(parts, 0, -2)
    return t.reshape(t.shape[:-2] + (t.shape[-2] * t.shape[-1],))


def kernel(x, a_norm_g, a_w_in, a_b_if, a_hnorm_g, a_w_out, kv_norm_g, w_kv, b_norm_g, b_w_q, b_w_out, rel_bias, f_norm_g, f_w_up, f_conv_w, f_conv_b, f_w_down, final_norm_g, loss_target, m_a_norm_g, m_a_w_in, m_a_b_if, m_a_hnorm_g, m_a_w_out, m_kv_norm_g, m_w_kv, m_b_norm_g, m_b_w_q, m_b_w_out, m_rel_bias, m_f_norm_g, m_f_w_up, m_f_conv_w, m_f_conv_b, m_f_w_down, m_final_norm_g, v_a_norm_g, v_a_w_in, v_a_b_if, v_a_hnorm_g, v_a_w_out, v_kv_norm_g, v_w_kv, v_b_norm_g, v_b_w_q, v_b_w_out, v_rel_bias, v_f_norm_g, v_f_w_up, v_f_conv_w, v_f_conv_b, v_f_w_down, v_final_norm_g):
    names = ["a_norm_g", "a_w_in", "a_b_if", "a_hnorm_g", "a_w_out", "kv_norm_g", "w_kv", "b_norm_g", "b_w_q", "b_w_out",
             "rel_bias", "f_norm_g", "f_w_up", "f_conv_w", "f_conv_b", "f_w_down", "final_norm_g"]
    w = dict(zip(names, (a_norm_g, a_w_in, a_b_if, a_hnorm_g, a_w_out, kv_norm_g, w_kv, b_norm_g, b_w_q, b_w_out,
                         rel_bias, f_norm_g, f_w_up, f_conv_w, f_conv_b, f_w_down, final_norm_g)))
    mom = dict(zip(names, (m_a_norm_g, m_a_w_in, m_a_b_if, m_a_hnorm_g, m_a_w_out, m_kv_norm_g, m_w_kv, m_b_norm_g, m_b_w_q,
                           m_b_w_out, m_rel_bias, m_f_norm_g, m_f_w_up, m_f_conv_w, m_f_conv_b, m_f_w_down, m_final_norm_g)))
    vel = dict(zip(names, (v_a_norm_g, v_a_w_in, v_a_b_if, v_a_hnorm_g, v_a_w_out, v_kv_norm_g, v_w_kv, v_b_norm_g, v_b_w_q,
                           v_b_w_out, v_rel_bias, v_f_norm_g, v_f_w_up, v_f_conv_w, v_f_conv_b, v_f_w_down, v_final_norm_g)))
    shapes = {n: w[n].shape for n in names}
    S = x.shape[1]
    assert x.shape[0] == 1 and S % (16 * BLK) == 0 and S % 1024 == 0
    X0 = x.reshape(S, D)
    target = loss_target.reshape(S, D)

    wb_all, ws_all = ag_weights(pack_shards(w, BF16, False),
                                pack_shards(w, F32, True)[PACK_ROWS - SMALL_ROWS:])
    seg = {}
    r0 = 0
    for n, r, p in BIG:
        seg[n] = wb_all[:, r0:r0 + r]
        r0 += p
    W_in = join_cols(seg["a_w_in"].reshape(N_DEV, D, 385))
    W_in = jnp.concatenate([jnp.pad(W_in[:, :3076], ((0, 0), (0, 124))),
                            jnp.pad(W_in[:, 3076:3080], ((0, 0), (0, 124)))], axis=1)
    W_out = seg["a_w_out"].reshape(1024, D)
    W_kv = join_cols(seg["w_kv"].reshape(N_DEV, D, 768))
    W_q = join_cols(seg["b_w_q"].reshape(N_DEV, D, 384))
    W_bout = seg["b_w_out"].reshape(1024, D)
    W_up = join_cols(seg["f_w_up"].reshape(N_DEV, 2, D, 704))
    W_down = jnp.moveaxis(seg["f_w_down"].reshape(N_DEV, 2, 352, D), 0, 1).reshape(2, D_FF, D)
    sflat = ws_all.reshape(N_DEV, SMALL_ROWS * 1024)
    g_a = sflat[:, 0:128].reshape(1, D)
    g_h = jnp.moveaxis(sflat[:, 128:256].reshape(N_DEV, A_HEADS, 32), 0, 1).reshape(1, A_HEADS * A_V)
    conv_w = ff_blocks(join_cols(sflat[:, 256:256 + 4224].reshape(N_DEV, 2, 3, 704)))
    conv_b = ff_blocks(f_conv_b)
    W_up = ff_blocks(W_up)
    bi = jnp.pad(a_b_if[:, :A_HEADS], ((0, 0), (0, 128 - A_HEADS)))
    bfg = jnp.pad(a_b_if[:, A_HEADS:], ((0, 0), (0, 128 - A_HEADS)))
    buckets = jnp.asarray(_bucket_tables())

    (xn_a,) = rms_fwd(X0, [g_a], name="rms_a")
    z = mm(xn_a, W_in, name="mm_a_in", out_dtype=F32, tn=1664)
    h, cst, nst = mlstm_fwd(z, bi, bfg, name="mlstm_fwd")
    X1, hg = ao_fwd(h, z, g_h, W_out, X0, name="ao_fwd")

    def ffn_fwd(X, l, tag):
        (xn,) = rms_fwd(X, [f_norm_g[l:l + 1]], name="rms_f" + tag)
        u, c, a = mm_up_conv(xn, W_up[l], conv_w[l], conv_b[l:l + 1], name="mm_up_conv" + tag)
        Xn = mm(a, W_down[l], name="mm_down" + tag, out_dtype=F32, tn=1024, tm=512, res=X)
        return Xn, (xn, u, c, a)

    X2, sav0 = ffn_fwd(X1, 0, "0")

    xkn, xbn = rms_fwd(X2, [kv_norm_g.reshape(1, D), b_norm_g], name="rms_kv_b")
    bias = bias_build(rel_bias, buckets, name="bias_build")
    col = lambda wmat, i: wmat[:, i * 1024:(i + 1) * 1024]
    qs, kk, vv, og, lg = [], [], [], [], []
    for g, (_, dil) in enumerate(B_GROUPS):
        qs.append(mm(xbn, col(W_q, g), name="mm_q%d" % g, out_dtype=BF16, tn=1024))
        kk.append(mm(xkn, col(W_kv, g), name="mm_k%d" % g, out_dtype=BF16, tn=1024))
        vv.append(mm(xkn, col(W_kv, N_GROUPS + g), name="mm_v%d" % g, out_dtype=BF16, tn=1024))
        o_, l_ = attn_fwd(qs[g], kk[g], vv[g], bias, g, dil, name="attn_fwd%d" % g)
        og.append(o_)
        lg.append(l_)
    att, lse = attn_merge(og, lg, name="attn_merge")
    X3 = mm(att, W_bout, name="mm_b_out", out_dtype=F32, tn=1024, res=X2)
    X4, sav1 = ffn_fwd(X3, 1, "1")

    dX4, d_final_g, loss_part = loss_head(X4, target, final_norm_g.reshape(1, D), name="loss_head")

    def ffn_bwd(X, dXn, l, sav, tag):
        xn, u, c, a = sav
        dW_down = mm_tn(a, dXn, name="tn_down" + tag, tn=1024, tk=512)
        duc = mm_da_act(dXn, W_down[l].T, c, name="mm_da_act" + tag)
        dW_up, du, dwb = tn_up_conv(xn, duc, u, conv_w[l], name="tn_up_conv" + tag)
        dX, (dg,) = mm_rms_bwd([du], [W_up[l].T], f_norm_g[l:l + 1], X, dXn, name="mm_rms_bwd_f" + tag)
        return dX, dW_down, dW_up, dwb, dg

    dX3, dWd1, dWu1, dwb1, dgf1 = ffn_bwd(X3, dX4, 1, sav1, "1")

    dW_bout = mm_tn(att, dX3, name="tn_b_out", tn=1024)
    dout, dsum = mm_dout(dX3, W_bout.T, att, name="mm_dout")
    dqs, dks, dvs, dbias = [], [], [], []
    for g, (_, dil) in enumerate(B_GROUPS):
        dq_, dk_, dv_, db_ = attn_bwd(qs[g], kk[g], vv[g], bias, dout, dsum, lse, g, dil, name="attn_bwd%d" % g)
        dqs.append(dq_)
        dks.append(dk_)
        dvs.append(dv_)
        dbias.append(db_)
    d_rel = bias_grad(jnp.stack(dbias), buckets, name="bias_grad")[:, :N_GROUPS * B_HEADS]
    dW_q = jnp.concatenate([mm_tn(xbn, d_, name="tn_q%d" % g, tn=1024) for g, d_ in enumerate(dqs)], axis=1)
    dW_kv = jnp.concatenate([mm_tn(xkn, d_, name="tn_kv%d" % i, tn=1024) for i, d_ in enumerate(dks + dvs)], axis=1)
    W_qT, W_kvT = W_q.T, W_kv.T
    rows = lambda wmat, i: wmat[i * 1024:(i + 1) * 1024]
    dxn_kv = mm_sum(dks + dvs, [rows(W_kvT, i) for i in range(2 * N_GROUPS)], name="mm_dxn_kv")
    dX2, (dg_b, dg_kv) = mm_rms_bwd(dqs, [rows(W_qT, g) for g in range(N_GROUPS)], b_norm_g, X2, dX3,
                                    extra=[(dxn_kv, kv_norm_g.reshape(1, D))], name="mm_rms_bwd_b_kv")

    dX1, dWd0, dWu0, dwb0, dgf0 = ffn_bwd(X1, dX2, 0, sav0, "0")

    dW_out = mm_tn(hg, dX1, name="tn_a_out", tn=1024)
    dh, dzo, dgh = ao_bwd(dX1, W_out.T, h, z, g_h, name="ao_bwd")
    dz, db_if = mlstm_bwd(z, bi, bfg, cst, nst, dh, dzo, name="mlstm_bwd")
    dW_in = mm_tn(xn_a, dz, name="tn_a_in", tn=1664)
    dW_in = jnp.concatenate([dW_in[:, :3076], dW_in[:, Z_GF:Z_GF + 4]], axis=1)
    dX0, (dg_a,) = mm_rms_bwd([dz], [W_in.T], g_a, X0, dX1, name="mm_rms_bwd_a")

    dWu = ff_blocks(jnp.stack([dWu0, dWu1]))
    dWd = jnp.stack([dWd0, dWd1])
    dwb = ff_blocks(jnp.stack([dwb0, dwb1]))
    slots = [
        jnp.pad(split_cols(dW_in, 385).reshape(N_DEV, 385, 1024), ((0, 0), (0, 15), (0, 0))),
        dW_out.reshape(N_DEV, 128, 1024),
        split_cols(dW_kv, 768).reshape(N_DEV, 768, 1024),
        split_cols(dW_q, 384).reshape(N_DEV, 384, 1024),
        dW_bout.reshape(N_DEV, 128, 1024),
        split_cols(dWu, 704).reshape(N_DEV, 1408, 1024),
        jnp.moveaxis(dWd.reshape(2, N_DEV, 352, D), 1, 0).reshape(N_DEV, 704, 1024),
    ]
    small = jnp.concatenate([
        dg_a.reshape(N_DEV, 128),
        split_cols(dgh.reshape(A_HEADS, A_V), 32).reshape(N_DEV, 128),
        split_cols(dwb[:, 0:3], 704).reshape(N_DEV, 4224)], axis=1)
    slots.append(jnp.pad(small, ((0, 0), (0, SMALL_ROWS * 1024 - small.shape[1]))).reshape(N_DEV, SMALL_ROWS, 1024))
    gpack = jnp.concatenate([t.astype(BF16) for t in slots], axis=1)
    repl_g = {"kv_norm_g": dg_kv, "b_norm_g": dg_b, "f_norm_g": jnp.concatenate([dgf0, dgf1]),
              "f_conv_b": dwb[:, 3], "final_norm_g": d_final_g, "rel_bias": d_rel,
              "a_b_if": jnp.concatenate([db_if[0, :A_HEADS], db_if[1, :A_HEADS]])}
    spack = pack_repl(repl_g)
    spack = spack.at[LOSS_ROW, 0].set(loss_part[0, 0])

    from_sibling, sparts = rs_sibling(gpack, spack)
    own = lax.dynamic_index_in_dim(gpack.reshape(4, 2, PACK_ROWS, 1024), lax.axis_index("c"), axis=1, keepdims=False)
    parts = rs_chips(pair_add(own, from_sibling, name="rs_pair_add", tr=PACK_ROWS // 8))
    gb, db, mb, vb = reduce_adam(parts, pack_shards(w, F32, True), pack_shards(mom, F32, True),
                                 pack_shards(vel, F32, True), name="reduce_adam_big", tr=PACK_ROWS // 8)
    gs, ds, ms, vs = reduce_adam(sparts, pack_repl(w), pack_repl(mom), pack_repl(vel), name="reduce_adam_small", tr=REPL_ROWS)
    loss = gs[LOSS_ROW, 0]

    def collect(big, sm):
        t = unpack_shards(big, shapes)
        t.update(unpack_repl(sm, shapes))
        return [t[n] for n in names]

    return (loss, dX0.reshape(1, S, D), *collect(gb, gs), *collect(db, ds), *collect(mb, ms), *collect(vb, vs))
```

```python
import functools
import math

import numpy as np
import jax
import jax.numpy as jnp
from jax import lax
from jax.experimental import pallas as pl
from jax.experimental.pallas import tpu as pltpu

F32 = jnp.float32
BF16 = jnp.bfloat16
HIGHEST = lax.Precision.HIGHEST
MESH = pl.DeviceIdType.MESH

D = 1024
A_HEADS = 4
A_QK = 128
A_V = 256
SOFTCAP = 15.0
N_GROUPS = 3
B_GROUPS = ((128, 1), (512, 4), (2048, 16))
B_HEADS = 16
B_DH = 64
BLK = 128
REL_BUCKETS = 32
REL_MAX_DIST = 2048
D_FF = 2816
EPS = 1e-6
ADAM_LR = 0.001
ADAM_B1 = 0.9
ADAM_B2 = 0.999
ADAM_EPS = 1e-08
ADAM_WD = 0.01
ADAM_STEP = 10

N_DEV = 8
V7X_VMEM_BYTES = 64 * 1024 * 1024
VMEM_LIMIT = V7X_VMEM_BYTES - 8 * 1024 * 1024
MLSTM_CHUNK = 256
Z_W = 3328
Z_GI = 3072
Z_GF = 3200


def _cp(sem):
    return pltpu.CompilerParams(dimension_semantics=sem, vmem_limit_bytes=VMEM_LIMIT)


def _dot(a, b, **kw):
    return jnp.dot(a, b, preferred_element_type=F32, **kw)


def _dot_nt(a, b):
    return lax.dot_general(a, b, (((1,), (1,)), ((), ())), preferred_element_type=F32)


def _dot_tn(a, b):
    return lax.dot_general(a, b, (((0,), (0,)), ((), ())), preferred_element_type=F32)


def _dot_split(a, b01):
    hi = a.astype(BF16)
    lo = (a - hi.astype(F32)).astype(BF16)
    return _dot(hi, b01) + _dot(lo, b01)


def mm(a, b, *, name, out_dtype, tn, tm=1024, res=None):
    M, K = a.shape
    N = b.shape[1]
    assert M % tm == 0 and N % tn == 0 and b.shape[0] == K

    def body(a_ref, b_ref, *rest):
        o_ref = rest[-1]
        acc = _dot(a_ref[...].astype(BF16), b_ref[...])
        if res is not None:
            acc = acc + rest[0][...]
        o_ref[...] = acc.astype(out_dtype)

    in_specs = [pl.BlockSpec((tm, K), lambda j, i: (i, 0)), pl.BlockSpec((K, tn), lambda j, i: (0, j))]
    args = [a, b]
    if res is not None:
        in_specs.append(pl.BlockSpec((tm, tn), lambda j, i: (i, j)))
        args.append(res)
    return pl.pallas_call(
        body, name=name, grid=(N // tn, M // tm), in_specs=in_specs,
        out_specs=pl.BlockSpec((tm, tn), lambda j, i: (i, j)),
        out_shape=jax.ShapeDtypeStruct((M, N), out_dtype),
        compiler_params=_cp(("parallel", "parallel")),
    )(*args)


def mm_sum(a_list, b_list, *, name, tm=512):
    M, K = a_list[0].shape
    N = b_list[0].shape[1]
    n = len(a_list)
    assert M % tm == 0

    def body(*refs):
        o_ref = refs[-1]
        acc = _dot(refs[0][...], refs[n][...])
        for i in range(1, n):
            acc = acc + _dot(refs[i][...], refs[n + i][...])
        o_ref[...] = acc

    return pl.pallas_call(
        body, name=name, grid=(M // tm,),
        in_specs=[pl.BlockSpec((tm, K), lambda i: (i, 0))] * n + [pl.BlockSpec((K, N), lambda i: (0, 0))] * n,
        out_specs=pl.BlockSpec((tm, N), lambda i: (i, 0)),
        out_shape=jax.ShapeDtypeStruct((M, N), F32),
        compiler_params=_cp(("parallel",)),
    )(*a_list, *b_list)


def mm_view(a_view, w, *, name, tm=1024):
    S2, W = a_view.shape
    tm = min(tm, S2)

    def body(a_ref, w_ref, o_ref):
        o_ref[...] = _dot(a_ref[...], w_ref[...]).astype(BF16)

    blk = pl.BlockSpec((tm, 1024), lambda r, i: (i, r))
    return pl.pallas_call(
        body, name=name, grid=(W // 1024, S2 // tm),
        in_specs=[blk, pl.BlockSpec((1024, 1024), lambda r, i: (0, 0))], out_specs=blk,
        out_shape=jax.ShapeDtypeStruct((S2, W), BF16), compiler_params=_cp(("parallel", "parallel")),
    )(a_view, w)


def mm_tn(a, b, *, name, tn, tk=1024):
    S, Kd = a.shape
    N = b.shape[1]
    assert S % tk == 0 and N % tn == 0 and b.shape[0] == S

    def body(a_ref, b_ref, o_ref):
        @pl.when(pl.program_id(1) == 0)
        def _():
            o_ref[...] = jnp.zeros_like(o_ref)

        o_ref[...] += _dot_tn(a_ref[...].astype(BF16), b_ref[...].astype(BF16))

    return pl.pallas_call(
        body, name=name, grid=(N // tn, S // tk),
        in_specs=[pl.BlockSpec((tk, Kd), lambda j, k: (k, 0)), pl.BlockSpec((tk, tn), lambda j, k: (k, j))],
        out_specs=pl.BlockSpec((Kd, tn), lambda j, k: (0, j)),
        out_shape=jax.ShapeDtypeStruct((Kd, N), F32),
        compiler_params=_cp(("parallel", "arbitrary")),
    )(a, b)


def rms_fwd(x, gains, *, name, tm=1024):
    S = x.shape[0]
    n = len(gains)

    def body(x_ref, *rest):
        xf = x_ref[...]
        y = xf * lax.rsqrt(jnp.mean(xf * xf, axis=-1, keepdims=True) + EPS)
        for i in range(n):
            rest[n + i][...] = (y * rest[i][...]).astype(BF16)

    return pl.pallas_call(
        body, name=name, grid=(S // tm,),
        in_specs=[pl.BlockSpec((tm, D), lambda i: (i, 0))] + [pl.BlockSpec((1, D), lambda i: (0, 0))] * n,
        out_specs=[pl.BlockSpec((tm, D), lambda i: (i, 0))] * n,
        out_shape=[jax.ShapeDtypeStruct((S, D), BF16)] * n,
        compiler_params=_cp(("parallel",)),
    )(x, *gains)


def mm_rms_bwd(a_list, b_list, g, x, dres, extra=(), *, name, tm=512):
    S = x.shape[0]
    n = len(a_list)
    ne = len(extra)

    def body(*refs):
        a_refs, b_refs = refs[:n], refs[n:2 * n]
        g_ref, x_ref, dres_ref = refs[2 * n:2 * n + 3]
        e_refs = refs[2 * n + 3:2 * n + 3 + 2 * ne]
        dx_ref = refs[2 * n + 3 + 2 * ne]
        dg_refs = refs[2 * n + 4 + 2 * ne:]

        @pl.when(pl.program_id(0) == 0)
        def _():
            for r in dg_refs:
                r[...] = jnp.zeros_like(r)

        acc = _dot(a_refs[0][...], b_refs[0][...])
        for i in range(1, n):
            acc = acc + _dot(a_refs[i][...], b_refs[i][...])
        xf = x_ref[...]
        r = lax.rsqrt(jnp.mean(xf * xf, axis=-1, keepdims=True) + EPS)
        xhat = xf * r
        total = dres_ref[...]
        branches = [(acc, g_ref[...])] + [(e_refs[2 * i][...], e_refs[2 * i + 1][...]) for i in range(ne)]
        for i, (dy, gg) in enumerate(branches):
            dg_refs[i][...] += jnp.sum(dy * xhat, axis=0, keepdims=True)
            dyg = dy * gg
            total = total + r * (dyg - xhat * jnp.mean(dyg * xhat, axis=-1, keepdims=True))
        dx_ref[...] = total

    row = pl.BlockSpec((tm, D), lambda i: (i, 0))
    vec = pl.BlockSpec((1, D), lambda i: (0, 0))
    in_specs = ([pl.BlockSpec((tm, a.shape[1]), lambda i: (i, 0)) for a in a_list]
                + [pl.BlockSpec(b.shape, lambda i: (0, 0)) for b in b_list] + [vec, row, row])
    args = list(a_list) + list(b_list) + [g, x, dres]
    for dxn_e, g_e in extra:
        in_specs += [row, vec]
        args += [dxn_e, g_e]
    outs = pl.pallas_call(
        body, name=name, grid=(S // tm,), in_specs=in_specs,
        out_specs=[row] + [vec] * (1 + ne),
        out_shape=[jax.ShapeDtypeStruct((S, D), F32)] + [jax.ShapeDtypeStruct((1, D), F32)] * (1 + ne),
        compiler_params=_cp(("arbitrary",)),
    )(*args)
    return outs[0], outs[1:]


def loss_head(x, target, g, *, name, tm=512):
    S = x.shape[0]

    def body(x_ref, t_ref, g_ref, dx_ref, dg_ref, loss_ref):
        @pl.when(pl.program_id(0) == 0)
        def _():
            dg_ref[...] = jnp.zeros_like(dg_ref)
            loss_ref[...] = jnp.zeros_like(loss_ref)

        xf = x_ref[...]
        gg = g_ref[...]
        r = lax.rsqrt(jnp.mean(xf * xf, axis=-1, keepdims=True) + EPS)
        xhat = xf * r
        e = xhat * gg - t_ref[...]
        per_tok = jnp.mean(e * e, axis=-1, keepdims=True)
        loss_ref[...] += 0.5 * jnp.sum(per_tok, axis=0, keepdims=True)
        dy = e * (1.0 / D)
        dg_ref[...] += jnp.sum(dy * xhat, axis=0, keepdims=True)
        dyg = dy * gg
        dx_ref[...] = r * (dyg - xhat * jnp.mean(dyg * xhat, axis=-1, keepdims=True))

    row = pl.BlockSpec((tm, D), lambda i: (i, 0))
    vec = pl.BlockSpec((1, D), lambda i: (0, 0))
    return pl.pallas_call(
        body, name=name, grid=(S // tm,),
        in_specs=[row, row, vec],
        out_specs=[row, vec, pl.BlockSpec((1, 128), lambda i: (0, 0))],
        out_shape=[jax.ShapeDtypeStruct((S, D), F32), jax.ShapeDtypeStruct((1, D), F32),
                   jax.ShapeDtypeStruct((1, 128), F32)],
        compiler_params=_cp(("arbitrary",)),
    )(x, target, g)


def _sigmoid(x):
    return 1.0 / (1.0 + jnp.exp(-x))


def _gates(z_ref, bi_ref, bf_ref):
    li = SOFTCAP * jnp.tanh((z_ref[:, Z_GI:Z_GI + 128] + bi_ref[...]) * (1.0 / SOFTCAP))
    scf = SOFTCAP * jnp.tanh((z_ref[:, Z_GF:Z_GF + 128] + bf_ref[...]) * (1.0 / SOFTCAP))
    lf = jnp.minimum(scf, 0.0) - jnp.log(1.0 + jnp.exp(-jnp.abs(scf)))
    return li, scf, lf


def _tri(L, lower):
    r = lax.broadcasted_iota(jnp.int32, (L, L), 0)
    c = lax.broadcasted_iota(jnp.int32, (L, L), 1)
    return (r >= c) if lower else (r <= c)


def mlstm_fwd(z, bi, bf, *, name):
    S = z.shape[0]
    L = MLSTM_CHUNK
    NC = S // L
    scale = A_QK ** -0.5

    def body(z_ref, bi_ref, bf_ref, h_ref, cst_ref, nst_ref, C_s, n_s):
        @pl.when(pl.program_id(0) == 0)
        def _():
            C_s[...] = jnp.zeros_like(C_s)
            n_s[...] = jnp.zeros_like(n_s)

        li, _, lf = _gates(z_ref, bi_ref, bf_ref)
        causal = _tri(L, True)
        b = _dot(causal.astype(F32), lf, precision=HIGHEST)
        liT = li.T
        bT = b.T
        cst_ref[0] = C_s[...].astype(BF16)
        nst_ref[0] = n_s[...]
        for h in range(A_HEADS):
            q = z_ref[:, h * A_QK:(h + 1) * A_QK] * scale
            k = z_ref[:, 512 + h * A_QK:512 + (h + 1) * A_QK]
            qb = q.astype(BF16)
            kb = k.astype(BF16)
            vb = z_ref[:, 1024 + h * A_V:1024 + (h + 1) * A_V].astype(BF16)
            a_col, b_col = li[:, h:h + 1], b[:, h:h + 1]
            a_row, b_row = liT[h:h + 1, :], bT[h:h + 1, :]
            Dm = jnp.exp(jnp.where(causal, b_col - b_row + a_row, -jnp.inf))
            A = _dot_nt(qb, kb) * Dm
            eb = jnp.exp(b_col)
            Ch = C_s[h]
            nh = n_s[h:h + 1, :]
            num = _dot(A.astype(BF16), vb) + eb * _dot(qb, Ch.astype(BF16))
            den = jnp.sum(A, axis=-1, keepdims=True) + eb * jnp.sum(q * nh, axis=-1, keepdims=True)
            h_ref[:, h * A_V:(h + 1) * A_V] = num / jnp.maximum(jnp.abs(den), 1.0)
            bL = b_col[L - 1:L, :]
            kw = jnp.exp(bL - b_col + a_col) * k
            decay = jnp.exp(bL)
            C_s[h] = decay * Ch + _dot_tn(kw.astype(BF16), vb)
            n_s[h:h + 1, :] = decay * nh + jnp.sum(kw, axis=0, keepdims=True)

    vec = pl.BlockSpec((1, 128), lambda c: (0, 0))
    return pl.pallas_call(
        body, name=name, grid=(NC,),
        in_specs=[pl.BlockSpec((L, Z_W), lambda c: (c, 0)), vec, vec],
        out_specs=[pl.BlockSpec((L, 1024), lambda c: (c, 0)),
                   pl.BlockSpec((1, A_HEADS, A_QK, A_V), lambda c: (c, 0, 0, 0)),
                   pl.BlockSpec((1, 8, 128), lambda c: (c, 0, 0))],
        out_shape=[jax.ShapeDtypeStruct((S, 1024), F32),
                   jax.ShapeDtypeStruct((NC, A_HEADS, A_QK, A_V), BF16),
                   jax.ShapeDtypeStruct((NC, 8, 128), F32)],
        scratch_shapes=[pltpu.VMEM((A_HEADS, A_QK, A_V), F32), pltpu.VMEM((8, 128), F32)],
        compiler_params=_cp(("arbitrary",)),
    )(z, bi, bf)


def mlstm_bwd(z, bi, bf, cst, nst, dh, dzo, *, name):
    S = z.shape[0]
    L = MLSTM_CHUNK
    NC = S // L
    scale = A_QK ** -0.5

    def body(z_ref, bi_ref, bf_ref, cst_ref, nst_ref, dh_ref, dzo_ref, dz_ref, db_ref, dC_s, dn_s):
        @pl.when(pl.program_id(0) == 0)
        def _():
            dC_s[...] = jnp.zeros_like(dC_s)
            dn_s[...] = jnp.zeros_like(dn_s)
            db_ref[...] = jnp.zeros_like(db_ref)

        li, scf, lf = _gates(z_ref, bi_ref, bf_ref)
        causal = _tri(L, True)
        b = _dot(causal.astype(F32), lf, precision=HIGHEST)
        liT = li.T
        bT = b.T
        lane = lax.broadcasted_iota(jnp.int32, (L, 128), 1)
        sub = lax.broadcasted_iota(jnp.int32, (128, L), 0)
        lane1 = lax.broadcasted_iota(jnp.int32, (1, 128), 1)
        Rm = jnp.zeros((L, 128), F32)
        KIm = jnp.zeros((L, 128), F32)
        csm = jnp.zeros((128, L), F32)
        Xm = jnp.zeros((1, 128), F32)
        for h in range(A_HEADS):
            q = z_ref[:, h * A_QK:(h + 1) * A_QK] * scale
            k = z_ref[:, 512 + h * A_QK:512 + (h + 1) * A_QK]
            qb = q.astype(BF16)
            kb = k.astype(BF16)
            vb = z_ref[:, 1024 + h * A_V:1024 + (h + 1) * A_V].astype(BF16)
            a_col, b_col = li[:, h:h + 1], b[:, h:h + 1]
            a_row, b_row = liT[h:h + 1, :], bT[h:h + 1, :]
            Dm = jnp.exp(jnp.where(causal, b_col - b_row + a_row, -jnp.inf))
            Sqk = _dot_nt(qb, kb)
            A = Sqk * Dm
            Ab = A.astype(BF16)
            eb = jnp.exp(b_col)
            Cb = cst_ref[0, h]
            nh = nst_ref[0, h:h + 1, :]
            num = _dot(Ab, vb) + eb * _dot(qb, Cb)
            den = jnp.sum(A, axis=-1, keepdims=True) + eb * jnp.sum(q * nh, axis=-1, keepdims=True)
            aden = jnp.abs(den)
            u = 1.0 / jnp.maximum(aden, 1.0)
            dhh = dh_ref[:, h * A_V:(h + 1) * A_V]
            dnum = dhh * u
            dden = jnp.where(aden > 1.0, -jnp.sum(dhh * num, axis=-1, keepdims=True) * u * u * jnp.sign(den), 0.0)
            dnb = dnum.astype(BF16)
            G = Dm * (_dot_nt(dnb, vb) + dden)
            Gb = G.astype(BF16)
            E = G * Sqk
            rs = jnp.sum(E, axis=-1, keepdims=True)
            cs = jnp.sum(E, axis=0, keepdims=True)
            dCh = dC_s[h]
            dnh = dn_s[h:h + 1, :]
            dCb = dCh.astype(BF16)
            bL = b_col[L - 1:L, :]
            wk = jnp.exp(bL - b_col + a_col)
            decay = jnp.exp(bL)
            dq_inter = eb * (_dot_nt(dnb, Cb) + dden * nh)
            dk_inter = wk * (_dot_nt(vb, dCb) + dnh)
            dq = _dot(Gb, kb) + dq_inter
            dk = _dot_tn(Gb, qb) + dk_inter
            dv = _dot_tn(Ab, dnb) + wk * _dot(kb, dCb)
            dz_ref[:, h * A_QK:(h + 1) * A_QK] = (dq * scale).astype(BF16)
            dz_ref[:, 512 + h * A_QK:512 + (h + 1) * A_QK] = dk.astype(BF16)
            dz_ref[:, 1024 + h * A_V:1024 + (h + 1) * A_V] = dv.astype(BF16)
            KI = jnp.sum(k * dk_inter, axis=-1, keepdims=True)
            R = rs + jnp.sum(q * dq_inter, axis=-1, keepdims=True)
            cross = (jnp.sum(jnp.sum(dCh * Cb.astype(F32), axis=0, keepdims=True), axis=1, keepdims=True)
                     + jnp.sum(dnh * nh, axis=1, keepdims=True))
            Xm = jnp.where(lane1 == h, decay * cross, Xm)
            Rm = jnp.where(lane == h, R, Rm)
            KIm = jnp.where(lane == h, KI, KIm)
            csm = jnp.where(sub == h, cs, csm)
            ebq = eb * q
            dC_s[h] = decay * dCh + _dot_tn(ebq.astype(BF16), dnb)
            dn_s[h:h + 1, :] = decay * dnh + jnp.sum(ebq * dden, axis=0, keepdims=True)
        dz_ref[:, 2048:3072] = dzo_ref[...]
        cs_col = csm.T
        da = cs_col + KIm
        rr = lax.broadcasted_iota(jnp.int32, (L, L), 0)
        cc = lax.broadcasted_iota(jnp.int32, (L, L), 1)
        dlf = (_dot((rr <= cc).astype(F32), Rm - cs_col, precision=HIGHEST)
               + _dot((rr > cc).astype(F32), KIm, precision=HIGHEST) + Xm)
        dpre_i = da * (1.0 - (li * (1.0 / SOFTCAP)) ** 2)
        dpre_f = dlf * (1.0 - _sigmoid(scf)) * (1.0 - (scf * (1.0 / SOFTCAP)) ** 2)
        dz_ref[:, Z_GI:Z_GI + 128] = dpre_i.astype(BF16)
        dz_ref[:, Z_GF:Z_GF + 128] = dpre_f.astype(BF16)
        db_ref[0:1, :] += jnp.sum(dpre_i, axis=0, keepdims=True)
        db_ref[1:2, :] += jnp.sum(dpre_f, axis=0, keepdims=True)

    vec = pl.BlockSpec((1, 128), lambda c: (0, 0))
    rev = lambda c: (NC - 1 - c, 0)
    return pl.pallas_call(
        body, name=name, grid=(NC,),
        in_specs=[pl.BlockSpec((L, Z_W), rev), vec, vec,
                  pl.BlockSpec((1, A_HEADS, A_QK, A_V), lambda c: (NC - 1 - c, 0, 0, 0)),
                  pl.BlockSpec((1, 8, 128), lambda c: (NC - 1 - c, 0, 0)),
                  pl.BlockSpec((L, 1024), rev), pl.BlockSpec((L, 1024), rev)],
        out_specs=[pl.BlockSpec((L, Z_W), rev), pl.BlockSpec((8, 128), lambda c: (0, 0))],
        out_shape=[jax.ShapeDtypeStruct((S, Z_W), BF16), jax.ShapeDtypeStruct((8, 128), F32)],
        scratch_shapes=[pltpu.VMEM((A_HEADS, A_QK, A_V), F32), pltpu.VMEM((8, 128), F32)],
        compiler_params=_cp(("arbitrary",)),
    )(z, bi, bf, cst, nst, dh, dzo)


def ao_fwd(h, z, gh, w_out, x, *, name, tm=512):
    S = h.shape[0]

    def body(h_ref, o_ref, gh_ref, w_ref, x_ref, x1_ref, hg_ref):
        for hd in range(A_HEADS):
            sl = slice(hd * A_V, (hd + 1) * A_V)
            hs = h_ref[:, sl]
            hn = hs * lax.rsqrt(jnp.mean(hs * hs, axis=-1, keepdims=True) + EPS) * gh_ref[:, sl]
            hg_ref[:, sl] = (hn * _sigmoid(o_ref[:, sl])).astype(BF16)
        x1_ref[...] = x_ref[...] + _dot(hg_ref[...], w_ref[...])

    row = pl.BlockSpec((tm, 1024), lambda i: (i, 0))
    return pl.pallas_call(
        body, name=name, grid=(S // tm,),
        in_specs=[row, pl.BlockSpec((tm, 1024), lambda i: (i, 2)), pl.BlockSpec((1, 1024), lambda i: (0, 0)),
                  pl.BlockSpec((1024, 1024), lambda i: (0, 0)), row],
        out_specs=[row, row],
        out_shape=[jax.ShapeDtypeStruct((S, 1024), F32), jax.ShapeDtypeStruct((S, 1024), BF16)],
        compiler_params=_cp(("parallel",)),
    )(h, z, gh, w_out, x)


def ao_bwd(g1, w_out_t, h, z, gh, *, name, tm=512):
    S = h.shape[0]

    def body(g_ref, w_ref, h_ref, o_ref, gh_ref, dh_ref, dzo_ref, dgh_ref):
        @pl.when(pl.program_id(0) == 0)
        def _():
            dgh_ref[...] = jnp.zeros_like(dgh_ref)

        dhg = _dot(g_ref[...].astype(BF16), w_ref[...])
        for hd in range(A_HEADS):
            sl = slice(hd * A_V, (hd + 1) * A_V)
            hs = h_ref[:, sl]
            r = lax.rsqrt(jnp.mean(hs * hs, axis=-1, keepdims=True) + EPS)
            hhat = hs * r
            ghs = gh_ref[:, sl]
            sig = _sigmoid(o_ref[:, sl])
            d = dhg[:, sl]
            dhn = d * sig
            dzo_ref[:, sl] = (d * hhat * ghs * sig * (1.0 - sig)).astype(BF16)
            dgh_ref[:, sl] += jnp.sum(dhn * hhat, axis=0, keepdims=True)
            dhhat = dhn * ghs
            dh_ref[:, sl] = r * (dhhat - hhat * jnp.mean(dhhat * hhat, axis=-1, keepdims=True))

    row = pl.BlockSpec((tm, 1024), lambda i: (i, 0))
    vec = pl.BlockSpec((1, 1024), lambda i: (0, 0))
    return pl.pallas_call(
        body, name=name, grid=(S // tm,),
        in_specs=[row, pl.BlockSpec((1024, 1024), lambda i: (0, 0)), row,
                  pl.BlockSpec((tm, 1024), lambda i: (i, 2)), vec],
        out_specs=[row, row, vec],
        out_shape=[jax.ShapeDtypeStruct((S, 1024), F32), jax.ShapeDtypeStruct((S, 1024), BF16),
                   jax.ShapeDtypeStruct((1, 1024), F32)],
        compiler_params=_cp(("arbitrary",)),
    )(g1, w_out_t, h, z, gh)


CONV_TC = 1408
CONV_HALO = 16


def mm_up_conv(xn, w_up, cw, cb, *, name, tm=512):
    S = xn.shape[0]
    TN = 2 * CONV_TC

    def body(x_ref, w_ref, cw_ref, cb_ref, u_ref, c_ref, a_ref, carry):
        @pl.when(pl.program_id(1) == 0)
        def _():
            carry[...] = jnp.zeros_like(carry)

        ub = _dot(x_ref[...], w_ref[...]).astype(BF16)
        u_ref[...] = ub
        u = ub.astype(F32)
        prev = carry[...]
        row = lax.broadcasted_iota(jnp.int32, u.shape, 0)
        u1 = jnp.where(row == 0, prev[7:8], pltpu.roll(u, 1, axis=0))
        u2 = jnp.where(row == 0, prev[6:7], jnp.where(row == 1, prev[7:8], pltpu.roll(u, 2, axis=0)))
        carry[...] = u[tm - 8:tm]
        w = cw_ref[...]
        c = u * w[2:3] + u1 * w[1:2] + u2 * w[0:1] + cb_ref[...]
        c_ref[...] = c.astype(BF16)
        g, v = c[:, :CONV_TC], c[:, CONV_TC:]
        a_ref[...] = (g * _sigmoid(g) * v).astype(BF16)

    big = pl.BlockSpec((tm, TN), lambda j, i: (i, j))
    return pl.pallas_call(
        body, name=name, grid=(2, S // tm),
        in_specs=[pl.BlockSpec((tm, D), lambda j, i: (i, 0)), pl.BlockSpec((D, TN), lambda j, i: (0, j)),
                  pl.BlockSpec((3, TN), lambda j, i: (0, j)), pl.BlockSpec((1, TN), lambda j, i: (0, j))],
        out_specs=[big, big, pl.BlockSpec((tm, CONV_TC), lambda j, i: (i, j))],
        out_shape=[jax.ShapeDtypeStruct((S, 2 * D_FF), BF16), jax.ShapeDtypeStruct((S, 2 * D_FF), BF16),
                   jax.ShapeDtypeStruct((S, D_FF), BF16)],
        scratch_shapes=[pltpu.VMEM((8, TN), F32)],
        compiler_params=_cp(("parallel", "arbitrary")),
    )(xn, w_up, cw, cb)


def mm_da_act(dxn, w_down_t, c, *, name, tm=512):
    S = c.shape[0]
    TN = 2 * CONV_TC

    def body(x_ref, w_ref, c_ref, duc_ref):
        dav = _dot(x_ref[...].astype(BF16), w_ref[...])
        g = c_ref[:, :CONV_TC].astype(F32)
        v = c_ref[:, CONV_TC:].astype(F32)
        sg = _sigmoid(g)
        gs = g * sg
        duc_ref[:, :CONV_TC] = (dav * v * (sg + gs * (1.0 - sg))).astype(BF16)
        duc_ref[:, CONV_TC:] = (dav * gs).astype(BF16)

    return pl.pallas_call(
        body, name=name, grid=(2, S // tm),
        in_specs=[pl.BlockSpec((tm, D), lambda j, i: (i, 0)), pl.BlockSpec((D, CONV_TC), lambda j, i: (0, j)),
                  pl.BlockSpec((tm, TN), lambda j, i: (i, j))],
        out_specs=pl.BlockSpec((tm, TN), lambda j, i: (i, j)),
        out_shape=jax.ShapeDtypeStruct((S, 2 * D_FF), BF16),
        compiler_params=_cp(("parallel", "parallel")),
    )(dxn, w_down_t, c)


def tn_up_conv(xn, duc, u, cw, *, name, tk=512):
    S, Kd = xn.shape
    N = duc.shape[1]
    hb = tk // CONV_HALO
    nblk = S // tk

    def body(x_ref, d_ref, halo_ref, u_ref, w_ref, o_ref, du_ref, dwb_ref):
        k = pl.program_id(1)

        @pl.when(k == 0)
        def _():
            o_ref[...] = jnp.zeros_like(o_ref)
            dwb_ref[...] = jnp.zeros_like(dwb_ref)

        d = d_ref[...].astype(F32)
        hl = jnp.where(k == nblk - 1, 0.0, halo_ref[...].astype(F32))
        row = lax.broadcasted_iota(jnp.int32, d.shape, 0)
        d1 = jnp.where(row == tk - 1, hl[0:1], pltpu.roll(d, tk - 1, axis=0))
        d2 = jnp.where(row == tk - 1, hl[1:2], jnp.where(row == tk - 2, hl[0:1], pltpu.roll(d, tk - 2, axis=0)))
        w = w_ref[...]
        du = (d * w[2:3] + d1 * w[1:2] + d2 * w[0:1]).astype(BF16)
        du_ref[...] = du
        o_ref[...] += _dot_tn(x_ref[...], du)
        u = u_ref[...].astype(F32)
        dwb_ref[0:1, :] += jnp.sum(u * d2, axis=0, keepdims=True)
        dwb_ref[1:2, :] += jnp.sum(u * d1, axis=0, keepdims=True)
        dwb_ref[2:3, :] += jnp.sum(u * d, axis=0, keepdims=True)
        dwb_ref[3:4, :] += jnp.sum(d, axis=0, keepdims=True)

    blk = pl.BlockSpec((tk, CONV_TC), lambda j, k: (k, j))
    return pl.pallas_call(
        body, name=name, grid=(N // CONV_TC, nblk),
        in_specs=[pl.BlockSpec((tk, Kd), lambda j, k: (k, 0)), blk,
                  pl.BlockSpec((CONV_HALO, CONV_TC), lambda j, k: (jnp.minimum((k + 1) * hb, nblk * hb - 1), j)),
                  blk, pl.BlockSpec((3, CONV_TC), lambda j, k: (0, j))],
        out_specs=[pl.BlockSpec((Kd, CONV_TC), lambda j, k: (0, j)), blk, pl.BlockSpec((8, CONV_TC), lambda j, k: (0, j))],
        out_shape=[jax.ShapeDtypeStruct((Kd, N), F32), jax.ShapeDtypeStruct((S, N), BF16),
                   jax.ShapeDtypeStruct((8, N), F32)],
        compiler_params=_cp(("parallel", "arbitrary")),
    )(xn, duc, duc, u, cw)


def _t5_bucket(dist):
    max_exact = REL_BUCKETS // 2
    d = np.maximum(dist, 0)
    log_ratio = np.log(np.maximum(d, 1) / max_exact) / math.log(REL_MAX_DIST / max_exact)
    large = np.minimum(max_exact + (log_ratio * (REL_BUCKETS - max_exact)).astype(np.int64), REL_BUCKETS - 1)
    return np.where(d < max_exact, d, large).astype(np.int32)


def _bucket_tables():
    delta = BLK + np.arange(BLK)[:, None] - np.arange(2 * BLK)[None, :]
    return np.stack([_t5_bucket(delta * dil) for _, dil in B_GROUPS]).astype(np.int32)


def bias_build(rel_bias, buckets, *, name):
    def body(rel_ref, bk_ref, o_ref):
        g = pl.program_id(0)
        bk = bk_ref[0]
        for h in range(B_HEADS):
            acc = jnp.zeros((BLK, 2 * BLK), F32)
            for bb in range(REL_BUCKETS):
                acc = jnp.where(bk == bb, rel_ref[bb, g * B_HEADS + h], acc)
            o_ref[0, h] = acc

    return pl.pallas_call(
        body, name=name, grid=(N_GROUPS,),
        in_specs=[pl.BlockSpec(memory_space=pltpu.SMEM), pl.BlockSpec((1, BLK, 2 * BLK), lambda g: (g, 0, 0))],
        out_specs=pl.BlockSpec((1, B_HEADS, BLK, 2 * BLK), lambda g: (g, 0, 0, 0)),
        out_shape=jax.ShapeDtypeStruct((N_GROUPS, B_HEADS, BLK, 2 * BLK), F32),
        compiler_params=_cp(("arbitrary",)),
    )(rel_bias, buckets)


def bias_grad(dbias, buckets, *, name):
    def body(db_ref, bk_ref, o_ref):
        g = pl.program_id(0)

        @pl.when(g == 0)
        def _():
            o_ref[...] = jnp.zeros_like(o_ref)

        bk = bk_ref[0]
        rr = lax.broadcasted_iota(jnp.int32, (REL_BUCKETS, 128), 0)
        cc = lax.broadcasted_iota(jnp.int32, (REL_BUCKETS, 128), 1)
        acc = jnp.zeros((REL_BUCKETS, 128), F32)
        for h in range(B_HEADS):
            dbh = db_ref[0, h]
            for bb in range(REL_BUCKETS):
                part = jnp.sum(jnp.where(bk == bb, dbh, 0.0), axis=0, keepdims=True)
                s = jnp.sum(part, axis=1, keepdims=True)
                acc = acc + jnp.where((rr == bb) & (cc == g * B_HEADS + h), s, 0.0)
        o_ref[...] += acc

    return pl.pallas_call(
        body, name=name, grid=(N_GROUPS,),
        in_specs=[pl.BlockSpec((1, B_HEADS, BLK, 2 * BLK), lambda g: (g, 0, 0, 0)),
                  pl.BlockSpec((1, BLK, 2 * BLK), lambda g: (g, 0, 0))],
        out_specs=pl.BlockSpec((REL_BUCKETS, 128), lambda g: (0, 0)),
        out_shape=jax.ShapeDtypeStruct((REL_BUCKETS, 128), F32),
        compiler_params=_cp(("arbitrary",)),
    )(dbias, buckets)


HG = 4
GW = HG * B_DH


def _head_masks(dtype):
    lane = lax.broadcasted_iota(jnp.int32, (BLK, GW), 1)
    return [((lane >= h * B_DH) & (lane < (h + 1) * B_DH)).astype(dtype) for h in range(HG)]


def _band_masks():
    iq = lax.broadcasted_iota(jnp.int32, (BLK, BLK), 0)
    ik = lax.broadcasted_iota(jnp.int32, (BLK, BLK), 1)
    return iq <= ik, iq >= ik


def attn_fwd(qg, kg, vg, bias, g, dil, *, name):
    S = qg.size // 1024
    S2 = S // dil
    nb = S2 // BLK
    W = dil * 1024
    scale = B_DH ** -0.5

    def body(q_ref, kc_ref, kp_ref, vc_ref, vp_ref, b_ref, o_ref, lse_ref):
        has_prev = pl.program_id(1) > 0
        vp_m, vc_m = _band_masks()
        valid = jnp.concatenate([vp_m & has_prev, vc_m], axis=1)
        mb = _head_masks(BF16)
        mf = _head_masks(F32)
        lane = lax.broadcasted_iota(jnp.int32, (BLK, 128), 1)
        lse_acc = jnp.zeros((BLK, 128), F32)
        for hg in range(B_HEADS // HG):
            sl = slice(hg * GW, (hg + 1) * GW)
            q4 = q_ref[:, sl]
            kcat = jnp.concatenate([kp_ref[:, sl], kc_ref[:, sl]], axis=0)
            vcat = jnp.concatenate([vp_ref[:, sl], vc_ref[:, sl]], axis=0)
            s4 = _dot_nt(jnp.concatenate([q4 * mb[h] for h in range(HG)], axis=0), kcat)
            ps, rl = [], []
            for h in range(HG):
                hh = hg * HG + h
                s = jnp.where(valid, s4[h * BLK:(h + 1) * BLK] * scale + b_ref[0, hh], -jnp.inf)
                m = jnp.max(s, axis=-1, keepdims=True)
                p = jnp.exp(s - m)
                l = jnp.sum(p, axis=-1, keepdims=True)
                ps.append(p.astype(BF16))
                rl.append(1.0 / l)
                lse_acc = jnp.where(lane == hh, m + jnp.log(l), lse_acc)
            o4 = _dot(jnp.concatenate(ps, axis=0), vcat)
            acc = jnp.zeros((BLK, GW), F32)
            for h in range(HG):
                acc = acc + o4[h * BLK:(h + 1) * BLK] * (mf[h] * rl[h])
            o_ref[:, sl] = acc.astype(BF16)
        lse_ref[...] = lse_acc

    cur = pl.BlockSpec((BLK, 1024), lambda r, n: (n, r))
    prev = pl.BlockSpec((BLK, 1024), lambda r, n: (jnp.maximum(n - 1, 0), r))
    q2, k2, v2 = qg.reshape(S2, W), kg.reshape(S2, W), vg.reshape(S2, W)
    o, lse = pl.pallas_call(
        body, name=name, grid=(dil, nb),
        in_specs=[cur, cur, prev, cur, prev, pl.BlockSpec((1, B_HEADS, BLK, 2 * BLK), lambda r, n: (g, 0, 0, 0))],
        out_specs=[cur, pl.BlockSpec((BLK, 128), lambda r, n: (n, r))],
        out_shape=[jax.ShapeDtypeStruct((S2, W), BF16), jax.ShapeDtypeStruct((S2, dil * 128), F32)],
        compiler_params=_cp(("parallel", "arbitrary")),
    )(q2, k2, k2, v2, v2, bias)
    return o.reshape(S, 1024), lse.reshape(S, 128)


def attn_merge(os_, lses, *, name, tm=512):
    S = os_[0].shape[0]
    expand = np.zeros((128, 1024), np.float32)
    for h in range(B_HEADS):
        expand[h, h * B_DH:(h + 1) * B_DH] = 1.0
    expand = jnp.asarray(expand, BF16)

    def body(o0, o1, o2, l0, l1, l2, e_ref, out_ref, lse_ref):
        ls = [l0[...], l1[...], l2[...]]
        m = jnp.maximum(jnp.maximum(ls[0], ls[1]), ls[2])
        ws = [jnp.exp(l - m) for l in ls]
        tot = ws[0] + ws[1] + ws[2]
        lse_ref[...] = m + jnp.log(tot)
        acc = jnp.zeros((tm, 1024), F32)
        for w, o in zip(ws, (o0, o1, o2)):
            acc = acc + _dot_split(w / tot, e_ref[...]) * o[...].astype(F32)
        out_ref[...] = acc.astype(BF16)

    row = pl.BlockSpec((tm, 1024), lambda i: (i, 0))
    lrow = pl.BlockSpec((tm, 128), lambda i: (i, 0))
    return pl.pallas_call(
        body, name=name, grid=(S // tm,),
        in_specs=[row, row, row, lrow, lrow, lrow, pl.BlockSpec((128, 1024), lambda i: (0, 0))],
        out_specs=[row, lrow],
        out_shape=[jax.ShapeDtypeStruct((S, 1024), BF16), jax.ShapeDtypeStruct((S, 128), F32)],
        compiler_params=_cp(("parallel",)),
    )(*os_, *lses, expand)


def mm_dout(dx, w_t, att, *, name, tm=512):
    S = dx.shape[0]
    heads = np.zeros((1024, 128), np.float32)
    for h in range(B_HEADS):
        heads[h * B_DH:(h + 1) * B_DH, h] = 1.0

    def body(x_ref, w_ref, att_ref, e_ref, do_ref, d_ref):
        acc = _dot(x_ref[...].astype(BF16), w_ref[...])
        do_ref[...] = acc.astype(BF16)
        d_ref[...] = _dot_split(acc * att_ref[...].astype(F32), e_ref[...])

    row = pl.BlockSpec((tm, 1024), lambda i: (i, 0))
    return pl.pallas_call(
        body, name=name, grid=(S // tm,),
        in_specs=[row, pl.BlockSpec((1024, 1024), lambda i: (0, 0)), row, pl.BlockSpec((1024, 128), lambda i: (0, 0))],
        out_specs=[row, pl.BlockSpec((tm, 128), lambda i: (i, 0))],
        out_shape=[jax.ShapeDtypeStruct((S, 1024), BF16), jax.ShapeDtypeStruct((S, 128), F32)],
        compiler_params=_cp(("parallel",)),
    )(dx, w_t, att, jnp.asarray(heads, BF16))


def attn_bwd(qg, kg, vg, bias, dout, dsum, lse, g, dil, *, name):
    S = qg.size // 1024
    S2 = S // dil
    nb = S2 // BLK
    W = dil * 1024
    scale = B_DH ** -0.5

    def body(q_ref, kc_ref, kp_ref, vc_ref, vp_ref, b_ref, do_ref, dsum_ref, lse_ref,
             dq_ref, dk_ref, dv_ref, db_ref, ck_s, cv_s):
        n = pl.program_id(1)

        @pl.when((pl.program_id(0) == 0) & (n == 0))
        def _():
            db_ref[...] = jnp.zeros_like(db_ref)

        @pl.when(n == 0)
        def _():
            ck_s[...] = jnp.zeros_like(ck_s)
            cv_s[...] = jnp.zeros_like(cv_s)

        @pl.when(n == nb)
        def _():
            dk_ref[...] = ck_s[...].astype(BF16)
            dv_ref[...] = cv_s[...].astype(BF16)

        @pl.when(n < nb)
        def _():
            vp_m, vc_m = _band_masks()
            valid = jnp.concatenate([vp_m & (n > 0), vc_m], axis=1)
            mb = _head_masks(BF16)
            mf = _head_masks(F32)
            lse_blk = lse_ref[...]
            dsum_blk = dsum_ref[...]
            for hg in range(B_HEADS // HG):
                sl = slice(hg * GW, (hg + 1) * GW)
                kcat = jnp.concatenate([kp_ref[:, sl], kc_ref[:, sl]], axis=0)
                vcat = jnp.concatenate([vp_ref[:, sl], vc_ref[:, sl]], axis=0)
                dob = do_ref[:, sl]
                q4 = q_ref[:, sl]
                q4m = jnp.concatenate([q4 * mb[h] for h in range(HG)], axis=0)
                do4m = jnp.concatenate([dob * mb[h] for h in range(HG)], axis=0)
                s4 = _dot_nt(q4m, kcat)
                dp4 = _dot_nt(do4m, vcat)
                ps, dss = [], []
                for h in range(HG):
                    hh = hg * HG + h
                    rows = slice(h * BLK, (h + 1) * BLK)
                    s = jnp.where(valid, s4[rows] * scale + b_ref[0, hh] - lse_blk[:, hh:hh + 1], -jnp.inf)
                    p = jnp.exp(s)
                    ds = p * (dp4[rows] - dsum_blk[:, hh:hh + 1])
                    db_ref[hh] += ds
                    ps.append(p.astype(BF16))
                    dss.append(ds.astype(BF16))
                p4 = jnp.concatenate(ps, axis=0)
                ds4 = jnp.concatenate(dss, axis=0)
                dq4 = _dot(ds4, kcat)
                acc = jnp.zeros((BLK, GW), F32)
                for h in range(HG):
                    acc = acc + dq4[h * BLK:(h + 1) * BLK] * mf[h]
                dq_ref[:, sl] = (acc * scale).astype(BF16)
                dkc = _dot_tn(ds4, q4m) * scale
                dvc = _dot_tn(p4, do4m)
                dk_ref[:, sl] = (ck_s[:, sl] + dkc[0:BLK]).astype(BF16)
                dv_ref[:, sl] = (cv_s[:, sl] + dvc[0:BLK]).astype(BF16)
                ck_s[:, sl] = dkc[BLK:2 * BLK]
                cv_s[:, sl] = dvc[BLK:2 * BLK]

    last = nb - 1
    cur = lambda r, n: (jnp.minimum(n, last), r)
    prev = lambda r, n: (jnp.clip(n - 1, 0, last), r)
    row = lambda im: pl.BlockSpec((BLK, 1024), im)
    k2, v2 = kg.reshape(S2, W), vg.reshape(S2, W)
    dq, dk, dv, dbias = pl.pallas_call(
        body, name=name, grid=(dil, nb + 1),
        in_specs=[row(cur), row(cur), row(prev), row(cur), row(prev),
                  pl.BlockSpec((1, B_HEADS, BLK, 2 * BLK), lambda r, n: (g, 0, 0, 0)),
                  row(cur), pl.BlockSpec((BLK, 128), cur), pl.BlockSpec((BLK, 128), cur)],
        out_specs=[row(cur), row(prev), row(prev), pl.BlockSpec((B_HEADS, BLK, 2 * BLK), lambda r, n: (0, 0, 0))],
        out_shape=[jax.ShapeDtypeStruct((S2, W), BF16)] * 3 + [jax.ShapeDtypeStruct((B_HEADS, BLK, 2 * BLK), F32)],
        scratch_shapes=[pltpu.VMEM((BLK, 1024), F32), pltpu.VMEM((BLK, 1024), F32)],
        compiler_params=_cp(("arbitrary", "arbitrary")),
    )(qg.reshape(S2, W), k2, k2, v2, v2, bias, dout.reshape(S2, W), dsum.reshape(S2, dil * 128),
      lse.reshape(S2, dil * 128))
    return dq.reshape(S, 1024), dk.reshape(S, 1024), dv.reshape(S, 1024), dbias


def _slot(px, py, pc):
    return 4 * px + 2 * py + pc


def ag_weights(wb, ws):
    def body(wb_ref, ws_ref, ob_ref, os_ref, send_sems, recv_sems, local_sems):
        x, y, c = lax.axis_index("x"), lax.axis_index("y"), lax.axis_index("c")
        me, sibling = (x, y, c), (x, y, 1 - c)
        chips = [(1 - x, y), (x, 1 - y), (1 - x, 1 - y)]
        arrays = [(wb_ref, ob_ref), (ws_ref, os_ref)]

        def copy(a, k, block, to, from_input=False):
            src_in, out = arrays[a]
            dst = out.at[_slot(*block)]
            return pltpu.make_async_remote_copy(
                src_ref=src_in if from_input else dst, dst_ref=dst,
                send_sem=send_sems.at[7 * a + k], recv_sem=recv_sems.at[7 * a + k],
                device_id=to, device_id_type=MESH)

        mine = [pltpu.make_async_copy(arrays[a][0], arrays[a][1].at[_slot(*me)], local_sems.at[a]) for a in range(2)]
        for cp in mine:
            cp.start()
        first = []
        for a in range(2):
            first.append(copy(a, 0, me, sibling, True))
            first += [copy(a, 1 + j, me, (*chip, c), True) for j, chip in enumerate(chips)]
        for cp in first:
            cp.start()
        passed = []
        for a in range(2):
            for j, chip in enumerate(chips):
                copy(a, 1 + j, (*chip, c), me).wait_recv()
                fw = copy(a, 4 + j, (*chip, c), sibling)
                fw.start()
                passed.append(fw)
        for a in range(2):
            copy(a, 0, sibling, me).wait_recv()
            for j, chip in enumerate(chips):
                copy(a, 4 + j, (*chip, 1 - c), me).wait_recv()
        for cp in first + passed:
            cp.wait_send()
        for cp in mine:
            cp.wait()

    any_spec = pl.BlockSpec(memory_space=pl.ANY)
    return pl.pallas_call(
        body, name="ag_weights",
        in_specs=[any_spec, any_spec], out_specs=[any_spec, any_spec],
        out_shape=[jax.ShapeDtypeStruct((N_DEV,) + wb.shape, wb.dtype), jax.ShapeDtypeStruct((N_DEV,) + ws.shape, ws.dtype)],
        scratch_shapes=[pltpu.SemaphoreType.DMA((14,)), pltpu.SemaphoreType.DMA((14,)), pltpu.SemaphoreType.DMA((2,))],
    )(wb, ws)


def rs_sibling(gpack, spack):
    def body(g_ref, s_ref, rb_ref, sa_ref, send_sems, recv_sems, local_sem):
        x, y, c = lax.axis_index("x"), lax.axis_index("y"), lax.axis_index("c")
        my = _slot(x, y, c)
        mine = pltpu.make_async_copy(s_ref, sa_ref.at[my], local_sem)
        mine.start()
        sends, recvs = [], []
        for j in range(4):
            both = dict(dst_ref=rb_ref.at[j], send_sem=send_sems.at[j], recv_sem=recv_sems.at[j],
                        device_id=(x, y, 1 - c), device_id_type=MESH)
            sends.append(pltpu.make_async_remote_copy(src_ref=g_ref.at[2 * j + 1 - c], **both))
            recvs.append(sends[-1])
        for k in range(1, N_DEV):
            peer = (1 - x if k & 4 else x, 1 - y if k & 2 else y, 1 - c if k & 1 else c)
            sems = dict(send_sem=send_sems.at[3 + k], recv_sem=recv_sems.at[3 + k], device_id=peer, device_id_type=MESH)
            sends.append(pltpu.make_async_remote_copy(src_ref=s_ref, dst_ref=sa_ref.at[my], **sems))
            recvs.append(pltpu.make_async_remote_copy(src_ref=s_ref, dst_ref=sa_ref.at[_slot(*peer)], **sems))
        for cp in sends:
            cp.start()
        for cp in recvs:
            cp.wait_recv()
        for cp in sends:
            cp.wait_send()
        mine.wait()

    any_spec = pl.BlockSpec(memory_space=pl.ANY)
    return pl.pallas_call(
        body, name="rs_sibling",
        in_specs=[any_spec, any_spec], out_specs=[any_spec, any_spec],
        out_shape=[jax.ShapeDtypeStruct((4,) + gpack.shape[1:], gpack.dtype),
                   jax.ShapeDtypeStruct((N_DEV,) + spack.shape, spack.dtype)],
        scratch_shapes=[pltpu.SemaphoreType.DMA((11,)), pltpu.SemaphoreType.DMA((11,)), pltpu.SemaphoreType.DMA],
    )(gpack, spack)


def pair_add(a, b, *, name, tr):
    R = a.shape[1]

    def body(a_ref, b_ref, o_ref):
        o_ref[...] = (a_ref[...].astype(F32) + b_ref[...].astype(F32)).astype(BF16)

    blk = pl.BlockSpec((4, tr, 1024), lambda i: (0, i, 0))
    return pl.pallas_call(
        body, name=name, grid=(R // tr,), in_specs=[blk, blk], out_specs=blk,
        out_shape=jax.ShapeDtypeStruct(a.shape, BF16), compiler_params=_cp(("parallel",)),
    )(a, b)


def rs_chips(part):
    def body(p_ref, rb_ref, send_sems, recv_sems, local_sem):
        x, y, c = lax.axis_index("x"), lax.axis_index("y"), lax.axis_index("c")
        jm = 2 * x + y
        mine = pltpu.make_async_copy(p_ref.at[jm], rb_ref.at[jm], local_sem)
        mine.start()
        sends, recvs = [], []
        for k in range(1, 4):
            px, py = (1 - x if k & 2 else x), (1 - y if k & 1 else y)
            sems = dict(send_sem=send_sems.at[k - 1], recv_sem=recv_sems.at[k - 1], device_id=(px, py, c), device_id_type=MESH)
            sends.append(pltpu.make_async_remote_copy(src_ref=p_ref.at[2 * px + py], dst_ref=rb_ref.at[jm], **sems))
            recvs.append(pltpu.make_async_remote_copy(src_ref=p_ref.at[jm], dst_ref=rb_ref.at[2 * px + py], **sems))
        for cp in sends:
            cp.start()
        for cp in recvs:
            cp.wait_recv()
        for cp in sends:
            cp.wait_send()
        mine.wait()

    any_spec = pl.BlockSpec(memory_space=pl.ANY)
    return pl.pallas_call(
        body, name="rs_chips", in_specs=[any_spec], out_specs=any_spec,
        out_shape=jax.ShapeDtypeStruct(part.shape, part.dtype),
        scratch_shapes=[pltpu.SemaphoreType.DMA((3,)), pltpu.SemaphoreType.DMA((3,)), pltpu.SemaphoreType.DMA],
    )(part)


def reduce_adam(parts, w, m, v, *, name, tr):
    R = w.shape[0]
    n_parts = parts.shape[0]
    assert R % tr == 0
    c1 = 1.0 - ADAM_B1 ** ADAM_STEP
    c2 = 1.0 - ADAM_B2 ** ADAM_STEP

    def body(p_ref, w_ref, m_ref, v_ref, g_ref, d_ref, mo_ref, vo_ref):
        g = p_ref[0].astype(F32)
        for i in range(1, n_parts):
            g = g + p_ref[i].astype(F32)
        mn = ADAM_B1 * m_ref[...] + (1.0 - ADAM_B1) * g
        vn = ADAM_B2 * v_ref[...] + (1.0 - ADAM_B2) * (g * g)
        g_ref[...] = g
        mo_ref[...] = mn
        vo_ref[...] = vn
        d_ref[...] = -ADAM_LR * ((mn / c1) / (jnp.sqrt(vn / c2) + ADAM_EPS) + ADAM_WD * w_ref[...])

    row = pl.BlockSpec((tr, 1024), lambda i: (i, 0))
    return pl.pallas_call(
        body, name=name, grid=(R // tr,),
        in_specs=[pl.BlockSpec((n_parts, tr, 1024), lambda i: (0, i, 0)), row, row, row],
        out_specs=[row] * 4,
        out_shape=[jax.ShapeDtypeStruct((R, 1024), F32)] * 4,
        compiler_params=_cp(("parallel",)),
    )(parts, w, m, v)


BIG = (("a_w_in", 385, 400), ("a_w_out", 128, 128), ("w_kv", 768, 768), ("b_w_q", 384, 384),
       ("b_w_out", 128, 128), ("f_w_up", 1408, 1408), ("f_w_down", 704, 704))
SMALL_SHARDED = (("a_norm_g", 128), ("a_hnorm_g", 128), ("f_conv_w", 4224))
SMALL_ROWS = 48
PACK_ROWS = sum(b[2] for b in BIG) + SMALL_ROWS
REPL = (("kv_norm_g", 1024, 1), ("b_norm_g", 1024, 1), ("f_norm_g", 2048, 2), ("f_conv_b", 11264, 11),
        ("final_norm_g", 1024, 1), ("rel_bias", 1536, 2), ("a_b_if", 8, 1))
REPL_ROWS = 24
LOSS_ROW = 19


def _rows(a, rows, padded):
    a = a.reshape(rows, 1024)
    return a if padded == rows else jnp.pad(a, ((0, padded - rows), (0, 0)))


def pack_shards(t, dtype, with_small):
    parts = [_rows(t[n].astype(dtype), r, p) for n, r, p in BIG]
    if with_small:
        flat = jnp.concatenate([t[n].astype(dtype).reshape(-1) for n, _ in SMALL_SHARDED])
        parts.append(jnp.pad(flat, (0, SMALL_ROWS * 1024 - flat.shape[0])).reshape(SMALL_ROWS, 1024))
    return jnp.concatenate(parts, axis=0)


def unpack_shards(pack, shapes):
    out = {}
    r0 = 0
    for n, r, p in BIG:
        out[n] = pack[r0:r0 + r].reshape(shapes[n])
        r0 += p
    flat = pack[r0:r0 + SMALL_ROWS].reshape(-1)
    e0 = 0
    for n, e in SMALL_SHARDED:
        out[n] = flat[e0:e0 + e].reshape(shapes[n])
        e0 += e
    return out


def pack_repl(t):
    parts = []
    for n, e, r in REPL:
        parts.append(jnp.pad(t[n].astype(F32).reshape(-1), (0, r * 1024 - e)))
    rows = sum(r for _, _, r in REPL)
    parts.append(jnp.zeros(((REPL_ROWS - rows) * 1024,), F32))
    return jnp.concatenate(parts).reshape(REPL_ROWS, 1024)


def unpack_repl(pack, shapes):
    out = {}
    r0 = 0
    for n, e, r in REPL:
        out[n] = pack[r0:r0 + r].reshape(-1)[:e].reshape(shapes[n])
        r0 += r
    return out


def ff_blocks(a):
    b = [a[..., i * CONV_TC:(i + 1) * CONV_TC] for i in range(4)]
    return jnp.concatenate([b[0], b[2], b[1], b[3]], axis=-1)


def split_cols(full, n):
    lead = full.shape[:-1]
    return jnp.moveaxis(full.reshape(lead + (N_DEV, n)), -2, 0)


def join_cols(parts):
    t = jnp.moveaxis(parts, 0, -2)
    return t.reshape(t.shape[:-2] + (t.shape[-2] * t.shape[-1],))


def kernel(x, a_norm_g, a_w_in, a_b_if, a_hnorm_g, a_w_out, kv_norm_g, w_kv, b_norm_g, b_w_q, b_w_out, rel_bias, f_norm_g, f_w_up, f_conv_w, f_conv_b, f_w_down, final_norm_g, loss_target, m_a_norm_g, m_a_w_in, m_a_b_if, m_a_hnorm_g, m_a_w_out, m_kv_norm_g, m_w_kv, m_b_norm_g, m_b_w_q, m_b_w_out, m_rel_bias, m_f_norm_g, m_f_w_up, m_f_conv_w, m_f_conv_b, m_f_w_down, m_final_norm_g, v_a_norm_g, v_a_w_in, v_a_b_if, v_a_hnorm_g, v_a_w_out, v_kv_norm_g, v_w_kv, v_b_norm_g, v_b_w_q, v_b_w_out, v_rel_bias, v_f_norm_g, v_f_w_up, v_f_conv_w, v_f_conv_b, v_f_w_down, v_final_norm_g):
    names = ["a_norm_g", "a_w_in", "a_b_if", "a_hnorm_g", "a_w_out", "kv_norm_g", "w_kv", "b_norm_g", "b_w_q", "b_w_out",
             "rel_bias", "f_norm_g", "f_w_up", "f_conv_w", "f_conv_b", "f_w_down", "final_norm_g"]
    w = dict(zip(names, (a_norm_g, a_w_in, a_b_if, a_hnorm_g, a_w_out, kv_norm_g, w_kv, b_norm_g, b_w_q, b_w_out,
                         rel_bias, f_norm_g, f_w_up, f_conv_w, f_conv_b, f_w_down, final_norm_g)))
    mom = dict(zip(names, (m_a_norm_g, m_a_w_in, m_a_b_if, m_a_hnorm_g, m_a_w_out, m_kv_norm_g, m_w_kv, m_b_norm_g, m_b_w_q,
                           m_b_w_out, m_rel_bias, m_f_norm_g, m_f_w_up, m_f_conv_w, m_f_conv_b, m_f_w_down, m_final_norm_g)))
    vel = dict(zip(names, (v_a_norm_g, v_a_w_in, v_a_b_if, v_a_hnorm_g, v_a_w_out, v_kv_norm_g, v_w_kv, v_b_norm_g, v_b_w_q,
                           v_b_w_out, v_rel_bias, v_f_norm_g, v_f_w_up, v_f_conv_w, v_f_conv_b, v_f_w_down, v_final_norm_g)))
    shapes = {n: w[n].shape for n in names}
    S = x.shape[1]
    assert x.shape[0] == 1 and S % (16 * BLK) == 0 and S % 1024 == 0
    X0 = x.reshape(S, D)
    target = loss_target.reshape(S, D)

    wb_all, ws_all = ag_weights(pack_shards(w, BF16, False),
                                pack_shards(w, F32, True)[PACK_ROWS - SMALL_ROWS:])
    seg = {}
    r0 = 0
    for n, r, p in BIG:
        seg[n] = wb_all[:, r0:r0 + r]
        r0 += p
    W_in = join_cols(seg["a_w_in"].reshape(N_DEV, D, 385))
    W_in = jnp.concatenate([jnp.pad(W_in[:, :3076], ((0, 0), (0, 124))),
                            jnp.pad(W_in[:, 3076:3080], ((0, 0), (0, 124)))], axis=1)
    W_out = seg["a_w_out"].reshape(1024, D)
    W_kv = join_cols(seg["w_kv"].reshape(N_DEV, D, 768))
    W_q = join_cols(seg["b_w_q"].reshape(N_DEV, D, 384))
    W_bout = seg["b_w_out"].reshape(1024, D)
    W_up = join_cols(seg["f_w_up"].reshape(N_DEV, 2, D, 704))
    W_down = jnp.moveaxis(seg["f_w_down"].reshape(N_DEV, 2, 352, D), 0, 1).reshape(2, D_FF, D)
    sflat = ws_all.reshape(N_DEV, SMALL_ROWS * 1024)
    g_a = sflat[:, 0:128].reshape(1, D)
    g_h = jnp.moveaxis(sflat[:, 128:256].reshape(N_DEV, A_HEADS, 32), 0, 1).reshape(1, A_HEADS * A_V)
    conv_w = ff_blocks(join_cols(sflat[:, 256:256 + 4224].reshape(N_DEV, 2, 3, 704)))
    conv_b = ff_blocks(f_conv_b)
    W_up = ff_blocks(W_up)
    bi = jnp.pad(a_b_if[:, :A_HEADS], ((0, 0), (0, 128 - A_HEADS)))
    bfg = jnp.pad(a_b_if[:, A_HEADS:], ((0, 0), (0, 128 - A_HEADS)))
    buckets = jnp.asarray(_bucket_tables())

    (xn_a,) = rms_fwd(X0, [g_a], name="rms_a")
    z = mm(xn_a, W_in, name="mm_a_in", out_dtype=F32, tn=1664)
    h, cst, nst = mlstm_fwd(z, bi, bfg, name="mlstm_fwd")
    X1, hg = ao_fwd(h, z, g_h, W_out, X0, name="ao_fwd")

    def ffn_fwd(X, l, tag):
        (xn,) = rms_fwd(X, [f_norm_g[l:l + 1]], name="rms_f" + tag)
        u, c, a = mm_up_conv(xn, W_up[l], conv_w[l], conv_b[l:l + 1], name="mm_up_conv" + tag)
        Xn = mm(a, W_down[l], name="mm_down" + tag, out_dtype=F32, tn=1024, tm=512, res=X)
        return Xn, (xn, u, c, a)

    X2, sav0 = ffn_fwd(X1, 0, "0")

    xkn, xbn = rms_fwd(X2, [kv_norm_g.reshape(1, D), b_norm_g], name="rms_kv_b")
    bias = bias_build(rel_bias, buckets, name="bias_build")
    col = lambda wmat, i: wmat[:, i * 1024:(i + 1) * 1024]
    qs, kk, vv, og, lg = [], [], [], [], []
    for g, (_, dil) in enumerate(B_GROUPS):
        xb_v, xk_v = xbn.reshape(S // dil, dil * D), xkn.reshape(S // dil, dil * D)
        qs.append(mm_view(xb_v, col(W_q, g), name="mm_q%d" % g))
        kk.append(mm_view(xk_v, col(W_kv, g), name="mm_k%d" % g))
        vv.append(mm_view(xk_v, col(W_kv, N_GROUPS + g), name="mm_v%d" % g))
        o_, l_ = attn_fwd(qs[g], kk[g], vv[g], bias, g, dil, name="attn_fwd%d" % g)
        og.append(o_)
        lg.append(l_)
    att, lse = attn_merge(og, lg, name="attn_merge")
    X3 = mm(att, W_bout, name="mm_b_out", out_dtype=F32, tn=1024, res=X2)
    X4, sav1 = ffn_fwd(X3, 1, "1")

    dX4, d_final_g, loss_part = loss_head(X4, target, final_norm_g.reshape(1, D), name="loss_head")

    def ffn_bwd(X, dXn, l, sav, tag):
        xn, u, c, a = sav
        dW_down = mm_tn(a, dXn, name="tn_down" + tag, tn=1024, tk=512)
        duc = mm_da_act(dXn, W_down[l].T, c, name="mm_da_act" + tag)
        dW_up, du, dwb = tn_up_conv(xn, duc, u, conv_w[l], name="tn_up_conv" + tag)
        dX, (dg,) = mm_rms_bwd([du], [W_up[l].T], f_norm_g[l:l + 1], X, dXn, name="mm_rms_bwd_f" + tag)
        return dX, dW_down, dW_up, dwb, dg

    dX3, dWd1, dWu1, dwb1, dgf1 = ffn_bwd(X3, dX4, 1, sav1, "1")

    dW_bout = mm_tn(att, dX3, name="tn_b_out", tn=1024)
    dout, dsum = mm_dout(dX3, W_bout.T, att, name="mm_dout")
    dqs, dks, dvs, dbias = [], [], [], []
    for g, (_, dil) in enumerate(B_GROUPS):
        dq_, dk_, dv_, db_ = attn_bwd(qs[g], kk[g], vv[g], bias, dout, dsum, lse, g, dil, name="attn_bwd%d" % g)
        dqs.append(dq_)
        dks.append(dk_)
        dvs.append(dv_)
        dbias.append(db_)
    d_rel = bias_grad(jnp.stack(dbias), buckets, name="bias_grad")[:, :N_GROUPS * B_HEADS]
    dW_q = jnp.concatenate([mm_tn(xbn, d_, name="tn_q%d" % g, tn=1024) for g, d_ in enumerate(dqs)], axis=1)
    dW_kv = jnp.concatenate([mm_tn(xkn, d_, name="tn_kv%d" % i, tn=1024) for i, d_ in enumerate(dks + dvs)], axis=1)
    W_qT, W_kvT = W_q.T, W_kv.T
    rows = lambda wmat, i: wmat[i * 1024:(i + 1) * 1024]
    dxn_kv = mm_sum(dks + dvs, [rows(W_kvT, i) for i in range(2 * N_GROUPS)], name="mm_dxn_kv")
    dX2, (dg_b, dg_kv) = mm_rms_bwd(dqs, [rows(W_qT, g) for g in range(N_GROUPS)], b_norm_g, X2, dX3,
                                    extra=[(dxn_kv, kv_norm_g.reshape(1, D))], name="mm_rms_bwd_b_kv")

    dX1, dWd0, dWu0, dwb0, dgf0 = ffn_bwd(X1, dX2, 0, sav0, "0")

    dW_out = mm_tn(hg, dX1, name="tn_a_out", tn=1024)
    dh, dzo, dgh = ao_bwd(dX1, W_out.T, h, z, g_h, name="ao_bwd")
    dz, db_if = mlstm_bwd(z, bi, bfg, cst, nst, dh, dzo, name="mlstm_bwd")
    dW_in = mm_tn(xn_a, dz, name="tn_a_in", tn=1664)
    dW_in = jnp.concatenate([dW_in[:, :3076], dW_in[:, Z_GF:Z_GF + 4]], axis=1)
    dX0, (dg_a,) = mm_rms_bwd([dz], [W_in.T], g_a, X0, dX1, name="mm_rms_bwd_a")

    dWu = ff_blocks(jnp.stack([dWu0, dWu1]))
    dWd = jnp.stack([dWd0, dWd1])
    dwb = ff_blocks(jnp.stack([dwb0, dwb1]))
    slots = [
        jnp.pad(split_cols(dW_in, 385).reshape(N_DEV, 385, 1024), ((0, 0), (0, 15), (0, 0))),
        dW_out.reshape(N_DEV, 128, 1024),
        split_cols(dW_kv, 768).reshape(N_DEV, 768, 1024),
        split_cols(dW_q, 384).reshape(N_DEV, 384, 1024),
        dW_bout.reshape(N_DEV, 128, 1024),
        split_cols(dWu, 704).reshape(N_DEV, 1408, 1024),
        jnp.moveaxis(dWd.reshape(2, N_DEV, 352, D), 1, 0).reshape(N_DEV, 704, 1024),
    ]
    small = jnp.concatenate([
        dg_a.reshape(N_DEV, 128),
        split_cols(dgh.reshape(A_HEADS, A_V), 32).reshape(N_DEV, 128),
        split_cols(dwb[:, 0:3], 704).reshape(N_DEV, 4224)], axis=1)
    slots.append(jnp.pad(small, ((0, 0), (0, SMALL_ROWS * 1024 - small.shape[1]))).reshape(N_DEV, SMALL_ROWS, 1024))
    gpack = jnp.concatenate([t.astype(BF16) for t in slots], axis=1)
    repl_g = {"kv_norm_g": dg_kv, "b_norm_g": dg_b, "f_norm_g": jnp.concatenate([dgf0, dgf1]),
              "f_conv_b": dwb[:, 3], "final_norm_g": d_final_g, "rel_bias": d_rel,
              "a_b_if": jnp.concatenate([db_if[0, :A_HEADS], db_if[1, :A_HEADS]])}
    spack = pack_repl(repl_g)
    spack = spack.at[LOSS_ROW, 0].set(loss_part[0, 0])

    from_sibling, sparts = rs_sibling(gpack, spack)
    own = lax.dynamic_index_in_dim(gpack.reshape(4, 2, PACK_ROWS, 1024), lax.axis_index("c"), axis=1, keepdims=False)
    parts = rs_chips(pair_add(own, from_sibling, name="rs_pair_add", tr=PACK_ROWS // 8))
    gb, db, mb, vb = reduce_adam(parts, pack_shards(w, F32, True), pack_shards(mom, F32, True),
                                 pack_shards(vel, F32, True), name="reduce_adam_big", tr=PACK_ROWS // 8)
    gs, ds, ms, vs = reduce_adam(sparts, pack_repl(w), pack_repl(mom), pack_repl(vel), name="reduce_adam_small", tr=REPL_ROWS)
    loss = gs[LOSS_ROW, 0]

    def collect(big, sm):
        t = unpack_shards(big, shapes)
        t.update(unpack_repl(sm, shapes))
        return [t[n] for n in names]

    return (loss, dX0.reshape(1, S, D), *collect(gb, gs), *collect(db, ds), *collect(mb, ms), *collect(vb, vs))
```

```python
import math

import numpy as np
import jax
import jax.numpy as jnp
from jax import lax
from jax.experimental import pallas as pl
from jax.experimental.pallas import tpu as pltpu

F32 = jnp.float32
BF16 = jnp.bfloat16
MESH = pl.DeviceIdType.MESH

D = 1024
A_HEADS = 4
A_QK = 128
A_V = 256
SOFTCAP = 15.0
N_GROUPS = 3
B_GROUPS = ((128, 1), (512, 4), (2048, 16))
B_HEADS = 16
B_DH = 64
BLK = 128
REL_BUCKETS = 32
REL_MAX_DIST = 2048
D_FF = 2816
EPS = 1e-6
ADAM_LR = 0.001
ADAM_B1 = 0.9
ADAM_B2 = 0.999
ADAM_EPS = 1e-08
ADAM_WD = 0.01
ADAM_STEP = 10

N_DEV = 8
V7X_VMEM_BYTES = 64 * 1024 * 1024
VMEM_LIMIT = V7X_VMEM_BYTES - 8 * 1024 * 1024
MLSTM_CHUNK = 256
Z_W = 3328
Z_GI = 3072
Z_GF = 3200


def _cp(sem):
    return pltpu.CompilerParams(dimension_semantics=sem, vmem_limit_bytes=VMEM_LIMIT)


def _dot(a, b, **kw):
    return jnp.dot(a, b, preferred_element_type=F32, **kw)


def _dot_nt(a, b):
    return lax.dot_general(a, b, (((1,), (1,)), ((), ())), preferred_element_type=F32)


def _dot_tn(a, b):
    return lax.dot_general(a, b, (((0,), (0,)), ((), ())), preferred_element_type=F32)


def _dot_split(a, b01):
    hi = a.astype(BF16)
    lo = (a - hi.astype(F32)).astype(BF16)
    return _dot(hi, b01) + _dot(lo, b01)


def _dot01(t01, a):
    t = t01.astype(BF16)
    hi = a.astype(BF16)
    r1 = a - hi.astype(F32)
    mid = r1.astype(BF16)
    lo = (r1 - mid.astype(F32)).astype(BF16)
    return _dot(t, hi) + _dot(t, mid) + _dot(t, lo)


def mm(a, b, *, name, out_dtype, tn, tm=1024, res=None):
    M, K = a.shape
    N = b.shape[1]
    assert M % tm == 0 and N % tn == 0 and b.shape[0] == K

    def body(a_ref, b_ref, *rest):
        o_ref = rest[-1]
        acc = _dot(a_ref[...].astype(BF16), b_ref[...])
        if res is not None:
            acc = acc + rest[0][...]
        o_ref[...] = acc.astype(out_dtype)

    in_specs = [pl.BlockSpec((tm, K), lambda j, i: (i, 0)), pl.BlockSpec((K, tn), lambda j, i: (0, j))]
    args = [a, b]
    if res is not None:
        in_specs.append(pl.BlockSpec((tm, tn), lambda j, i: (i, j)))
        args.append(res)
    return pl.pallas_call(
        body, name=name, grid=(N // tn, M // tm), in_specs=in_specs,
        out_specs=pl.BlockSpec((tm, tn), lambda j, i: (i, j)),
        out_shape=jax.ShapeDtypeStruct((M, N), out_dtype),
        compiler_params=_cp(("parallel", "parallel")),
    )(*args)


def mm_sum(a_list, b_list, *, name, tm=512):
    M, K = a_list[0].shape
    N = b_list[0].shape[1]
    n = len(a_list)
    assert M % tm == 0

    def body(*refs):
        o_ref = refs[-1]
        acc = _dot(refs[0][...], refs[n][...])
        for i in range(1, n):
            acc = acc + _dot(refs[i][...], refs[n + i][...])
        o_ref[...] = acc

    return pl.pallas_call(
        body, name=name, grid=(M // tm,),
        in_specs=[pl.BlockSpec((tm, K), lambda i: (i, 0))] * n + [pl.BlockSpec((K, N), lambda i: (0, 0))] * n,
        out_specs=pl.BlockSpec((tm, N), lambda i: (i, 0)),
        out_shape=jax.ShapeDtypeStruct((M, N), F32),
        compiler_params=_cp(("parallel",)),
    )(*a_list, *b_list)


def mm_view(a_view, w, *, name, tm=1024):
    S2, W = a_view.shape
    tm = min(tm, S2)

    def body(a_ref, w_ref, o_ref):
        o_ref[...] = _dot(a_ref[...], w_ref[...]).astype(BF16)

    blk = pl.BlockSpec((tm, 1024), lambda r, i: (i, r))
    return pl.pallas_call(
        body, name=name, grid=(W // 1024, S2 // tm),
        in_specs=[blk, pl.BlockSpec((1024, 1024), lambda r, i: (0, 0))], out_specs=blk,
        out_shape=jax.ShapeDtypeStruct((S2, W), BF16), compiler_params=_cp(("parallel", "parallel")),
    )(a_view, w)


def mm_tn(a, b, *, name, tn, tk=1024):
    S, Kd = a.shape
    N = b.shape[1]
    assert S % tk == 0 and N % tn == 0 and b.shape[0] == S

    def body(a_ref, b_ref, o_ref):
        @pl.when(pl.program_id(1) == 0)
        def _():
            o_ref[...] = jnp.zeros_like(o_ref)

        o_ref[...] += _dot_tn(a_ref[...].astype(BF16), b_ref[...].astype(BF16))

    return pl.pallas_call(
        body, name=name, grid=(N // tn, S // tk),
        in_specs=[pl.BlockSpec((tk, Kd), lambda j, k: (k, 0)), pl.BlockSpec((tk, tn), lambda j, k: (k, j))],
        out_specs=pl.BlockSpec((Kd, tn), lambda j, k: (0, j)),
        out_shape=jax.ShapeDtypeStruct((Kd, N), F32),
        compiler_params=_cp(("parallel", "arbitrary")),
    )(a, b)


def rms_fwd(x, gains, *, name, tm=1024):
    S = x.shape[0]
    n = len(gains)

    def body(x_ref, *rest):
        xf = x_ref[...]
        y = xf * lax.rsqrt(jnp.mean(xf * xf, axis=-1, keepdims=True) + EPS)
        for i in range(n):
            rest[n + i][...] = (y * rest[i][...]).astype(BF16)

    return pl.pallas_call(
        body, name=name, grid=(S // tm,),
        in_specs=[pl.BlockSpec((tm, D), lambda i: (i, 0))] + [pl.BlockSpec((1, D), lambda i: (0, 0))] * n,
        out_specs=[pl.BlockSpec((tm, D), lambda i: (i, 0))] * n,
        out_shape=[jax.ShapeDtypeStruct((S, D), BF16)] * n,
        compiler_params=_cp(("parallel",)),
    )(x, *gains)


def mm_rms_bwd(a_list, b_list, g, x, dres, extra=(), *, name, tm=512):
    S = x.shape[0]
    n = len(a_list)
    ne = len(extra)

    def body(*refs):
        a_refs, b_refs = refs[:n], refs[n:2 * n]
        g_ref, x_ref, dres_ref = refs[2 * n:2 * n + 3]
        e_refs = refs[2 * n + 3:2 * n + 3 + 2 * ne]
        dx_ref = refs[2 * n + 3 + 2 * ne]
        dg_refs = refs[2 * n + 4 + 2 * ne:]

        @pl.when(pl.program_id(0) == 0)
        def _():
            for r in dg_refs:
                r[...] = jnp.zeros_like(r)

        acc = _dot(a_refs[0][...], b_refs[0][...])
        for i in range(1, n):
            acc = acc + _dot(a_refs[i][...], b_refs[i][...])
        xf = x_ref[...]
        r = lax.rsqrt(jnp.mean(xf * xf, axis=-1, keepdims=True) + EPS)
        xhat = xf * r
        total = dres_ref[...]
        branches = [(acc, g_ref[...])] + [(e_refs[2 * i][...], e_refs[2 * i + 1][...]) for i in range(ne)]
        for i, (dy, gg) in enumerate(branches):
            dg_refs[i][...] += jnp.sum(dy * xhat, axis=0, keepdims=True)
            dyg = dy * gg
            total = total + r * (dyg - xhat * jnp.mean(dyg * xhat, axis=-1, keepdims=True))
        dx_ref[...] = total

    row = pl.BlockSpec((tm, D), lambda i: (i, 0))
    vec = pl.BlockSpec((1, D), lambda i: (0, 0))
    in_specs = ([pl.BlockSpec((tm, a.shape[1]), lambda i: (i, 0)) for a in a_list]
                + [pl.BlockSpec(b.shape, lambda i: (0, 0)) for b in b_list] + [vec, row, row])
    args = list(a_list) + list(b_list) + [g, x, dres]
    for dxn_e, g_e in extra:
        in_specs += [row, vec]
        args += [dxn_e, g_e]
    outs = pl.pallas_call(
        body, name=name, grid=(S // tm,), in_specs=in_specs,
        out_specs=[row] + [vec] * (1 + ne),
        out_shape=[jax.ShapeDtypeStruct((S, D), F32)] + [jax.ShapeDtypeStruct((1, D), F32)] * (1 + ne),
        compiler_params=_cp(("arbitrary",)),
    )(*args)
    return outs[0], outs[1:]


def loss_head(x, target, g, *, name, tm=512):
    S = x.shape[0]

    def body(x_ref, t_ref, g_ref, dx_ref, dg_ref, loss_ref):
        @pl.when(pl.program_id(0) == 0)
        def _():
            dg_ref[...] = jnp.zeros_like(dg_ref)
            loss_ref[...] = jnp.zeros_like(loss_ref)

        xf = x_ref[...]
        gg = g_ref[...]
        r = lax.rsqrt(jnp.mean(xf * xf, axis=-1, keepdims=True) + EPS)
        xhat = xf * r
        e = xhat * gg - t_ref[...]
        per_tok = jnp.mean(e * e, axis=-1, keepdims=True)
        loss_ref[...] += 0.5 * jnp.sum(per_tok, axis=0, keepdims=True)
        dy = e * (1.0 / D)
        dg_ref[...] += jnp.sum(dy * xhat, axis=0, keepdims=True)
        dyg = dy * gg
        dx_ref[...] = r * (dyg - xhat * jnp.mean(dyg * xhat, axis=-1, keepdims=True))

    row = pl.BlockSpec((tm, D), lambda i: (i, 0))
    vec = pl.BlockSpec((1, D), lambda i: (0, 0))
    return pl.pallas_call(
        body, name=name, grid=(S // tm,),
        in_specs=[row, row, vec],
        out_specs=[row, vec, pl.BlockSpec((1, 128), lambda i: (0, 0))],
        out_shape=[jax.ShapeDtypeStruct((S, D), F32), jax.ShapeDtypeStruct((1, D), F32),
                   jax.ShapeDtypeStruct((1, 128), F32)],
        compiler_params=_cp(("arbitrary",)),
    )(x, target, g)


def _sigmoid(x):
    return 1.0 / (1.0 + jnp.exp(-x))


def _gates(z_ref, bi_ref, bf_ref):
    li = SOFTCAP * jnp.tanh((z_ref[:, Z_GI:Z_GI + 128] + bi_ref[...]) * (1.0 / SOFTCAP))
    scf = SOFTCAP * jnp.tanh((z_ref[:, Z_GF:Z_GF + 128] + bf_ref[...]) * (1.0 / SOFTCAP))
    lf = jnp.minimum(scf, 0.0) - jnp.log(1.0 + jnp.exp(-jnp.abs(scf)))
    return li, scf, lf


def _tri(L, lower):
    r = lax.broadcasted_iota(jnp.int32, (L, L), 0)
    c = lax.broadcasted_iota(jnp.int32, (L, L), 1)
    return (r >= c) if lower else (r <= c)


def mlstm_fwd(z, bi, bf, *, name):
    S = z.shape[0]
    L = MLSTM_CHUNK
    NC = S // L
    scale = A_QK ** -0.5

    def body(z_ref, bi_ref, bf_ref, h_ref, cst_ref, nst_ref, C_s, n_s):
        @pl.when(pl.program_id(0) == 0)
        def _():
            C_s[...] = jnp.zeros_like(C_s)
            n_s[...] = jnp.zeros_like(n_s)

        li, _, lf = _gates(z_ref, bi_ref, bf_ref)
        causal = _tri(L, True)
        b = _dot01(causal, lf)
        liT = li.T
        bT = b.T
        cst_ref[0] = C_s[...].astype(BF16)
        nst_ref[0] = n_s[...]
        for h in range(A_HEADS):
            q = z_ref[:, h * A_QK:(h + 1) * A_QK] * scale
            k = z_ref[:, 512 + h * A_QK:512 + (h + 1) * A_QK]
            qb = q.astype(BF16)
            kb = k.astype(BF16)
            vb = z_ref[:, 1024 + h * A_V:1024 + (h + 1) * A_V].astype(BF16)
            a_col, b_col = li[:, h:h + 1], b[:, h:h + 1]
            a_row, b_row = liT[h:h + 1, :], bT[h:h + 1, :]
            Dm = jnp.exp(jnp.where(causal, b_col - b_row + a_row, -jnp.inf))
            A = _dot_nt(qb, kb) * Dm
            eb = jnp.exp(b_col)
            Ch = C_s[h]
            nh = n_s[h:h + 1, :]
            num = _dot(A.astype(BF16), vb) + eb * _dot(qb, Ch.astype(BF16))
            den = jnp.sum(A, axis=-1, keepdims=True) + eb * jnp.sum(q * nh, axis=-1, keepdims=True)
            h_ref[:, h * A_V:(h + 1) * A_V] = num / jnp.maximum(jnp.abs(den), 1.0)
            bL = b_col[L - 1:L, :]
            kw = jnp.exp(bL - b_col + a_col) * k
            decay = jnp.exp(bL)
            C_s[h] = decay * Ch + _dot_tn(kw.astype(BF16), vb)
            n_s[h:h + 1, :] = decay * nh + jnp.sum(kw, axis=0, keepdims=True)

    vec = pl.BlockSpec((1, 128), lambda c: (0, 0))
    return pl.pallas_call(
        body, name=name, grid=(NC,),
        in_specs=[pl.BlockSpec((L, Z_W), lambda c: (c, 0)), vec, vec],
        out_specs=[pl.BlockSpec((L, 1024), lambda c: (c, 0)),
                   pl.BlockSpec((1, A_HEADS, A_QK, A_V), lambda c: (c, 0, 0, 0)),
                   pl.BlockSpec((1, 8, 128), lambda c: (c, 0, 0))],
        out_shape=[jax.ShapeDtypeStruct((S, 1024), F32),
                   jax.ShapeDtypeStruct((NC, A_HEADS, A_QK, A_V), BF16),
                   jax.ShapeDtypeStruct((NC, 8, 128), F32)],
        scratch_shapes=[pltpu.VMEM((A_HEADS, A_QK, A_V), F32), pltpu.VMEM((8, 128), F32)],
        compiler_params=_cp(("arbitrary",)),
    )(z, bi, bf)


def mlstm_bwd(z, bi, bf, cst, nst, dh, dzo, *, name):
    S = z.shape[0]
    L = MLSTM_CHUNK
    NC = S // L
    scale = A_QK ** -0.5

    def body(z_ref, bi_ref, bf_ref, cst_ref, nst_ref, dh_ref, dzo_ref, dz_ref, db_ref, dC_s, dn_s):
        @pl.when(pl.program_id(0) == 0)
        def _():
            dC_s[...] = jnp.zeros_like(dC_s)
            dn_s[...] = jnp.zeros_like(dn_s)
            db_ref[...] = jnp.zeros_like(db_ref)

        li, scf, lf = _gates(z_ref, bi_ref, bf_ref)
        causal = _tri(L, True)
        b = _dot01(causal, lf)
        liT = li.T
        bT = b.T
        lane = lax.broadcasted_iota(jnp.int32, (L, 128), 1)
        sub = lax.broadcasted_iota(jnp.int32, (128, L), 0)
        lane1 = lax.broadcasted_iota(jnp.int32, (1, 128), 1)
        Rm = jnp.zeros((L, 128), F32)
        KIm = jnp.zeros((L, 128), F32)
        csm = jnp.zeros((128, L), F32)
        Xm = jnp.zeros((1, 128), F32)
        for h in range(A_HEADS):
            q = z_ref[:, h * A_QK:(h + 1) * A_QK] * scale
            k = z_ref[:, 512 + h * A_QK:512 + (h + 1) * A_QK]
            qb = q.astype(BF16)
            kb = k.astype(BF16)
            vb = z_ref[:, 1024 + h * A_V:1024 + (h + 1) * A_V].astype(BF16)
            a_col, b_col = li[:, h:h + 1], b[:, h:h + 1]
            a_row, b_row = liT[h:h + 1, :], bT[h:h + 1, :]
            Dm = jnp.exp(jnp.where(causal, b_col - b_row + a_row, -jnp.inf))
            Sqk = _dot_nt(qb, kb)
            A = Sqk * Dm
            Ab = A.astype(BF16)
            eb = jnp.exp(b_col)
            Cb = cst_ref[0, h]
            nh = nst_ref[0, h:h + 1, :]
            num = _dot(Ab, vb) + eb * _dot(qb, Cb)
            den = jnp.sum(A, axis=-1, keepdims=True) + eb * jnp.sum(q * nh, axis=-1, keepdims=True)
            aden = jnp.abs(den)
            u = 1.0 / jnp.maximum(aden, 1.0)
            dhh = dh_ref[:, h * A_V:(h + 1) * A_V]
            dnum = dhh * u
            dden = jnp.where(aden > 1.0, -jnp.sum(dhh * num, axis=-1, keepdims=True) * u * u * jnp.sign(den), 0.0)
            dnb = dnum.astype(BF16)
            G = Dm * (_dot_nt(dnb, vb) + dden)
            Gb = G.astype(BF16)
            E = G * Sqk
            rs = jnp.sum(E, axis=-1, keepdims=True)
            cs = jnp.sum(E, axis=0, keepdims=True)
            dCh = dC_s[h]
            dnh = dn_s[h:h + 1, :]
            dCb = dCh.astype(BF16)
            bL = b_col[L - 1:L, :]
            wk = jnp.exp(bL - b_col + a_col)
            decay = jnp.exp(bL)
            dq_inter = eb * (_dot_nt(dnb, Cb) + dden * nh)
            dk_inter = wk * (_dot_nt(vb, dCb) + dnh)
            dq = _dot(Gb, kb) + dq_inter
            dk = _dot_tn(Gb, qb) + dk_inter
            dv = _dot_tn(Ab, dnb) + wk * _dot(kb, dCb)
            dz_ref[:, h * A_QK:(h + 1) * A_QK] = (dq * scale).astype(BF16)
            dz_ref[:, 512 + h * A_QK:512 + (h + 1) * A_QK] = dk.astype(BF16)
            dz_ref[:, 1024 + h * A_V:1024 + (h + 1) * A_V] = dv.astype(BF16)
            KI = jnp.sum(k * dk_inter, axis=-1, keepdims=True)
            R = rs + jnp.sum(q * dq_inter, axis=-1, keepdims=True)
            cross = (jnp.sum(jnp.sum(dCh * Cb.astype(F32), axis=0, keepdims=True), axis=1, keepdims=True)
                     + jnp.sum(dnh * nh, axis=1, keepdims=True))
            Xm = jnp.where(lane1 == h, decay * cross, Xm)
            Rm = jnp.where(lane == h, R, Rm)
            KIm = jnp.where(lane == h, KI, KIm)
            csm = jnp.where(sub == h, cs, csm)
            ebq = eb * q
            dC_s[h] = decay * dCh + _dot_tn(ebq.astype(BF16), dnb)
            dn_s[h:h + 1, :] = decay * dnh + jnp.sum(ebq * dden, axis=0, keepdims=True)
        dz_ref[:, 2048:3072] = dzo_ref[...]
        cs_col = csm.T
        da = cs_col + KIm
        rr = lax.broadcasted_iota(jnp.int32, (L, L), 0)
        cc = lax.broadcasted_iota(jnp.int32, (L, L), 1)
        dlf = _dot01(rr <= cc, Rm - cs_col) + _dot01(rr > cc, KIm) + Xm
        dpre_i = da * (1.0 - (li * (1.0 / SOFTCAP)) ** 2)
        dpre_f = dlf * (1.0 - _sigmoid(scf)) * (1.0 - (scf * (1.0 / SOFTCAP)) ** 2)
        dz_ref[:, Z_GI:Z_GI + 128] = dpre_i.astype(BF16)
        dz_ref[:, Z_GF:Z_GF + 128] = dpre_f.astype(BF16)
        db_ref[0:1, :] += jnp.sum(dpre_i, axis=0, keepdims=True)
        db_ref[1:2, :] += jnp.sum(dpre_f, axis=0, keepdims=True)

    vec = pl.BlockSpec((1, 128), lambda c: (0, 0))
    rev = lambda c: (NC - 1 - c, 0)
    return pl.pallas_call(
        body, name=name, grid=(NC,),
        in_specs=[pl.BlockSpec((L, Z_W), rev), vec, vec,
                  pl.BlockSpec((1, A_HEADS, A_QK, A_V), lambda c: (NC - 1 - c, 0, 0, 0)),
                  pl.BlockSpec((1, 8, 128), lambda c: (NC - 1 - c, 0, 0)),
                  pl.BlockSpec((L, 1024), rev), pl.BlockSpec((L, 1024), rev)],
        out_specs=[pl.BlockSpec((L, Z_W), rev), pl.BlockSpec((8, 128), lambda c: (0, 0))],
        out_shape=[jax.ShapeDtypeStruct((S, Z_W), BF16), jax.ShapeDtypeStruct((8, 128), F32)],
        scratch_shapes=[pltpu.VMEM((A_HEADS, A_QK, A_V), F32), pltpu.VMEM((8, 128), F32)],
        compiler_params=_cp(("arbitrary",)),
    )(z, bi, bf, cst, nst, dh, dzo)


def ao_fwd(h, z, gh, w_out, x, *, name, tm=512):
    S = h.shape[0]

    def body(h_ref, o_ref, gh_ref, w_ref, x_ref, x1_ref, hg_ref):
        for hd in range(A_HEADS):
            sl = slice(hd * A_V, (hd + 1) * A_V)
            hs = h_ref[:, sl]
            hn = hs * lax.rsqrt(jnp.mean(hs * hs, axis=-1, keepdims=True) + EPS) * gh_ref[:, sl]
            hg_ref[:, sl] = (hn * _sigmoid(o_ref[:, sl])).astype(BF16)
        x1_ref[...] = x_ref[...] + _dot(hg_ref[...], w_ref[...])

    row = pl.BlockSpec((tm, 1024), lambda i: (i, 0))
    return pl.pallas_call(
        body, name=name, grid=(S // tm,),
        in_specs=[row, pl.BlockSpec((tm, 1024), lambda i: (i, 2)), pl.BlockSpec((1, 1024), lambda i: (0, 0)),
                  pl.BlockSpec((1024, 1024), lambda i: (0, 0)), row],
        out_specs=[row, row],
        out_shape=[jax.ShapeDtypeStruct((S, 1024), F32), jax.ShapeDtypeStruct((S, 1024), BF16)],
        compiler_params=_cp(("parallel",)),
    )(h, z, gh, w_out, x)


def ao_bwd(g1, w_out_t, h, z, gh, *, name, tm=512):
    S = h.shape[0]

    def body(g_ref, w_ref, h_ref, o_ref, gh_ref, dh_ref, dzo_ref, dgh_ref):
        @pl.when(pl.program_id(0) == 0)
        def _():
            dgh_ref[...] = jnp.zeros_like(dgh_ref)

        dhg = _dot(g_ref[...].astype(BF16), w_ref[...])
        for hd in range(A_HEADS):
            sl = slice(hd * A_V, (hd + 1) * A_V)
            hs = h_ref[:, sl]
            r = lax.rsqrt(jnp.mean(hs * hs, axis=-1, keepdims=True) + EPS)
            hhat = hs * r
            ghs = gh_ref[:, sl]
            sig = _sigmoid(o_ref[:, sl])
            d = dhg[:, sl]
            dhn = d * sig
            dzo_ref[:, sl] = (d * hhat * ghs * sig * (1.0 - sig)).astype(BF16)
            dgh_ref[:, sl] += jnp.sum(dhn * hhat, axis=0, keepdims=True)
            dhhat = dhn * ghs
            dh_ref[:, sl] = r * (dhhat - hhat * jnp.mean(dhhat * hhat, axis=-1, keepdims=True))

    row = pl.BlockSpec((tm, 1024), lambda i: (i, 0))
    vec = pl.BlockSpec((1, 1024), lambda i: (0, 0))
    return pl.pallas_call(
        body, name=name, grid=(S // tm,),
        in_specs=[row, pl.BlockSpec((1024, 1024), lambda i: (0, 0)), row,
                  pl.BlockSpec((tm, 1024), lambda i: (i, 2)), vec],
        out_specs=[row, row, vec],
        out_shape=[jax.ShapeDtypeStruct((S, 1024), F32), jax.ShapeDtypeStruct((S, 1024), BF16),
                   jax.ShapeDtypeStruct((1, 1024), F32)],
        compiler_params=_cp(("arbitrary",)),
    )(g1, w_out_t, h, z, gh)


CONV_TC = 1408
CONV_HALO = 16


def mm_up_conv(xn, w_up, cw, cb, *, name, tm=512):
    S = xn.shape[0]
    TN = 2 * CONV_TC

    def body(x_ref, w_ref, cw_ref, cb_ref, u_ref, c_ref, a_ref, carry):
        @pl.when(pl.program_id(1) == 0)
        def _():
            carry[...] = jnp.zeros_like(carry)

        ub = _dot(x_ref[...], w_ref[...]).astype(BF16)
        u_ref[...] = ub
        u = ub.astype(F32)
        prev = carry[...]
        row = lax.broadcasted_iota(jnp.int32, u.shape, 0)
        u1 = jnp.where(row == 0, prev[7:8], pltpu.roll(u, 1, axis=0))
        u2 = jnp.where(row == 0, prev[6:7], jnp.where(row == 1, prev[7:8], pltpu.roll(u, 2, axis=0)))
        carry[...] = u[tm - 8:tm]
        w = cw_ref[...]
        c = u * w[2:3] + u1 * w[1:2] + u2 * w[0:1] + cb_ref[...]
        c_ref[...] = c.astype(BF16)
        g, v = c[:, :CONV_TC], c[:, CONV_TC:]
        a_ref[...] = (g * _sigmoid(g) * v).astype(BF16)

    big = pl.BlockSpec((tm, TN), lambda j, i: (i, j))
    return pl.pallas_call(
        body, name=name, grid=(2, S // tm),
        in_specs=[pl.BlockSpec((tm, D), lambda j, i: (i, 0)), pl.BlockSpec((D, TN), lambda j, i: (0, j)),
                  pl.BlockSpec((3, TN), lambda j, i: (0, j)), pl.BlockSpec((1, TN), lambda j, i: (0, j))],
        out_specs=[big, big, pl.BlockSpec((tm, CONV_TC), lambda j, i: (i, j))],
        out_shape=[jax.ShapeDtypeStruct((S, 2 * D_FF), BF16), jax.ShapeDtypeStruct((S, 2 * D_FF), BF16),
                   jax.ShapeDtypeStruct((S, D_FF), BF16)],
        scratch_shapes=[pltpu.VMEM((8, TN), F32)],
        compiler_params=_cp(("parallel", "arbitrary")),
    )(xn, w_up, cw, cb)


def mm_da_act(dxn, w_down_t, c, *, name, tm=512):
    S = c.shape[0]
    TN = 2 * CONV_TC

    def body(x_ref, w_ref, c_ref, duc_ref):
        dav = _dot(x_ref[...].astype(BF16), w_ref[...])
        g = c_ref[:, :CONV_TC].astype(F32)
        v = c_ref[:, CONV_TC:].astype(F32)
        sg = _sigmoid(g)
        gs = g * sg
        duc_ref[:, :CONV_TC] = (dav * v * (sg + gs * (1.0 - sg))).astype(BF16)
        duc_ref[:, CONV_TC:] = (dav * gs).astype(BF16)

    return pl.pallas_call(
        body, name=name, grid=(2, S // tm),
        in_specs=[pl.BlockSpec((tm, D), lambda j, i: (i, 0)), pl.BlockSpec((D, CONV_TC), lambda j, i: (0, j)),
                  pl.BlockSpec((tm, TN), lambda j, i: (i, j))],
        out_specs=pl.BlockSpec((tm, TN), lambda j, i: (i, j)),
        out_shape=jax.ShapeDtypeStruct((S, 2 * D_FF), BF16),
        compiler_params=_cp(("parallel", "parallel")),
    )(dxn, w_down_t, c)


def tn_up_conv(xn, duc, u, cw, *, name, tk=512):
    S, Kd = xn.shape
    N = duc.shape[1]
    hb = tk // CONV_HALO
    nblk = S // tk

    def body(x_ref, d_ref, halo_ref, u_ref, w_ref, o_ref, du_ref, dwb_ref):
        k = pl.program_id(1)

        @pl.when(k == 0)
        def _():
            o_ref[...] = jnp.zeros_like(o_ref)
            dwb_ref[...] = jnp.zeros_like(dwb_ref)

        d = d_ref[...].astype(F32)
        hl = jnp.where(k == nblk - 1, 0.0, halo_ref[...].astype(F32))
        row = lax.broadcasted_iota(jnp.int32, d.shape, 0)
        d1 = jnp.where(row == tk - 1, hl[0:1], pltpu.roll(d, tk - 1, axis=0))
        d2 = jnp.where(row == tk - 1, hl[1:2], jnp.where(row == tk - 2, hl[0:1], pltpu.roll(d, tk - 2, axis=0)))
        w = w_ref[...]
        du = (d * w[2:3] + d1 * w[1:2] + d2 * w[0:1]).astype(BF16)
        du_ref[...] = du
        o_ref[...] += _dot_tn(x_ref[...], du)
        u = u_ref[...].astype(F32)
        dwb_ref[0:1, :] += jnp.sum(u * d2, axis=0, keepdims=True)
        dwb_ref[1:2, :] += jnp.sum(u * d1, axis=0, keepdims=True)
        dwb_ref[2:3, :] += jnp.sum(u * d, axis=0, keepdims=True)
        dwb_ref[3:4, :] += jnp.sum(d, axis=0, keepdims=True)

    blk = pl.BlockSpec((tk, CONV_TC), lambda j, k: (k, j))
    return pl.pallas_call(
        body, name=name, grid=(N // CONV_TC, nblk),
        in_specs=[pl.BlockSpec((tk, Kd), lambda j, k: (k, 0)), blk,
                  pl.BlockSpec((CONV_HALO, CONV_TC), lambda j, k: (jnp.minimum((k + 1) * hb, nblk * hb - 1), j)),
                  blk, pl.BlockSpec((3, CONV_TC), lambda j, k: (0, j))],
        out_specs=[pl.BlockSpec((Kd, CONV_TC), lambda j, k: (0, j)), blk, pl.BlockSpec((8, CONV_TC), lambda j, k: (0, j))],
        out_shape=[jax.ShapeDtypeStruct((Kd, N), F32), jax.ShapeDtypeStruct((S, N), BF16),
                   jax.ShapeDtypeStruct((8, N), F32)],
        compiler_params=_cp(("parallel", "arbitrary")),
    )(xn, duc, duc, u, cw)


def _t5_bucket(dist):
    max_exact = REL_BUCKETS // 2
    d = np.maximum(dist, 0)
    log_ratio = np.log(np.maximum(d, 1) / max_exact) / math.log(REL_MAX_DIST / max_exact)
    large = np.minimum(max_exact + (log_ratio * (REL_BUCKETS - max_exact)).astype(np.int64), REL_BUCKETS - 1)
    return np.where(d < max_exact, d, large).astype(np.int32)


def _bucket_tables():
    delta = BLK + np.arange(BLK)[:, None] - np.arange(2 * BLK)[None, :]
    return np.stack([_t5_bucket(delta * dil) for _, dil in B_GROUPS]).astype(np.int32)


def bias_build(rel_bias, buckets, *, name):
    def body(rel_ref, bk_ref, o_ref):
        g = pl.program_id(0)
        bk = bk_ref[0]
        for h in range(B_HEADS):
            acc = jnp.zeros((BLK, 2 * BLK), F32)
            for bb in range(REL_BUCKETS):
                acc = jnp.where(bk == bb, rel_ref[bb, g * B_HEADS + h], acc)
            o_ref[0, h] = acc

    return pl.pallas_call(
        body, name=name, grid=(N_GROUPS,),
        in_specs=[pl.BlockSpec(memory_space=pltpu.SMEM), pl.BlockSpec((1, BLK, 2 * BLK), lambda g: (g, 0, 0))],
        out_specs=pl.BlockSpec((1, B_HEADS, BLK, 2 * BLK), lambda g: (g, 0, 0, 0)),
        out_shape=jax.ShapeDtypeStruct((N_GROUPS, B_HEADS, BLK, 2 * BLK), F32),
        compiler_params=_cp(("arbitrary",)),
    )(rel_bias, buckets)


def bias_grad(dbias, buckets, *, name):
    def body(db_ref, bk_ref, o_ref):
        g = pl.program_id(0)

        @pl.when(g == 0)
        def _():
            o_ref[...] = jnp.zeros_like(o_ref)

        bk = bk_ref[0]
        rr = lax.broadcasted_iota(jnp.int32, (REL_BUCKETS, 128), 0)
        cc = lax.broadcasted_iota(jnp.int32, (REL_BUCKETS, 128), 1)
        acc = jnp.zeros((REL_BUCKETS, 128), F32)
        for h in range(B_HEADS):
            dbh = db_ref[0, h]
            for bb in range(REL_BUCKETS):
                part = jnp.sum(jnp.where(bk == bb, dbh, 0.0), axis=0, keepdims=True)
                s = jnp.sum(part, axis=1, keepdims=True)
                acc = acc + jnp.where((rr == bb) & (cc == g * B_HEADS + h), s, 0.0)
        o_ref[...] += acc

    return pl.pallas_call(
        body, name=name, grid=(N_GROUPS,),
        in_specs=[pl.BlockSpec((1, B_HEADS, BLK, 2 * BLK), lambda g: (g, 0, 0, 0)),
                  pl.BlockSpec((1, BLK, 2 * BLK), lambda g: (g, 0, 0))],
        out_specs=pl.BlockSpec((REL_BUCKETS, 128), lambda g: (0, 0)),
        out_shape=jax.ShapeDtypeStruct((REL_BUCKETS, 128), F32),
        compiler_params=_cp(("arbitrary",)),
    )(dbias, buckets)


HG = 4
GW = HG * B_DH


def _head_masks(dtype):
    lane = lax.broadcasted_iota(jnp.int32, (BLK, GW), 1)
    return [((lane >= h * B_DH) & (lane < (h + 1) * B_DH)).astype(dtype) for h in range(HG)]


def _band_masks():
    iq = lax.broadcasted_iota(jnp.int32, (BLK, BLK), 0)
    ik = lax.broadcasted_iota(jnp.int32, (BLK, BLK), 1)
    return iq <= ik, iq >= ik


def attn_fwd(qg, kg, vg, bias, g, dil, *, name):
    S = qg.size // 1024
    S2 = S // dil
    nb = S2 // BLK
    W = dil * 1024
    scale = B_DH ** -0.5

    def body(q_ref, kc_ref, kp_ref, vc_ref, vp_ref, b_ref, o_ref, lse_ref):
        has_prev = pl.program_id(1) > 0
        vp_m, vc_m = _band_masks()
        valid = jnp.concatenate([vp_m & has_prev, vc_m], axis=1)
        mb = _head_masks(BF16)
        mf = _head_masks(F32)
        lane = lax.broadcasted_iota(jnp.int32, (BLK, 128), 1)
        lse_acc = jnp.zeros((BLK, 128), F32)
        for hg in range(B_HEADS // HG):
            sl = slice(hg * GW, (hg + 1) * GW)
            q4 = q_ref[:, sl]
            kcat = jnp.concatenate([kp_ref[:, sl], kc_ref[:, sl]], axis=0)
            vcat = jnp.concatenate([vp_ref[:, sl], vc_ref[:, sl]], axis=0)
            s4 = _dot_nt(jnp.concatenate([q4 * mb[h] for h in range(HG)], axis=0), kcat)
            ps, rl = [], []
            for h in range(HG):
                hh = hg * HG + h
                s = jnp.where(valid, s4[h * BLK:(h + 1) * BLK] * scale + b_ref[0, hh], -jnp.inf)
                m = jnp.max(s, axis=-1, keepdims=True)
                p = jnp.exp(s - m)
                l = jnp.sum(p, axis=-1, keepdims=True)
                ps.append(p.astype(BF16))
                rl.append(1.0 / l)
                lse_acc = jnp.where(lane == hh, m + jnp.log(l), lse_acc)
            o4 = _dot(jnp.concatenate(ps, axis=0), vcat)
            acc = jnp.zeros((BLK, GW), F32)
            for h in range(HG):
                acc = acc + o4[h * BLK:(h + 1) * BLK] * (mf[h] * rl[h])
            o_ref[:, sl] = acc.astype(BF16)
        lse_ref[...] = lse_acc

    cur = pl.BlockSpec((BLK, 1024), lambda r, n: (n, r))
    prev = pl.BlockSpec((BLK, 1024), lambda r, n: (jnp.maximum(n - 1, 0), r))
    q2, k2, v2 = qg.reshape(S2, W), kg.reshape(S2, W), vg.reshape(S2, W)
    o, lse = pl.pallas_call(
        body, name=name, grid=(dil, nb),
        in_specs=[cur, cur, prev, cur, prev, pl.BlockSpec((1, B_HEADS, BLK, 2 * BLK), lambda r, n: (g, 0, 0, 0))],
        out_specs=[cur, pl.BlockSpec((BLK, 128), lambda r, n: (n, r))],
        out_shape=[jax.ShapeDtypeStruct((S2, W), BF16), jax.ShapeDtypeStruct((S2, dil * 128), F32)],
        compiler_params=_cp(("parallel", "arbitrary")),
    )(q2, k2, k2, v2, v2, bias)
    return o.reshape(S, 1024), lse.reshape(S, 128)


def attn_merge(os_, lses, *, name, tm=512):
    S = os_[0].shape[0]
    expand = np.zeros((128, 1024), np.float32)
    for h in range(B_HEADS):
        expand[h, h * B_DH:(h + 1) * B_DH] = 1.0
    expand = jnp.asarray(expand, BF16)

    def body(o0, o1, o2, l0, l1, l2, e_ref, out_ref, lse_ref):
        ls = [l0[...], l1[...], l2[...]]
        m = jnp.maximum(jnp.maximum(ls[0], ls[1]), ls[2])
        ws = [jnp.exp(l - m) for l in ls]
        tot = ws[0] + ws[1] + ws[2]
        lse_ref[...] = m + jnp.log(tot)
        acc = jnp.zeros((tm, 1024), F32)
        for w, o in zip(ws, (o0, o1, o2)):
            acc = acc + _dot_split(w / tot, e_ref[...]) * o[...].astype(F32)
        out_ref[...] = acc.astype(BF16)

    row = pl.BlockSpec((tm, 1024), lambda i: (i, 0))
    lrow = pl.BlockSpec((tm, 128), lambda i: (i, 0))
    return pl.pallas_call(
        body, name=name, grid=(S // tm,),
        in_specs=[row, row, row, lrow, lrow, lrow, pl.BlockSpec((128, 1024), lambda i: (0, 0))],
        out_specs=[row, lrow],
        out_shape=[jax.ShapeDtypeStruct((S, 1024), BF16), jax.ShapeDtypeStruct((S, 128), F32)],
        compiler_params=_cp(("parallel",)),
    )(*os_, *lses, expand)


def mm_dout(dx, w_t, att, *, name, tm=512):
    S = dx.shape[0]
    heads = np.zeros((1024, 128), np.float32)
    for h in range(B_HEADS):
        heads[h * B_DH:(h + 1) * B_DH, h] = 1.0

    def body(x_ref, w_ref, att_ref, e_ref, do_ref, d_ref):
        acc = _dot(x_ref[...].astype(BF16), w_ref[...])
        do_ref[...] = acc.astype(BF16)
        d_ref[...] = _dot_split(acc * att_ref[...].astype(F32), e_ref[...])

    row = pl.BlockSpec((tm, 1024), lambda i: (i, 0))
    return pl.pallas_call(
        body, name=name, grid=(S // tm,),
        in_specs=[row, pl.BlockSpec((1024, 1024), lambda i: (0, 0)), row, pl.BlockSpec((1024, 128), lambda i: (0, 0))],
        out_specs=[row, pl.BlockSpec((tm, 128), lambda i: (i, 0))],
        out_shape=[jax.ShapeDtypeStruct((S, 1024), BF16), jax.ShapeDtypeStruct((S, 128), F32)],
        compiler_params=_cp(("parallel",)),
    )(dx, w_t, att, jnp.asarray(heads, BF16))


def attn_bwd(qg, kg, vg, bias, dout, dsum, lse, g, dil, *, name):
    S = qg.size // 1024
    S2 = S // dil
    nb = S2 // BLK
    W = dil * 1024
    scale = B_DH ** -0.5

    def body(q_ref, kc_ref, kp_ref, vc_ref, vp_ref, b_ref, do_ref, dsum_ref, lse_ref,
             dq_ref, dk_ref, dv_ref, db_ref, ck_s, cv_s):
        n = pl.program_id(1)

        @pl.when((pl.program_id(0) == 0) & (n == 0))
        def _():
            db_ref[...] = jnp.zeros_like(db_ref)

        @pl.when(n == 0)
        def _():
            ck_s[...] = jnp.zeros_like(ck_s)
            cv_s[...] = jnp.zeros_like(cv_s)

        @pl.when(n == nb)
        def _():
            dk_ref[...] = ck_s[...].astype(BF16)
            dv_ref[...] = cv_s[...].astype(BF16)

        @pl.when(n < nb)
        def _():
            vp_m, vc_m = _band_masks()
            valid = jnp.concatenate([vp_m & (n > 0), vc_m], axis=1)
            mb = _head_masks(BF16)
            mf = _head_masks(F32)
            lse_blk = lse_ref[...]
            dsum_blk = dsum_ref[...]
            for hg in range(B_HEADS // HG):
                sl = slice(hg * GW, (hg + 1) * GW)
                kcat = jnp.concatenate([kp_ref[:, sl], kc_ref[:, sl]], axis=0)
                vcat = jnp.concatenate([vp_ref[:, sl], vc_ref[:, sl]], axis=0)
                dob = do_ref[:, sl]
                q4 = q_ref[:, sl]
                q4m = jnp.concatenate([q4 * mb[h] for h in range(HG)], axis=0)
                do4m = jnp.concatenate([dob * mb[h] for h in range(HG)], axis=0)
                s4 = _dot_nt(q4m, kcat)
                dp4 = _dot_nt(do4m, vcat)
                ps, dss = [], []
                for h in range(HG):
                    hh = hg * HG + h
                    rows = slice(h * BLK, (h + 1) * BLK)
                    s = jnp.where(valid, s4[rows] * scale + b_ref[0, hh] - lse_blk[:, hh:hh + 1], -jnp.inf)
                    p = jnp.exp(s)
                    ds = p * (dp4[rows] - dsum_blk[:, hh:hh + 1])
                    db_ref[hh] += ds
                    ps.append(p.astype(BF16))
                    dss.append(ds.astype(BF16))
                p4 = jnp.concatenate(ps, axis=0)
                ds4 = jnp.concatenate(dss, axis=0)
                dq4 = _dot(ds4, kcat)
                acc = jnp.zeros((BLK, GW), F32)
                for h in range(HG):
                    acc = acc + dq4[h * BLK:(h + 1) * BLK] * mf[h]
                dq_ref[:, sl] = (acc * scale).astype(BF16)
                dkc = _dot_tn(ds4, q4m) * scale
                dvc = _dot_tn(p4, do4m)
                dk_ref[:, sl] = (ck_s[:, sl] + dkc[0:BLK]).astype(BF16)
                dv_ref[:, sl] = (cv_s[:, sl] + dvc[0:BLK]).astype(BF16)
                ck_s[:, sl] = dkc[BLK:2 * BLK]
                cv_s[:, sl] = dvc[BLK:2 * BLK]

    last = nb - 1
    cur = lambda r, n: (jnp.minimum(n, last), r)
    prev = lambda r, n: (jnp.clip(n - 1, 0, last), r)
    row = lambda im: pl.BlockSpec((BLK, 1024), im)
    k2, v2 = kg.reshape(S2, W), vg.reshape(S2, W)
    dq, dk, dv, dbias = pl.pallas_call(
        body, name=name, grid=(dil, nb + 1),
        in_specs=[row(cur), row(cur), row(prev), row(cur), row(prev),
                  pl.BlockSpec((1, B_HEADS, BLK, 2 * BLK), lambda r, n: (g, 0, 0, 0)),
                  row(cur), pl.BlockSpec((BLK, 128), cur), pl.BlockSpec((BLK, 128), cur)],
        out_specs=[row(cur), row(prev), row(prev), pl.BlockSpec((B_HEADS, BLK, 2 * BLK), lambda r, n: (0, 0, 0))],
        out_shape=[jax.ShapeDtypeStruct((S2, W), BF16)] * 3 + [jax.ShapeDtypeStruct((B_HEADS, BLK, 2 * BLK), F32)],
        scratch_shapes=[pltpu.VMEM((BLK, 1024), F32), pltpu.VMEM((BLK, 1024), F32)],
        compiler_params=_cp(("arbitrary", "arbitrary")),
    )(qg.reshape(S2, W), k2, k2, v2, v2, bias, dout.reshape(S2, W), dsum.reshape(S2, dil * 128),
      lse.reshape(S2, dil * 128))
    return dq.reshape(S, 1024), dk.reshape(S, 1024), dv.reshape(S, 1024), dbias


def _slot(px, py, pc):
    return 4 * px + 2 * py + pc


def ag_weights(wb, ws):
    def body(wb_ref, ws_ref, ob_ref, os_ref, send_sems, recv_sems, local_sems):
        x, y, c = lax.axis_index("x"), lax.axis_index("y"), lax.axis_index("c")
        me, sibling = (x, y, c), (x, y, 1 - c)
        chips = [(1 - x, y), (x, 1 - y), (1 - x, 1 - y)]
        arrays = [(wb_ref, ob_ref), (ws_ref, os_ref)]

        def copy(a, k, block, to, from_input=False):
            src_in, out = arrays[a]
            dst = out.at[_slot(*block)]
            return pltpu.make_async_remote_copy(
                src_ref=src_in if from_input else dst, dst_ref=dst,
                send_sem=send_sems.at[7 * a + k], recv_sem=recv_sems.at[7 * a + k],
                device_id=to, device_id_type=MESH)

        mine = [pltpu.make_async_copy(arrays[a][0], arrays[a][1].at[_slot(*me)], local_sems.at[a]) for a in range(2)]
        for cp in mine:
            cp.start()
        first = []
        for a in range(2):
            first.append(copy(a, 0, me, sibling, True))
            first += [copy(a, 1 + j, me, (*chip, c), True) for j, chip in enumerate(chips)]
        for cp in first:
            cp.start()
        passed = []
        for a in range(2):
            for j, chip in enumerate(chips):
                copy(a, 1 + j, (*chip, c), me).wait_recv()
                fw = copy(a, 4 + j, (*chip, c), sibling)
                fw.start()
                passed.append(fw)
        for a in range(2):
            copy(a, 0, sibling, me).wait_recv()
            for j, chip in enumerate(chips):
                copy(a, 4 + j, (*chip, 1 - c), me).wait_recv()
        for cp in first + passed:
            cp.wait_send()
        for cp in mine:
            cp.wait()

    any_spec = pl.BlockSpec(memory_space=pl.ANY)
    return pl.pallas_call(
        body, name="ag_weights",
        in_specs=[any_spec, any_spec], out_specs=[any_spec, any_spec],
        out_shape=[jax.ShapeDtypeStruct((N_DEV,) + wb.shape, wb.dtype), jax.ShapeDtypeStruct((N_DEV,) + ws.shape, ws.dtype)],
        scratch_shapes=[pltpu.SemaphoreType.DMA((14,)), pltpu.SemaphoreType.DMA((14,)), pltpu.SemaphoreType.DMA((2,))],
    )(wb, ws)


def rs_sibling(gpack, spack):
    def body(g_ref, s_ref, rb_ref, sa_ref, send_sems, recv_sems, local_sem):
        x, y, c = lax.axis_index("x"), lax.axis_index("y"), lax.axis_index("c")
        my = _slot(x, y, c)
        mine = pltpu.make_async_copy(s_ref, sa_ref.at[my], local_sem)
        mine.start()
        sends, recvs = [], []
        for j in range(4):
            both = dict(dst_ref=rb_ref.at[j], send_sem=send_sems.at[j], recv_sem=recv_sems.at[j],
                        device_id=(x, y, 1 - c), device_id_type=MESH)
            sends.append(pltpu.make_async_remote_copy(src_ref=g_ref.at[2 * j + 1 - c], **both))
            recvs.append(sends[-1])
        for k in range(1, N_DEV):
            peer = (1 - x if k & 4 else x, 1 - y if k & 2 else y, 1 - c if k & 1 else c)
            sems = dict(send_sem=send_sems.at[3 + k], recv_sem=recv_sems.at[3 + k], device_id=peer, device_id_type=MESH)
            sends.append(pltpu.make_async_remote_copy(src_ref=s_ref, dst_ref=sa_ref.at[my], **sems))
            recvs.append(pltpu.make_async_remote_copy(src_ref=s_ref, dst_ref=sa_ref.at[_slot(*peer)], **sems))
        for cp in sends:
            cp.start()
        for cp in recvs:
            cp.wait_recv()
        for cp in sends:
            cp.wait_send()
        mine.wait()

    any_spec = pl.BlockSpec(memory_space=pl.ANY)
    return pl.pallas_call(
        body, name="rs_sibling",
        in_specs=[any_spec, any_spec], out_specs=[any_spec, any_spec],
        out_shape=[jax.ShapeDtypeStruct((4,) + gpack.shape[1:], gpack.dtype),
                   jax.ShapeDtypeStruct((N_DEV,) + spack.shape, spack.dtype)],
        scratch_shapes=[pltpu.SemaphoreType.DMA((11,)), pltpu.SemaphoreType.DMA((11,)), pltpu.SemaphoreType.DMA],
    )(gpack, spack)


def pair_add(a, b, *, name, tr):
    R = a.shape[1]

    def body(a_ref, b_ref, o_ref):
        o_ref[...] = (a_ref[...].astype(F32) + b_ref[...].astype(F32)).astype(BF16)

    blk = pl.BlockSpec((4, tr, 1024), lambda i: (0, i, 0))
    return pl.pallas_call(
        body, name=name, grid=(R // tr,), in_specs=[blk, blk], out_specs=blk,
        out_shape=jax.ShapeDtypeStruct(a.shape, BF16), compiler_params=_cp(("parallel",)),
    )(a, b)


def rs_chips(part):
    def body(p_ref, rb_ref, send_sems, recv_sems, local_sem):
        x, y, c = lax.axis_index("x"), lax.axis_index("y"), lax.axis_index("c")
        jm = 2 * x + y
        mine = pltpu.make_async_copy(p_ref.at[jm], rb_ref.at[jm], local_sem)
        mine.start()
        sends, recvs = [], []
        for k in range(1, 4):
            px, py = (1 - x if k & 2 else x), (1 - y if k & 1 else y)
            sems = dict(send_sem=send_sems.at[k - 1], recv_sem=recv_sems.at[k - 1], device_id=(px, py, c), device_id_type=MESH)
            sends.append(pltpu.make_async_remote_copy(src_ref=p_ref.at[2 * px + py], dst_ref=rb_ref.at[jm], **sems))
            recvs.append(pltpu.make_async_remote_copy(src_ref=p_ref.at[jm], dst_ref=rb_ref.at[2 * px + py], **sems))
        for cp in sends:
            cp.start()
        for cp in recvs:
            cp.wait_recv()
        for cp in sends:
            cp.wait_send()
        mine.wait()

    any_spec = pl.BlockSpec(memory_space=pl.ANY)
    return pl.pallas_call(
        body, name="rs_chips", in_specs=[any_spec], out_specs=any_spec,
        out_shape=jax.ShapeDtypeStruct(part.shape, part.dtype),
        scratch_shapes=[pltpu.SemaphoreType.DMA((3,)), pltpu.SemaphoreType.DMA((3,)), pltpu.SemaphoreType.DMA],
    )(part)


def reduce_adam(parts, w, m, v, *, name, tr):
    R = w.shape[0]
    n_parts = parts.shape[0]
    assert R % tr == 0
    c1 = 1.0 - ADAM_B1 ** ADAM_STEP
    c2 = 1.0 - ADAM_B2 ** ADAM_STEP

    def body(p_ref, w_ref, m_ref, v_ref, g_ref, d_ref, mo_ref, vo_ref):
        g = p_ref[0].astype(F32)
        for i in range(1, n_parts):
            g = g + p_ref[i].astype(F32)
        mn = ADAM_B1 * m_ref[...] + (1.0 - ADAM_B1) * g
        vn = ADAM_B2 * v_ref[...] + (1.0 - ADAM_B2) * (g * g)
        g_ref[...] = g
        mo_ref[...] = mn
        vo_ref[...] = vn
        d_ref[...] = -ADAM_LR * ((mn / c1) / (jnp.sqrt(vn / c2) + ADAM_EPS) + ADAM_WD * w_ref[...])

    row = pl.BlockSpec((tr, 1024), lambda i: (i, 0))
    return pl.pallas_call(
        body, name=name, grid=(R // tr,),
        in_specs=[pl.BlockSpec((n_parts, tr, 1024), lambda i: (0, i, 0)), row, row, row],
        out_specs=[row] * 4,
        out_shape=[jax.ShapeDtypeStruct((R, 1024), F32)] * 4,
        compiler_params=_cp(("parallel",)),
    )(parts, w, m, v)


BIG = (("a_w_in", 385, 400), ("a_w_out", 128, 128), ("w_kv", 768, 768), ("b_w_q", 384, 384),
       ("b_w_out", 128, 128), ("f_w_up", 1408, 1408), ("f_w_down", 704, 704))
SMALL_SHARDED = (("a_norm_g", 128), ("a_hnorm_g", 128), ("f_conv_w", 4224))
SMALL_ROWS = 48
PACK_ROWS = sum(b[2] for b in BIG) + SMALL_ROWS
REPL = (("kv_norm_g", 1024, 1), ("b_norm_g", 1024, 1), ("f_norm_g", 2048, 2), ("f_conv_b", 11264, 11),
        ("final_norm_g", 1024, 1), ("rel_bias", 1536, 2), ("a_b_if", 8, 1))
REPL_ROWS = 24
LOSS_ROW = 19


def _rows(a, rows, padded):
    a = a.reshape(rows, 1024)
    return a if padded == rows else jnp.pad(a, ((0, padded - rows), (0, 0)))


def pack_shards(t, dtype, with_small):
    parts = [_rows(t[n].astype(dtype), r, p) for n, r, p in BIG]
    if with_small:
        flat = jnp.concatenate([t[n].astype(dtype).reshape(-1) for n, _ in SMALL_SHARDED])
        parts.append(jnp.pad(flat, (0, SMALL_ROWS * 1024 - flat.shape[0])).reshape(SMALL_ROWS, 1024))
    return jnp.concatenate(parts, axis=0)


def unpack_shards(pack, shapes):
    out = {}
    r0 = 0
    for n, r, p in BIG:
        out[n] = pack[r0:r0 + r].reshape(shapes[n])
        r0 += p
    flat = pack[r0:r0 + SMALL_ROWS].reshape(-1)
    e0 = 0
    for n, e in SMALL_SHARDED:
        out[n] = flat[e0:e0 + e].reshape(shapes[n])
        e0 += e
    return out


def pack_repl(t):
    parts = []
    for n, e, r in REPL:
        parts.append(jnp.pad(t[n].astype(F32).reshape(-1), (0, r * 1024 - e)))
    rows = sum(r for _, _, r in REPL)
    parts.append(jnp.zeros(((REPL_ROWS - rows) * 1024,), F32))
    return jnp.concatenate(parts).reshape(REPL_ROWS, 1024)


def unpack_repl(pack, shapes):
    out = {}
    r0 = 0
    for n, e, r in REPL:
        out[n] = pack[r0:r0 + r].reshape(-1)[:e].reshape(shapes[n])
        r0 += r
    return out


def ff_blocks(a):
    b = [a[..., i * CONV_TC:(i + 1) * CONV_TC] for i in range(4)]
    return jnp.concatenate([b[0], b[2], b[1], b[3]], axis=-1)


def split_cols(full, n):
    lead = full.shape[:-1]
    return jnp.moveaxis(full.reshape(lead + (N_DEV, n)), -2, 0)


def join_cols(parts):
    t = jnp.moveaxis(parts, 0, -2)
    return t.reshape(t.shape[:-2] + (t.shape[-2] * t.shape[-1],))


def kernel(x, a_norm_g, a_w_in, a_b_if, a_hnorm_g, a_w_out, kv_norm_g, w_kv, b_norm_g, b_w_q, b_w_out, rel_bias, f_norm_g, f_w_up, f_conv_w, f_conv_b, f_w_down, final_norm_g, loss_target, m_a_norm_g, m_a_w_in, m_a_b_if, m_a_hnorm_g, m_a_w_out, m_kv_norm_g, m_w_kv, m_b_norm_g, m_b_w_q, m_b_w_out, m_rel_bias, m_f_norm_g, m_f_w_up, m_f_conv_w, m_f_conv_b, m_f_w_down, m_final_norm_g, v_a_norm_g, v_a_w_in, v_a_b_if, v_a_hnorm_g, v_a_w_out, v_kv_norm_g, v_w_kv, v_b_norm_g, v_b_w_q, v_b_w_out, v_rel_bias, v_f_norm_g, v_f_w_up, v_f_conv_w, v_f_conv_b, v_f_w_down, v_final_norm_g):
    names = ["a_norm_g", "a_w_in", "a_b_if", "a_hnorm_g", "a_w_out", "kv_norm_g", "w_kv", "b_norm_g", "b_w_q", "b_w_out",
             "rel_bias", "f_norm_g", "f_w_up", "f_conv_w", "f_conv_b", "f_w_down", "final_norm_g"]
    w = dict(zip(names, (a_norm_g, a_w_in, a_b_if, a_hnorm_g, a_w_out, kv_norm_g, w_kv, b_norm_g, b_w_q, b_w_out,
                         rel_bias, f_norm_g, f_w_up, f_conv_w, f_conv_b, f_w_down, final_norm_g)))
    mom = dict(zip(names, (m_a_norm_g, m_a_w_in, m_a_b_if, m_a_hnorm_g, m_a_w_out, m_kv_norm_g, m_w_kv, m_b_norm_g, m_b_w_q,
                           m_b_w_out, m_rel_bias, m_f_norm_g, m_f_w_up, m_f_conv_w, m_f_conv_b, m_f_w_down, m_final_norm_g)))
    vel = dict(zip(names, (v_a_norm_g, v_a_w_in, v_a_b_if, v_a_hnorm_g, v_a_w_out, v_kv_norm_g, v_w_kv, v_b_norm_g, v_b_w_q,
                           v_b_w_out, v_rel_bias, v_f_norm_g, v_f_w_up, v_f_conv_w, v_f_conv_b, v_f_w_down, v_final_norm_g)))
    shapes = {n: w[n].shape for n in names}
    S = x.shape[1]
    assert x.shape[0] == 1 and S % (16 * BLK) == 0 and S % 1024 == 0
    X0 = x.reshape(S, D)
    target = loss_target.reshape(S, D)

    wb_all, ws_all = ag_weights(pack_shards(w, BF16, False),
                                pack_shards(w, F32, True)[PACK_ROWS - SMALL_ROWS:])
    seg = {}
    r0 = 0
    for n, r, p in BIG:
        seg[n] = wb_all[:, r0:r0 + r]
        r0 += p
    W_in = join_cols(seg["a_w_in"].reshape(N_DEV, D, 385))
    W_in = jnp.concatenate([jnp.pad(W_in[:, :3076], ((0, 0), (0, 124))),
                            jnp.pad(W_in[:, 3076:3080], ((0, 0), (0, 124)))], axis=1)
    W_out = seg["a_w_out"].reshape(1024, D)
    W_kv = join_cols(seg["w_kv"].reshape(N_DEV, D, 768))
    W_q = join_cols(seg["b_w_q"].reshape(N_DEV, D, 384))
    W_bout = seg["b_w_out"].reshape(1024, D)
    W_up = join_cols(seg["f_w_up"].reshape(N_DEV, 2, D, 704))
    W_down = jnp.moveaxis(seg["f_w_down"].reshape(N_DEV, 2, 352, D), 0, 1).reshape(2, D_FF, D)
    sflat = ws_all.reshape(N_DEV, SMALL_ROWS * 1024)
    g_a = sflat[:, 0:128].reshape(1, D)
    g_h = jnp.moveaxis(sflat[:, 128:256].reshape(N_DEV, A_HEADS, 32), 0, 1).reshape(1, A_HEADS * A_V)
    conv_w = ff_blocks(join_cols(sflat[:, 256:256 + 4224].reshape(N_DEV, 2, 3, 704)))
    conv_b = ff_blocks(f_conv_b)
    W_up = ff_blocks(W_up)
    bi = jnp.pad(a_b_if[:, :A_HEADS], ((0, 0), (0, 128 - A_HEADS)))
    bfg = jnp.pad(a_b_if[:, A_HEADS:], ((0, 0), (0, 128 - A_HEADS)))
    buckets = jnp.asarray(_bucket_tables())

    (xn_a,) = rms_fwd(X0, [g_a], name="rms_a")
    z = mm(xn_a, W_in, name="mm_a_in", out_dtype=F32, tn=1664)
    h, cst, nst = mlstm_fwd(z, bi, bfg, name="mlstm_fwd")
    X1, hg = ao_fwd(h, z, g_h, W_out, X0, name="ao_fwd")

    def ffn_fwd(X, l, tag):
        (xn,) = rms_fwd(X, [f_norm_g[l:l + 1]], name="rms_f" + tag)
        u, c, a = mm_up_conv(xn, W_up[l], conv_w[l], conv_b[l:l + 1], name="mm_up_conv" + tag)
        Xn = mm(a, W_down[l], name="mm_down" + tag, out_dtype=F32, tn=1024, tm=512, res=X)
        return Xn, (xn, u, c, a)

    X2, sav0 = ffn_fwd(X1, 0, "0")

    xkn, xbn = rms_fwd(X2, [kv_norm_g.reshape(1, D), b_norm_g], name="rms_kv_b")
    bias = bias_build(rel_bias, buckets, name="bias_build")
    col = lambda wmat, i: wmat[:, i * 1024:(i + 1) * 1024]
    qs, kk, vv, og, lg = [], [], [], [], []
    for g, (_, dil) in enumerate(B_GROUPS):
        xb_v, xk_v = xbn.reshape(S // dil, dil * D), xkn.reshape(S // dil, dil * D)
        qs.append(mm_view(xb_v, col(W_q, g), name="mm_q%d" % g))
        kk.append(mm_view(xk_v, col(W_kv, g), name="mm_k%d" % g))
        vv.append(mm_view(xk_v, col(W_kv, N_GROUPS + g), name="mm_v%d" % g))
        o_, l_ = attn_fwd(qs[g], kk[g], vv[g], bias, g, dil, name="attn_fwd%d" % g)
        og.append(o_)
        lg.append(l_)
    att, lse = attn_merge(og, lg, name="attn_merge")
    X3 = mm(att, W_bout, name="mm_b_out", out_dtype=F32, tn=1024, res=X2)
    X4, sav1 = ffn_fwd(X3, 1, "1")

    dX4, d_final_g, loss_part = loss_head(X4, target, final_norm_g.reshape(1, D), name="loss_head")

    def ffn_bwd(X, dXn, l, sav, tag):
        xn, u, c, a = sav
        dW_down = mm_tn(a, dXn, name="tn_down" + tag, tn=1024, tk=512)
        duc = mm_da_act(dXn, W_down[l].T, c, name="mm_da_act" + tag)
        dW_up, du, dwb = tn_up_conv(xn, duc, u, conv_w[l], name="tn_up_conv" + tag)
        dX, (dg,) = mm_rms_bwd([du], [W_up[l].T], f_norm_g[l:l + 1], X, dXn, name="mm_rms_bwd_f" + tag)
        return dX, dW_down, dW_up, dwb, dg

    dX3, dWd1, dWu1, dwb1, dgf1 = ffn_bwd(X3, dX4, 1, sav1, "1")

    dW_bout = mm_tn(att, dX3, name="tn_b_out", tn=1024)
    dout, dsum = mm_dout(dX3, W_bout.T, att, name="mm_dout")
    dqs, dks, dvs, dbias = [], [], [], []
    for g, (_, dil) in enumerate(B_GROUPS):
        dq_, dk_, dv_, db_ = attn_bwd(qs[g], kk[g], vv[g], bias, dout, dsum, lse, g, dil, name="attn_bwd%d" % g)
        dqs.append(dq_)
        dks.append(dk_)
        dvs.append(dv_)
        dbias.append(db_)
    d_rel = bias_grad(jnp.stack(dbias), buckets, name="bias_grad")[:, :N_GROUPS * B_HEADS]
    dW_q = jnp.concatenate([mm_tn(xbn, d_, name="tn_q%d" % g, tn=1024) for g, d_ in enumerate(dqs)], axis=1)
    dW_kv = jnp.concatenate([mm_tn(xkn, d_, name="tn_kv%d" % i, tn=1024) for i, d_ in enumerate(dks + dvs)], axis=1)
    W_qT, W_kvT = W_q.T, W_kv.T
    rows = lambda wmat, i: wmat[i * 1024:(i + 1) * 1024]
    dxn_kv = mm_sum(dks + dvs, [rows(W_kvT, i) for i in range(2 * N_GROUPS)], name="mm_dxn_kv")
    dX2, (dg_b, dg_kv) = mm_rms_bwd(dqs, [rows(W_qT, g) for g in range(N_GROUPS)], b_norm_g, X2, dX3,
                                    extra=[(dxn_kv, kv_norm_g.reshape(1, D))], name="mm_rms_bwd_b_kv")

    dX1, dWd0, dWu0, dwb0, dgf0 = ffn_bwd(X1, dX2, 0, sav0, "0")

    dW_out = mm_tn(hg, dX1, name="tn_a_out", tn=1024)
    dh, dzo, dgh = ao_bwd(dX1, W_out.T, h, z, g_h, name="ao_bwd")
    dz, db_if = mlstm_bwd(z, bi, bfg, cst, nst, dh, dzo, name="mlstm_bwd")
    dW_in = mm_tn(xn_a, dz, name="tn_a_in", tn=1664)
    dW_in = jnp.concatenate([dW_in[:, :3076], dW_in[:, Z_GF:Z_GF + 4]], axis=1)
    dX0, (dg_a,) = mm_rms_bwd([dz], [W_in.T], g_a, X0, dX1, name="mm_rms_bwd_a")

    dWu = ff_blocks(jnp.stack([dWu0, dWu1]))
    dWd = jnp.stack([dWd0, dWd1])
    dwb = ff_blocks(jnp.stack([dwb0, dwb1]))
    slots = [
        jnp.pad(split_cols(dW_in, 385).reshape(N_DEV, 385, 1024), ((0, 0), (0, 15), (0, 0))),
        dW_out.reshape(N_DEV, 128, 1024),
        split_cols(dW_kv, 768).reshape(N_DEV, 768, 1024),
        split_cols(dW_q, 384).reshape(N_DEV, 384, 1024),
        dW_bout.reshape(N_DEV, 128, 1024),
        split_cols(dWu, 704).reshape(N_DEV, 1408, 1024),
        jnp.moveaxis(dWd.reshape(2, N_DEV, 352, D), 1, 0).reshape(N_DEV, 704, 1024),
    ]
    small = jnp.concatenate([
        dg_a.reshape(N_DEV, 128),
        split_cols(dgh.reshape(A_HEADS, A_V), 32).reshape(N_DEV, 128),
        split_cols(dwb[:, 0:3], 704).reshape(N_DEV, 4224)], axis=1)
    slots.append(jnp.pad(small, ((0, 0), (0, SMALL_ROWS * 1024 - small.shape[1]))).reshape(N_DEV, SMALL_ROWS, 1024))
    gpack = jnp.concatenate([t.astype(BF16) for t in slots], axis=1)
    repl_g = {"kv_norm_g": dg_kv, "b_norm_g": dg_b, "f_norm_g": jnp.concatenate([dgf0, dgf1]),
              "f_conv_b": dwb[:, 3], "final_norm_g": d_final_g, "rel_bias": d_rel,
              "a_b_if": jnp.concatenate([db_if[0, :A_HEADS], db_if[1, :A_HEADS]])}
    spack = pack_repl(repl_g)
    spack = spack.at[LOSS_ROW, 0].set(loss_part[0, 0])

    from_sibling, sparts = rs_sibling(gpack, spack)
    own = lax.dynamic_index_in_dim(gpack.reshape(4, 2, PACK_ROWS, 1024), lax.axis_index("c"), axis=1, keepdims=False)
    parts = rs_chips(pair_add(own, from_sibling, name="rs_pair_add", tr=PACK_ROWS // 8))
    gb, db, mb, vb = reduce_adam(parts, pack_shards(w, F32, True), pack_shards(mom, F32, True),
                                 pack_shards(vel, F32, True), name="reduce_adam_big", tr=PACK_ROWS // 8)
    gs, ds, ms, vs = reduce_adam(sparts, pack_repl(w), pack_repl(mom), pack_repl(vel), name="reduce_adam_small", tr=REPL_ROWS)
    loss = gs[LOSS_ROW, 0]

    def collect(big, sm):
        t = unpack_shards(big, shapes)
        t.update(unpack_repl(sm, shapes))
        return [t[n] for n in names]

    return (loss, dX0.reshape(1, S, D), *collect(gb, gs), *collect(db, ds), *collect(mb, ms), *collect(vb, vs))
```

```python
import functools
import math

import numpy as np
import jax
import jax.numpy as jnp
from jax import lax
from jax.experimental import pallas as pl
from jax.experimental.pallas import tpu as pltpu

F32 = jnp.float32
BF16 = jnp.bfloat16
HIGHEST = lax.Precision.HIGHEST
MESH = pl.DeviceIdType.MESH

D = 1024
A_HEADS = 4
A_QK = 128
A_V = 256
SOFTCAP = 15.0
N_GROUPS = 3
B_GROUPS = ((128, 1), (512, 4), (2048, 16))
B_HEADS = 16
B_DH = 64
BLK = 128
REL_BUCKETS = 32
REL_MAX_DIST = 2048
D_FF = 2816
EPS = 1e-6
ADAM_LR = 0.001
ADAM_B1 = 0.9
ADAM_B2 = 0.999
ADAM_EPS = 1e-08
ADAM_WD = 0.01
ADAM_STEP = 10

N_DEV = 8
V7X_VMEM_BYTES = 64 * 1024 * 1024
VMEM_LIMIT = V7X_VMEM_BYTES - 8 * 1024 * 1024
MLSTM_CHUNK = 256
Z_W = 3328
Z_GI = 3072
Z_GF = 3200


def _cp(sem):
    return pltpu.CompilerParams(dimension_semantics=sem, vmem_limit_bytes=VMEM_LIMIT)


def _dot(a, b, **kw):
    return jnp.dot(a, b, preferred_element_type=F32, **kw)


def _dot_nt(a, b):
    return lax.dot_general(a, b, (((1,), (1,)), ((), ())), preferred_element_type=F32)


def _dot_tn(a, b):
    return lax.dot_general(a, b, (((0,), (0,)), ((), ())), preferred_element_type=F32)


def _dot_split(a, b01):
    hi = a.astype(BF16)
    lo = (a - hi.astype(F32)).astype(BF16)
    return _dot(hi, b01) + _dot(lo, b01)


def mm(a, b, *, name, out_dtype, tn, tm=1024, res=None, norm_gains=()):
    M, K = a.shape
    N = b.shape[1]
    ng = len(norm_gains)
    assert M % tm == 0 and N % tn == 0 and b.shape[0] == K and (ng == 0 or tn == N)

    def body(a_ref, b_ref, *rest):
        nres = 0 if res is None else 1
        g_refs = rest[nres:nres + ng]
        o_ref = rest[nres + ng]
        acc = _dot(a_ref[...].astype(BF16), b_ref[...])
        if res is not None:
            acc = acc + rest[0][...]
        o_ref[...] = acc.astype(out_dtype)
        if ng:
            y = acc * lax.rsqrt(jnp.mean(acc * acc, axis=-1, keepdims=True) + EPS)
            for i in range(ng):
                rest[nres + ng + 1 + i][...] = (y * g_refs[i][...]).astype(BF16)

    blk = pl.BlockSpec((tm, tn), lambda j, i: (i, j))
    in_specs = [pl.BlockSpec((tm, K), lambda j, i: (i, 0)), pl.BlockSpec((K, tn), lambda j, i: (0, j))]
    args = [a, b]
    if res is not None:
        in_specs.append(blk)
        args.append(res)
    in_specs += [pl.BlockSpec((1, tn), lambda j, i: (0, j))] * ng
    outs = pl.pallas_call(
        body, name=name, grid=(N // tn, M // tm), in_specs=in_specs,
        out_specs=[blk] * (1 + ng),
        out_shape=[jax.ShapeDtypeStruct((M, N), out_dtype)] + [jax.ShapeDtypeStruct((M, N), BF16)] * ng,
        compiler_params=_cp(("parallel", "parallel")),
    )(*args, *norm_gains)
    return outs if ng else outs[0]


def mm_loss(a, b, res, target, g, *, name, tm=512):
    S, K = a.shape

    def body(a_ref, b_ref, r_ref, t_ref, g_ref, dx_ref, dg_ref, loss_ref):
        @pl.when(pl.program_id(0) == 0)
        def _():
            dg_ref[...] = jnp.zeros_like(dg_ref)
            loss_ref[...] = jnp.zeros_like(loss_ref)

        xf = r_ref[...] + _dot(a_ref[...], b_ref[...])
        gg = g_ref[...]
        r = lax.rsqrt(jnp.mean(xf * xf, axis=-1, keepdims=True) + EPS)
        xhat = xf * r
        e = xhat * gg - t_ref[...]
        loss_ref[...] += 0.5 * jnp.sum(jnp.mean(e * e, axis=-1, keepdims=True), axis=0, keepdims=True)
        dy = e * (1.0 / D)
        dg_ref[...] += jnp.sum(dy * xhat, axis=0, keepdims=True)
        dyg = dy * gg
        dx_ref[...] = r * (dyg - xhat * jnp.mean(dyg * xhat, axis=-1, keepdims=True))

    row = pl.BlockSpec((tm, D), lambda i: (i, 0))
    vec = pl.BlockSpec((1, D), lambda i: (0, 0))
    return pl.pallas_call(
        body, name=name, grid=(S // tm,),
        in_specs=[pl.BlockSpec((tm, K), lambda i: (i, 0)), pl.BlockSpec((K, D), lambda i: (0, 0)), row, row, vec],
        out_specs=[row, vec, pl.BlockSpec((1, 128), lambda i: (0, 0))],
        out_shape=[jax.ShapeDtypeStruct((S, D), F32), jax.ShapeDtypeStruct((1, D), F32),
                   jax.ShapeDtypeStruct((1, 128), F32)],
        compiler_params=_cp(("arbitrary",)),
    )(a, b, res, target, g)


def mm_sum(a_list, b_list, *, name, tm=512):
    M, K = a_list[0].shape
    N = b_list[0].shape[1]
    n = len(a_list)
    assert M % tm == 0

    def body(*refs):
        o_ref = refs[-1]
        acc = _dot(refs[0][...], refs[n][...])
        for i in range(1, n):
            acc = acc + _dot(refs[i][...], refs[n + i][...])
        o_ref[...] = acc

    return pl.pallas_call(
        body, name=name, grid=(M // tm,),
        in_specs=[pl.BlockSpec((tm, K), lambda i: (i, 0))] * n + [pl.BlockSpec((K, N), lambda i: (0, 0))] * n,
        out_specs=pl.BlockSpec((tm, N), lambda i: (i, 0)),
        out_shape=jax.ShapeDtypeStruct((M, N), F32),
        compiler_params=_cp(("parallel",)),
    )(*a_list, *b_list)


def mm_view(a_view, w, *, name, tm=1024):
    S2, W = a_view.shape
    tm = min(tm, S2)

    def body(a_ref, w_ref, o_ref):
        o_ref[...] = _dot(a_ref[...], w_ref[...]).astype(BF16)

    blk = pl.BlockSpec((tm, 1024), lambda r, i: (i, r))
    return pl.pallas_call(
        body, name=name, grid=(W // 1024, S2 // tm),
        in_specs=[blk, pl.BlockSpec((1024, 1024), lambda r, i: (0, 0))], out_specs=blk,
        out_shape=jax.ShapeDtypeStruct((S2, W), BF16), compiler_params=_cp(("parallel", "parallel")),
    )(a_view, w)


def mm_tn(a, b, *, name, tn, tk=1024):
    S, Kd = a.shape
    N = b.shape[1]
    assert S % tk == 0 and N % tn == 0 and b.shape[0] == S

    def body(a_ref, b_ref, o_ref):
        @pl.when(pl.program_id(1) == 0)
        def _():
            o_ref[...] = jnp.zeros_like(o_ref)

        o_ref[...] += _dot_tn(a_ref[...].astype(BF16), b_ref[...].astype(BF16))

    return pl.pallas_call(
        body, name=name, grid=(N // tn, S // tk),
        in_specs=[pl.BlockSpec((tk, Kd), lambda j, k: (k, 0)), pl.BlockSpec((tk, tn), lambda j, k: (k, j))],
        out_specs=pl.BlockSpec((Kd, tn), lambda j, k: (0, j)),
        out_shape=jax.ShapeDtypeStruct((Kd, N), F32),
        compiler_params=_cp(("parallel", "arbitrary")),
    )(a, b)


def rms_fwd(x, gains, *, name, tm=1024):
    S = x.shape[0]
    n = len(gains)

    def body(x_ref, *rest):
        xf = x_ref[...]
        y = xf * lax.rsqrt(jnp.mean(xf * xf, axis=-1, keepdims=True) + EPS)
        for i in range(n):
            rest[n + i][...] = (y * rest[i][...]).astype(BF16)

    return pl.pallas_call(
        body, name=name, grid=(S // tm,),
        in_specs=[pl.BlockSpec((tm, D), lambda i: (i, 0))] + [pl.BlockSpec((1, D), lambda i: (0, 0))] * n,
        out_specs=[pl.BlockSpec((tm, D), lambda i: (i, 0))] * n,
        out_shape=[jax.ShapeDtypeStruct((S, D), BF16)] * n,
        compiler_params=_cp(("parallel",)),
    )(x, *gains)


def mm_rms_bwd(a_list, b_list, g, x, dres, extra=(), *, name, tm=512):
    S = x.shape[0]
    n = len(a_list)
    ne = len(extra)

    def body(*refs):
        a_refs, b_refs = refs[:n], refs[n:2 * n]
        g_ref, x_ref, dres_ref = refs[2 * n:2 * n + 3]
        e_refs = refs[2 * n + 3:2 * n + 3 + 2 * ne]
        dx_ref = refs[2 * n + 3 + 2 * ne]
        dg_refs = refs[2 * n + 4 + 2 * ne:]

        @pl.when(pl.program_id(0) == 0)
        def _():
            for r in dg_refs:
                r[...] = jnp.zeros_like(r)

        acc = _dot(a_refs[0][...], b_refs[0][...])
        for i in range(1, n):
            acc = acc + _dot(a_refs[i][...], b_refs[i][...])
        xf = x_ref[...]
        r = lax.rsqrt(jnp.mean(xf * xf, axis=-1, keepdims=True) + EPS)
        xhat = xf * r
        total = dres_ref[...]
        branches = [(acc, g_ref[...])] + [(e_refs[2 * i][...], e_refs[2 * i + 1][...]) for i in range(ne)]
        for i, (dy, gg) in enumerate(branches):
            dg_refs[i][...] += jnp.sum(dy * xhat, axis=0, keepdims=True)
            dyg = dy * gg
            total = total + r * (dyg - xhat * jnp.mean(dyg * xhat, axis=-1, keepdims=True))
        dx_ref[...] = total

    row = pl.BlockSpec((tm, D), lambda i: (i, 0))
    vec = pl.BlockSpec((1, D), lambda i: (0, 0))
    in_specs = ([pl.BlockSpec((tm, a.shape[1]), lambda i: (i, 0)) for a in a_list]
                + [pl.BlockSpec(b.shape, lambda i: (0, 0)) for b in b_list] + [vec, row, row])
    args = list(a_list) + list(b_list) + [g, x, dres]
    for dxn_e, g_e in extra:
        in_specs += [row, vec]
        args += [dxn_e, g_e]
    outs = pl.pallas_call(
        body, name=name, grid=(S // tm,), in_specs=in_specs,
        out_specs=[row] + [vec] * (1 + ne),
        out_shape=[jax.ShapeDtypeStruct((S, D), F32)] + [jax.ShapeDtypeStruct((1, D), F32)] * (1 + ne),
        compiler_params=_cp(("arbitrary",)),
    )(*args)
    return outs[0], outs[1:]


def _sigmoid(x):
    return 1.0 / (1.0 + jnp.exp(-x))


def _gates(z_ref, bi_ref, bf_ref):
    li = SOFTCAP * jnp.tanh((z_ref[:, Z_GI:Z_GI + 128] + bi_ref[...]) * (1.0 / SOFTCAP))
    scf = SOFTCAP * jnp.tanh((z_ref[:, Z_GF:Z_GF + 128] + bf_ref[...]) * (1.0 / SOFTCAP))
    lf = jnp.minimum(scf, 0.0) - jnp.log(1.0 + jnp.exp(-jnp.abs(scf)))
    return li, scf, lf


def _tri(L, lower):
    r = lax.broadcasted_iota(jnp.int32, (L, L), 0)
    c = lax.broadcasted_iota(jnp.int32, (L, L), 1)
    return (r >= c) if lower else (r <= c)


def mlstm_fwd(z, bi, bf, *, name):
    S = z.shape[0]
    L = MLSTM_CHUNK
    NC = S // L
    scale = A_QK ** -0.5

    def body(z_ref, bi_ref, bf_ref, h_ref, cst_ref, nst_ref, C_s, n_s):
        @pl.when(pl.program_id(0) == 0)
        def _():
            C_s[...] = jnp.zeros_like(C_s)
            n_s[...] = jnp.zeros_like(n_s)

        li, _, lf = _gates(z_ref, bi_ref, bf_ref)
        causal = _tri(L, True)
        b = _dot(causal.astype(F32), lf, precision=HIGHEST)
        liT = li.T
        bT = b.T
        cst_ref[0] = C_s[...].astype(BF16)
        nst_ref[0] = n_s[...]
        for h in range(A_HEADS):
            q = z_ref[:, h * A_QK:(h + 1) * A_QK] * scale
            k = z_ref[:, 512 + h * A_QK:512 + (h + 1) * A_QK]
            qb = q.astype(BF16)
            kb = k.astype(BF16)
            vb = z_ref[:, 1024 + h * A_V:1024 + (h + 1) * A_V].astype(BF16)
            a_col, b_col = li[:, h:h + 1], b[:, h:h + 1]
            a_row, b_row = liT[h:h + 1, :], bT[h:h + 1, :]
            Dm = jnp.exp(jnp.where(causal, b_col - b_row + a_row, -jnp.inf))
            A = _dot_nt(qb, kb) * Dm
            eb = jnp.exp(b_col)
            Ch = C_s[h]
            nh = n_s[h:h + 1, :]
            num = _dot(A.astype(BF16), vb) + eb * _dot(qb, Ch.astype(BF16))
            den = jnp.sum(A, axis=-1, keepdims=True) + eb * jnp.sum(q * nh, axis=-1, keepdims=True)
            h_ref[:, h * A_V:(h + 1) * A_V] = num / jnp.maximum(jnp.abs(den), 1.0)
            bL = b_col[L - 1:L, :]
            kw = jnp.exp(bL - b_col + a_col) * k
            decay = jnp.exp(bL)
            C_s[h] = decay * Ch + _dot_tn(kw.astype(BF16), vb)
            n_s[h:h + 1, :] = decay * nh + jnp.sum(kw, axis=0, keepdims=True)

    vec = pl.BlockSpec((1, 128), lambda c: (0, 0))
    return pl.pallas_call(
        body, name=name, grid=(NC,),
        in_specs=[pl.BlockSpec((L, Z_W), lambda c: (c, 0)), vec, vec],
        out_specs=[pl.BlockSpec((L, 1024), lambda c: (c, 0)),
                   pl.BlockSpec((1, A_HEADS, A_QK, A_V), lambda c: (c, 0, 0, 0)),
                   pl.BlockSpec((1, 8, 128), lambda c: (c, 0, 0))],
        out_shape=[jax.ShapeDtypeStruct((S, 1024), F32),
                   jax.ShapeDtypeStruct((NC, A_HEADS, A_QK, A_V), BF16),
                   jax.ShapeDtypeStruct((NC, 8, 128), F32)],
        scratch_shapes=[pltpu.VMEM((A_HEADS, A_QK, A_V), F32), pltpu.VMEM((8, 128), F32)],
        compiler_params=_cp(("arbitrary",)),
    )(z, bi, bf)


def mlstm_bwd(z, bi, bf, cst, nst, dh, dzo, *, name):
    S = z.shape[0]
    L = MLSTM_CHUNK
    NC = S // L
    scale = A_QK ** -0.5

    def body(z_ref, bi_ref, bf_ref, cst_ref, nst_ref, dh_ref, dzo_ref, dz_ref, db_ref, dC_s, dn_s):
        @pl.when(pl.program_id(0) == 0)
        def _():
            dC_s[...] = jnp.zeros_like(dC_s)
            dn_s[...] = jnp.zeros_like(dn_s)
            db_ref[...] = jnp.zeros_like(db_ref)

        li, scf, lf = _gates(z_ref, bi_ref, bf_ref)
        causal = _tri(L, True)
        b = _dot(causal.astype(F32), lf, precision=HIGHEST)
        liT = li.T
        bT = b.T
        lane = lax.broadcasted_iota(jnp.int32, (L, 128), 1)
        sub = lax.broadcasted_iota(jnp.int32, (128, L), 0)
        lane1 = lax.broadcasted_iota(jnp.int32, (1, 128), 1)
        Rm = jnp.zeros((L, 128), F32)
        KIm = jnp.zeros((L, 128), F32)
        csm = jnp.zeros((128, L), F32)
        Xm = jnp.zeros((1, 128), F32)
        for h in range(A_HEADS):
            q = z_ref[:, h * A_QK:(h + 1) * A_QK] * scale
            k = z_ref[:, 512 + h * A_QK:512 + (h + 1) * A_QK]
            qb = q.astype(BF16)
            kb = k.astype(BF16)
            vb = z_ref[:, 1024 + h * A_V:1024 + (h + 1) * A_V].astype(BF16)
            a_col, b_col = li[:, h:h + 1], b[:, h:h + 1]
            a_row, b_row = liT[h:h + 1, :], bT[h:h + 1, :]
            Dm = jnp.exp(jnp.where(causal, b_col - b_row + a_row, -jnp.inf))
            Sqk = _dot_nt(qb, kb)
            A = Sqk * Dm
            Ab = A.astype(BF16)
            eb = jnp.exp(b_col)
            Cb = cst_ref[0, h]
            nh = nst_ref[0, h:h + 1, :]
            num = _dot(Ab, vb) + eb * _dot(qb, Cb)
            den = jnp.sum(A, axis=-1, keepdims=True) + eb * jnp.sum(q * nh, axis=-1, keepdims=True)
            aden = jnp.abs(den)
            u = 1.0 / jnp.maximum(aden, 1.0)
            dhh = dh_ref[:, h * A_V:(h + 1) * A_V]
            dnum = dhh * u
            dden = jnp.where(aden > 1.0, -jnp.sum(dhh * num, axis=-1, keepdims=True) * u * u * jnp.sign(den), 0.0)
            dnb = dnum.astype(BF16)
            G = Dm * (_dot_nt(dnb, vb) + dden)
            Gb = G.astype(BF16)
            E = G * Sqk
            rs = jnp.sum(E, axis=-1, keepdims=True)
            cs = jnp.sum(E, axis=0, keepdims=True)
            dCh = dC_s[h]
            dnh = dn_s[h:h + 1, :]
            dCb = dCh.astype(BF16)
            bL = b_col[L - 1:L, :]
            wk = jnp.exp(bL - b_col + a_col)
            decay = jnp.exp(bL)
            dq_inter = eb * (_dot_nt(dnb, Cb) + dden * nh)
            dk_inter = wk * (_dot_nt(vb, dCb) + dnh)
            dq = _dot(Gb, kb) + dq_inter
            dk = _dot_tn(Gb, qb) + dk_inter
            dv = _dot_tn(Ab, dnb) + wk * _dot(kb, dCb)
            dz_ref[:, h * A_QK:(h + 1) * A_QK] = (dq * scale).astype(BF16)
            dz_ref[:, 512 + h * A_QK:512 + (h + 1) * A_QK] = dk.astype(BF16)
            dz_ref[:, 1024 + h * A_V:1024 + (h + 1) * A_V] = dv.astype(BF16)
            KI = jnp.sum(k * dk_inter, axis=-1, keepdims=True)
            R = rs + jnp.sum(q * dq_inter, axis=-1, keepdims=True)
            cross = (jnp.sum(jnp.sum(dCh * Cb.astype(F32), axis=0, keepdims=True), axis=1, keepdims=True)
                     + jnp.sum(dnh * nh, axis=1, keepdims=True))
            Xm = jnp.where(lane1 == h, decay * cross, Xm)
            Rm = jnp.where(lane == h, R, Rm)
            KIm = jnp.where(lane == h, KI, KIm)
            csm = jnp.where(sub == h, cs, csm)
            ebq = eb * q
            dC_s[h] = decay * dCh + _dot_tn(ebq.astype(BF16), dnb)
            dn_s[h:h + 1, :] = decay * dnh + jnp.sum(ebq * dden, axis=0, keepdims=True)
        dz_ref[:, 2048:3072] = dzo_ref[...]
        cs_col = csm.T
        da = cs_col + KIm
        rr = lax.broadcasted_iota(jnp.int32, (L, L), 0)
        cc = lax.broadcasted_iota(jnp.int32, (L, L), 1)
        dlf = (_dot((rr <= cc).astype(F32), Rm - cs_col, precision=HIGHEST)
               + _dot((rr > cc).astype(F32), KIm, precision=HIGHEST) + Xm)
        dpre_i = da * (1.0 - (li * (1.0 / SOFTCAP)) ** 2)
        dpre_f = dlf * (1.0 - _sigmoid(scf)) * (1.0 - (scf * (1.0 / SOFTCAP)) ** 2)
        dz_ref[:, Z_GI:Z_GI + 128] = dpre_i.astype(BF16)
        dz_ref[:, Z_GF:Z_GF + 128] = dpre_f.astype(BF16)
        db_ref[0:1, :] += jnp.sum(dpre_i, axis=0, keepdims=True)
        db_ref[1:2, :] += jnp.sum(dpre_f, axis=0, keepdims=True)

    vec = pl.BlockSpec((1, 128), lambda c: (0, 0))
    rev = lambda c: (NC - 1 - c, 0)
    return pl.pallas_call(
        body, name=name, grid=(NC,),
        in_specs=[pl.BlockSpec((L, Z_W), rev), vec, vec,
                  pl.BlockSpec((1, A_HEADS, A_QK, A_V), lambda c: (NC - 1 - c, 0, 0, 0)),
                  pl.BlockSpec((1, 8, 128), lambda c: (NC - 1 - c, 0, 0)),
                  pl.BlockSpec((L, 1024), rev), pl.BlockSpec((L, 1024), rev)],
        out_specs=[pl.BlockSpec((L, Z_W), rev), pl.BlockSpec((8, 128), lambda c: (0, 0))],
        out_shape=[jax.ShapeDtypeStruct((S, Z_W), BF16), jax.ShapeDtypeStruct((8, 128), F32)],
        scratch_shapes=[pltpu.VMEM((A_HEADS, A_QK, A_V), F32), pltpu.VMEM((8, 128), F32)],
        compiler_params=_cp(("arbitrary",)),
    )(z, bi, bf, cst, nst, dh, dzo)


def ao_fwd(h, z, gh, w_out, x, g_next, *, name, tm=512):
    S = h.shape[0]

    def body(h_ref, o_ref, gh_ref, w_ref, x_ref, gn_ref, x1_ref, hg_ref, xn_ref):
        for hd in range(A_HEADS):
            sl = slice(hd * A_V, (hd + 1) * A_V)
            hs = h_ref[:, sl]
            hn = hs * lax.rsqrt(jnp.mean(hs * hs, axis=-1, keepdims=True) + EPS) * gh_ref[:, sl]
            hg_ref[:, sl] = (hn * _sigmoid(o_ref[:, sl])).astype(BF16)
        x1 = x_ref[...] + _dot(hg_ref[...], w_ref[...])
        x1_ref[...] = x1
        xn_ref[...] = (x1 * lax.rsqrt(jnp.mean(x1 * x1, axis=-1, keepdims=True) + EPS) * gn_ref[...]).astype(BF16)

    row = pl.BlockSpec((tm, 1024), lambda i: (i, 0))
    vec = pl.BlockSpec((1, 1024), lambda i: (0, 0))
    return pl.pallas_call(
        body, name=name, grid=(S // tm,),
        in_specs=[row, pl.BlockSpec((tm, 1024), lambda i: (i, 2)), vec,
                  pl.BlockSpec((1024, 1024), lambda i: (0, 0)), row, vec],
        out_specs=[row, row, row],
        out_shape=[jax.ShapeDtypeStruct((S, 1024), F32), jax.ShapeDtypeStruct((S, 1024), BF16),
                   jax.ShapeDtypeStruct((S, 1024), BF16)],
        compiler_params=_cp(("parallel",)),
    )(h, z, gh, w_out, x, g_next)


def ao_bwd(g1, w_out_t, h, z, gh, *, name, tm=512):
    S = h.shape[0]

    def body(g_ref, w_ref, h_ref, o_ref, gh_ref, dh_ref, dzo_ref, dgh_ref):
        @pl.when(pl.program_id(0) == 0)
        def _():
            dgh_ref[...] = jnp.zeros_like(dgh_ref)

        dhg = _dot(g_ref[...].astype(BF16), w_ref[...])
        for hd in range(A_HEADS):
            sl = slice(hd * A_V, (hd + 1) * A_V)
            hs = h_ref[:, sl]
            r = lax.rsqrt(jnp.mean(hs * hs, axis=-1, keepdims=True) + EPS)
            hhat = hs * r
            ghs = gh_ref[:, sl]
            sig = _sigmoid(o_ref[:, sl])
            d = dhg[:, sl]
            dhn = d * sig
            dzo_ref[:, sl] = (d * hhat * ghs * sig * (1.0 - sig)).astype(BF16)
            dgh_ref[:, sl] += jnp.sum(dhn * hhat, axis=0, keepdims=True)
            dhhat = dhn * ghs
            dh_ref[:, sl] = r * (dhhat - hhat * jnp.mean(dhhat * hhat, axis=-1, keepdims=True))

    row = pl.BlockSpec((tm, 1024), lambda i: (i, 0))
    vec = pl.BlockSpec((1, 1024), lambda i: (0, 0))
    return pl.pallas_call(
        body, name=name, grid=(S // tm,),
        in_specs=[row, pl.BlockSpec((1024, 1024), lambda i: (0, 0)), row,
                  pl.BlockSpec((tm, 1024), lambda i: (i, 2)), vec],
        out_specs=[row, row, vec],
        out_shape=[jax.ShapeDtypeStruct((S, 1024), F32), jax.ShapeDtypeStruct((S, 1024), BF16),
                   jax.ShapeDtypeStruct((1, 1024), F32)],
        compiler_params=_cp(("arbitrary",)),
    )(g1, w_out_t, h, z, gh)


CONV_TC = 1408
CONV_HALO = 16


def mm_up_conv(xn, w_up, cw, cb, *, name, tm=512):
    S = xn.shape[0]
    TN = 2 * CONV_TC

    def body(x_ref, w_ref, cw_ref, cb_ref, u_ref, c_ref, a_ref, carry):
        @pl.when(pl.program_id(1) == 0)
        def _():
            carry[...] = jnp.zeros_like(carry)

        ub = _dot(x_ref[...], w_ref[...]).astype(BF16)
        u_ref[...] = ub
        u = ub.astype(F32)
        prev = carry[...]
        row = lax.broadcasted_iota(jnp.int32, u.shape, 0)
        u1 = jnp.where(row == 0, prev[7:8], pltpu.roll(u, 1, axis=0))
        u2 = jnp.where(row == 0, prev[6:7], jnp.where(row == 1, prev[7:8], pltpu.roll(u, 2, axis=0)))
        carry[...] = u[tm - 8:tm]
        w = cw_ref[...]
        c = u * w[2:3] + u1 * w[1:2] + u2 * w[0:1] + cb_ref[...]
        c_ref[...] = c.astype(BF16)
        g, v = c[:, :CONV_TC], c[:, CONV_TC:]
        a_ref[...] = (g * _sigmoid(g) * v).astype(BF16)

    big = pl.BlockSpec((tm, TN), lambda j, i: (i, j))
    return pl.pallas_call(
        body, name=name, grid=(2, S // tm),
        in_specs=[pl.BlockSpec((tm, D), lambda j, i: (i, 0)), pl.BlockSpec((D, TN), lambda j, i: (0, j)),
                  pl.BlockSpec((3, TN), lambda j, i: (0, j)), pl.BlockSpec((1, TN), lambda j, i: (0, j))],
        out_specs=[big, big, pl.BlockSpec((tm, CONV_TC), lambda j, i: (i, j))],
        out_shape=[jax.ShapeDtypeStruct((S, 2 * D_FF), BF16), jax.ShapeDtypeStruct((S, 2 * D_FF), BF16),
                   jax.ShapeDtypeStruct((S, D_FF), BF16)],
        scratch_shapes=[pltpu.VMEM((8, TN), F32)],
        compiler_params=_cp(("parallel", "arbitrary")),
    )(xn, w_up, cw, cb)


def mm_da_act(dxn, w_down_t, c, *, name, tm=512):
    S = c.shape[0]
    TN = 2 * CONV_TC

    def body(x_ref, w_ref, c_ref, duc_ref):
        dav = _dot(x_ref[...].astype(BF16), w_ref[...])
        g = c_ref[:, :CONV_TC].astype(F32)
        v = c_ref[:, CONV_TC:].astype(F32)
        sg = _sigmoid(g)
        gs = g * sg
        duc_ref[:, :CONV_TC] = (dav * v * (sg + gs * (1.0 - sg))).astype(BF16)
        duc_ref[:, CONV_TC:] = (dav * gs).astype(BF16)

    return pl.pallas_call(
        body, name=name, grid=(2, S // tm),
        in_specs=[pl.BlockSpec((tm, D), lambda j, i: (i, 0)), pl.BlockSpec((D, CONV_TC), lambda j, i: (0, j)),
                  pl.BlockSpec((tm, TN), lambda j, i: (i, j))],
        out_specs=pl.BlockSpec((tm, TN), lambda j, i: (i, j)),
        out_shape=jax.ShapeDtypeStruct((S, 2 * D_FF), BF16),
        compiler_params=_cp(("parallel", "parallel")),
    )(dxn, w_down_t, c)


def tn_up_conv(xn, duc, u, cw, *, name, tk=512):
    S, Kd = xn.shape
    N = duc.shape[1]
    hb = tk // CONV_HALO
    nblk = S // tk

    def body(x_ref, d_ref, halo_ref, u_ref, w_ref, o_ref, du_ref, dwb_ref):
        k = pl.program_id(1)

        @pl.when(k == 0)
        def _():
            o_ref[...] = jnp.zeros_like(o_ref)
            dwb_ref[...] = jnp.zeros_like(dwb_ref)

        d = d_ref[...].astype(F32)
        hl = jnp.where(k == nblk - 1, 0.0, halo_ref[...].astype(F32))
        row = lax.broadcasted_iota(jnp.int32, d.shape, 0)
        d1 = jnp.where(row == tk - 1, hl[0:1], pltpu.roll(d, tk - 1, axis=0))
        d2 = jnp.where(row == tk - 1, hl[1:2], jnp.where(row == tk - 2, hl[0:1], pltpu.roll(d, tk - 2, axis=0)))
        w = w_ref[...]
        du = (d * w[2:3] + d1 * w[1:2] + d2 * w[0:1]).astype(BF16)
        du_ref[...] = du
        o_ref[...] += _dot_tn(x_ref[...], du)
        u = u_ref[...].astype(F32)
        dwb_ref[0:1, :] += jnp.sum(u * d2, axis=0, keepdims=True)
        dwb_ref[1:2, :] += jnp.sum(u * d1, axis=0, keepdims=True)
        dwb_ref[2:3, :] += jnp.sum(u * d, axis=0, keepdims=True)
        dwb_ref[3:4, :] += jnp.sum(d, axis=0, keepdims=True)

    blk = pl.BlockSpec((tk, CONV_TC), lambda j, k: (k, j))
    return pl.pallas_call(
        body, name=name, grid=(N // CONV_TC, nblk),
        in_specs=[pl.BlockSpec((tk, Kd), lambda j, k: (k, 0)), blk,
                  pl.BlockSpec((CONV_HALO, CONV_TC), lambda j, k: (jnp.minimum((k + 1) * hb, nblk * hb - 1), j)),
                  blk, pl.BlockSpec((3, CONV_TC), lambda j, k: (0, j))],
        out_specs=[pl.BlockSpec((Kd, CONV_TC), lambda j, k: (0, j)), blk, pl.BlockSpec((8, CONV_TC), lambda j, k: (0, j))],
        out_shape=[jax.ShapeDtypeStruct((Kd, N), F32), jax.ShapeDtypeStruct((S, N), BF16),
                   jax.ShapeDtypeStruct((8, N), F32)],
        compiler_params=_cp(("parallel", "arbitrary")),
    )(xn, duc, duc, u, cw)


def _t5_bucket(dist):
    max_exact = REL_BUCKETS // 2
    d = np.maximum(dist, 0)
    log_ratio = np.log(np.maximum(d, 1) / max_exact) / math.log(REL_MAX_DIST / max_exact)
    large = np.minimum(max_exact + (log_ratio * (REL_BUCKETS - max_exact)).astype(np.int64), REL_BUCKETS - 1)
    return np.where(d < max_exact, d, large).astype(np.int32)


def _bucket_tables():
    delta = BLK + np.arange(BLK)[:, None] - np.arange(2 * BLK)[None, :]
    return np.stack([_t5_bucket(delta * dil) for _, dil in B_GROUPS]).astype(np.int32)


def bias_build(rel_bias, buckets, *, name):
    def body(rel_ref, bk_ref, o_ref):
        g = pl.program_id(0)
        bk = bk_ref[0]
        for h in range(B_HEADS):
            acc = jnp.zeros((BLK, 2 * BLK), F32)
            for bb in range(REL_BUCKETS):
                acc = jnp.where(bk == bb, rel_ref[bb, g * B_HEADS + h], acc)
            o_ref[0, h] = acc

    return pl.pallas_call(
        body, name=name, grid=(N_GROUPS,),
        in_specs=[pl.BlockSpec(memory_space=pltpu.SMEM), pl.BlockSpec((1, BLK, 2 * BLK), lambda g: (g, 0, 0))],
        out_specs=pl.BlockSpec((1, B_HEADS, BLK, 2 * BLK), lambda g: (g, 0, 0, 0)),
        out_shape=jax.ShapeDtypeStruct((N_GROUPS, B_HEADS, BLK, 2 * BLK), F32),
        compiler_params=_cp(("arbitrary",)),
    )(rel_bias, buckets)


def bias_grad(dbias, buckets, *, name):
    def body(db_ref, bk_ref, o_ref):
        g = pl.program_id(0)

        @pl.when(g == 0)
        def _():
            o_ref[...] = jnp.zeros_like(o_ref)

        bk = bk_ref[0]
        rr = lax.broadcasted_iota(jnp.int32, (REL_BUCKETS, 128), 0)
        cc = lax.broadcasted_iota(jnp.int32, (REL_BUCKETS, 128), 1)
        acc = jnp.zeros((REL_BUCKETS, 128), F32)
        for h in range(B_HEADS):
            dbh = db_ref[0, h]
            for bb in range(REL_BUCKETS):
                part = jnp.sum(jnp.where(bk == bb, dbh, 0.0), axis=0, keepdims=True)
                s = jnp.sum(part, axis=1, keepdims=True)
                acc = acc + jnp.where((rr == bb) & (cc == g * B_HEADS + h), s, 0.0)
        o_ref[...] += acc

    return pl.pallas_call(
        body, name=name, grid=(N_GROUPS,),
        in_specs=[pl.BlockSpec((1, B_HEADS, BLK, 2 * BLK), lambda g: (g, 0, 0, 0)),
                  pl.BlockSpec((1, BLK, 2 * BLK), lambda g: (g, 0, 0))],
        out_specs=pl.BlockSpec((REL_BUCKETS, 128), lambda g: (0, 0)),
        out_shape=jax.ShapeDtypeStruct((REL_BUCKETS, 128), F32),
        compiler_params=_cp(("arbitrary",)),
    )(dbias, buckets)


HG = 4
GW = HG * B_DH


def _head_masks(dtype):
    lane = lax.broadcasted_iota(jnp.int32, (BLK, GW), 1)
    return [((lane >= h * B_DH) & (lane < (h + 1) * B_DH)).astype(dtype) for h in range(HG)]


def _band_masks():
    iq = lax.broadcasted_iota(jnp.int32, (BLK, BLK), 0)
    ik = lax.broadcasted_iota(jnp.int32, (BLK, BLK), 1)
    return iq <= ik, iq >= ik


def attn_fwd(qg, kg, vg, bias, g, dil, *, name):
    S = qg.size // 1024
    S2 = S // dil
    nb = S2 // BLK
    W = dil * 1024
    scale = B_DH ** -0.5

    def body(q_ref, kc_ref, kp_ref, vc_ref, vp_ref, b_ref, o_ref, lse_ref):
        has_prev = pl.program_id(1) > 0
        vp_m, vc_m = _band_masks()
        valid = jnp.concatenate([vp_m & has_prev, vc_m], axis=1)
        mb = _head_masks(BF16)
        mf = _head_masks(F32)
        lane = lax.broadcasted_iota(jnp.int32, (BLK, 128), 1)
        lse_acc = jnp.zeros((BLK, 128), F32)
        for hg in range(B_HEADS // HG):
            sl = slice(hg * GW, (hg + 1) * GW)
            q4 = q_ref[:, sl]
            kcat = jnp.concatenate([kp_ref[:, sl], kc_ref[:, sl]], axis=0)
            vcat = jnp.concatenate([vp_ref[:, sl], vc_ref[:, sl]], axis=0)
            s4 = _dot_nt(jnp.concatenate([q4 * mb[h] for h in range(HG)], axis=0), kcat)
            ps, rl = [], []
            for h in range(HG):
                hh = hg * HG + h
                s = jnp.where(valid, s4[h * BLK:(h + 1) * BLK] * scale + b_ref[0, hh], -jnp.inf)
                m = jnp.max(s, axis=-1, keepdims=True)
                p = jnp.exp(s - m)
                l = jnp.sum(p, axis=-1, keepdims=True)
                ps.append(p.astype(BF16))
                rl.append(1.0 / l)
                lse_acc = jnp.where(lane == hh, m + jnp.log(l), lse_acc)
            o4 = _dot(jnp.concatenate(ps, axis=0), vcat)
            acc = jnp.zeros((BLK, GW), F32)
            for h in range(HG):
                acc = acc + o4[h * BLK:(h + 1) * BLK] * (mf[h] * rl[h])
            o_ref[:, sl] = acc.astype(BF16)
        lse_ref[...] = lse_acc

    cur = pl.BlockSpec((BLK, 1024), lambda r, n: (n, r))
    prev = pl.BlockSpec((BLK, 1024), lambda r, n: (jnp.maximum(n - 1, 0), r))
    q2, k2, v2 = qg.reshape(S2, W), kg.reshape(S2, W), vg.reshape(S2, W)
    o, lse = pl.pallas_call(
        body, name=name, grid=(dil, nb),
        in_specs=[cur, cur, prev, cur, prev, pl.BlockSpec((1, B_HEADS, BLK, 2 * BLK), lambda r, n: (g, 0, 0, 0))],
        out_specs=[cur, pl.BlockSpec((BLK, 128), lambda r, n: (n, r))],
        out_shape=[jax.ShapeDtypeStruct((S2, W), BF16), jax.ShapeDtypeStruct((S2, dil * 128), F32)],
        compiler_params=_cp(("parallel", "arbitrary")),
    )(q2, k2, k2, v2, v2, bias)
    return o.reshape(S, 1024), lse.reshape(S, 128)


def attn_merge(os_, lses, *, name, tm=512):
    S = os_[0].shape[0]
    expand = np.zeros((128, 1024), np.float32)
    for h in range(B_HEADS):
        expand[h, h * B_DH:(h + 1) * B_DH] = 1.0
    expand = jnp.asarray(expand, BF16)

    def body(o0, o1, o2, l0, l1, l2, e_ref, out_ref, lse_ref):
        ls = [l0[...], l1[...], l2[...]]
        m = jnp.maximum(jnp.maximum(ls[0], ls[1]), ls[2])
        ws = [jnp.exp(l - m) for l in ls]
        tot = ws[0] + ws[1] + ws[2]
        lse_ref[...] = m + jnp.log(tot)
        acc = jnp.zeros((tm, 1024), F32)
        for w, o in zip(ws, (o0, o1, o2)):
            acc = acc + _dot_split(w / tot, e_ref[...]) * o[...].astype(F32)
        out_ref[...] = acc.astype(BF16)

    row = pl.BlockSpec((tm, 1024), lambda i: (i, 0))
    lrow = pl.BlockSpec((tm, 128), lambda i: (i, 0))
    return pl.pallas_call(
        body, name=name, grid=(S // tm,),
        in_specs=[row, row, row, lrow, lrow, lrow, pl.BlockSpec((128, 1024), lambda i: (0, 0))],
        out_specs=[row, lrow],
        out_shape=[jax.ShapeDtypeStruct((S, 1024), BF16), jax.ShapeDtypeStruct((S, 128), F32)],
        compiler_params=_cp(("parallel",)),
    )(*os_, *lses, expand)


def mm_dout(dx, w_t, att, *, name, tm=512):
    S = dx.shape[0]
    heads = np.zeros((1024, 128), np.float32)
    for h in range(B_HEADS):
        heads[h * B_DH:(h + 1) * B_DH, h] = 1.0

    def body(x_ref, w_ref, att_ref, e_ref, do_ref, d_ref):
        acc = _dot(x_ref[...].astype(BF16), w_ref[...])
        do_ref[...] = acc.astype(BF16)
        d_ref[...] = _dot_split(acc * att_ref[...].astype(F32), e_ref[...])

    row = pl.BlockSpec((tm, 1024), lambda i: (i, 0))
    return pl.pallas_call(
        body, name=name, grid=(S // tm,),
        in_specs=[row, pl.BlockSpec((1024, 1024), lambda i: (0, 0)), row, pl.BlockSpec((1024, 128), lambda i: (0, 0))],
        out_specs=[row, pl.BlockSpec((tm, 128), lambda i: (i, 0))],
        out_shape=[jax.ShapeDtypeStruct((S, 1024), BF16), jax.ShapeDtypeStruct((S, 128), F32)],
        compiler_params=_cp(("parallel",)),
    )(dx, w_t, att, jnp.asarray(heads, BF16))


def attn_bwd(qg, kg, vg, bias, dout, dsum, lse, g, dil, *, name):
    S = qg.size // 1024
    S2 = S // dil
    nb = S2 // BLK
    W = dil * 1024
    scale = B_DH ** -0.5

    def body(q_ref, kc_ref, kp_ref, vc_ref, vp_ref, b_ref, do_ref, dsum_ref, lse_ref,
             dq_ref, dk_ref, dv_ref, db_ref, ck_s, cv_s):
        n = pl.program_id(1)

        @pl.when((pl.program_id(0) == 0) & (n == 0))
        def _():
            db_ref[...] = jnp.zeros_like(db_ref)

        @pl.when(n == 0)
        def _():
            ck_s[...] = jnp.zeros_like(ck_s)
            cv_s[...] = jnp.zeros_like(cv_s)

        @pl.when(n == nb)
        def _():
            dk_ref[...] = ck_s[...].astype(BF16)
            dv_ref[...] = cv_s[...].astype(BF16)

        @pl.when(n < nb)
        def _():
            vp_m, vc_m = _band_masks()
            valid = jnp.concatenate([vp_m & (n > 0), vc_m], axis=1)
            mb = _head_masks(BF16)
            mf = _head_masks(F32)
            lse_blk = lse_ref[...]
            dsum_blk = dsum_ref[...]
            for hg in range(B_HEADS // HG):
                sl = slice(hg * GW, (hg + 1) * GW)
                kcat = jnp.concatenate([kp_ref[:, sl], kc_ref[:, sl]], axis=0)
                vcat = jnp.concatenate([vp_ref[:, sl], vc_ref[:, sl]], axis=0)
                dob = do_ref[:, sl]
                q4 = q_ref[:, sl]
                q4m = jnp.concatenate([q4 * mb[h] for h in range(HG)], axis=0)
                do4m = jnp.concatenate([dob * mb[h] for h in range(HG)], axis=0)
                s4 = _dot_nt(q4m, kcat)
                dp4 = _dot_nt(do4m, vcat)
                ps, dss = [], []
                for h in range(HG):
                    hh = hg * HG + h
                    rows = slice(h * BLK, (h + 1) * BLK)
                    s = jnp.where(valid, s4[rows] * scale + b_ref[0, hh] - lse_blk[:, hh:hh + 1], -jnp.inf)
                    p = jnp.exp(s)
                    ds = p * (dp4[rows] - dsum_blk[:, hh:hh + 1])
                    db_ref[hh] += ds
                    ps.append(p.astype(BF16))
                    dss.append(ds.astype(BF16))
                p4 = jnp.concatenate(ps, axis=0)
                ds4 = jnp.concatenate(dss, axis=0)
                dq4 = _dot(ds4, kcat)
                acc = jnp.zeros((BLK, GW), F32)
                for h in range(HG):
                    acc = acc + dq4[h * BLK:(h + 1) * BLK] * mf[h]
                dq_ref[:, sl] = (acc * scale).astype(BF16)
                dkc = _dot_tn(ds4, q4m) * scale
                dvc = _dot_tn(p4, do4m)
                dk_ref[:, sl] = (ck_s[:, sl] + dkc[0:BLK]).astype(BF16)
                dv_ref[:, sl] = (cv_s[:, sl] + dvc[0:BLK]).astype(BF16)
                ck_s[:, sl] = dkc[BLK:2 * BLK]
                cv_s[:, sl] = dvc[BLK:2 * BLK]

    last = nb - 1
    cur = lambda r, n: (jnp.minimum(n, last), r)
    prev = lambda r, n: (jnp.clip(n - 1, 0, last), r)
    row = lambda im: pl.BlockSpec((BLK, 1024), im)
    k2, v2 = kg.reshape(S2, W), vg.reshape(S2, W)
    dq, dk, dv, dbias = pl.pallas_call(
        body, name=name, grid=(dil, nb + 1),
        in_specs=[row(cur), row(cur), row(prev), row(cur), row(prev),
                  pl.BlockSpec((1, B_HEADS, BLK, 2 * BLK), lambda r, n: (g, 0, 0, 0)),
                  row(cur), pl.BlockSpec((BLK, 128), cur), pl.BlockSpec((BLK, 128), cur)],
        out_specs=[row(cur), row(prev), row(prev), pl.BlockSpec((B_HEADS, BLK, 2 * BLK), lambda r, n: (0, 0, 0))],
        out_shape=[jax.ShapeDtypeStruct((S2, W), BF16)] * 3 + [jax.ShapeDtypeStruct((B_HEADS, BLK, 2 * BLK), F32)],
        scratch_shapes=[pltpu.VMEM((BLK, 1024), F32), pltpu.VMEM((BLK, 1024), F32)],
        compiler_params=_cp(("arbitrary", "arbitrary")),
    )(qg.reshape(S2, W), k2, k2, v2, v2, bias, dout.reshape(S2, W), dsum.reshape(S2, dil * 128),
      lse.reshape(S2, dil * 128))
    return dq.reshape(S, 1024), dk.reshape(S, 1024), dv.reshape(S, 1024), dbias


def _slot(px, py, pc):
    return 4 * px + 2 * py + pc


def ag_weights(wb, ws):
    def body(wb_ref, ws_ref, ob_ref, os_ref, send_sems, recv_sems, local_sems):
        x, y, c = lax.axis_index("x"), lax.axis_index("y"), lax.axis_index("c")
        me, sibling = (x, y, c), (x, y, 1 - c)
        chips = [(1 - x, y), (x, 1 - y), (1 - x, 1 - y)]
        arrays = [(wb_ref, ob_ref), (ws_ref, os_ref)]

        def copy(a, k, block, to, from_input=False):
            src_in, out = arrays[a]
            dst = out.at[_slot(*block)]
            return pltpu.make_async_remote_copy(
                src_ref=src_in if from_input else dst, dst_ref=dst,
                send_sem=send_sems.at[7 * a + k], recv_sem=recv_sems.at[7 * a + k],
                device_id=to, device_id_type=MESH)

        mine = [pltpu.make_async_copy(arrays[a][0], arrays[a][1].at[_slot(*me)], local_sems.at[a]) for a in range(2)]
        for cp in mine:
            cp.start()
        first = []
        for a in range(2):
            first.append(copy(a, 0, me, sibling, True))
            first += [copy(a, 1 + j, me, (*chip, c), True) for j, chip in enumerate(chips)]
        for cp in first:
            cp.start()
        passed = []
        for a in range(2):
            for j, chip in enumerate(chips):
                copy(a, 1 + j, (*chip, c), me).wait_recv()
                fw = copy(a, 4 + j, (*chip, c), sibling)
                fw.start()
                passed.append(fw)
        for a in range(2):
            copy(a, 0, sibling, me).wait_recv()
            for j, chip in enumerate(chips):
                copy(a, 4 + j, (*chip, 1 - c), me).wait_recv()
        for cp in first + passed:
            cp.wait_send()
        for cp in mine:
            cp.wait()

    any_spec = pl.BlockSpec(memory_space=pl.ANY)
    return pl.pallas_call(
        body, name="ag_weights",
        in_specs=[any_spec, any_spec], out_specs=[any_spec, any_spec],
        out_shape=[jax.ShapeDtypeStruct((N_DEV,) + wb.shape, wb.dtype), jax.ShapeDtypeStruct((N_DEV,) + ws.shape, ws.dtype)],
        scratch_shapes=[pltpu.SemaphoreType.DMA((14,)), pltpu.SemaphoreType.DMA((14,)), pltpu.SemaphoreType.DMA((2,))],
    )(wb, ws)


def rs_sibling(gpack, spack):
    def body(g_ref, s_ref, rb_ref, sa_ref, send_sems, recv_sems, local_sem):
        x, y, c = lax.axis_index("x"), lax.axis_index("y"), lax.axis_index("c")
        my = _slot(x, y, c)
        mine = pltpu.make_async_copy(s_ref, sa_ref.at[my], local_sem)
        mine.start()
        sends, recvs = [], []
        for j in range(4):
            both = dict(dst_ref=rb_ref.at[j], send_sem=send_sems.at[j], recv_sem=recv_sems.at[j],
                        device_id=(x, y, 1 - c), device_id_type=MESH)
            sends.append(pltpu.make_async_remote_copy(src_ref=g_ref.at[2 * j + 1 - c], **both))
            recvs.append(sends[-1])
        for k in range(1, N_DEV):
            peer = (1 - x if k & 4 else x, 1 - y if k & 2 else y, 1 - c if k & 1 else c)
            sems = dict(send_sem=send_sems.at[3 + k], recv_sem=recv_sems.at[3 + k], device_id=peer, device_id_type=MESH)
            sends.append(pltpu.make_async_remote_copy(src_ref=s_ref, dst_ref=sa_ref.at[my], **sems))
            recvs.append(pltpu.make_async_remote_copy(src_ref=s_ref, dst_ref=sa_ref.at[_slot(*peer)], **sems))
        for cp in sends:
            cp.start()
        for cp in recvs:
            cp.wait_recv()
        for cp in sends:
            cp.wait_send()
        mine.wait()

    any_spec = pl.BlockSpec(memory_space=pl.ANY)
    return pl.pallas_call(
        body, name="rs_sibling",
        in_specs=[any_spec, any_spec], out_specs=[any_spec, any_spec],
        out_shape=[jax.ShapeDtypeStruct((4,) + gpack.shape[1:], gpack.dtype),
                   jax.ShapeDtypeStruct((N_DEV,) + spack.shape, spack.dtype)],
        scratch_shapes=[pltpu.SemaphoreType.DMA((11,)), pltpu.SemaphoreType.DMA((11,)), pltpu.SemaphoreType.DMA],
    )(gpack, spack)


def pair_add(a, b, *, name, tr):
    R = a.shape[1]

    def body(a_ref, b_ref, o_ref):
        o_ref[...] = (a_ref[...].astype(F32) + b_ref[...].astype(F32)).astype(BF16)

    blk = pl.BlockSpec((4, tr, 1024), lambda i: (0, i, 0))
    return pl.pallas_call(
        body, name=name, grid=(R // tr,), in_specs=[blk, blk], out_specs=blk,
        out_shape=jax.ShapeDtypeStruct(a.shape, BF16), compiler_params=_cp(("parallel",)),
    )(a, b)


def rs_chips(part):
    def body(p_ref, rb_ref, send_sems, recv_sems, local_sem):
        x, y, c = lax.axis_index("x"), lax.axis_index("y"), lax.axis_index("c")
        jm = 2 * x + y
        mine = pltpu.make_async_copy(p_ref.at[jm], rb_ref.at[jm], local_sem)
        mine.start()
        sends, recvs = [], []
        for k in range(1, 4):
            px, py = (1 - x if k & 2 else x), (1 - y if k & 1 else y)
            sems = dict(send_sem=send_sems.at[k - 1], recv_sem=recv_sems.at[k - 1], device_id=(px, py, c), device_id_type=MESH)
            sends.append(pltpu.make_async_remote_copy(src_ref=p_ref.at[2 * px + py], dst_ref=rb_ref.at[jm], **sems))
            recvs.append(pltpu.make_async_remote_copy(src_ref=p_ref.at[jm], dst_ref=rb_ref.at[2 * px + py], **sems))
        for cp in sends:
            cp.start()
        for cp in recvs:
            cp.wait_recv()
        for cp in sends:
            cp.wait_send()
        mine.wait()

    any_spec = pl.BlockSpec(memory_space=pl.ANY)
    return pl.pallas_call(
        body, name="rs_chips", in_specs=[any_spec], out_specs=any_spec,
        out_shape=jax.ShapeDtypeStruct(part.shape, part.dtype),
        scratch_shapes=[pltpu.SemaphoreType.DMA((3,)), pltpu.SemaphoreType.DMA((3,)), pltpu.SemaphoreType.DMA],
    )(part)


def reduce_adam(parts, w, m, v, *, name, tr):
    R = w.shape[0]
    n_parts = parts.shape[0]
    assert R % tr == 0
    c1 = 1.0 - ADAM_B1 ** ADAM_STEP
    c2 = 1.0 - ADAM_B2 ** ADAM_STEP

    def body(p_ref, w_ref, m_ref, v_ref, g_ref, d_ref, mo_ref, vo_ref):
        g = p_ref[0].astype(F32)
        for i in range(1, n_parts):
            g = g + p_ref[i].astype(F32)
        mn = ADAM_B1 * m_ref[...] + (1.0 - ADAM_B1) * g
        vn = ADAM_B2 * v_ref[...] + (1.0 - ADAM_B2) * (g * g)
        g_ref[...] = g
        mo_ref[...] = mn
        vo_ref[...] = vn
        d_ref[...] = -ADAM_LR * ((mn / c1) / (jnp.sqrt(vn / c2) + ADAM_EPS) + ADAM_WD * w_ref[...])

    row = pl.BlockSpec((tr, 1024), lambda i: (i, 0))
    return pl.pallas_call(
        body, name=name, grid=(R // tr,),
        in_specs=[pl.BlockSpec((n_parts, tr, 1024), lambda i: (0, i, 0)), row, row, row],
        out_specs=[row] * 4,
        out_shape=[jax.ShapeDtypeStruct((R, 1024), F32)] * 4,
        compiler_params=_cp(("parallel",)),
    )(parts, w, m, v)


BIG = (("a_w_in", 385, 400), ("a_w_out", 128, 128), ("w_kv", 768, 768), ("b_w_q", 384, 384),
       ("b_w_out", 128, 128), ("f_w_up", 1408, 1408), ("f_w_down", 704, 704))
SMALL_SHARDED = (("a_norm_g", 128), ("a_hnorm_g", 128), ("f_conv_w", 4224))
SMALL_ROWS = 48
PACK_ROWS = sum(b[2] for b in BIG) + SMALL_ROWS
REPL = (("kv_norm_g", 1024, 1), ("b_norm_g", 1024, 1), ("f_norm_g", 2048, 2), ("f_conv_b", 11264, 11),
        ("final_norm_g", 1024, 1), ("rel_bias", 1536, 2), ("a_b_if", 8, 1))
REPL_ROWS = 24
LOSS_ROW = 19


def _rows(a, rows, padded):
    a = a.reshape(rows, 1024)
    return a if padded == rows else jnp.pad(a, ((0, padded - rows), (0, 0)))


def pack_shards(t, dtype, with_small):
    parts = [_rows(t[n].astype(dtype), r, p) for n, r, p in BIG]
    if with_small:
        flat = jnp.concatenate([t[n].astype(dtype).reshape(-1) for n, _ in SMALL_SHARDED])
        parts.append(jnp.pad(flat, (0, SMALL_ROWS * 1024 - flat.shape[0])).reshape(SMALL_ROWS, 1024))
    return jnp.concatenate(parts, axis=0)


def unpack_shards(pack, shapes):
    out = {}
    r0 = 0
    for n, r, p in BIG:
        out[n] = pack[r0:r0 + r].reshape(shapes[n])
        r0 += p
    flat = pack[r0:r0 + SMALL_ROWS].reshape(-1)
    e0 = 0
    for n, e in SMALL_SHARDED:
        out[n] = flat[e0:e0 + e].reshape(shapes[n])
        e0 += e
    return out


def pack_repl(t):
    parts = []
    for n, e, r in REPL:
        parts.append(jnp.pad(t[n].astype(F32).reshape(-1), (0, r * 1024 - e)))
    rows = sum(r for _, _, r in REPL)
    parts.append(jnp.zeros(((REPL_ROWS - rows) * 1024,), F32))
    return jnp.concatenate(parts).reshape(REPL_ROWS, 1024)


def unpack_repl(pack, shapes):
    out = {}
    r0 = 0
    for n, e, r in REPL:
        out[n] = pack[r0:r0 + r].reshape(-1)[:e].reshape(shapes[n])
        r0 += r
    return out


def ff_blocks(a):
    b = [a[..., i * CONV_TC:(i + 1) * CONV_TC] for i in range(4)]
    return jnp.concatenate([b[0], b[2], b[1], b[3]], axis=-1)


def split_cols(full, n):
    lead = full.shape[:-1]
    return jnp.moveaxis(full.reshape(lead + (N_DEV, n)), -2, 0)


def join_cols(parts):
    t = jnp.moveaxis(parts, 0, -2)
    return t.reshape(t.shape[:-2] + (t.shape[-2] * t.shape[-1],))


def kernel(x, a_norm_g, a_w_in, a_b_if, a_hnorm_g, a_w_out, kv_norm_g, w_kv, b_norm_g, b_w_q, b_w_out, rel_bias, f_norm_g, f_w_up, f_conv_w, f_conv_b, f_w_down, final_norm_g, loss_target, m_a_norm_g, m_a_w_in, m_a_b_if, m_a_hnorm_g, m_a_w_out, m_kv_norm_g, m_w_kv, m_b_norm_g, m_b_w_q, m_b_w_out, m_rel_bias, m_f_norm_g, m_f_w_up, m_f_conv_w, m_f_conv_b, m_f_w_down, m_final_norm_g, v_a_norm_g, v_a_w_in, v_a_b_if, v_a_hnorm_g, v_a_w_out, v_kv_norm_g, v_w_kv, v_b_norm_g, v_b_w_q, v_b_w_out, v_rel_bias, v_f_norm_g, v_f_w_up, v_f_conv_w, v_f_conv_b, v_f_w_down, v_final_norm_g):
    names = ["a_norm_g", "a_w_in", "a_b_if", "a_hnorm_g", "a_w_out", "kv_norm_g", "w_kv", "b_norm_g", "b_w_q", "b_w_out",
             "rel_bias", "f_norm_g", "f_w_up", "f_conv_w", "f_conv_b", "f_w_down", "final_norm_g"]
    w = dict(zip(names, (a_norm_g, a_w_in, a_b_if, a_hnorm_g, a_w_out, kv_norm_g, w_kv, b_norm_g, b_w_q, b_w_out,
                         rel_bias, f_norm_g, f_w_up, f_conv_w, f_conv_b, f_w_down, final_norm_g)))
    mom = dict(zip(names, (m_a_norm_g, m_a_w_in, m_a_b_if, m_a_hnorm_g, m_a_w_out, m_kv_norm_g, m_w_kv, m_b_norm_g, m_b_w_q,
                           m_b_w_out, m_rel_bias, m_f_norm_g, m_f_w_up, m_f_conv_w, m_f_conv_b, m_f_w_down, m_final_norm_g)))
    vel = dict(zip(names, (v_a_norm_g, v_a_w_in, v_a_b_if, v_a_hnorm_g, v_a_w_out, v_kv_norm_g, v_w_kv, v_b_norm_g, v_b_w_q,
                           v_b_w_out, v_rel_bias, v_f_norm_g, v_f_w_up, v_f_conv_w, v_f_conv_b, v_f_w_down, v_final_norm_g)))
    shapes = {n: w[n].shape for n in names}
    S = x.shape[1]
    assert x.shape[0] == 1 and S % (16 * BLK) == 0 and S % 1024 == 0
    X0 = x.reshape(S, D)
    target = loss_target.reshape(S, D)

    wb_all, ws_all = ag_weights(pack_shards(w, BF16, False),
                                pack_shards(w, F32, True)[PACK_ROWS - SMALL_ROWS:])
    seg = {}
    r0 = 0
    for n, r, p in BIG:
        seg[n] = wb_all[:, r0:r0 + r]
        r0 += p
    W_in = join_cols(seg["a_w_in"].reshape(N_DEV, D, 385))
    W_in = jnp.concatenate([jnp.pad(W_in[:, :3076], ((0, 0), (0, 124))),
                            jnp.pad(W_in[:, 3076:3080], ((0, 0), (0, 124)))], axis=1)
    W_out = seg["a_w_out"].reshape(1024, D)
    W_kv = join_cols(seg["w_kv"].reshape(N_DEV, D, 768))
    W_q = join_cols(seg["b_w_q"].reshape(N_DEV, D, 384))
    W_bout = seg["b_w_out"].reshape(1024, D)
    W_up = join_cols(seg["f_w_up"].reshape(N_DEV, 2, D, 704))
    W_down = jnp.moveaxis(seg["f_w_down"].reshape(N_DEV, 2, 352, D), 0, 1).reshape(2, D_FF, D)
    sflat = ws_all.reshape(N_DEV, SMALL_ROWS * 1024)
    g_a = sflat[:, 0:128].reshape(1, D)
    g_h = jnp.moveaxis(sflat[:, 128:256].reshape(N_DEV, A_HEADS, 32), 0, 1).reshape(1, A_HEADS * A_V)
    conv_w = ff_blocks(join_cols(sflat[:, 256:256 + 4224].reshape(N_DEV, 2, 3, 704)))
    conv_b = ff_blocks(f_conv_b)
    W_up = ff_blocks(W_up)
    bi = jnp.pad(a_b_if[:, :A_HEADS], ((0, 0), (0, 128 - A_HEADS)))
    bfg = jnp.pad(a_b_if[:, A_HEADS:], ((0, 0), (0, 128 - A_HEADS)))
    buckets = jnp.asarray(_bucket_tables())

    (xn_a,) = rms_fwd(X0, [g_a], name="rms_a")
    z = mm(xn_a, W_in, name="mm_a_in", out_dtype=F32, tn=1664)
    h, cst, nst = mlstm_fwd(z, bi, bfg, name="mlstm_fwd")
    X1, hg, xn_f0 = ao_fwd(h, z, g_h, W_out, X0, f_norm_g[0:1], name="ao_fwd")

    u0, c0, a0 = mm_up_conv(xn_f0, W_up[0], conv_w[0], conv_b[0:1], name="mm_up_conv0")
    X2, xkn, xbn = mm(a0, W_down[0], name="mm_down0", out_dtype=F32, tn=1024, tm=512, res=X1,
                      norm_gains=[kv_norm_g.reshape(1, D), b_norm_g])
    sav0 = (xn_f0, u0, c0, a0)

    bias = bias_build(rel_bias, buckets, name="bias_build")
    col = lambda wmat, i: wmat[:, i * 1024:(i + 1) * 1024]
    qs, kk, vv, og, lg = [], [], [], [], []
    for g, (_, dil) in enumerate(B_GROUPS):
        xb_v, xk_v = xbn.reshape(S // dil, dil * D), xkn.reshape(S // dil, dil * D)
        qs.append(mm_view(xb_v, col(W_q, g), name="mm_q%d" % g))
        kk.append(mm_view(xk_v, col(W_kv, g), name="mm_k%d" % g))
        vv.append(mm_view(xk_v, col(W_kv, N_GROUPS + g), name="mm_v%d" % g))
        o_, l_ = attn_fwd(qs[g], kk[g], vv[g], bias, g, dil, name="attn_fwd%d" % g)
        og.append(o_)
        lg.append(l_)
    att, lse = attn_merge(og, lg, name="attn_merge")
    X3, xn_f1 = mm(att, W_bout, name="mm_b_out", out_dtype=F32, tn=1024, res=X2, norm_gains=[f_norm_g[1:2]])
    u1, c1, a1 = mm_up_conv(xn_f1, W_up[1], conv_w[1], conv_b[1:2], name="mm_up_conv1")
    sav1 = (xn_f1, u1, c1, a1)

    dX4, d_final_g, loss_part = mm_loss(a1, W_down[1], X3, target, final_norm_g.reshape(1, D), name="mm_down_loss")

    def ffn_bwd(X, dXn, l, sav, tag):
        xn, u, c, a = sav
        dW_down = mm_tn(a, dXn, name="tn_down" + tag, tn=1024, tk=512)
        duc = mm_da_act(dXn, W_down[l].T, c, name="mm_da_act" + tag)
        dW_up, du, dwb = tn_up_conv(xn, duc, u, conv_w[l], name="tn_up_conv" + tag)
        dX, (dg,) = mm_rms_bwd([du], [W_up[l].T], f_norm_g[l:l + 1], X, dXn, name="mm_rms_bwd_f" + tag)
        return dX, dW_down, dW_up, dwb, dg

    dX3, dWd1, dWu1, dwb1, dgf1 = ffn_bwd(X3, dX4, 1, sav1, "1")

    dW_bout = mm_tn(att, dX3, name="tn_b_out", tn=1024)
    dout, dsum = mm_dout(dX3, W_bout.T, att, name="mm_dout")
    dqs, dks, dvs, dbias = [], [], [], []
    for g, (_, dil) in enumerate(B_GROUPS):
        dq_, dk_, dv_, db_ = attn_bwd(qs[g], kk[g], vv[g], bias, dout, dsum, lse, g, dil, name="attn_bwd%d" % g)
        dqs.append(dq_)
        dks.append(dk_)
        dvs.append(dv_)
        dbias.append(db_)
    d_rel = bias_grad(jnp.stack(dbias), buckets, name="bias_grad")[:, :N_GROUPS * B_HEADS]
    dW_q = jnp.concatenate([mm_tn(xbn, d_, name="tn_q%d" % g, tn=1024) for g, d_ in enumerate(dqs)], axis=1)
    dW_kv = jnp.concatenate([mm_tn(xkn, d_, name="tn_kv%d" % i, tn=1024) for i, d_ in enumerate(dks + dvs)], axis=1)
    W_qT, W_kvT = W_q.T, W_kv.T
    rows = lambda wmat, i: wmat[i * 1024:(i + 1) * 1024]
    dxn_kv = mm_sum(dks + dvs, [rows(W_kvT, i) for i in range(2 * N_GROUPS)], name="mm_dxn_kv")
    dX2, (dg_b, dg_kv) = mm_rms_bwd(dqs, [rows(W_qT, g) for g in range(N_GROUPS)], b_norm_g, X2, dX3,
                                    extra=[(dxn_kv, kv_norm_g.reshape(1, D))], name="mm_rms_bwd_b_kv")

    dX1, dWd0, dWu0, dwb0, dgf0 = ffn_bwd(X1, dX2, 0, sav0, "0")

    dW_out = mm_tn(hg, dX1, name="tn_a_out", tn=1024)
    dh, dzo, dgh = ao_bwd(dX1, W_out.T, h, z, g_h, name="ao_bwd")
    dz, db_if = mlstm_bwd(z, bi, bfg, cst, nst, dh, dzo, name="mlstm_bwd")
    dW_in = mm_tn(xn_a, dz, name="tn_a_in", tn=1664)
    dW_in = jnp.concatenate([dW_in[:, :3076], dW_in[:, Z_GF:Z_GF + 4]], axis=1)
    dX0, (dg_a,) = mm_rms_bwd([dz], [W_in.T], g_a, X0, dX1, name="mm_rms_bwd_a")

    dWu = ff_blocks(jnp.stack([dWu0, dWu1]))
    dWd = jnp.stack([dWd0, dWd1])
    dwb = ff_blocks(jnp.stack([dwb0, dwb1]))
    slots = [
        jnp.pad(split_cols(dW_in, 385).reshape(N_DEV, 385, 1024), ((0, 0), (0, 15), (0, 0))),
        dW_out.reshape(N_DEV, 128, 1024),
        split_cols(dW_kv, 768).reshape(N_DEV, 768, 1024),
        split_cols(dW_q, 384).reshape(N_DEV, 384, 1024),
        dW_bout.reshape(N_DEV, 128, 1024),
        split_cols(dWu, 704).reshape(N_DEV, 1408, 1024),
        jnp.moveaxis(dWd.reshape(2, N_DEV, 352, D), 1, 0).reshape(N_DEV, 704, 1024),
    ]
    small = jnp.concatenate([
        dg_a.reshape(N_DEV, 128),
        split_cols(dgh.reshape(A_HEADS, A_V), 32).reshape(N_DEV, 128),
        split_cols(dwb[:, 0:3], 704).reshape(N_DEV, 4224)], axis=1)
    slots.append(jnp.pad(small, ((0, 0), (0, SMALL_ROWS * 1024 - small.shape[1]))).reshape(N_DEV, SMALL_ROWS, 1024))
    gpack = jnp.concatenate([t.astype(BF16) for t in slots], axis=1)
    repl_g = {"kv_norm_g": dg_kv, "b_norm_g": dg_b, "f_norm_g": jnp.concatenate([dgf0, dgf1]),
              "f_conv_b": dwb[:, 3], "final_norm_g": d_final_g, "rel_bias": d_rel,
              "a_b_if": jnp.concatenate([db_if[0, :A_HEADS], db_if[1, :A_HEADS]])}
    spack = pack_repl(repl_g)
    spack = spack.at[LOSS_ROW, 0].set(loss_part[0, 0])

    from_sibling, sparts = rs_sibling(gpack, spack)
    own = lax.dynamic_index_in_dim(gpack.reshape(4, 2, PACK_ROWS, 1024), lax.axis_index("c"), axis=1, keepdims=False)
    parts = rs_chips(pair_add(own, from_sibling, name="rs_pair_add", tr=PACK_ROWS // 8))
    gb, db, mb, vb = reduce_adam(parts, pack_shards(w, F32, True), pack_shards(mom, F32, True),
                                 pack_shards(vel, F32, True), name="reduce_adam_big", tr=PACK_ROWS // 8)
    gs, ds, ms, vs = reduce_adam(sparts, pack_repl(w), pack_repl(mom), pack_repl(vel), name="reduce_adam_small", tr=REPL_ROWS)
    loss = gs[LOSS_ROW, 0]

    def collect(big, sm):
        t = unpack_shards(big, shapes)
        t.update(unpack_repl(sm, shapes))
        return [t[n] for n in names]

    return (loss, dX0.reshape(1, S, D), *collect(gb, gs), *collect(db, ds), *collect(mb, ms), *collect(vb, vs))
```

```python
import functools
import math

import numpy as np
import jax
import jax.numpy as jnp
from jax import lax
from jax.experimental import pallas as pl
from jax.experimental.pallas import tpu as pltpu

F32 = jnp.float32
BF16 = jnp.bfloat16
HIGHEST = lax.Precision.HIGHEST
MESH = pl.DeviceIdType.MESH

D = 1024
A_HEADS = 4
A_QK = 128
A_V = 256
SOFTCAP = 15.0
N_GROUPS = 3
B_GROUPS = ((128, 1), (512, 4), (2048, 16))
B_HEADS = 16
B_DH = 64
BLK = 128
REL_BUCKETS = 32
REL_MAX_DIST = 2048
D_FF = 2816
EPS = 1e-6
ADAM_LR = 0.001
ADAM_B1 = 0.9
ADAM_B2 = 0.999
ADAM_EPS = 1e-08
ADAM_WD = 0.01
ADAM_STEP = 10

N_DEV = 8
V7X_VMEM_BYTES = 64 * 1024 * 1024
VMEM_LIMIT = V7X_VMEM_BYTES - 8 * 1024 * 1024
MLSTM_CHUNK = 256
Z_W = 3328
Z_GI = 3072
Z_GF = 3200


def _cp(sem):
    return pltpu.CompilerParams(dimension_semantics=sem, vmem_limit_bytes=VMEM_LIMIT)


def _dot(a, b, **kw):
    return jnp.dot(a, b, preferred_element_type=F32, **kw)


def _dot_nt(a, b):
    return lax.dot_general(a, b, (((1,), (1,)), ((), ())), preferred_element_type=F32)


def _dot_tn(a, b):
    return lax.dot_general(a, b, (((0,), (0,)), ((), ())), preferred_element_type=F32)


def _dot_split(a, b01):
    hi = a.astype(BF16)
    lo = (a - hi.astype(F32)).astype(BF16)
    return _dot(hi, b01) + _dot(lo, b01)


def mm(a, b, *, name, out_dtype, tn, tm=1024, res=None, norm_gains=()):
    M, K = a.shape
    N = b.shape[1]
    ng = len(norm_gains)
    assert M % tm == 0 and N % tn == 0 and b.shape[0] == K and (ng == 0 or tn == N)

    def body(a_ref, b_ref, *rest):
        nres = 0 if res is None else 1
        g_refs = rest[nres:nres + ng]
        o_ref = rest[nres + ng]
        acc = _dot(a_ref[...].astype(BF16), b_ref[...])
        if res is not None:
            acc = acc + rest[0][...]
        o_ref[...] = acc.astype(out_dtype)
        if ng:
            y = acc * lax.rsqrt(jnp.mean(acc * acc, axis=-1, keepdims=True) + EPS)
            for i in range(ng):
                rest[nres + ng + 1 + i][...] = (y * g_refs[i][...]).astype(BF16)

    blk = pl.BlockSpec((tm, tn), lambda j, i: (i, j))
    in_specs = [pl.BlockSpec((tm, K), lambda j, i: (i, 0)), pl.BlockSpec((K, tn), lambda j, i: (0, j))]
    args = [a, b]
    if res is not None:
        in_specs.append(blk)
        args.append(res)
    in_specs += [pl.BlockSpec((1, tn), lambda j, i: (0, j))] * ng
    outs = pl.pallas_call(
        body, name=name, grid=(N // tn, M // tm), in_specs=in_specs,
        out_specs=[blk] * (1 + ng),
        out_shape=[jax.ShapeDtypeStruct((M, N), out_dtype)] + [jax.ShapeDtypeStruct((M, N), BF16)] * ng,
        compiler_params=_cp(("parallel", "parallel")),
    )(*args, *norm_gains)
    return outs if ng else outs[0]


def mm_loss(a, b, res, target, g, *, name, tm=512):
    S, K = a.shape

    def body(a_ref, b_ref, r_ref, t_ref, g_ref, dx_ref, dg_ref, loss_ref):
        @pl.when(pl.program_id(0) == 0)
        def _():
            dg_ref[...] = jnp.zeros_like(dg_ref)
            loss_ref[...] = jnp.zeros_like(loss_ref)

        xf = r_ref[...] + _dot(a_ref[...], b_ref[...])
        gg = g_ref[...]
        r = lax.rsqrt(jnp.mean(xf * xf, axis=-1, keepdims=True) + EPS)
        xhat = xf * r
        e = xhat * gg - t_ref[...]
        loss_ref[...] += 0.5 * jnp.sum(jnp.mean(e * e, axis=-1, keepdims=True), axis=0, keepdims=True)
        dy = e * (1.0 / D)
        dg_ref[...] += jnp.sum(dy * xhat, axis=0, keepdims=True)
        dyg = dy * gg
        dx_ref[...] = r * (dyg - xhat * jnp.mean(dyg * xhat, axis=-1, keepdims=True))

    row = pl.BlockSpec((tm, D), lambda i: (i, 0))
    vec = pl.BlockSpec((1, D), lambda i: (0, 0))
    return pl.pallas_call(
        body, name=name, grid=(S // tm,),
        in_specs=[pl.BlockSpec((tm, K), lambda i: (i, 0)), pl.BlockSpec((K, D), lambda i: (0, 0)), row, row, vec],
        out_specs=[row, vec, pl.BlockSpec((1, 128), lambda i: (0, 0))],
        out_shape=[jax.ShapeDtypeStruct((S, D), F32), jax.ShapeDtypeStruct((1, D), F32),
                   jax.ShapeDtypeStruct((1, 128), F32)],
        compiler_params=_cp(("arbitrary",)),
    )(a, b, res, target, g)


def mm_sum(a_list, b_list, *, name, tm=512):
    M, K = a_list[0].shape
    N = b_list[0].shape[1]
    n = len(a_list)
    assert M % tm == 0

    def body(*refs):
        o_ref = refs[-1]
        acc = _dot(refs[0][...], refs[n][...])
        for i in range(1, n):
            acc = acc + _dot(refs[i][...], refs[n + i][...])
        o_ref[...] = acc

    return pl.pallas_call(
        body, name=name, grid=(M // tm,),
        in_specs=[pl.BlockSpec((tm, K), lambda i: (i, 0))] * n + [pl.BlockSpec((K, N), lambda i: (0, 0))] * n,
        out_specs=pl.BlockSpec((tm, N), lambda i: (i, 0)),
        out_shape=jax.ShapeDtypeStruct((M, N), F32),
        compiler_params=_cp(("parallel",)),
    )(*a_list, *b_list)


def mm_view(a_view, w, *, name, tm=1024):
    S2, W = a_view.shape
    tm = min(tm, S2)

    def body(a_ref, w_ref, o_ref):
        o_ref[...] = _dot(a_ref[...], w_ref[...]).astype(BF16)

    blk = pl.BlockSpec((tm, 1024), lambda r, i: (i, r))
    return pl.pallas_call(
        body, name=name, grid=(W // 1024, S2 // tm),
        in_specs=[blk, pl.BlockSpec((1024, 1024), lambda r, i: (0, 0))], out_specs=blk,
        out_shape=jax.ShapeDtypeStruct((S2, W), BF16), compiler_params=_cp(("parallel", "parallel")),
    )(a_view, w)


def mm_tn(a, b, *, name, tn, tk=1024):
    S, Kd = a.shape
    N = b.shape[1]
    assert S % tk == 0 and N % tn == 0 and b.shape[0] == S

    def body(a_ref, b_ref, o_ref):
        @pl.when(pl.program_id(1) == 0)
        def _():
            o_ref[...] = jnp.zeros_like(o_ref)

        o_ref[...] += _dot_tn(a_ref[...].astype(BF16), b_ref[...].astype(BF16))

    return pl.pallas_call(
        body, name=name, grid=(N // tn, S // tk),
        in_specs=[pl.BlockSpec((tk, Kd), lambda j, k: (k, 0)), pl.BlockSpec((tk, tn), lambda j, k: (k, j))],
        out_specs=pl.BlockSpec((Kd, tn), lambda j, k: (0, j)),
        out_shape=jax.ShapeDtypeStruct((Kd, N), F32),
        compiler_params=_cp(("parallel", "arbitrary")),
    )(a, b)


def rms_fwd(x, gains, *, name, tm=1024):
    S = x.shape[0]
    n = len(gains)

    def body(x_ref, *rest):
        xf = x_ref[...]
        y = xf * lax.rsqrt(jnp.mean(xf * xf, axis=-1, keepdims=True) + EPS)
        for i in range(n):
            rest[n + i][...] = (y * rest[i][...]).astype(BF16)

    return pl.pallas_call(
        body, name=name, grid=(S // tm,),
        in_specs=[pl.BlockSpec((tm, D), lambda i: (i, 0))] + [pl.BlockSpec((1, D), lambda i: (0, 0))] * n,
        out_specs=[pl.BlockSpec((tm, D), lambda i: (i, 0))] * n,
        out_shape=[jax.ShapeDtypeStruct((S, D), BF16)] * n,
        compiler_params=_cp(("parallel",)),
    )(x, *gains)


def mm_rms_bwd(a_list, b_list, g, x, dres, extra=(), *, name, tm=512):
    S = x.shape[0]
    n = len(a_list)
    ne = len(extra)

    def body(*refs):
        a_refs, b_refs = refs[:n], refs[n:2 * n]
        g_ref, x_ref, dres_ref = refs[2 * n:2 * n + 3]
        e_refs = refs[2 * n + 3:2 * n + 3 + 2 * ne]
        dx_ref = refs[2 * n + 3 + 2 * ne]
        dg_refs = refs[2 * n + 4 + 2 * ne:]

        @pl.when(pl.program_id(0) == 0)
        def _():
            for r in dg_refs:
                r[...] = jnp.zeros_like(r)

        acc = _dot(a_refs[0][...], b_refs[0][...])
        for i in range(1, n):
            acc = acc + _dot(a_refs[i][...], b_refs[i][...])
        xf = x_ref[...]
        r = lax.rsqrt(jnp.mean(xf * xf, axis=-1, keepdims=True) + EPS)
        xhat = xf * r
        total = dres_ref[...]
        branches = [(acc, g_ref[...])] + [(e_refs[2 * i][...], e_refs[2 * i + 1][...]) for i in range(ne)]
        for i, (dy, gg) in enumerate(branches):
            dg_refs[i][...] += jnp.sum(dy * xhat, axis=0, keepdims=True)
            dyg = dy * gg
            total = total + r * (dyg - xhat * jnp.mean(dyg * xhat, axis=-1, keepdims=True))
        dx_ref[...] = total

    row = pl.BlockSpec((tm, D), lambda i: (i, 0))
    vec = pl.BlockSpec((1, D), lambda i: (0, 0))
    in_specs = ([pl.BlockSpec((tm, a.shape[1]), lambda i: (i, 0)) for a in a_list]
                + [pl.BlockSpec(b.shape, lambda i: (0, 0)) for b in b_list] + [vec, row, row])
    args = list(a_list) + list(b_list) + [g, x, dres]
    for dxn_e, g_e in extra:
        in_specs += [row, vec]
        args += [dxn_e, g_e]
    outs = pl.pallas_call(
        body, name=name, grid=(S // tm,), in_specs=in_specs,
        out_specs=[row] + [vec] * (1 + ne),
        out_shape=[jax.ShapeDtypeStruct((S, D), F32)] + [jax.ShapeDtypeStruct((1, D), F32)] * (1 + ne),
        compiler_params=_cp(("arbitrary",)),
    )(*args)
    return outs[0], outs[1:]


def _sigmoid(x):
    return 1.0 / (1.0 + jnp.exp(-x))


def _gates(z_ref, bi_ref, bf_ref):
    li = SOFTCAP * jnp.tanh((z_ref[:, Z_GI:Z_GI + 128] + bi_ref[...]) * (1.0 / SOFTCAP))
    scf = SOFTCAP * jnp.tanh((z_ref[:, Z_GF:Z_GF + 128] + bf_ref[...]) * (1.0 / SOFTCAP))
    lf = jnp.minimum(scf, 0.0) - jnp.log(1.0 + jnp.exp(-jnp.abs(scf)))
    return li, scf, lf


def _tri(L, lower):
    r = lax.broadcasted_iota(jnp.int32, (L, L), 0)
    c = lax.broadcasted_iota(jnp.int32, (L, L), 1)
    return (r >= c) if lower else (r <= c)


def mlstm_fwd(z, bi, bf, *, name):
    S = z.shape[0]
    L = MLSTM_CHUNK
    NC = S // L
    scale = A_QK ** -0.5

    def body(z_ref, bi_ref, bf_ref, h_ref, cst_ref, nst_ref, C_s, n_s):
        @pl.when(pl.program_id(0) == 0)
        def _():
            C_s[...] = jnp.zeros_like(C_s)
            n_s[...] = jnp.zeros_like(n_s)

        li, _, lf = _gates(z_ref, bi_ref, bf_ref)
        causal = _tri(L, True)
        b = _dot(causal.astype(F32), lf, precision=HIGHEST)
        liT = li.T
        bT = b.T
        cst_ref[0] = C_s[...].astype(BF16)
        nst_ref[0] = n_s[...]
        for h in range(A_HEADS):
            q = z_ref[:, h * A_QK:(h + 1) * A_QK] * scale
            k = z_ref[:, 512 + h * A_QK:512 + (h + 1) * A_QK]
            qb = q.astype(BF16)
            kb = k.astype(BF16)
            vb = z_ref[:, 1024 + h * A_V:1024 + (h + 1) * A_V].astype(BF16)
            a_col, b_col = li[:, h:h + 1], b[:, h:h + 1]
            a_row, b_row = liT[h:h + 1, :], bT[h:h + 1, :]
            Dm = jnp.exp(jnp.where(causal, b_col - b_row + a_row, -jnp.inf))
            A = _dot_nt(qb, kb) * Dm
            eb = jnp.exp(b_col)
            Ch = C_s[h]
            nh = n_s[h:h + 1, :]
            num = _dot(A.astype(BF16), vb) + eb * _dot(qb, Ch.astype(BF16))
            den = jnp.sum(A, axis=-1, keepdims=True) + eb * jnp.sum(q * nh, axis=-1, keepdims=True)
            h_ref[:, h * A_V:(h + 1) * A_V] = num / jnp.maximum(jnp.abs(den), 1.0)
            bL = b_col[L - 1:L, :]
            kw = jnp.exp(bL - b_col + a_col) * k
            decay = jnp.exp(bL)
            C_s[h] = decay * Ch + _dot_tn(kw.astype(BF16), vb)
            n_s[h:h + 1, :] = decay * nh + jnp.sum(kw, axis=0, keepdims=True)

    vec = pl.BlockSpec((1, 128), lambda c: (0, 0))
    return pl.pallas_call(
        body, name=name, grid=(NC,),
        in_specs=[pl.BlockSpec((L, Z_W), lambda c: (c, 0)), vec, vec],
        out_specs=[pl.BlockSpec((L, 1024), lambda c: (c, 0)),
                   pl.BlockSpec((1, A_HEADS, A_QK, A_V), lambda c: (c, 0, 0, 0)),
                   pl.BlockSpec((1, 8, 128), lambda c: (c, 0, 0))],
        out_shape=[jax.ShapeDtypeStruct((S, 1024), F32),
                   jax.ShapeDtypeStruct((NC, A_HEADS, A_QK, A_V), BF16),
                   jax.ShapeDtypeStruct((NC, 8, 128), F32)],
        scratch_shapes=[pltpu.VMEM((A_HEADS, A_QK, A_V), F32), pltpu.VMEM((8, 128), F32)],
        compiler_params=_cp(("arbitrary",)),
    )(z, bi, bf)


def mlstm_bwd(z, bi, bf, cst, nst, dh, dzo, *, name):
    S = z.shape[0]
    L = MLSTM_CHUNK
    NC = S // L
    scale = A_QK ** -0.5

    def body(z_ref, bi_ref, bf_ref, cst_ref, nst_ref, dh_ref, dzo_ref, dz_ref, db_ref, dC_s, dn_s):
        @pl.when(pl.program_id(0) == 0)
        def _():
            dC_s[...] = jnp.zeros_like(dC_s)
            dn_s[...] = jnp.zeros_like(dn_s)
            db_ref[...] = jnp.zeros_like(db_ref)

        li, scf, lf = _gates(z_ref, bi_ref, bf_ref)
        causal = _tri(L, True)
        b = _dot(causal.astype(F32), lf, precision=HIGHEST)
        liT = li.T
        bT = b.T
        lane = lax.broadcasted_iota(jnp.int32, (L, 128), 1)
        sub = lax.broadcasted_iota(jnp.int32, (128, L), 0)
        lane1 = lax.broadcasted_iota(jnp.int32, (1, 128), 1)
        Rm = jnp.zeros((L, 128), F32)
        KIm = jnp.zeros((L, 128), F32)
        csm = jnp.zeros((128, L), F32)
        Xm = jnp.zeros((1, 128), F32)
        for h in range(A_HEADS):
            q = z_ref[:, h * A_QK:(h + 1) * A_QK] * scale
            k = z_ref[:, 512 + h * A_QK:512 + (h + 1) * A_QK]
            qb = q.astype(BF16)
            kb = k.astype(BF16)
            vb = z_ref[:, 1024 + h * A_V:1024 + (h + 1) * A_V].astype(BF16)
            a_col, b_col = li[:, h:h + 1], b[:, h:h + 1]
            a_row, b_row = liT[h:h + 1, :], bT[h:h + 1, :]
            Dm = jnp.exp(jnp.where(causal, b_col - b_row + a_row, -jnp.inf))
            Sqk = _dot_nt(qb, kb)
            A = Sqk * Dm
            Ab = A.astype(BF16)
            eb = jnp.exp(b_col)
            Cb = cst_ref[0, h]
            nh = nst_ref[0, h:h + 1, :]
            num = _dot(Ab, vb) + eb * _dot(qb, Cb)
            den = jnp.sum(A, axis=-1, keepdims=True) + eb * jnp.sum(q * nh, axis=-1, keepdims=True)
            aden = jnp.abs(den)
            u = 1.0 / jnp.maximum(aden, 1.0)
            dhh = dh_ref[:, h * A_V:(h + 1) * A_V]
            dnum = dhh * u
            dden = jnp.where(aden > 1.0, -jnp.sum(dhh * num, axis=-1, keepdims=True) * u * u * jnp.sign(den), 0.0)
            dnb = dnum.astype(BF16)
            G = Dm * (_dot_nt(dnb, vb) + dden)
            Gb = G.astype(BF16)
            E = G * Sqk
            rs = jnp.sum(E, axis=-1, keepdims=True)
            cs = jnp.sum(E, axis=0, keepdims=True)
            dCh = dC_s[h]
            dnh = dn_s[h:h + 1, :]
            dCb = dCh.astype(BF16)
            bL = b_col[L - 1:L, :]
            wk = jnp.exp(bL - b_col + a_col)
            decay = jnp.exp(bL)
            dq_inter = eb * (_dot_nt(dnb, Cb) + dden * nh)
            dk_inter = wk * (_dot_nt(vb, dCb) + dnh)
            dq = _dot(Gb, kb) + dq_inter
            dk = _dot_tn(Gb, qb) + dk_inter
            dv = _dot_tn(Ab, dnb) + wk * _dot(kb, dCb)
            dz_ref[:, h * A_QK:(h + 1) * A_QK] = (dq * scale).astype(BF16)
            dz_ref[:, 512 + h * A_QK:512 + (h + 1) * A_QK] = dk.astype(BF16)
            dz_ref[:, 1024 + h * A_V:1024 + (h + 1) * A_V] = dv.astype(BF16)
            KI = jnp.sum(k * dk_inter, axis=-1, keepdims=True)
            R = rs + jnp.sum(q * dq_inter, axis=-1, keepdims=True)
            cross = (jnp.sum(jnp.sum(dCh * Cb.astype(F32), axis=0, keepdims=True), axis=1, keepdims=True)
                     + jnp.sum(dnh * nh, axis=1, keepdims=True))
            Xm = jnp.where(lane1 == h, decay * cross, Xm)
            Rm = jnp.where(lane == h, R, Rm)
            KIm = jnp.where(lane == h, KI, KIm)
            csm = jnp.where(sub == h, cs, csm)
            ebq = eb * q
            dC_s[h] = decay * dCh + _dot_tn(ebq.astype(BF16), dnb)
            dn_s[h:h + 1, :] = decay * dnh + jnp.sum(ebq * dden, axis=0, keepdims=True)
        dz_ref[:, 2048:3072] = dzo_ref[...]
        cs_col = csm.T
        da = cs_col + KIm
        rr = lax.broadcasted_iota(jnp.int32, (L, L), 0)
        cc = lax.broadcasted_iota(jnp.int32, (L, L), 1)
        dlf = (_dot((rr <= cc).astype(F32), Rm - cs_col, precision=HIGHEST)
               + _dot((rr > cc).astype(F32), KIm, precision=HIGHEST) + Xm)
        dpre_i = da * (1.0 - (li * (1.0 / SOFTCAP)) ** 2)
        dpre_f = dlf * (1.0 - _sigmoid(scf)) * (1.0 - (scf * (1.0 / SOFTCAP)) ** 2)
        dz_ref[:, Z_GI:Z_GI + 128] = dpre_i.astype(BF16)
        dz_ref[:, Z_GF:Z_GF + 128] = dpre_f.astype(BF16)
        db_ref[0:1, :] += jnp.sum(dpre_i, axis=0, keepdims=True)
        db_ref[1:2, :] += jnp.sum(dpre_f, axis=0, keepdims=True)

    vec = pl.BlockSpec((1, 128), lambda c: (0, 0))
    rev = lambda c: (NC - 1 - c, 0)
    return pl.pallas_call(
        body, name=name, grid=(NC,),
        in_specs=[pl.BlockSpec((L, Z_W), rev), vec, vec,
                  pl.BlockSpec((1, A_HEADS, A_QK, A_V), lambda c: (NC - 1 - c, 0, 0, 0)),
                  pl.BlockSpec((1, 8, 128), lambda c: (NC - 1 - c, 0, 0)),
                  pl.BlockSpec((L, 1024), rev), pl.BlockSpec((L, 1024), rev)],
        out_specs=[pl.BlockSpec((L, Z_W), rev), pl.BlockSpec((8, 128), lambda c: (0, 0))],
        out_shape=[jax.ShapeDtypeStruct((S, Z_W), BF16), jax.ShapeDtypeStruct((8, 128), F32)],
        scratch_shapes=[pltpu.VMEM((A_HEADS, A_QK, A_V), F32), pltpu.VMEM((8, 128), F32)],
        compiler_params=_cp(("arbitrary",)),
    )(z, bi, bf, cst, nst, dh, dzo)


def ao_fwd(h, z, gh, w_out, x, g_next, *, name, tm=512):
    S = h.shape[0]

    def body(h_ref, o_ref, gh_ref, w_ref, x_ref, gn_ref, x1_ref, hg_ref, xn_ref):
        for hd in range(A_HEADS):
            sl = slice(hd * A_V, (hd + 1) * A_V)
            hs = h_ref[:, sl]
            hn = hs * lax.rsqrt(jnp.mean(hs * hs, axis=-1, keepdims=True) + EPS) * gh_ref[:, sl]
            hg_ref[:, sl] = (hn * _sigmoid(o_ref[:, sl])).astype(BF16)
        x1 = x_ref[...] + _dot(hg_ref[...], w_ref[...])
        x1_ref[...] = x1
        xn_ref[...] = (x1 * lax.rsqrt(jnp.mean(x1 * x1, axis=-1, keepdims=True) + EPS) * gn_ref[...]).astype(BF16)

    row = pl.BlockSpec((tm, 1024), lambda i: (i, 0))
    vec = pl.BlockSpec((1, 1024), lambda i: (0, 0))
    return pl.pallas_call(
        body, name=name, grid=(S // tm,),
        in_specs=[row, pl.BlockSpec((tm, 1024), lambda i: (i, 2)), vec,
                  pl.BlockSpec((1024, 1024), lambda i: (0, 0)), row, vec],
        out_specs=[row, row, row],
        out_shape=[jax.ShapeDtypeStruct((S, 1024), F32), jax.ShapeDtypeStruct((S, 1024), BF16),
                   jax.ShapeDtypeStruct((S, 1024), BF16)],
        compiler_params=_cp(("parallel",)),
    )(h, z, gh, w_out, x, g_next)


def ao_bwd(g1, w_out_t, h, z, gh, *, name, tm=512):
    S = h.shape[0]

    def body(g_ref, w_ref, h_ref, o_ref, gh_ref, dh_ref, dzo_ref, dgh_ref):
        @pl.when(pl.program_id(0) == 0)
        def _():
            dgh_ref[...] = jnp.zeros_like(dgh_ref)

        dhg = _dot(g_ref[...].astype(BF16), w_ref[...])
        for hd in range(A_HEADS):
            sl = slice(hd * A_V, (hd + 1) * A_V)
            hs = h_ref[:, sl]
            r = lax.rsqrt(jnp.mean(hs * hs, axis=-1, keepdims=True) + EPS)
            hhat = hs * r
            ghs = gh_ref[:, sl]
            sig = _sigmoid(o_ref[:, sl])
            d = dhg[:, sl]
            dhn = d * sig
            dzo_ref[:, sl] = (d * hhat * ghs * sig * (1.0 - sig)).astype(BF16)
            dgh_ref[:, sl] += jnp.sum(dhn * hhat, axis=0, keepdims=True)
            dhhat = dhn * ghs
            dh_ref[:, sl] = r * (dhhat - hhat * jnp.mean(dhhat * hhat, axis=-1, keepdims=True))

    row = pl.BlockSpec((tm, 1024), lambda i: (i, 0))
    vec = pl.BlockSpec((1, 1024), lambda i: (0, 0))
    return pl.pallas_call(
        body, name=name, grid=(S // tm,),
        in_specs=[row, pl.BlockSpec((1024, 1024), lambda i: (0, 0)), row,
                  pl.BlockSpec((tm, 1024), lambda i: (i, 2)), vec],
        out_specs=[row, row, vec],
        out_shape=[jax.ShapeDtypeStruct((S, 1024), F32), jax.ShapeDtypeStruct((S, 1024), BF16),
                   jax.ShapeDtypeStruct((1, 1024), F32)],
        compiler_params=_cp(("arbitrary",)),
    )(g1, w_out_t, h, z, gh)


CONV_TC = 1408
CONV_HALO = 16


def mm_up_conv(xn, w_up, cw, cb, *, name, tm=512):
    S = xn.shape[0]
    TN = 2 * CONV_TC

    def body(x_ref, w_ref, cw_ref, cb_ref, u_ref, c_ref, a_ref, carry):
        @pl.when(pl.program_id(1) == 0)
        def _():
            carry[...] = jnp.zeros_like(carry)

        ub = _dot(x_ref[...], w_ref[...]).astype(BF16)
        u_ref[...] = ub
        u = ub.astype(F32)
        prev = carry[...]
        row = lax.broadcasted_iota(jnp.int32, u.shape, 0)
        u1 = jnp.where(row == 0, prev[7:8], pltpu.roll(u, 1, axis=0))
        u2 = jnp.where(row == 0, prev[6:7], jnp.where(row == 1, prev[7:8], pltpu.roll(u, 2, axis=0)))
        carry[...] = u[tm - 8:tm]
        w = cw_ref[...]
        c = u * w[2:3] + u1 * w[1:2] + u2 * w[0:1] + cb_ref[...]
        c_ref[...] = c.astype(BF16)
        g, v = c[:, :CONV_TC], c[:, CONV_TC:]
        a_ref[...] = (g * _sigmoid(g) * v).astype(BF16)

    big = pl.BlockSpec((tm, TN), lambda j, i: (i, j))
    return pl.pallas_call(
        body, name=name, grid=(2, S // tm),
        in_specs=[pl.BlockSpec((tm, D), lambda j, i: (i, 0)), pl.BlockSpec((D, TN), lambda j, i: (0, j)),
                  pl.BlockSpec((3, TN), lambda j, i: (0, j)), pl.BlockSpec((1, TN), lambda j, i: (0, j))],
        out_specs=[big, big, pl.BlockSpec((tm, CONV_TC), lambda j, i: (i, j))],
        out_shape=[jax.ShapeDtypeStruct((S, 2 * D_FF), BF16), jax.ShapeDtypeStruct((S, 2 * D_FF), BF16),
                   jax.ShapeDtypeStruct((S, D_FF), BF16)],
        scratch_shapes=[pltpu.VMEM((8, TN), F32)],
        compiler_params=_cp(("parallel", "arbitrary")),
    )(xn, w_up, cw, cb)


def mm_da_act(dxn, w_down_t, c, *, name, tm=512):
    S = c.shape[0]
    TN = 2 * CONV_TC

    def body(x_ref, w_ref, c_ref, duc_ref):
        dav = _dot(x_ref[...].astype(BF16), w_ref[...])
        g = c_ref[:, :CONV_TC].astype(F32)
        v = c_ref[:, CONV_TC:].astype(F32)
        sg = _sigmoid(g)
        gs = g * sg
        duc_ref[:, :CONV_TC] = (dav * v * (sg + gs * (1.0 - sg))).astype(BF16)
        duc_ref[:, CONV_TC:] = (dav * gs).astype(BF16)

    return pl.pallas_call(
        body, name=name, grid=(2, S // tm),
        in_specs=[pl.BlockSpec((tm, D), lambda j, i: (i, 0)), pl.BlockSpec((D, CONV_TC), lambda j, i: (0, j)),
                  pl.BlockSpec((tm, TN), lambda j, i: (i, j))],
        out_specs=pl.BlockSpec((tm, TN), lambda j, i: (i, j)),
        out_shape=jax.ShapeDtypeStruct((S, 2 * D_FF), BF16),
        compiler_params=_cp(("parallel", "parallel")),
    )(dxn, w_down_t, c)


def tn_up_conv(xn, duc, u, cw, *, name, tk=1024):
    S, Kd = xn.shape
    N = duc.shape[1]
    hb = tk // CONV_HALO
    nblk = S // tk

    def body(x_ref, d_ref, halo_ref, u_ref, w_ref, o_ref, du_ref, dwb_ref):
        k = pl.program_id(1)

        @pl.when(k == 0)
        def _():
            o_ref[...] = jnp.zeros_like(o_ref)
            dwb_ref[...] = jnp.zeros_like(dwb_ref)

        d = d_ref[...].astype(F32)
        hl = jnp.where(k == nblk - 1, 0.0, halo_ref[...].astype(F32))
        row = lax.broadcasted_iota(jnp.int32, d.shape, 0)
        d1 = jnp.where(row == tk - 1, hl[0:1], pltpu.roll(d, tk - 1, axis=0))
        d2 = jnp.where(row == tk - 1, hl[1:2], jnp.where(row == tk - 2, hl[0:1], pltpu.roll(d, tk - 2, axis=0)))
        w = w_ref[...]
        du = (d * w[2:3] + d1 * w[1:2] + d2 * w[0:1]).astype(BF16)
        du_ref[...] = du
        o_ref[...] += _dot_tn(x_ref[...], du)
        u = u_ref[...].astype(F32)
        dwb_ref[0:1, :] += jnp.sum(u * d2, axis=0, keepdims=True)
        dwb_ref[1:2, :] += jnp.sum(u * d1, axis=0, keepdims=True)
        dwb_ref[2:3, :] += jnp.sum(u * d, axis=0, keepdims=True)
        dwb_ref[3:4, :] += jnp.sum(d, axis=0, keepdims=True)

    blk = pl.BlockSpec((tk, CONV_TC), lambda j, k: (k, j))
    return pl.pallas_call(
        body, name=name, grid=(N // CONV_TC, nblk),
        in_specs=[pl.BlockSpec((tk, Kd), lambda j, k: (k, 0)), blk,
                  pl.BlockSpec((CONV_HALO, CONV_TC), lambda j, k: (jnp.minimum((k + 1) * hb, nblk * hb - 1), j)),
                  blk, pl.BlockSpec((3, CONV_TC), lambda j, k: (0, j))],
        out_specs=[pl.BlockSpec((Kd, CONV_TC), lambda j, k: (0, j)), blk, pl.BlockSpec((8, CONV_TC), lambda j, k: (0, j))],
        out_shape=[jax.ShapeDtypeStruct((Kd, N), F32), jax.ShapeDtypeStruct((S, N), BF16),
                   jax.ShapeDtypeStruct((8, N), F32)],
        compiler_params=_cp(("parallel", "arbitrary")),
    )(xn, duc, duc, u, cw)


def _t5_bucket(dist):
    max_exact = REL_BUCKETS // 2
    d = np.maximum(dist, 0)
    log_ratio = np.log(np.maximum(d, 1) / max_exact) / math.log(REL_MAX_DIST / max_exact)
    large = np.minimum(max_exact + (log_ratio * (REL_BUCKETS - max_exact)).astype(np.int64), REL_BUCKETS - 1)
    return np.where(d < max_exact, d, large).astype(np.int32)


def _bucket_tables():
    delta = BLK + np.arange(BLK)[:, None] - np.arange(2 * BLK)[None, :]
    return np.stack([_t5_bucket(delta * dil) for _, dil in B_GROUPS]).astype(np.int32)


def bias_build(rel_bias, buckets, *, name):
    def body(rel_ref, bk_ref, o_ref):
        g = pl.program_id(0)
        bk = bk_ref[0]
        for h in range(B_HEADS):
            acc = jnp.zeros((BLK, 2 * BLK), F32)
            for bb in range(REL_BUCKETS):
                acc = jnp.where(bk == bb, rel_ref[bb, g * B_HEADS + h], acc)
            o_ref[0, h] = acc

    return pl.pallas_call(
        body, name=name, grid=(N_GROUPS,),
        in_specs=[pl.BlockSpec(memory_space=pltpu.SMEM), pl.BlockSpec((1, BLK, 2 * BLK), lambda g: (g, 0, 0))],
        out_specs=pl.BlockSpec((1, B_HEADS, BLK, 2 * BLK), lambda g: (g, 0, 0, 0)),
        out_shape=jax.ShapeDtypeStruct((N_GROUPS, B_HEADS, BLK, 2 * BLK), F32),
        compiler_params=_cp(("arbitrary",)),
    )(rel_bias, buckets)


def bias_grad(dbias, buckets, *, name):
    def body(db_ref, bk_ref, o_ref):
        g = pl.program_id(0)

        @pl.when(g == 0)
        def _():
            o_ref[...] = jnp.zeros_like(o_ref)

        bk = bk_ref[0]
        rr = lax.broadcasted_iota(jnp.int32, (REL_BUCKETS, 128), 0)
        cc = lax.broadcasted_iota(jnp.int32, (REL_BUCKETS, 128), 1)
        acc = jnp.zeros((REL_BUCKETS, 128), F32)
        for h in range(B_HEADS):
            dbh = db_ref[0, h]
            for bb in range(REL_BUCKETS):
                part = jnp.sum(jnp.where(bk == bb, dbh, 0.0), axis=0, keepdims=True)
                s = jnp.sum(part, axis=1, keepdims=True)
                acc = acc + jnp.where((rr == bb) & (cc == g * B_HEADS + h), s, 0.0)
        o_ref[...] += acc

    return pl.pallas_call(
        body, name=name, grid=(N_GROUPS,),
        in_specs=[pl.BlockSpec((1, B_HEADS, BLK, 2 * BLK), lambda g: (g, 0, 0, 0)),
                  pl.BlockSpec((1, BLK, 2 * BLK), lambda g: (g, 0, 0))],
        out_specs=pl.BlockSpec((REL_BUCKETS, 128), lambda g: (0, 0)),
        out_shape=jax.ShapeDtypeStruct((REL_BUCKETS, 128), F32),
        compiler_params=_cp(("arbitrary",)),
    )(dbias, buckets)


HG = 4
GW = HG * B_DH


def _head_masks(dtype):
    lane = lax.broadcasted_iota(jnp.int32, (BLK, GW), 1)
    return [((lane >= h * B_DH) & (lane < (h + 1) * B_DH)).astype(dtype) for h in range(HG)]


def _band_masks():
    iq = lax.broadcasted_iota(jnp.int32, (BLK, BLK), 0)
    ik = lax.broadcasted_iota(jnp.int32, (BLK, BLK), 1)
    return iq <= ik, iq >= ik


def attn_fwd(qg, kg, vg, bias, g, dil, *, name):
    S = qg.size // 1024
    S2 = S // dil
    nb = S2 // BLK
    W = dil * 1024
    scale = B_DH ** -0.5

    def body(q_ref, kc_ref, kp_ref, vc_ref, vp_ref, b_ref, o_ref, lse_ref):
        has_prev = pl.program_id(1) > 0
        vp_m, vc_m = _band_masks()
        valid = jnp.concatenate([vp_m & has_prev, vc_m], axis=1)
        mb = _head_masks(BF16)
        mf = _head_masks(F32)
        lane = lax.broadcasted_iota(jnp.int32, (BLK, 128), 1)
        lse_acc = jnp.zeros((BLK, 128), F32)
        for hg in range(B_HEADS // HG):
            sl = slice(hg * GW, (hg + 1) * GW)
            q4 = q_ref[:, sl]
            kcat = jnp.concatenate([kp_ref[:, sl], kc_ref[:, sl]], axis=0)
            vcat = jnp.concatenate([vp_ref[:, sl], vc_ref[:, sl]], axis=0)
            s4 = _dot_nt(jnp.concatenate([q4 * mb[h] for h in range(HG)], axis=0), kcat)
            ps, rl = [], []
            for h in range(HG):
                hh = hg * HG + h
                s = jnp.where(valid, s4[h * BLK:(h + 1) * BLK] * scale + b_ref[0, hh], -jnp.inf)
                m = jnp.max(s, axis=-1, keepdims=True)
                p = jnp.exp(s - m)
                l = jnp.sum(p, axis=-1, keepdims=True)
                ps.append(p.astype(BF16))
                rl.append(1.0 / l)
                lse_acc = jnp.where(lane == hh, m + jnp.log(l), lse_acc)
            o4 = _dot(jnp.concatenate(ps, axis=0), vcat)
            acc = jnp.zeros((BLK, GW), F32)
            for h in range(HG):
                acc = acc + o4[h * BLK:(h + 1) * BLK] * (mf[h] * rl[h])
            o_ref[:, sl] = acc.astype(BF16)
        lse_ref[...] = lse_acc

    cur = pl.BlockSpec((BLK, 1024), lambda r, n: (n, r))
    prev = pl.BlockSpec((BLK, 1024), lambda r, n: (jnp.maximum(n - 1, 0), r))
    q2, k2, v2 = qg.reshape(S2, W), kg.reshape(S2, W), vg.reshape(S2, W)
    o, lse = pl.pallas_call(
        body, name=name, grid=(dil, nb),
        in_specs=[cur, cur, prev, cur, prev, pl.BlockSpec((1, B_HEADS, BLK, 2 * BLK), lambda r, n: (g, 0, 0, 0))],
        out_specs=[cur, pl.BlockSpec((BLK, 128), lambda r, n: (n, r))],
        out_shape=[jax.ShapeDtypeStruct((S2, W), BF16), jax.ShapeDtypeStruct((S2, dil * 128), F32)],
        compiler_params=_cp(("parallel", "arbitrary")),
    )(q2, k2, k2, v2, v2, bias)
    return o.reshape(S, 1024), lse.reshape(S, 128)


def attn_merge(os_, lses, *, name, tm=512):
    S = os_[0].shape[0]
    expand = np.zeros((128, 1024), np.float32)
    for h in range(B_HEADS):
        expand[h, h * B_DH:(h + 1) * B_DH] = 1.0
    expand = jnp.asarray(expand, BF16)

    def body(o0, o1, o2, l0, l1, l2, e_ref, out_ref, lse_ref):
        ls = [l0[...], l1[...], l2[...]]
        m = jnp.maximum(jnp.maximum(ls[0], ls[1]), ls[2])
        ws = [jnp.exp(l - m) for l in ls]
        tot = ws[0] + ws[1] + ws[2]
        lse_ref[...] = m + jnp.log(tot)
        acc = jnp.zeros((tm, 1024), F32)
        for w, o in zip(ws, (o0, o1, o2)):
            acc = acc + _dot_split(w / tot, e_ref[...]) * o[...].astype(F32)
        out_ref[...] = acc.astype(BF16)

    row = pl.BlockSpec((tm, 1024), lambda i: (i, 0))
    lrow = pl.BlockSpec((tm, 128), lambda i: (i, 0))
    return pl.pallas_call(
        body, name=name, grid=(S // tm,),
        in_specs=[row, row, row, lrow, lrow, lrow, pl.BlockSpec((128, 1024), lambda i: (0, 0))],
        out_specs=[row, lrow],
        out_shape=[jax.ShapeDtypeStruct((S, 1024), BF16), jax.ShapeDtypeStruct((S, 128), F32)],
        compiler_params=_cp(("parallel",)),
    )(*os_, *lses, expand)


def mm_dout(dx, w_t, att, *, name, tm=512):
    S = dx.shape[0]
    heads = np.zeros((1024, 128), np.float32)
    for h in range(B_HEADS):
        heads[h * B_DH:(h + 1) * B_DH, h] = 1.0

    def body(x_ref, w_ref, att_ref, e_ref, do_ref, d_ref):
        acc = _dot(x_ref[...].astype(BF16), w_ref[...])
        do_ref[...] = acc.astype(BF16)
        d_ref[...] = _dot_split(acc * att_ref[...].astype(F32), e_ref[...])

    row = pl.BlockSpec((tm, 1024), lambda i: (i, 0))
    return pl.pallas_call(
        body, name=name, grid=(S // tm,),
        in_specs=[row, pl.BlockSpec((1024, 1024), lambda i: (0, 0)), row, pl.BlockSpec((1024, 128), lambda i: (0, 0))],
        out_specs=[row, pl.BlockSpec((tm, 128), lambda i: (i, 0))],
        out_shape=[jax.ShapeDtypeStruct((S, 1024), BF16), jax.ShapeDtypeStruct((S, 128), F32)],
        compiler_params=_cp(("parallel",)),
    )(dx, w_t, att, jnp.asarray(heads, BF16))


def attn_bwd(qg, kg, vg, bias, dout, dsum, lse, g, dil, *, name):
    S = qg.size // 1024
    S2 = S // dil
    nb = S2 // BLK
    W = dil * 1024
    scale = B_DH ** -0.5

    def body(q_ref, kc_ref, kp_ref, vc_ref, vp_ref, b_ref, do_ref, dsum_ref, lse_ref,
             dq_ref, dk_ref, dv_ref, db_ref, ck_s, cv_s):
        n = pl.program_id(1)

        @pl.when((pl.program_id(0) == 0) & (n == 0))
        def _():
            db_ref[...] = jnp.zeros_like(db_ref)

        @pl.when(n == 0)
        def _():
            ck_s[...] = jnp.zeros_like(ck_s)
            cv_s[...] = jnp.zeros_like(cv_s)

        @pl.when(n == nb)
        def _():
            dk_ref[...] = ck_s[...].astype(BF16)
            dv_ref[...] = cv_s[...].astype(BF16)

        @pl.when(n < nb)
        def _():
            vp_m, vc_m = _band_masks()
            valid = jnp.concatenate([vp_m & (n > 0), vc_m], axis=1)
            mb = _head_masks(BF16)
            mf = _head_masks(F32)
            lse_blk = lse_ref[...]
            dsum_blk = dsum_ref[...]
            for hg in range(B_HEADS // HG):
                sl = slice(hg * GW, (hg + 1) * GW)
                kcat = jnp.concatenate([kp_ref[:, sl], kc_ref[:, sl]], axis=0)
                vcat = jnp.concatenate([vp_ref[:, sl], vc_ref[:, sl]], axis=0)
                dob = do_ref[:, sl]
                q4 = q_ref[:, sl]
                q4m = jnp.concatenate([q4 * mb[h] for h in range(HG)], axis=0)
                do4m = jnp.concatenate([dob * mb[h] for h in range(HG)], axis=0)
                s4 = _dot_nt(q4m, kcat)
                dp4 = _dot_nt(do4m, vcat)
                ps, dss = [], []
                for h in range(HG):
                    hh = hg * HG + h
                    rows = slice(h * BLK, (h + 1) * BLK)
                    s = jnp.where(valid, s4[rows] * scale + b_ref[0, hh] - lse_blk[:, hh:hh + 1], -jnp.inf)
                    p = jnp.exp(s)
                    ds = p * (dp4[rows] - dsum_blk[:, hh:hh + 1])
                    db_ref[hh] += ds
                    ps.append(p.astype(BF16))
                    dss.append(ds.astype(BF16))
                p4 = jnp.concatenate(ps, axis=0)
                ds4 = jnp.concatenate(dss, axis=0)
                dq4 = _dot(ds4, kcat)
                acc = jnp.zeros((BLK, GW), F32)
                for h in range(HG):
                    acc = acc + dq4[h * BLK:(h + 1) * BLK] * mf[h]
                dq_ref[:, sl] = (acc * scale).astype(BF16)
                dkc = _dot_tn(ds4, q4m) * scale
                dvc = _dot_tn(p4, do4m)
                dk_ref[:, sl] = (ck_s[:, sl] + dkc[0:BLK]).astype(BF16)
                dv_ref[:, sl] = (cv_s[:, sl] + dvc[0:BLK]).astype(BF16)
                ck_s[:, sl] = dkc[BLK:2 * BLK]
                cv_s[:, sl] = dvc[BLK:2 * BLK]

    last = nb - 1
    cur = lambda r, n: (jnp.minimum(n, last), r)
    prev = lambda r, n: (jnp.clip(n - 1, 0, last), r)
    row = lambda im: pl.BlockSpec((BLK, 1024), im)
    k2, v2 = kg.reshape(S2, W), vg.reshape(S2, W)
    dq, dk, dv, dbias = pl.pallas_call(
        body, name=name, grid=(dil, nb + 1),
        in_specs=[row(cur), row(cur), row(prev), row(cur), row(prev),
                  pl.BlockSpec((1, B_HEADS, BLK, 2 * BLK), lambda r, n: (g, 0, 0, 0)),
                  row(cur), pl.BlockSpec((BLK, 128), cur), pl.BlockSpec((BLK, 128), cur)],
        out_specs=[row(cur), row(prev), row(prev), pl.BlockSpec((B_HEADS, BLK, 2 * BLK), lambda r, n: (0, 0, 0))],
        out_shape=[jax.ShapeDtypeStruct((S2, W), BF16)] * 3 + [jax.ShapeDtypeStruct((B_HEADS, BLK, 2 * BLK), F32)],
        scratch_shapes=[pltpu.VMEM((BLK, 1024), F32), pltpu.VMEM((BLK, 1024), F32)],
        compiler_params=_cp(("arbitrary", "arbitrary")),
    )(qg.reshape(S2, W), k2, k2, v2, v2, bias, dout.reshape(S2, W), dsum.reshape(S2, dil * 128),
      lse.reshape(S2, dil * 128))
    return dq.reshape(S, 1024), dk.reshape(S, 1024), dv.reshape(S, 1024), dbias


def _slot(px, py, pc):
    return 4 * px + 2 * py + pc


def ag_weights(wb, ws):
    def body(wb_ref, ws_ref, ob_ref, os_ref, send_sems, recv_sems, local_sems):
        x, y, c = lax.axis_index("x"), lax.axis_index("y"), lax.axis_index("c")
        me, sibling = (x, y, c), (x, y, 1 - c)
        chips = [(1 - x, y), (x, 1 - y), (1 - x, 1 - y)]
        arrays = [(wb_ref, ob_ref), (ws_ref, os_ref)]

        def copy(a, k, block, to, from_input=False):
            src_in, out = arrays[a]
            dst = out.at[_slot(*block)]
            return pltpu.make_async_remote_copy(
                src_ref=src_in if from_input else dst, dst_ref=dst,
                send_sem=send_sems.at[7 * a + k], recv_sem=recv_sems.at[7 * a + k],
                device_id=to, device_id_type=MESH)

        mine = [pltpu.make_async_copy(arrays[a][0], arrays[a][1].at[_slot(*me)], local_sems.at[a]) for a in range(2)]
        for cp in mine:
            cp.start()
        first = []
        for a in range(2):
            first.append(copy(a, 0, me, sibling, True))
            first += [copy(a, 1 + j, me, (*chip, c), True) for j, chip in enumerate(chips)]
        for cp in first:
            cp.start()
        passed = []
        for a in range(2):
            for j, chip in enumerate(chips):
                copy(a, 1 + j, (*chip, c), me).wait_recv()
                fw = copy(a, 4 + j, (*chip, c), sibling)
                fw.start()
                passed.append(fw)
        for a in range(2):
            copy(a, 0, sibling, me).wait_recv()
            for j, chip in enumerate(chips):
                copy(a, 4 + j, (*chip, 1 - c), me).wait_recv()
        for cp in first + passed:
            cp.wait_send()
        for cp in mine:
            cp.wait()

    any_spec = pl.BlockSpec(memory_space=pl.ANY)
    return pl.pallas_call(
        body, name="ag_weights",
        in_specs=[any_spec, any_spec], out_specs=[any_spec, any_spec],
        out_shape=[jax.ShapeDtypeStruct((N_DEV,) + wb.shape, wb.dtype), jax.ShapeDtypeStruct((N_DEV,) + ws.shape, ws.dtype)],
        scratch_shapes=[pltpu.SemaphoreType.DMA((14,)), pltpu.SemaphoreType.DMA((14,)), pltpu.SemaphoreType.DMA((2,))],
    )(wb, ws)


def rs_sibling(gpack, spack):
    def body(g_ref, s_ref, rb_ref, sa_ref, send_sems, recv_sems, local_sem):
        x, y, c = lax.axis_index("x"), lax.axis_index("y"), lax.axis_index("c")
        my = _slot(x, y, c)
        mine = pltpu.make_async_copy(s_ref, sa_ref.at[my], local_sem)
        mine.start()
        sends, recvs = [], []
        for j in range(4):
            both = dict(dst_ref=rb_ref.at[j], send_sem=send_sems.at[j], recv_sem=recv_sems.at[j],
                        device_id=(x, y, 1 - c), device_id_type=MESH)
            sends.append(pltpu.make_async_remote_copy(src_ref=g_ref.at[2 * j + 1 - c], **both))
            recvs.append(sends[-1])
        for k in range(1, N_DEV):
            peer = (1 - x if k & 4 else x, 1 - y if k & 2 else y, 1 - c if k & 1 else c)
            sems = dict(send_sem=send_sems.at[3 + k], recv_sem=recv_sems.at[3 + k], device_id=peer, device_id_type=MESH)
            sends.append(pltpu.make_async_remote_copy(src_ref=s_ref, dst_ref=sa_ref.at[my], **sems))
            recvs.append(pltpu.make_async_remote_copy(src_ref=s_ref, dst_ref=sa_ref.at[_slot(*peer)], **sems))
        for cp in sends:
            cp.start()
        for cp in recvs:
            cp.wait_recv()
        for cp in sends:
            cp.wait_send()
        mine.wait()

    any_spec = pl.BlockSpec(memory_space=pl.ANY)
    return pl.pallas_call(
        body, name="rs_sibling",
        in_specs=[any_spec, any_spec], out_specs=[any_spec, any_spec],
        out_shape=[jax.ShapeDtypeStruct((4,) + gpack.shape[1:], gpack.dtype),
                   jax.ShapeDtypeStruct((N_DEV,) + spack.shape, spack.dtype)],
        scratch_shapes=[pltpu.SemaphoreType.DMA((11,)), pltpu.SemaphoreType.DMA((11,)), pltpu.SemaphoreType.DMA],
    )(gpack, spack)


def pair_add(a, b, *, name, tr):
    R = a.shape[1]

    def body(a_ref, b_ref, o_ref):
        o_ref[...] = (a_ref[...].astype(F32) + b_ref[...].astype(F32)).astype(BF16)

    blk = pl.BlockSpec((4, tr, 1024), lambda i: (0, i, 0))
    return pl.pallas_call(
        body, name=name, grid=(R // tr,), in_specs=[blk, blk], out_specs=blk,
        out_shape=jax.ShapeDtypeStruct(a.shape, BF16), compiler_params=_cp(("parallel",)),
    )(a, b)


def rs_chips(part):
    def body(p_ref, rb_ref, send_sems, recv_sems, local_sem):
        x, y, c = lax.axis_index("x"), lax.axis_index("y"), lax.axis_index("c")
        jm = 2 * x + y
        mine = pltpu.make_async_copy(p_ref.at[jm], rb_ref.at[jm], local_sem)
        mine.start()
        sends, recvs = [], []
        for k in range(1, 4):
            px, py = (1 - x if k & 2 else x), (1 - y if k & 1 else y)
            sems = dict(send_sem=send_sems.at[k - 1], recv_sem=recv_sems.at[k - 1], device_id=(px, py, c), device_id_type=MESH)
            sends.append(pltpu.make_async_remote_copy(src_ref=p_ref.at[2 * px + py], dst_ref=rb_ref.at[jm], **sems))
            recvs.append(pltpu.make_async_remote_copy(src_ref=p_ref.at[jm], dst_ref=rb_ref.at[2 * px + py], **sems))
        for cp in sends:
            cp.start()
        for cp in recvs:
            cp.wait_recv()
        for cp in sends:
            cp.wait_send()
        mine.wait()

    any_spec = pl.BlockSpec(memory_space=pl.ANY)
    return pl.pallas_call(
        body, name="rs_chips", in_specs=[any_spec], out_specs=any_spec,
        out_shape=jax.ShapeDtypeStruct(part.shape, part.dtype),
        scratch_shapes=[pltpu.SemaphoreType.DMA((3,)), pltpu.SemaphoreType.DMA((3,)), pltpu.SemaphoreType.DMA],
    )(part)


def reduce_adam(parts, w, m, v, *, name, tr):
    R = w.shape[0]
    n_parts = parts.shape[0]
    assert R % tr == 0
    c1 = 1.0 - ADAM_B1 ** ADAM_STEP
    c2 = 1.0 - ADAM_B2 ** ADAM_STEP

    def body(p_ref, w_ref, m_ref, v_ref, g_ref, d_ref, mo_ref, vo_ref):
        g = p_ref[0].astype(F32)
        for i in range(1, n_parts):
            g = g + p_ref[i].astype(F32)
        mn = ADAM_B1 * m_ref[...] + (1.0 - ADAM_B1) * g
        vn = ADAM_B2 * v_ref[...] + (1.0 - ADAM_B2) * (g * g)
        g_ref[...] = g
        mo_ref[...] = mn
        vo_ref[...] = vn
        d_ref[...] = -ADAM_LR * ((mn / c1) / (jnp.sqrt(vn / c2) + ADAM_EPS) + ADAM_WD * w_ref[...])

    row = pl.BlockSpec((tr, 1024), lambda i: (i, 0))
    return pl.pallas_call(
        body, name=name, grid=(R // tr,),
        in_specs=[pl.BlockSpec((n_parts, tr, 1024), lambda i: (0, i, 0)), row, row, row],
        out_specs=[row] * 4,
        out_shape=[jax.ShapeDtypeStruct((R, 1024), F32)] * 4,
        compiler_params=_cp(("parallel",)),
    )(parts, w, m, v)


BIG = (("a_w_in", 385, 400), ("a_w_out", 128, 128), ("w_kv", 768, 768), ("b_w_q", 384, 384),
       ("b_w_out", 128, 128), ("f_w_up", 1408, 1408), ("f_w_down", 704, 704))
SMALL_SHARDED = (("a_norm_g", 128), ("a_hnorm_g", 128), ("f_conv_w", 4224))
SMALL_ROWS = 48
PACK_ROWS = sum(b[2] for b in BIG) + SMALL_ROWS
REPL = (("kv_norm_g", 1024, 1), ("b_norm_g", 1024, 1), ("f_norm_g", 2048, 2), ("f_conv_b", 11264, 11),
        ("final_norm_g", 1024, 1), ("rel_bias", 1536, 2), ("a_b_if", 8, 1))
REPL_ROWS = 24
LOSS_ROW = 19


def _rows(a, rows, padded):
    a = a.reshape(rows, 1024)
    return a if padded == rows else jnp.pad(a, ((0, padded - rows), (0, 0)))


def pack_shards(t, dtype, with_small):
    parts = [_rows(t[n].astype(dtype), r, p) for n, r, p in BIG]
    if with_small:
        flat = jnp.concatenate([t[n].astype(dtype).reshape(-1) for n, _ in SMALL_SHARDED])
        parts.append(jnp.pad(flat, (0, SMALL_ROWS * 1024 - flat.shape[0])).reshape(SMALL_ROWS, 1024))
    return jnp.concatenate(parts, axis=0)


def unpack_shards(pack, shapes):
    out = {}
    r0 = 0
    for n, r, p in BIG:
        out[n] = pack[r0:r0 + r].reshape(shapes[n])
        r0 += p
    flat = pack[r0:r0 + SMALL_ROWS].reshape(-1)
    e0 = 0
    for n, e in SMALL_SHARDED:
        out[n] = flat[e0:e0 + e].reshape(shapes[n])
        e0 += e
    return out


def pack_repl(t):
    parts = []
    for n, e, r in REPL:
        parts.append(jnp.pad(t[n].astype(F32).reshape(-1), (0, r * 1024 - e)))
    rows = sum(r for _, _, r in REPL)
    parts.append(jnp.zeros(((REPL_ROWS - rows) * 1024,), F32))
    return jnp.concatenate(parts).reshape(REPL_ROWS, 1024)


def unpack_repl(pack, shapes):
    out = {}
    r0 = 0
    for n, e, r in REPL:
        out[n] = pack[r0:r0 + r].reshape(-1)[:e].reshape(shapes[n])
        r0 += r
    return out


def ff_blocks(a):
    b = [a[..., i * CONV_TC:(i + 1) * CONV_TC] for i in range(4)]
    return jnp.concatenate([b[0], b[2], b[1], b[3]], axis=-1)


def split_cols(full, n):
    lead = full.shape[:-1]
    return jnp.moveaxis(full.reshape(lead + (N_DEV, n)), -2, 0)


def join_cols(parts):
    t = jnp.moveaxis(parts, 0, -2)
    return t.reshape(t.shape[:-2] + (t.shape[-2] * t.shape[-1],))


def kernel(x, a_norm_g, a_w_in, a_b_if, a_hnorm_g, a_w_out, kv_norm_g, w_kv, b_norm_g, b_w_q, b_w_out, rel_bias, f_norm_g, f_w_up, f_conv_w, f_conv_b, f_w_down, final_norm_g, loss_target, m_a_norm_g, m_a_w_in, m_a_b_if, m_a_hnorm_g, m_a_w_out, m_kv_norm_g, m_w_kv, m_b_norm_g, m_b_w_q, m_b_w_out, m_rel_bias, m_f_norm_g, m_f_w_up, m_f_conv_w, m_f_conv_b, m_f_w_down, m_final_norm_g, v_a_norm_g, v_a_w_in, v_a_b_if, v_a_hnorm_g, v_a_w_out, v_kv_norm_g, v_w_kv, v_b_norm_g, v_b_w_q, v_b_w_out, v_rel_bias, v_f_norm_g, v_f_w_up, v_f_conv_w, v_f_conv_b, v_f_w_down, v_final_norm_g):
    names = ["a_norm_g", "a_w_in", "a_b_if", "a_hnorm_g", "a_w_out", "kv_norm_g", "w_kv", "b_norm_g", "b_w_q", "b_w_out",
             "rel_bias", "f_norm_g", "f_w_up", "f_conv_w", "f_conv_b", "f_w_down", "final_norm_g"]
    w = dict(zip(names, (a_norm_g, a_w_in, a_b_if, a_hnorm_g, a_w_out, kv_norm_g, w_kv, b_norm_g, b_w_q, b_w_out,
                         rel_bias, f_norm_g, f_w_up, f_conv_w, f_conv_b, f_w_down, final_norm_g)))
    mom = dict(zip(names, (m_a_norm_g, m_a_w_in, m_a_b_if, m_a_hnorm_g, m_a_w_out, m_kv_norm_g, m_w_kv, m_b_norm_g, m_b_w_q,
                           m_b_w_out, m_rel_bias, m_f_norm_g, m_f_w_up, m_f_conv_w, m_f_conv_b, m_f_w_down, m_final_norm_g)))
    vel = dict(zip(names, (v_a_norm_g, v_a_w_in, v_a_b_if, v_a_hnorm_g, v_a_w_out, v_kv_norm_g, v_w_kv, v_b_norm_g, v_b_w_q,
                           v_b_w_out, v_rel_bias, v_f_norm_g, v_f_w_up, v_f_conv_w, v_f_conv_b, v_f_w_down, v_final_norm_g)))
    shapes = {n: w[n].shape for n in names}
    S = x.shape[1]
    assert x.shape[0] == 1 and S % (16 * BLK) == 0 and S % 1024 == 0
    X0 = x.reshape(S, D)
    target = loss_target.reshape(S, D)

    wb_all, ws_all = ag_weights(pack_shards(w, BF16, False),
                                pack_shards(w, F32, True)[PACK_ROWS - SMALL_ROWS:])
    seg = {}
    r0 = 0
    for n, r, p in BIG:
        seg[n] = wb_all[:, r0:r0 + r]
        r0 += p
    W_in = join_cols(seg["a_w_in"].reshape(N_DEV, D, 385))
    W_in = jnp.concatenate([jnp.pad(W_in[:, :3076], ((0, 0), (0, 124))),
                            jnp.pad(W_in[:, 3076:3080], ((0, 0), (0, 124)))], axis=1)
    W_out = seg["a_w_out"].reshape(1024, D)
    W_kv = join_cols(seg["w_kv"].reshape(N_DEV, D, 768))
    W_q = join_cols(seg["b_w_q"].reshape(N_DEV, D, 384))
    W_bout = seg["b_w_out"].reshape(1024, D)
    W_up = join_cols(seg["f_w_up"].reshape(N_DEV, 2, D, 704))
    W_down = jnp.moveaxis(seg["f_w_down"].reshape(N_DEV, 2, 352, D), 0, 1).reshape(2, D_FF, D)
    sflat = ws_all.reshape(N_DEV, SMALL_ROWS * 1024)
    g_a = sflat[:, 0:128].reshape(1, D)
    g_h = jnp.moveaxis(sflat[:, 128:256].reshape(N_DEV, A_HEADS, 32), 0, 1).reshape(1, A_HEADS * A_V)
    conv_w = ff_blocks(join_cols(sflat[:, 256:256 + 4224].reshape(N_DEV, 2, 3, 704)))
    conv_b = ff_blocks(f_conv_b)
    W_up = ff_blocks(W_up)
    bi = jnp.pad(a_b_if[:, :A_HEADS], ((0, 0), (0, 128 - A_HEADS)))
    bfg = jnp.pad(a_b_if[:, A_HEADS:], ((0, 0), (0, 128 - A_HEADS)))
    buckets = jnp.asarray(_bucket_tables())

    (xn_a,) = rms_fwd(X0, [g_a], name="rms_a")
    z = mm(xn_a, W_in, name="mm_a_in", out_dtype=F32, tn=1664)
    h, cst, nst = mlstm_fwd(z, bi, bfg, name="mlstm_fwd")
    X1, hg, xn_f0 = ao_fwd(h, z, g_h, W_out, X0, f_norm_g[0:1], name="ao_fwd")

    u0, c0, a0 = mm_up_conv(xn_f0, W_up[0], conv_w[0], conv_b[0:1], name="mm_up_conv0")
    X2, xkn, xbn = mm(a0, W_down[0], name="mm_down0", out_dtype=F32, tn=1024, tm=512, res=X1,
                      norm_gains=[kv_norm_g.reshape(1, D), b_norm_g])
    sav0 = (xn_f0, u0, c0, a0)

    bias = bias_build(rel_bias, buckets, name="bias_build")
    col = lambda wmat, i: wmat[:, i * 1024:(i + 1) * 1024]
    qs, kk, vv, og, lg = [], [], [], [], []
    for g, (_, dil) in enumerate(B_GROUPS):
        xb_v, xk_v = xbn.reshape(S // dil, dil * D), xkn.reshape(S // dil, dil * D)
        qs.append(mm_view(xb_v, col(W_q, g), name="mm_q%d" % g))
        kk.append(mm_view(xk_v, col(W_kv, g), name="mm_k%d" % g))
        vv.append(mm_view(xk_v, col(W_kv, N_GROUPS + g), name="mm_v%d" % g))
        o_, l_ = attn_fwd(qs[g], kk[g], vv[g], bias, g, dil, name="attn_fwd%d" % g)
        og.append(o_)
        lg.append(l_)
    att, lse = attn_merge(og, lg, name="attn_merge")
    X3, xn_f1 = mm(att, W_bout, name="mm_b_out", out_dtype=F32, tn=1024, res=X2, norm_gains=[f_norm_g[1:2]])
    u1, c1, a1 = mm_up_conv(xn_f1, W_up[1], conv_w[1], conv_b[1:2], name="mm_up_conv1")
    sav1 = (xn_f1, u1, c1, a1)

    dX4, d_final_g, loss_part = mm_loss(a1, W_down[1], X3, target, final_norm_g.reshape(1, D), name="mm_down_loss")

    def ffn_bwd(X, dXn, l, sav, tag):
        xn, u, c, a = sav
        dW_down = mm_tn(a, dXn, name="tn_down" + tag, tn=1024, tk=512)
        duc = mm_da_act(dXn, W_down[l].T, c, name="mm_da_act" + tag)
        dW_up, du, dwb = tn_up_conv(xn, duc, u, conv_w[l], name="tn_up_conv" + tag)
        dX, (dg,) = mm_rms_bwd([du], [W_up[l].T], f_norm_g[l:l + 1], X, dXn, name="mm_rms_bwd_f" + tag)
        return dX, dW_down, dW_up, dwb, dg

    dX3, dWd1, dWu1, dwb1, dgf1 = ffn_bwd(X3, dX4, 1, sav1, "1")

    dW_bout = mm_tn(att, dX3, name="tn_b_out", tn=1024)
    dout, dsum = mm_dout(dX3, W_bout.T, att, name="mm_dout")
    dqs, dks, dvs, dbias = [], [], [], []
    for g, (_, dil) in enumerate(B_GROUPS):
        dq_, dk_, dv_, db_ = attn_bwd(qs[g], kk[g], vv[g], bias, dout, dsum, lse, g, dil, name="attn_bwd%d" % g)
        dqs.append(dq_)
        dks.append(dk_)
        dvs.append(dv_)
        dbias.append(db_)
    d_rel = bias_grad(jnp.stack(dbias), buckets, name="bias_grad")[:, :N_GROUPS * B_HEADS]
    dW_q = jnp.concatenate([mm_tn(xbn, d_, name="tn_q%d" % g, tn=1024) for g, d_ in enumerate(dqs)], axis=1)
    dW_kv = jnp.concatenate([mm_tn(xkn, d_, name="tn_kv%d" % i, tn=1024) for i, d_ in enumerate(dks + dvs)], axis=1)
    W_qT, W_kvT = W_q.T, W_kv.T
    rows = lambda wmat, i: wmat[i * 1024:(i + 1) * 1024]
    dxn_kv = mm_sum(dks + dvs, [rows(W_kvT, i) for i in range(2 * N_GROUPS)], name="mm_dxn_kv")
    dX2, (dg_b, dg_kv) = mm_rms_bwd(dqs, [rows(W_qT, g) for g in range(N_GROUPS)], b_norm_g, X2, dX3,
                                    extra=[(dxn_kv, kv_norm_g.reshape(1, D))], name="mm_rms_bwd_b_kv")

    dX1, dWd0, dWu0, dwb0, dgf0 = ffn_bwd(X1, dX2, 0, sav0, "0")

    dW_out = mm_tn(hg, dX1, name="tn_a_out", tn=1024)
    dh, dzo, dgh = ao_bwd(dX1, W_out.T, h, z, g_h, name="ao_bwd")
    dz, db_if = mlstm_bwd(z, bi, bfg, cst, nst, dh, dzo, name="mlstm_bwd")
    dW_in = mm_tn(xn_a, dz, name="tn_a_in", tn=1664)
    dW_in = jnp.concatenate([dW_in[:, :3076], dW_in[:, Z_GF:Z_GF + 4]], axis=1)
    dX0, (dg_a,) = mm_rms_bwd([dz], [W_in.T], g_a, X0, dX1, name="mm_rms_bwd_a")

    dWu = ff_blocks(jnp.stack([dWu0, dWu1]))
    dWd = jnp.stack([dWd0, dWd1])
    dwb = ff_blocks(jnp.stack([dwb0, dwb1]))
    slots = [
        jnp.pad(split_cols(dW_in, 385).reshape(N_DEV, 385, 1024), ((0, 0), (0, 15), (0, 0))),
        dW_out.reshape(N_DEV, 128, 1024),
        split_cols(dW_kv, 768).reshape(N_DEV, 768, 1024),
        split_cols(dW_q, 384).reshape(N_DEV, 384, 1024),
        dW_bout.reshape(N_DEV, 128, 1024),
        split_cols(dWu, 704).reshape(N_DEV, 1408, 1024),
        jnp.moveaxis(dWd.reshape(2, N_DEV, 352, D), 1, 0).reshape(N_DEV, 704, 1024),
    ]
    small = jnp.concatenate([
        dg_a.reshape(N_DEV, 128),
        split_cols(dgh.reshape(A_HEADS, A_V), 32).reshape(N_DEV, 128),
        split_cols(dwb[:, 0:3], 704).reshape(N_DEV, 4224)], axis=1)
    slots.append(jnp.pad(small, ((0, 0), (0, SMALL_ROWS * 1024 - small.shape[1]))).reshape(N_DEV, SMALL_ROWS, 1024))
    gpack = jnp.concatenate([t.astype(BF16) for t in slots], axis=1)
    repl_g = {"kv_norm_g": dg_kv, "b_norm_g": dg_b, "f_norm_g": jnp.concatenate([dgf0, dgf1]),
              "f_conv_b": dwb[:, 3], "final_norm_g": d_final_g, "rel_bias": d_rel,
              "a_b_if": jnp.concatenate([db_if[0, :A_HEADS], db_if[1, :A_HEADS]])}
    spack = pack_repl(repl_g)
    spack = spack.at[LOSS_ROW, 0].set(loss_part[0, 0])

    from_sibling, sparts = rs_sibling(gpack, spack)
    own = lax.dynamic_index_in_dim(gpack.reshape(4, 2, PACK_ROWS, 1024), lax.axis_index("c"), axis=1, keepdims=False)
    parts = rs_chips(pair_add(own, from_sibling, name="rs_pair_add", tr=PACK_ROWS // 8))
    gb, db, mb, vb = reduce_adam(parts, pack_shards(w, F32, True), pack_shards(mom, F32, True),
                                 pack_shards(vel, F32, True), name="reduce_adam_big", tr=PACK_ROWS // 8)
    gs, ds, ms, vs = reduce_adam(sparts, pack_repl(w), pack_repl(mom), pack_repl(vel), name="reduce_adam_small", tr=REPL_ROWS)
    loss = gs[LOSS_ROW, 0]

    def collect(big, sm):
        t = unpack_shards(big, shapes)
        t.update(unpack_repl(sm, shapes))
        return [t[n] for n in names]

    return (loss, dX0.reshape(1, S, D), *collect(gb, gs), *collect(db, ds), *collect(mb, ms), *collect(vb, vs))
```
